```python
import math
import jax, jax.numpy as jnp
from jax import lax
import numpy as np

D_MODEL = 1024
BATCH = 4
SEQ = 8192
DEPTH = 2
DEC_BATCH = 16
DEC_SEQ = 32
PAST_LEN = 1024

CHUNK = 64
Q_BLOCK = 128
ROPE_THETA = 500000.0
EPS = 1e-6
NEG = -1e30
N_EVEN = (DEPTH + 1) // 2
N_ODD = DEPTH // 2

S5_WIDTH = D_MODEL // 2
S5_GROUP = 16
S5_GROUPS = S5_WIDTH // S5_GROUP
S5_STATE = 64
GM_WIDTH = D_MODEL // 2
GM_HEADS = 4
GM_HEAD_DIM = GM_WIDTH // GM_HEADS
GM_CHUNK = 128
DA_HEADS = 4
DA_DH = 64
DA_DV = 2 * DA_DH
DA_ROPE = DA_DH // 4
MLA_HEADS = 4
MLA_Q_RANK = 256
MLA_KV_RANK = 128
MLA_NOPE = 64
MLA_ROPE = 32
MLA_DV = 128
PEER_HEADS = 8
PEER_NKEYS = 128
PEER_EXPERTS = PEER_NKEYS * PEER_NKEYS
PEER_DKEY = 256
PEER_TOPK = 16
PEER_BLOCK = 256

EVEN_IN = S5_WIDTH + 2 * GM_WIDTH
EVEN_MIX = S5_WIDTH + GM_WIDTH
DA_QK = DA_HEADS * 2 * DA_DH
DA_VW = DA_HEADS * DA_DV
ODD_IN = 2 * DA_QK + DA_VW + MLA_Q_RANK + MLA_KV_RANK + MLA_ROPE
ODD_MIX = DA_VW + MLA_HEADS * MLA_DV

kernel_name = 'hybrid_streaming_encoder_step'


def rmsnorm(x, g):
    xf = x.astype(jnp.float32)
    y = xf * lax.rsqrt(jnp.mean(xf * xf, axis=-1, keepdims=True) + EPS)
    return (y * g.astype(jnp.float32)).astype(x.dtype)


def rope(x, pos, rot_dim):
    half = rot_dim // 2
    inv = ROPE_THETA ** (-jnp.arange(half, dtype=jnp.float32) * 2.0 / rot_dim)
    ang = pos.astype(jnp.float32)[:, None] * inv
    shape = (1, pos.shape[0]) + (1,) * (x.ndim - 3) + (half,)
    cos = jnp.cos(ang).reshape(shape)
    sin = jnp.sin(ang).reshape(shape)
    xf = x.astype(jnp.float32)
    x1, x2, rest = xf[..., :half], xf[..., half:rot_dim], xf[..., rot_dim:]
    out = jnp.concatenate([x1 * cos - x2 * sin, x2 * cos + x1 * sin, rest], axis=-1)
    return out.astype(x.dtype)


def chunk_mask(q_pos, k_pos):
    return k_pos[None, :] < (q_pos[:, None] // CHUNK + 1) * CHUNK


def cat_rows(past, new):
    return new if past is None else jnp.concatenate([past.astype(new.dtype), new], axis=1)


def attend_queries(core, q_args, q_pos):
    L = q_pos.shape[0]
    if L > Q_BLOCK and L % Q_BLOCK == 0:
        nb = L // Q_BLOCK
        split = lambda a: jnp.moveaxis(a.reshape((a.shape[0], nb, Q_BLOCK) + a.shape[2:]), 1, 0)
        out = lax.map(lambda t: core(*t[0], t[1]),
                      (tuple(split(a) for a in q_args), q_pos.reshape(nb, Q_BLOCK)))
        out = jnp.moveaxis(out, 0, 1)
        return out.reshape((out.shape[0], L) + out.shape[3:])
    return core(*q_args, q_pos)


def s5_mixer(xa, h0_re, h0_im, lam_re, lam_im, log_dt, b_re, b_im, c_re, c_im, d_skip, w_glu, b_glu):
    Bsz, L, _ = xa.shape
    f32 = jnp.float32
    lam = lax.complex(lam_re.astype(f32), lam_im.astype(f32))
    dt = jnp.exp(log_dt.astype(f32))[:, None]
    a_bar = jnp.exp(lam * dt)
    bmat = lax.complex(b_re.astype(f32), b_im.astype(f32))
    b_bar = ((a_bar - 1.0) / lam)[..., None] * bmat
    u = xa.astype(f32).reshape(Bsz, L, S5_GROUPS, S5_GROUP)
    bu = jnp.einsum('blgc,gpc->blgp', u.astype(jnp.complex64), b_bar)
    h0 = lax.complex(h0_re.astype(f32), h0_im.astype(f32))
    bu = bu.at[:, 0].add(a_bar * h0)
    a = jnp.broadcast_to(a_bar, bu.shape)

    def comb(e1, e2):
        a1, b1 = e1
        a2, b2 = e2
        return a2 * a1, a2 * b1 + b2

    _, h = lax.associative_scan(comb, (a, bu), axis=1)
    cmat = lax.complex(c_re.astype(f32), c_im.astype(f32))
    y = jnp.real(jnp.einsum('blgp,gcp->blgc', h, cmat)) + d_skip.astype(f32).reshape(S5_GROUPS, S5_GROUP) * u
    y = jax.nn.gelu(y.reshape(Bsz, L, S5_WIDTH)).astype(xa.dtype)
    z = y @ w_glu + b_glu
    out = z[..., :S5_WIDTH] * jax.nn.sigmoid(z[..., S5_WIDTH:])
    h_last = h[:, -1]
    return out, jnp.real(h_last).astype(h0_re.dtype), jnp.imag(h_last).astype(h0_im.dtype)


def gmlp_mixer(u, v, g_v, w_s, b_s):
    Bsz, L, _ = u.shape
    u = jax.nn.gelu(u)
    vn = rmsnorm(jax.nn.gelu(v), g_v)
    pad = (-L) % GM_CHUNK
    vp = jnp.pad(vn, ((0, 0), (0, pad), (0, 0)))
    nc = (L + pad) // GM_CHUNK
    vc = vp.reshape(Bsz, nc, GM_CHUNK, GM_HEADS, GM_HEAD_DIM)
    tri = jnp.tril(jnp.ones((GM_CHUNK, GM_CHUNK), dtype=bool))
    ws = jnp.where(tri[None], w_s, 0.0)
    mixed = jnp.einsum('hts,bnshc->bnthc', ws, vc) + jnp.transpose(b_s)[None, None, :, :, None]
    mixed = mixed.reshape(Bsz, nc * GM_CHUNK, GM_WIDTH)[:, :L]
    return u * mixed, vn


def diff_attention(q, k_all, v_all, q_pos, lam, lam_init, g_sub):
    k_pos = jnp.arange(k_all.shape[1], dtype=jnp.int32)
    scale = DA_DH ** -0.5

    def core(qb, qp):
        s = jnp.einsum('bqhjd,bkhjd->bhjqk', qb, k_all).astype(jnp.float32) * scale
        s = jnp.where(chunk_mask(qp, k_pos), s, NEG)
        pr = jax.nn.softmax(s, axis=-1)
        pr = pr[:, :, 0] - lam * pr[:, :, 1]
        return jnp.einsum('bhqk,bkhe->bqhe', pr.astype(v_all.dtype), v_all)

    o = attend_queries(core, (q,), q_pos)
    o = rmsnorm(o, g_sub) * (1.0 - lam_init)
    return o.reshape(o.shape[0], o.shape[1], DA_VW)


def mla_attention(q_nope, q_pe, lat_all, kpe_all, q_pos, w_ukv):
    Bsz, K, _ = lat_all.shape
    kv = (lat_all @ w_ukv).reshape(Bsz, K, MLA_HEADS, MLA_NOPE + MLA_DV)
    k_nope, v = kv[..., :MLA_NOPE], kv[..., MLA_NOPE:]
    k_pos = jnp.arange(K, dtype=jnp.int32)
    scale = (MLA_NOPE + MLA_ROPE) ** -0.5

    def core(qn, qr, qp):
        s = (jnp.einsum('bqhd,bkhd->bhqk', qn, k_nope)
             + jnp.einsum('bqhr,bkr->bhqk', qr, kpe_all)).astype(jnp.float32) * scale
        s = jnp.where(chunk_mask(qp, k_pos), s, NEG)
        pr = jax.nn.softmax(s, axis=-1)
        return jnp.einsum('bhqk,bkhe->bqhe', pr.astype(v.dtype), v)

    o = attend_queries(core, (q_nope, q_pe), q_pos)
    return o.reshape(o.shape[0], o.shape[1], MLA_HEADS * MLA_DV)


def peer_ffn(x, w_q, k1, k2, u_tab, v_tab):
    Bsz, L, D = x.shape
    T = Bsz * L
    pad = (-T) % PEER_BLOCK
    xb = jnp.pad(x.reshape(T, D), ((0, pad), (0, 0))).reshape(-1, PEER_BLOCK, D)
    half = PEER_DKEY // 2

    def block(xblk):
        q = (xblk @ w_q).reshape(PEER_BLOCK, PEER_HEADS, 2, half)
        s1 = jnp.einsum('thd,hnd->thn', q[:, :, 0], k1).astype(jnp.float32)
        s2 = jnp.einsum('thd,hnd->thn', q[:, :, 1], k2).astype(jnp.float32)
        v1, i1 = lax.top_k(s1, PEER_TOPK)
        v2, i2 = lax.top_k(s2, PEER_TOPK)
        cand = (v1[..., :, None] + v2[..., None, :]).reshape(PEER_BLOCK, PEER_HEADS, PEER_TOPK * PEER_TOPK)
        cidx = (i1[..., :, None] * PEER_NKEYS + i2[..., None, :]).reshape(PEER_BLOCK, PEER_HEADS, PEER_TOPK * PEER_TOPK)
        sc, j = lax.top_k(cand, PEER_TOPK)
        idx = jnp.take_along_axis(cidx, j, axis=-1)
        g = jax.nn.softmax(sc, axis=-1)
        ue = jnp.take(u_tab, idx, axis=0)
        act = jax.nn.gelu(jnp.einsum('td,thkd->thk', xblk, ue).astype(jnp.float32))
        coef = (g * act).astype(xblk.dtype)
        ve = jnp.take(v_tab, idx, axis=0)
        return jnp.einsum('thk,thkd->td', coef, ve)

    out = lax.map(block, xb)
    return out.reshape(-1, D)[:T].reshape(Bsz, L, D)


def trunk(x, c, pos0, past, p):
    Bsz, L, _ = x.shape
    q_pos = pos0 + jnp.arange(L, dtype=jnp.int32)
    cond = jax.nn.silu(c)
    s5_re, s5_im, gm_v, da_k, da_v, mla_lat, mla_kpe = [], [], [], [], [], [], []
    for l in range(DEPTH):
        mod = (cond @ p['w_ada'][l] + p['b_ada'][l])[:, None, :]
        sh1, sc1, gt1, sh2, sc2, gt2 = jnp.split(mod, 6, axis=-1)
        h = rmsnorm(x, p['g_mix'][l]) * (1.0 + sc1) + sh1
        if l % 2 == 0:
            e = l // 2
            xa, u, v = jnp.split(h @ p['w_in_e'][e], [S5_WIDTH, S5_WIDTH + GM_WIDTH], axis=-1)
            if past is None:
                h0_re = jnp.zeros((Bsz, S5_GROUPS, S5_STATE), jnp.float32)
                h0_im = jnp.zeros((Bsz, S5_GROUPS, S5_STATE), jnp.float32)
            else:
                h0_re = past['s5_re'][e]
                h0_im = past['s5_im'][e]
            ya, hr, hi = s5_mixer(xa, h0_re, h0_im, p['s5_lam_re'][e], p['s5_lam_im'][e], p['s5_log_dt'][e],
                                  p['s5_b_re'][e], p['s5_b_im'][e], p['s5_c_re'][e], p['s5_c_im'][e],
                                  p['s5_d'][e], p['w_glu'][e], p['b_glu'][e])
            yb, vn = gmlp_mixer(u, v, p['gm_g_v'][e], p['gm_w_s'][e], p['gm_b_s'][e])
            mix = jnp.concatenate([ya, yb], axis=-1) @ p['w_out_e'][e]
            s5_re.append(hr)
            s5_im.append(hi)
            gm_v.append(vn)
        else:
            o = l // 2
            cuts = [DA_QK, 2 * DA_QK, 2 * DA_QK + DA_VW, 2 * DA_QK + DA_VW + MLA_Q_RANK,
                    2 * DA_QK + DA_VW + MLA_Q_RANK + MLA_KV_RANK]
            q, k, v, cq, ckv, kpe = jnp.split(h @ p['w_in_o'][o], cuts, axis=-1)
            q = rope(q.reshape(Bsz, L, DA_HEADS, 2, DA_DH), q_pos, DA_ROPE)
            k = rope(k.reshape(Bsz, L, DA_HEADS, 2, DA_DH), q_pos, DA_ROPE)
            v = v.reshape(Bsz, L, DA_HEADS, DA_DV)
            k_past = None if past is None else past['da_k'][o].reshape(Bsz, -1, DA_HEADS, 2, DA_DH)
            v_past = None if past is None else past['da_v'][o]
            lam_init = 0.8 - 0.6 * math.exp(-0.3 * l)
            f32 = jnp.float32
            lam = (jnp.exp(jnp.sum(p['da_lq1'][o].astype(f32) * p['da_lk1'][o].astype(f32)))
                   - jnp.exp(jnp.sum(p['da_lq2'][o].astype(f32) * p['da_lk2'][o].astype(f32))) + lam_init)
            yc = diff_attention(q, cat_rows(k_past, k), cat_rows(v_past, v), q_pos, lam, lam_init, p['da_g_sub'][o])
            qm = (rmsnorm(cq, p['mla_g_q'][o]) @ p['mla_w_uq'][o]).reshape(Bsz, L, MLA_HEADS, MLA_NOPE + MLA_ROPE)
            q_nope = qm[..., :MLA_NOPE]
            q_pe = rope(qm[..., MLA_NOPE:], q_pos, MLA_ROPE)
            lat = rmsnorm(ckv, p['mla_g_kv'][o])
            kpe = rope(kpe, q_pos, MLA_ROPE)
            lat_past = None if past is None else past['mla_lat'][o]
            kpe_past = None if past is None else past['mla_kpe'][o]
            yd = mla_attention(q_nope, q_pe, cat_rows(lat_past, lat), cat_rows(kpe_past, kpe), q_pos, p['mla_w_ukv'][o])
            mix = jnp.concatenate([yc, yd], axis=-1) @ p['w_out_o'][o]
            da_k.append(k.reshape(Bsz, L, DA_HEADS, 2 * DA_DH))
            da_v.append(v)
            mla_lat.append(lat)
            mla_kpe.append(kpe)
        x = x + gt1 * mix
        h = rmsnorm(x, p['g_ffn'][l]) * (1.0 + sc2) + sh2
        x = x + gt2 * peer_ffn(h, p['peer_w_q'][l], p['peer_k1'][l], p['peer_k2'][l], p['peer_u'][l], p['peer_v'][l])
    y = rmsnorm(x, p['g_final'])
    st = lambda xs: jnp.stack(xs, axis=0)
    return y, st(s5_re), st(s5_im), st(gm_v), st(da_k), st(da_v), st(mla_lat), st(mla_kpe)


def setup_inputs(seed: int = 0) -> dict:
    key = jax.random.key(seed)
    ks = iter(jax.random.split(key, 64))
    f32 = jnp.float32
    nrm = lambda shape, s=1.0: s * jax.random.normal(next(ks), shape, f32)
    gain = lambda shape: 1.0 + 0.01 * jax.random.normal(next(ks), shape, f32)
    D = D_MODEL
    return {
        'x_prompt': nrm((BATCH, SEQ, D)),
        'x_sample': nrm((DEC_BATCH, DEC_SEQ, D)),
        'c_prompt': nrm((BATCH, D)),
        'c_sample': nrm((DEC_BATCH, D)),
        'state_s5_re': nrm((N_EVEN, DEC_BATCH, S5_GROUPS, S5_STATE), 0.5),
        'state_s5_im': nrm((N_EVEN, DEC_BATCH, S5_GROUPS, S5_STATE), 0.5),
        'cache_da_k': nrm((N_ODD, DEC_BATCH, PAST_LEN, DA_HEADS, 2 * DA_DH)),
        'cache_da_v': nrm((N_ODD, DEC_BATCH, PAST_LEN, DA_HEADS, DA_DV)),
        'cache_mla_latent': nrm((N_ODD, DEC_BATCH, PAST_LEN, MLA_KV_RANK)),
        'cache_mla_kpe': nrm((N_ODD, DEC_BATCH, PAST_LEN, MLA_ROPE)),
        'w_ada': nrm((DEPTH, D, 6 * D), 0.5 * D ** -0.5),
        'b_ada': nrm((DEPTH, 6 * D), 0.01),
        'g_mix': gain((DEPTH, D)),
        'g_ffn': gain((DEPTH, D)),
        'g_final': gain((D,)),
        'w_in_e': nrm((N_EVEN, D, EVEN_IN), D ** -0.5),
        'w_out_e': nrm((N_EVEN, EVEN_MIX, D), EVEN_MIX ** -0.5),
        's5_lam_re': -0.5 + nrm((N_EVEN, S5_GROUPS, S5_STATE), 0.01),
        's5_lam_im': jnp.pi * jnp.arange(S5_STATE, dtype=f32) + nrm((N_EVEN, S5_GROUPS, S5_STATE), 0.01),
        's5_log_dt': jax.random.uniform(next(ks), (N_EVEN, S5_GROUPS), f32, math.log(1e-3), math.log(1e-1)),
        's5_b_re': nrm((N_EVEN, S5_GROUPS, S5_STATE, S5_GROUP), (2 * S5_GROUP) ** -0.5),
        's5_b_im': nrm((N_EVEN, S5_GROUPS, S5_STATE, S5_GROUP), (2 * S5_GROUP) ** -0.5),
        's5_c_re': nrm((N_EVEN, S5_GROUPS, S5_GROUP, S5_STATE), S5_STATE ** -0.5),
        's5_c_im': nrm((N_EVEN, S5_GROUPS, S5_GROUP, S5_STATE), S5_STATE ** -0.5),
        's5_d': nrm((N_EVEN, S5_WIDTH)),
        'w_glu': nrm((N_EVEN, S5_WIDTH, 2 * S5_WIDTH), S5_WIDTH ** -0.5),
        'b_glu': nrm((N_EVEN, 2 * S5_WIDTH), 0.01),
        'gm_g_v': gain((N_EVEN, GM_WIDTH)),
        'gm_w_s': nrm((N_EVEN, GM_HEADS, GM_CHUNK, GM_CHUNK), GM_CHUNK ** -0.5),
        'gm_b_s': 1.0 + nrm((N_EVEN, GM_HEADS, GM_CHUNK), 0.1),
        'w_in_o': nrm((N_ODD, D, ODD_IN), D ** -0.5),
        'w_out_o': nrm((N_ODD, ODD_MIX, D), ODD_MIX ** -0.5),
        'da_lq1': nrm((N_ODD, DA_DH), 0.1),
        'da_lk1': nrm((N_ODD, DA_DH), 0.1),
        'da_lq2': nrm((N_ODD, DA_DH), 0.1),
        'da_lk2': nrm((N_ODD, DA_DH), 0.1),
        'da_g_sub': gain((N_ODD, DA_DV)),
        'mla_g_q': gain((N_ODD, MLA_Q_RANK)),
        'mla_w_uq': nrm((N_ODD, MLA_Q_RANK, MLA_HEADS * (MLA_NOPE + MLA_ROPE)), MLA_Q_RANK ** -0.5),
        'mla_g_kv': gain((N_ODD, MLA_KV_RANK)),
        'mla_w_ukv': nrm((N_ODD, MLA_KV_RANK, MLA_HEADS * (MLA_NOPE + MLA_DV)), MLA_KV_RANK ** -0.5),
        'peer_w_q': nrm((DEPTH, D, PEER_HEADS * PEER_DKEY), D ** -0.5),
        'peer_k1': nrm((DEPTH, PEER_HEADS, PEER_NKEYS, PEER_DKEY // 2), (PEER_DKEY // 2) ** -0.5),
        'peer_k2': nrm((DEPTH, PEER_HEADS, PEER_NKEYS, PEER_DKEY // 2), (PEER_DKEY // 2) ** -0.5),
        'peer_u': nrm((DEPTH, PEER_EXPERTS, D), D ** -0.5),
        'peer_v': nrm((DEPTH, PEER_EXPERTS, D), 0.5),
    }


def reference(x_prompt, x_sample, c_prompt, c_sample, state_s5_re, state_s5_im, cache_da_k, cache_da_v,
              cache_mla_latent, cache_mla_kpe, w_ada, b_ada, g_mix, g_ffn, g_final, w_in_e, w_out_e,
              s5_lam_re, s5_lam_im, s5_log_dt, s5_b_re, s5_b_im, s5_c_re, s5_c_im, s5_d, w_glu, b_glu,
              gm_g_v, gm_w_s, gm_b_s, w_in_o, w_out_o, da_lq1, da_lk1, da_lq2, da_lk2, da_g_sub,
              mla_g_q, mla_w_uq, mla_g_kv, mla_w_ukv, peer_w_q, peer_k1, peer_k2, peer_u, peer_v):
    p = dict(w_ada=w_ada, b_ada=b_ada, g_mix=g_mix, g_ffn=g_ffn, g_final=g_final, w_in_e=w_in_e,
             w_out_e=w_out_e, s5_lam_re=s5_lam_re, s5_lam_im=s5_lam_im, s5_log_dt=s5_log_dt,
             s5_b_re=s5_b_re, s5_b_im=s5_b_im, s5_c_re=s5_c_re, s5_c_im=s5_c_im, s5_d=s5_d,
             w_glu=w_glu, b_glu=b_glu, gm_g_v=gm_g_v, gm_w_s=gm_w_s, gm_b_s=gm_b_s, w_in_o=w_in_o,
             w_out_o=w_out_o, da_lq1=da_lq1, da_lk1=da_lk1, da_lq2=da_lq2, da_lk2=da_lk2,
             da_g_sub=da_g_sub, mla_g_q=mla_g_q, mla_w_uq=mla_w_uq, mla_g_kv=mla_g_kv,
             mla_w_ukv=mla_w_ukv, peer_w_q=peer_w_q, peer_k1=peer_k1, peer_k2=peer_k2,
             peer_u=peer_u, peer_v=peer_v)
    past = dict(s5_re=state_s5_re, s5_im=state_s5_im, da_k=cache_da_k, da_v=cache_da_v,
                mla_lat=cache_mla_latent, mla_kpe=cache_mla_kpe)
    (y_prompt, s5_re_p, s5_im_p, _gm_v_p, da_k_p, da_v_p, lat_p, kpe_p) = trunk(x_prompt, c_prompt, 0, None, p)
    (y_sample, s5_re_s, s5_im_s, gm_v_s, da_k_s, da_v_s, lat_s, kpe_s) = trunk(x_sample, c_sample, PAST_LEN, past, p)
    return (y_prompt, y_sample, s5_re_p, s5_im_p, s5_re_s, s5_im_s, gm_v_s,
            da_k_p, da_v_p, da_k_s, da_v_s, lat_p, kpe_p, lat_s, kpe_s)
```

```python
import functools
import math

import jax
import jax.numpy as jnp
from jax import lax
from jax.experimental import pallas as pl
from jax.experimental.pallas import tpu as pltpu
from jax.experimental.pallas import tpu_sc as plsc

F32 = jnp.float32
BF16 = jnp.bfloat16

D_MODEL = 1024
DEPTH = 2
CHUNK = 64
ROPE_THETA = 500000.0
EPS = 1e-6
NEG = -1e30

S5_WIDTH = 512
S5_GROUP = 16
S5_GROUPS = 32
S5_STATE = 64
GM_WIDTH = 512
GM_HEADS = 4
GM_CHUNK = 128
DA_HEADS = 4
DA_DH = 64
DA_DV = 128
DA_ROPE = 16
MLA_HEADS = 4
MLA_Q_RANK = 256
MLA_KV_RANK = 128
MLA_NOPE = 64
MLA_ROPE = 32
MLA_DV = 128
PEER_HEADS = 8
PEER_NKEYS = 128
PEER_TOPK = 16
PEER_HK = PEER_HEADS * PEER_TOPK

LANES = 128
VMEM_LIMIT = 48 << 20
PEER_TT = 256
PEER_HKC = 16
SC_WINDOW = 64

_HI = lax.Precision.HIGHEST


def _cparams(*sem):
    return pltpu.CompilerParams(dimension_semantics=sem, vmem_limit_bytes=VMEM_LIMIT)


def _full(shape):
    n = len(shape)
    return pl.BlockSpec(shape, lambda *_: (0,) * n)


def _ada_kernel(c_ref, w_ref, b_ref, o_ref):
    c = c_ref[...]
    s = c * jax.nn.sigmoid(c)
    o_ref[...] = jnp.dot(s, w_ref[...], precision=_HI, preferred_element_type=F32) + b_ref[...]


def _ada(c_all, w_ada, b_ada):
    r = c_all.shape[0]
    tn = 1536
    return pl.pallas_call(
        _ada_kernel,
        grid=(DEPTH, 6 * D_MODEL // tn),
        in_specs=[_full((r, D_MODEL)),
                  pl.BlockSpec((None, D_MODEL, tn), lambda l, j: (l, 0, j)),
                  pl.BlockSpec((None, 1, tn), lambda l, j: (l, 0, j))],
        out_specs=pl.BlockSpec((None, r, tn), lambda l, j: (l, 0, j)),
        out_shape=jax.ShapeDtypeStruct((DEPTH, r, 6 * D_MODEL), F32),
        compiler_params=_cparams("parallel", "parallel"),
        name="ada",
    )(c_all, w_ada, b_ada.reshape(DEPTH, 1, 6 * D_MODEL))


def _nmm_kernel(x_ref, g_ref, sc_ref, sh_ref, w_ref, o_ref, *h_ref):
    x = x_ref[...]
    ms = jnp.mean(x * x, axis=-1, keepdims=True)
    h = x * lax.rsqrt(ms + EPS) * g_ref[...] * (1.0 + sc_ref[...]) + sh_ref[...]
    o_ref[...] = jnp.dot(h.astype(BF16), w_ref[...], preferred_element_type=F32)
    if h_ref:
        h_ref[0][...] = h


def _norm_mod_matmul(x, g, sc, sh, w, tl, want_h=False):
    b, l, d = x.shape
    n = w.shape[1]
    row = pl.BlockSpec((None, 1, d), lambda bi, i: (bi, 0, 0))
    out_shape = [jax.ShapeDtypeStruct((b, l, n), F32)]
    out_specs = [pl.BlockSpec((None, tl, n), lambda bi, i: (bi, i, 0))]
    if want_h:
        out_shape.append(jax.ShapeDtypeStruct((b, l, d), F32))
        out_specs.append(pl.BlockSpec((None, tl, d), lambda bi, i: (bi, i, 0)))
    res = pl.pallas_call(
        _nmm_kernel,
        grid=(b, l // tl),
        in_specs=[pl.BlockSpec((None, tl, d), lambda bi, i: (bi, i, 0)), _full((1, d)), row, row, _full((d, n))],
        out_specs=out_specs, out_shape=out_shape,
        compiler_params=_cparams("parallel", "parallel"),
        name="norm_mod_matmul",
    )(x, g.reshape(1, d), sc, sh, w)
    return res if want_h else res[0]


def _s5_kernel(xa_ref, h0r_ref, h0i_ref, bblk_ref, ar_ref, ai_ref, cr_ref, ci_ref, d_ref, wglu_ref, bglu_ref,
               ya_ref, hr_out_ref, hi_out_ref, bur, bui, st_r, st_i, *, tl):
    nj = S5_WIDTH // LANES
    half = 8 * S5_STATE

    @pl.when(pl.program_id(1) == 0)
    def _():
        st_r[...] = h0r_ref[...]
        st_i[...] = h0i_ref[...]

    xa = xa_ref[...]
    for j in range(nj):
        bu = jnp.dot(xa[:, j * LANES:(j + 1) * LANES], bblk_ref[j], precision=_HI, preferred_element_type=F32)
        bur[j] = bu[:, :half]
        bui[j] = bu[:, half:]

    for j0 in range(0, nj, 2):
        js = (j0, j0 + 1)
        a_r = [ar_ref[j] for j in js]
        a_i = [ai_ref[j] for j in js]

        def body(k, carry, js=js, a_r=a_r, a_i=a_i):
            carry = list(carry)
            base = pl.multiple_of(k * 8, 8)
            for s in range(8):
                t = base + s
                for q, j in enumerate(js):
                    hr, hi = carry[2 * q], carry[2 * q + 1]
                    nhr = a_r[q] * hr - a_i[q] * hi + bur[j, pl.ds(t, 1), :]
                    nhi = a_r[q] * hi + a_i[q] * hr + bui[j, pl.ds(t, 1), :]
                    bur[j, pl.ds(t, 1), :] = nhr
                    bui[j, pl.ds(t, 1), :] = nhi
                    carry[2 * q], carry[2 * q + 1] = nhr, nhi
            return tuple(carry)

        init = (st_r[js[0]], st_i[js[0]], st_r[js[1]], st_i[js[1]])
        fin = lax.fori_loop(0, tl // 8, body, init)
        st_r[js[0]], st_i[js[0]], st_r[js[1]], st_i[js[1]] = fin

    hr_out_ref[...] = st_r[...]
    hi_out_ref[...] = st_i[...]

    ys = []
    for j in range(nj):
        y = (jnp.dot(bur[j].astype(BF16), cr_ref[j], preferred_element_type=F32)
             + jnp.dot(bui[j].astype(BF16), ci_ref[j], preferred_element_type=F32))
        ys.append(y + d_ref[:, j * LANES:(j + 1) * LANES] * xa[:, j * LANES:(j + 1) * LANES])
    y = jax.nn.gelu(jnp.concatenate(ys, axis=1))
    z = jnp.dot(y.astype(BF16), wglu_ref[...], preferred_element_type=F32) + bglu_ref[...]
    ya_ref[...] = z[:, :S5_WIDTH] * jax.nn.sigmoid(z[:, S5_WIDTH:])


def _s5_params(lam_re, lam_im, log_dt, b_re, b_im, c_re, c_im):
    lam = lax.complex(lam_re.astype(F32), lam_im.astype(F32))
    dt = jnp.exp(log_dt.astype(F32))[:, None]
    a_bar = jnp.exp(lam * dt)
    b_bar = ((a_bar - 1.0) / lam)[..., None] * lax.complex(b_re.astype(F32), b_im.astype(F32))
    nj = S5_GROUPS // 8
    eye = jnp.eye(8, dtype=F32)

    def blk_b(m):
        m = m.reshape(nj, 8, S5_STATE, S5_GROUP)
        return jnp.einsum('jgpc,gh->jgchp', m, eye).reshape(nj, 8 * S5_GROUP, 8 * S5_STATE)

    def blk_c(m):
        m = m.reshape(nj, 8, S5_GROUP, S5_STATE)
        return jnp.einsum('jgcp,gh->jgphc', m, eye).reshape(nj, 8 * S5_STATE, 8 * S5_GROUP)

    bblk = jnp.concatenate([blk_b(jnp.real(b_bar)), blk_b(jnp.imag(b_bar))], axis=-1)
    cr = blk_c(c_re.astype(F32)).astype(BF16)
    ci = (-blk_c(c_im.astype(F32))).astype(BF16)
    a_r = jnp.real(a_bar).reshape(nj, 1, 8 * S5_STATE)
    a_i = jnp.imag(a_bar).reshape(nj, 1, 8 * S5_STATE)
    return bblk, a_r, a_i, cr, ci


def _s5(y_in, h0_re, h0_im, sp, d_skip, w_glu, b_glu, tl):
    b, l, _ = y_in.shape
    nj = S5_GROUPS // 8
    half = 8 * S5_STATE
    bblk, a_r, a_i, cr, ci = sp
    st = pl.BlockSpec((None, nj, 1, half), lambda bi, i: (bi, 0, 0, 0))
    ya, hr, hi = pl.pallas_call(
        functools.partial(_s5_kernel, tl=tl),
        grid=(b, l // tl),
        in_specs=[pl.BlockSpec((None, tl, S5_WIDTH), lambda bi, i: (bi, i, 0)), st, st,
                  _full(bblk.shape), _full(a_r.shape), _full(a_i.shape), _full(cr.shape), _full(ci.shape),
                  _full((1, S5_WIDTH)), _full((S5_WIDTH, 2 * S5_WIDTH)), _full((1, 2 * S5_WIDTH))],
        out_specs=[pl.BlockSpec((None, tl, S5_WIDTH), lambda bi, i: (bi, i, 0)), st, st],
        out_shape=[jax.ShapeDtypeStruct((b, l, S5_WIDTH), F32),
                   jax.ShapeDtypeStruct((b, nj, 1, half), F32), jax.ShapeDtypeStruct((b, nj, 1, half), F32)],
        scratch_shapes=[pltpu.VMEM((nj, tl, half), F32), pltpu.VMEM((nj, tl, half), F32),
                        pltpu.VMEM((nj, 1, half), F32), pltpu.VMEM((nj, 1, half), F32)],
        compiler_params=_cparams("parallel", "arbitrary"),
        name="s5",
    )(y_in, h0_re.reshape(b, nj, 1, half), h0_im.reshape(b, nj, 1, half), bblk, a_r, a_i, cr, ci,
      d_skip.reshape(1, S5_WIDTH), w_glu.astype(BF16), b_glu.reshape(1, 2 * S5_WIDTH))
    return ya, hr.reshape(b, S5_GROUPS, S5_STATE), hi.reshape(b, S5_GROUPS, S5_STATE)


def _gmlp_kernel(u_ref, v_ref, gv_ref, ws_ref, bias_ref, yb_ref, *vn_ref, tl, lc):
    hd = GM_WIDTH // GM_HEADS
    gv = jax.nn.gelu(v_ref[...])
    vn = gv * lax.rsqrt(jnp.mean(gv * gv, axis=-1, keepdims=True) + EPS) * gv_ref[...]
    if vn_ref:
        vn_ref[0][...] = vn
    gu = jax.nn.gelu(u_ref[...])
    vb = vn.astype(BF16)
    for c in range(tl // lc):
        rows = slice(c * lc, (c + 1) * lc)
        for h in range(GM_HEADS):
            cols = slice(h * hd, (h + 1) * hd)
            mixed = jnp.dot(ws_ref[h], vb[rows, cols], preferred_element_type=F32) + bias_ref[:, cols]
            yb_ref[rows, cols] = gu[rows, cols] * mixed


def _gmlp(y_in, g_v, w_s, b_s, tl, want_vn):
    b, l, _ = y_in.shape
    lc = min(GM_CHUNK, l)
    hd = GM_WIDTH // GM_HEADS
    tri = jnp.tril(jnp.ones((GM_CHUNK, GM_CHUNK), dtype=bool))
    ws = jnp.where(tri[None], w_s, 0.0)[:, :lc, :lc].astype(BF16)
    bias = jnp.repeat(jnp.transpose(b_s)[:lc], hd, axis=1)
    out_shape = [jax.ShapeDtypeStruct((b, l, GM_WIDTH), F32)]
    out_specs = [pl.BlockSpec((None, tl, GM_WIDTH), lambda bi, i: (bi, i, 0))]
    if want_vn:
        out_shape.append(jax.ShapeDtypeStruct((b, l, GM_WIDTH), F32))
        out_specs.append(pl.BlockSpec((None, tl, GM_WIDTH), lambda bi, i: (bi, i, 0)))
    res = pl.pallas_call(
        functools.partial(_gmlp_kernel, tl=tl, lc=lc),
        grid=(b, l // tl),
        in_specs=[pl.BlockSpec((None, tl, GM_WIDTH), lambda bi, i: (bi, i, 1)),
                  pl.BlockSpec((None, tl, GM_WIDTH), lambda bi, i: (bi, i, 2)),
                  _full((1, GM_WIDTH)), _full(ws.shape), _full(bias.shape)],
        out_specs=out_specs, out_shape=out_shape,
        compiler_params=_cparams("parallel", "parallel"),
        name="gmlp",
    )(y_in, y_in, g_v.reshape(1, GM_WIDTH), ws, bias)
    return (res[0], res[1]) if want_vn else (res[0], None)


def _proj_res_kernel(x_ref, gt_ref, a_ref, b_ref, wa_ref, wb_ref, o_ref):
    mix = (jnp.dot(a_ref[...].astype(BF16), wa_ref[...], preferred_element_type=F32)
           + jnp.dot(b_ref[...].astype(BF16), wb_ref[...], preferred_element_type=F32))
    o_ref[...] = x_ref[...] + gt_ref[...] * mix


def _proj_residual(x, gt, a, bm, w, tl):
    b, l, d = x.shape
    ka = a.shape[-1]
    wa, wb = w[:ka].astype(BF16), w[ka:].astype(BF16)
    blk = lambda n: pl.BlockSpec((None, tl, n), lambda bi, i: (bi, i, 0))
    return pl.pallas_call(
        _proj_res_kernel,
        grid=(b, l // tl),
        in_specs=[blk(d), pl.BlockSpec((None, 1, d), lambda bi, i: (bi, 0, 0)), blk(ka), blk(bm.shape[-1]),
                  _full(wa.shape), _full(wb.shape)],
        out_specs=blk(d), out_shape=jax.ShapeDtypeStruct((b, l, d), F32),
        compiler_params=_cparams("parallel", "parallel"),
        name="proj_residual",
    )(x, gt, a, bm, wa, wb)


def _rope_tables(pos0, l):
    pos = (pos0 + jnp.arange(l, dtype=jnp.int32)).astype(F32)[:, None]
    lane = jnp.arange(LANES)

    def table(period, start, rot):
        half = rot // 2
        r = lane % period - start
        inside = (r >= 0) & (r < rot)
        k = jnp.where(inside, r % half, 0)
        inv = ROPE_THETA ** (-k.astype(F32) * 2.0 / rot)
        ang = pos * inv[None, :]
        cos = jnp.where(inside[None], jnp.cos(ang), 1.0)
        sgn = jnp.where(r < half, -1.0, 1.0)
        sin = jnp.where(inside[None], jnp.sin(ang) * sgn[None], 0.0)
        return cos, sin

    return table(DA_DH, 0, DA_ROPE), table(LANES, MLA_NOPE, MLA_ROPE), table(LANES, 0, MLA_ROPE)


def _rotate(x, cos, sin, period, start, rot):
    half = rot // 2
    n = x.shape[-1]
    reps = n // LANES
    lane = (lax.broadcasted_iota(jnp.int32, x.shape, 1) & (period - 1)) - start
    up = pltpu.roll(x, n - half, 1)
    dn = pltpu.roll(x, half, 1)
    partner = jnp.where(lane < half, up, dn)
    if reps > 1:
        cos = jnp.concatenate([cos] * reps, axis=1)
        sin = jnp.concatenate([sin] * reps, axis=1)
    return x * cos + partner * sin


def _odd_prep_kernel(q_ref, k_ref, cq_ref, ckv_ref, kpe_ref, cd_ref, sd_ref, cm_ref, sm_ref, ck_ref, sk_ref,
                     gq_ref, wuq_ref, gkv_ref, qd_ref, kd_ref, qm_ref, lat_ref, kpe_out_ref):
    cd, sd = cd_ref[...], sd_ref[...]
    qd_ref[...] = _rotate(q_ref[...], cd, sd, DA_DH, 0, DA_ROPE).astype(BF16)
    kd_ref[...] = _rotate(k_ref[...], cd, sd, DA_DH, 0, DA_ROPE)
    cq = cq_ref[...]
    cqn = cq * lax.rsqrt(jnp.mean(cq * cq, axis=-1, keepdims=True) + EPS) * gq_ref[...]
    qm = jnp.dot(cqn.astype(BF16), wuq_ref[...], preferred_element_type=F32)
    qm_ref[...] = _rotate(qm, cm_ref[...], sm_ref[...], LANES, MLA_NOPE, MLA_ROPE).astype(BF16)
    ckv = ckv_ref[...]
    lat_ref[...] = ckv * lax.rsqrt(jnp.mean(ckv * ckv, axis=-1, keepdims=True) + EPS) * gkv_ref[...]
    kpe = _rotate(kpe_ref[...], ck_ref[...], sk_ref[...], LANES, 0, MLA_ROPE)
    kpe_out_ref[...] = kpe[:, :MLA_ROPE]


def _odd_prep(y_in, pos0, g_q, w_uq, g_kv, tl):
    b, l, _ = y_in.shape
    (cd, sd), (cm, sm), (ck, sk) = _rope_tables(pos0, l)
    per = MLA_NOPE + MLA_ROPE
    wuq = jnp.pad(w_uq.reshape(MLA_Q_RANK, MLA_HEADS, per), ((0, 0), (0, 0), (0, LANES - per)))
    wuq = wuq.reshape(MLA_Q_RANK, MLA_HEADS * LANES).astype(BF16)
    col = lambda w, j: pl.BlockSpec((None, tl, w), lambda bi, i: (bi, i, j))
    tab = pl.BlockSpec((tl, LANES), lambda bi, i: (i, 0))
    out = lambda w: pl.BlockSpec((None, tl, w), lambda bi, i: (bi, i, 0))
    return pl.pallas_call(
        _odd_prep_kernel,
        grid=(b, l // tl),
        in_specs=[col(512, 0), col(512, 1), col(256, 6), col(128, 14), col(128, 15),
                  tab, tab, tab, tab, tab, tab,
                  _full((1, MLA_Q_RANK)), _full(wuq.shape), _full((1, MLA_KV_RANK))],
        out_specs=[out(512), out(512), out(512), out(MLA_KV_RANK), out(MLA_ROPE)],
        out_shape=[jax.ShapeDtypeStruct((b, l, 512), BF16), jax.ShapeDtypeStruct((b, l, 512), F32),
                   jax.ShapeDtypeStruct((b, l, 512), BF16), jax.ShapeDtypeStruct((b, l, MLA_KV_RANK), F32),
                   jax.ShapeDtypeStruct((b, l, MLA_ROPE), F32)],
        compiler_params=_cparams("parallel", "parallel"),
        name="odd_prep",
    )(y_in, y_in, y_in, y_in, y_in, cd, sd, cm, sm, ck, sk,
      g_q.reshape(1, MLA_Q_RANK), wuq, g_kv.reshape(1, MLA_KV_RANK))


def _mla_kv_kernel(lat_ref, kpe_ref, wk_ref, wv_ref, place_ref, k_ref, v_ref):
    lat = lat_ref[...].astype(BF16)
    k = (jnp.dot(lat, wk_ref[...], preferred_element_type=F32)
         + jnp.dot(kpe_ref[...].astype(BF16), place_ref[...], preferred_element_type=F32))
    k_ref[...] = k.astype(BF16)
    v_ref[...] = jnp.dot(lat, wv_ref[...], preferred_element_type=F32).astype(BF16)


def _mla_kv(lat_all, kpe_all, w_ukv, tk):
    b, k, _ = lat_all.shape
    per = MLA_NOPE + MLA_DV
    w3 = w_ukv.reshape(MLA_KV_RANK, MLA_HEADS, per)
    wk = jnp.pad(w3[:, :, :MLA_NOPE], ((0, 0), (0, 0), (0, LANES - MLA_NOPE))).reshape(MLA_KV_RANK, MLA_HEADS * LANES)
    wv = w3[:, :, MLA_NOPE:].reshape(MLA_KV_RANK, MLA_HEADS * MLA_DV)
    place = jnp.pad(jnp.eye(MLA_ROPE, dtype=F32), ((0, 0), (MLA_NOPE, LANES - MLA_NOPE - MLA_ROPE)))
    place = jnp.tile(place, (1, MLA_HEADS))
    blk = lambda w: pl.BlockSpec((None, tk, w), lambda bi, i: (bi, i, 0))
    return pl.pallas_call(
        _mla_kv_kernel,
        grid=(b, k // tk),
        in_specs=[blk(MLA_KV_RANK), blk(MLA_ROPE), _full(wk.shape), _full(wv.shape), _full(place.shape)],
        out_specs=[blk(512), blk(512)],
        out_shape=[jax.ShapeDtypeStruct((b, k, 512), BF16), jax.ShapeDtypeStruct((b, k, 512), BF16)],
        compiler_params=_cparams("parallel", "parallel"),
        name="mla_kv",
    )(lat_all, kpe_all, wk.astype(BF16), wv.astype(BF16), place.astype(BF16))


def _visible_limit(pos_last):
    shift = CHUNK.bit_length() - 1
    return ((pos_last >> shift) + 1) << shift


def _flash_kernel(lam_ref, q_ref, k_ref, v_ref, g_ref, o_ref, m_ref, l_ref, acc_ref, *,
                  nhm, dqk, maps, scale, pos0, tq, tk, nkv, out_scale):
    i, j = pl.program_id(1), pl.program_id(2)

    @pl.when(j == 0)
    def _():
        m_ref[...] = jnp.full(m_ref.shape, NEG, F32)
        l_ref[...] = jnp.zeros(l_ref.shape, F32)
        acc_ref[...] = jnp.zeros(acc_ref.shape, F32)

    q_pos = pos0 + i * tq + lax.broadcasted_iota(jnp.int32, (tq, tk), 0)
    k_pos = j * tk + lax.broadcasted_iota(jnp.int32, (tq, tk), 1)
    mask = k_pos < _visible_limit(q_pos)

    @pl.when(j * tk < _visible_limit(pos0 + (i + 1) * tq - 1))
    def _():
        for hm in range(nhm):
            q = q_ref[:, hm * dqk:(hm + 1) * dqk].astype(BF16)
            k = k_ref[:, hm * dqk:(hm + 1) * dqk].astype(BF16)
            h = hm // maps
            v = v_ref[:, h * DA_DV:(h + 1) * DA_DV].astype(BF16)
            s = lax.dot_general(q, k, (((1,), (1,)), ((), ())), preferred_element_type=F32) * scale
            s = jnp.where(mask, s, NEG)
            m_prev = m_ref[hm]
            m_new = jnp.maximum(m_prev, jnp.max(s, axis=-1, keepdims=True))
            p = jnp.exp(s - m_new)
            alpha = jnp.exp(m_prev - m_new)
            l_ref[hm] = alpha * l_ref[hm] + jnp.sum(p, axis=-1, keepdims=True)
            acc_ref[hm] = alpha * acc_ref[hm] + jnp.dot(p.astype(BF16), v, preferred_element_type=F32)
            m_ref[hm] = m_new

    @pl.when(j == nkv - 1)
    def _():
        for h in range(nhm // maps):
            if maps == 2:
                o = acc_ref[2 * h] / l_ref[2 * h] - lam_ref[0] * (acc_ref[2 * h + 1] / l_ref[2 * h + 1])
                o = o * lax.rsqrt(jnp.mean(o * o, axis=-1, keepdims=True) + EPS) * g_ref[...] * out_scale
            else:
                o = acc_ref[h] / l_ref[h]
            o_ref[:, h * DA_DV:(h + 1) * DA_DV] = o


def _flash(q, k, v, lam, g_sub, *, nhm, dqk, maps, scale, pos0, tq, tk, out_scale, v_col=0):
    b, l, _ = q.shape
    kk = k.shape[1]
    nkv = kk // tk
    nh = nhm // maps

    def kv_idx(bi, i, j):
        last = (_visible_limit(pos0 + (i + 1) * tq - 1) - 1) // tk
        return jnp.minimum(j, last)

    return pl.pallas_call(
        functools.partial(_flash_kernel, nhm=nhm, dqk=dqk, maps=maps, scale=scale, pos0=pos0, tq=tq, tk=tk,
                          nkv=nkv, out_scale=out_scale),
        grid=(b, l // tq, nkv),
        in_specs=[pl.BlockSpec(memory_space=pltpu.SMEM),
                  pl.BlockSpec((None, tq, nhm * dqk), lambda bi, i, j: (bi, i, 0)),
                  pl.BlockSpec((None, tk, nhm * dqk), lambda bi, i, j: (bi, kv_idx(bi, i, j), 0)),
                  pl.BlockSpec((None, tk, nh * DA_DV), lambda bi, i, j: (bi, kv_idx(bi, i, j), v_col)),
                  _full((1, DA_DV))],
        out_specs=pl.BlockSpec((None, tq, nh * DA_DV), lambda bi, i, j: (bi, i, 0)),
        out_shape=jax.ShapeDtypeStruct((b, l, nh * DA_DV), F32),
        scratch_shapes=[pltpu.VMEM((nhm, tq, 1), F32), pltpu.VMEM((nhm, tq, 1), F32),
                        pltpu.VMEM((nhm, tq, DA_DV), F32)],
        compiler_params=_cparams("parallel", "parallel", "arbitrary"),
        name="flash_da" if maps == 2 else "flash_mla",
    )(lam, q, k, v, g_sub)


def _top16(s, rid, payload, val_ref, pay_ref):
    big = float(s.shape[0])
    for it in range(PEER_TOPK):
        m = jnp.max(s, axis=0, keepdims=True)
        am = jnp.min(jnp.where(s == m, rid, big), axis=0, keepdims=True)
        sel = rid == am
        val_ref[it:it + 1, :] = m
        pay_ref[it:it + 1, :] = jnp.max(jnp.where(sel, payload, -1.0), axis=0, keepdims=True)
        s = jnp.where(sel, -jnp.inf, s)


def _route_kernel(q_ref, k1_ref, k2_ref, idx_ref, g_ref, v1_ref, i1_ref, v2_ref, i2_ref, vc_ref, ic_ref):
    tt = q_ref.shape[0]
    half = q_ref.shape[1] // 2
    nt = (((1,), (1,)), ((), ()))
    s1 = lax.dot_general(k1_ref[...].astype(BF16), q_ref[:, :half].astype(BF16), nt, preferred_element_type=F32)
    s2 = lax.dot_general(k2_ref[...].astype(BF16), q_ref[:, half:].astype(BF16), nt, preferred_element_type=F32)
    rid = lax.broadcasted_iota(jnp.int32, (PEER_NKEYS, tt), 0).astype(F32)
    _top16(s1, rid, rid, v1_ref, i1_ref)
    _top16(s2, rid, rid, v2_ref, i2_ref)
    v2, i2 = v2_ref[...], i2_ref[...]
    cand = jnp.concatenate([v1_ref[a:a + 1, :] + v2 for a in range(PEER_TOPK)], axis=0)
    cidx = jnp.concatenate([i1_ref[a:a + 1, :] * float(PEER_NKEYS) + i2 for a in range(PEER_TOPK)], axis=0)
    rid2 = lax.broadcasted_iota(jnp.int32, cand.shape, 0).astype(F32)
    _top16(cand, rid2, cidx, vc_ref, ic_ref)
    sc = vc_ref[...]
    e = jnp.exp(sc - sc[0:1, :])
    g_ref[...] = e / jnp.sum(e, axis=0, keepdims=True)
    idx_ref[...] = ic_ref[...].astype(jnp.int32)


def _peer_route(q, k1, k2):
    t = q.shape[0]
    tt = PEER_TT
    dk = 2 * PEER_NKEYS
    kspec = pl.BlockSpec((None, PEER_NKEYS, dk // 2), lambda i, h: (h, 0, 0))
    ospec = pl.BlockSpec((None, PEER_TOPK, tt), lambda i, h: (i, h, 0))
    sm = pltpu.VMEM((PEER_TOPK, tt), F32)
    return pl.pallas_call(
        _route_kernel,
        grid=(t // tt, PEER_HEADS),
        in_specs=[pl.BlockSpec((tt, dk), lambda i, h: (i, h)), kspec, kspec],
        out_specs=[ospec, ospec],
        out_shape=[jax.ShapeDtypeStruct((t // tt, PEER_HK, tt), jnp.int32),
                   jax.ShapeDtypeStruct((t // tt, PEER_HK, tt), F32)],
        scratch_shapes=[sm, sm, sm, sm, sm, sm],
        compiler_params=_cparams("parallel", "parallel"),
        name="peer_route",
    )(q, k1, k2)


def _pack_table(tab):
    e, d = tab.shape
    tb = lax.bitcast_convert_type(tab.astype(BF16), jnp.uint16).astype(jnp.uint32)
    words = tb[:, :d // 2] | (tb[:, d // 2:] << 16)
    return lax.bitcast_convert_type(words, F32)


def _sc_gather(tab, idx):
    n = idx.shape[0]
    w = tab.shape[1]
    mesh = plsc.VectorSubcoreMesh(core_axis_name="core", subcore_axis_name="subcore")

    @pl.kernel(out_type=jax.ShapeDtypeStruct((n, w), tab.dtype), mesh=mesh)
    def gather(tab_hbm, idx_hbm, out_hbm):
        def body(idx_vmem, out_vmem):
            pltpu.sync_copy(tab_hbm.at[idx_vmem.at[0]], out_vmem)

        pltpu.emit_pipeline(
            body, grid=(n // SC_WINDOW,),
            in_specs=[pl.BlockSpec((1, SC_WINDOW), index_map=lambda i: (i, 0))],
            out_specs=[pl.BlockSpec((SC_WINDOW, w), index_map=lambda i: (i, 0))],
            core_axis_name=("core", "subcore"), dimension_semantics=(pltpu.PARALLEL,),
        )(idx_hbm, out_hbm)

    return gather(tab, idx.reshape(n // SC_WINDOW, SC_WINDOW))


def _unpack(words):
    w = pltpu.bitcast(words, jnp.uint32)
    lo = pltpu.bitcast(w << 16, F32)
    hi = pltpu.bitcast(w & jnp.uint32(0xFFFF0000), F32)
    return lo, hi


def _peer_act_kernel(h_ref, gu_ref, g_ref, coef_ref, act_ref):
    c = pl.program_id(1)
    half = h_ref.shape[1] // 2
    h_lo, h_hi = h_ref[:, :half], h_ref[:, half:]
    lane = lax.broadcasted_iota(jnp.int32, act_ref.shape, 1)

    @pl.when(c == 0)
    def _():
        act_ref[...] = jnp.zeros(act_ref.shape, F32)

    act = act_ref[...]
    for k in range(PEER_HKC):
        lo, hi = _unpack(gu_ref[k])
        a = jnp.sum(lo * h_lo + hi * h_hi, axis=-1, keepdims=True)
        act = jnp.where(lane == c * PEER_HKC + k, a, act)
    act_ref[...] = act

    @pl.when(c == pl.num_programs(1) - 1)
    def _():
        coef_ref[...] = g_ref[...].T * jax.nn.gelu(act)


def _peer_act(h, gu, g):
    t, d = h.shape
    tt = PEER_TT
    nt = t // tt
    return pl.pallas_call(
        _peer_act_kernel,
        grid=(nt, PEER_HK // PEER_HKC),
        in_specs=[pl.BlockSpec((tt, d), lambda i, c: (i, 0)),
                  pl.BlockSpec((None, PEER_HKC, tt, d // 2), lambda i, c: (i, c, 0, 0)),
                  pl.BlockSpec((None, PEER_HK, tt), lambda i, c: (i, 0, 0))],
        out_specs=pl.BlockSpec((None, tt, PEER_HK), lambda i, c: (i, 0, 0)),
        out_shape=jax.ShapeDtypeStruct((nt, tt, PEER_HK), F32),
        scratch_shapes=[pltpu.VMEM((tt, PEER_HK), F32)],
        compiler_params=_cparams("parallel", "arbitrary"),
        name="peer_act",
    )(h, gu, g)


def _peer_mix_kernel(coef_ref, gv_ref, x_ref, gt_ref, gf_ref, o_ref, acc_ref, *, final):
    c = pl.program_id(1)
    half = x_ref.shape[1] // 2

    @pl.when(c == 0)
    def _():
        acc_ref[...] = jnp.zeros(acc_ref.shape, F32)

    coef = coef_ref[...]
    lane = lax.broadcasted_iota(jnp.int32, coef.shape, 1)
    acc_lo, acc_hi = acc_ref[:, :half], acc_ref[:, half:]
    for k in range(PEER_HKC):
        ck = jnp.sum(jnp.where(lane == c * PEER_HKC + k, coef, 0.0), axis=-1, keepdims=True)
        lo, hi = _unpack(gv_ref[k])
        acc_lo = acc_lo + ck * lo
        acc_hi = acc_hi + ck * hi
    acc_ref[:, :half] = acc_lo
    acc_ref[:, half:] = acc_hi

    @pl.when(c == pl.num_programs(1) - 1)
    def _():
        x = x_ref[...] + gt_ref[...] * acc_ref[...]
        if final:
            x = x * lax.rsqrt(jnp.mean(x * x, axis=-1, keepdims=True) + EPS) * gf_ref[...]
        o_ref[...] = x


def _peer_mix(coef, gv, x, gt, g_final, final):
    t, d = x.shape
    tt = PEER_TT
    nt = t // tt
    if gt.ndim == 3:
        per_seq = nt // gt.shape[0]
        gt_spec = pl.BlockSpec((None, 1, d), lambda i, c: (i // per_seq, 0, 0))
    else:
        gt_spec = pl.BlockSpec((tt, d), lambda i, c: (i, 0))
    return pl.pallas_call(
        functools.partial(_peer_mix_kernel, final=final),
        grid=(nt, PEER_HK // PEER_HKC),
        in_specs=[pl.BlockSpec((None, tt, PEER_HK), lambda i, c: (i, 0, 0)),
                  pl.BlockSpec((None, PEER_HKC, tt, d // 2), lambda i, c: (i, c, 0, 0)),
                  pl.BlockSpec((tt, d), lambda i, c: (i, 0)), gt_spec, _full((1, d))],
        out_specs=pl.BlockSpec((tt, d), lambda i, c: (i, 0)),
        out_shape=jax.ShapeDtypeStruct((t, d), F32),
        scratch_shapes=[pltpu.VMEM((tt, d), F32)],
        compiler_params=_cparams("parallel", "arbitrary"),
        name="peer_mix",
    )(coef, gv, x, gt, g_final.reshape(1, d))


def _peer(x, g_ffn, sc2, sh2, gt2, w_q, k1, k2, u_pack, v_pack, g_final, final, tl):
    b, l, d = x.shape
    t = b * l
    q, h = _norm_mod_matmul(x, g_ffn, sc2, sh2, w_q, tl, want_h=True)
    idx, g = _peer_route(q.reshape(t, -1), k1, k2)
    nt = t // PEER_TT
    flat = idx.reshape(-1)
    gu = _sc_gather(u_pack, flat).reshape(nt, PEER_HK, PEER_TT, d // 2)
    coef = _peer_act(h.reshape(t, d), gu, g)
    gv = _sc_gather(v_pack, flat).reshape(nt, PEER_HK, PEER_TT, d // 2)
    if l % PEER_TT:
        gt2 = jnp.broadcast_to(gt2, (b, l, d)).reshape(t, d)
    return _peer_mix(coef, gv, x.reshape(t, d), gt2, g_final, final).reshape(b, l, d)


def _trunk(x, mod, pos0, past, p, tl):
    b, l, d = x.shape
    outs = {}
    for layer in range(DEPTH):
        sh1, sc1, gt1, sh2, sc2, gt2 = [m[:, None, :] for m in jnp.split(mod[layer], 6, axis=-1)]
        if layer % 2 == 0:
            e = layer // 2
            y_in = _norm_mod_matmul(x, p['g_mix'][layer], sc1, sh1, p['w_in_e'][e].astype(BF16), tl)
            if past is None:
                h0_re = jnp.zeros((b, S5_GROUPS, S5_STATE), F32)
                h0_im = jnp.zeros((b, S5_GROUPS, S5_STATE), F32)
            else:
                h0_re, h0_im = past['s5_re'][e], past['s5_im'][e]
            ya, hr, hi = _s5(y_in, h0_re, h0_im, p['s5'][e], p['s5_d'][e], p['w_glu'][e], p['b_glu'][e], tl)
            yb, vn = _gmlp(y_in, p['gm_g_v'][e], p['gm_w_s'][e], p['gm_b_s'][e], tl, want_vn=past is not None)
            x = _proj_residual(x, gt1, ya, yb, p['w_out_e'][e], tl)
            outs.update(s5_re=hr, s5_im=hi, gm_v=vn)
        else:
            o = layer // 2
            w_in = jnp.pad(p['w_in_o'][o], ((0, 0), (0, 2048 - p['w_in_o'][o].shape[1]))).astype(BF16)
            y_in = _norm_mod_matmul(x, p['g_mix'][layer], sc1, sh1, w_in, tl)
            qd, kd, qm, lat, kpe = _odd_prep(y_in, pos0, p['mla_g_q'][o], p['mla_w_uq'][o], p['mla_g_kv'][o], tl)
            v_new = y_in[:, :, 2 * 512:3 * 512]
            lam_init = 0.8 - 0.6 * math.exp(-0.3 * layer)
            lam = (jnp.exp(jnp.sum(p['da_lq1'][o] * p['da_lk1'][o])) - jnp.exp(jnp.sum(p['da_lq2'][o] * p['da_lk2'][o]))
                   + lam_init).reshape(1).astype(F32)
            g_sub = p['da_g_sub'][o].reshape(1, DA_DV)
            if past is None:
                k_all, v_all, v_col, lat_all, kpe_all = kd, y_in, 2, lat, kpe
                tq = tk = tl
            else:
                k_all = jnp.concatenate([past['da_k'][o].reshape(b, -1, 512), kd], axis=1)
                v_all = jnp.concatenate([past['da_v'][o].reshape(b, -1, 512), v_new], axis=1)
                lat_all = jnp.concatenate([past['mla_lat'][o], lat], axis=1)
                kpe_all = jnp.concatenate([past['mla_kpe'][o], kpe], axis=1)
                v_col, tq, tk = 0, l, k_all.shape[1]
            yc = _flash(qd, k_all, v_all, lam, g_sub, nhm=2 * DA_HEADS, dqk=DA_DH, maps=2, scale=DA_DH ** -0.5,
                        pos0=pos0, tq=tq, tk=tk, out_scale=1.0 - lam_init, v_col=v_col)
            km, vm = _mla_kv(lat_all, kpe_all, p['mla_w_ukv'][o], tk)
            yd = _flash(qm, km, vm, lam, g_sub, nhm=MLA_HEADS, dqk=LANES, maps=1,
                        scale=(MLA_NOPE + MLA_ROPE) ** -0.5, pos0=pos0, tq=tq, tk=tk, out_scale=1.0)
            x = _proj_residual(x, gt1, yc, yd, p['w_out_o'][o], tl)
            outs.update(da_k=kd.reshape(b, l, DA_HEADS, 2 * DA_DH), da_v=v_new.reshape(b, l, DA_HEADS, DA_DV),
                        mla_lat=lat, mla_kpe=kpe)
        x = _peer(x, p['g_ffn'][layer], sc2, sh2, gt2, p['peer_w_q'][layer].astype(BF16), p['peer_k1'][layer],
                  p['peer_k2'][layer], p['u_pack'][layer], p['v_pack'][layer], p['g_final'],
                  final=layer == DEPTH - 1, tl=tl)
    return x, outs


def kernel(x_prompt, x_sample, c_prompt, c_sample, state_s5_re, state_s5_im, cache_da_k, cache_da_v, cache_mla_latent, cache_mla_kpe, w_ada, b_ada, g_mix, g_ffn, g_final, w_in_e, w_out_e, s5_lam_re, s5_lam_im, s5_log_dt, s5_b_re, s5_b_im, s5_c_re, s5_c_im, s5_d, w_glu, b_glu, gm_g_v, gm_w_s, gm_b_s, w_in_o, w_out_o, da_lq1, da_lk1, da_lq2, da_lk2, da_g_sub, mla_g_q, mla_w_uq, mla_g_kv, mla_w_ukv, peer_w_q, peer_k1, peer_k2, peer_u, peer_v):
    p = dict(g_mix=g_mix, g_ffn=g_ffn, g_final=g_final, w_in_e=w_in_e, w_out_e=w_out_e, s5_d=s5_d, w_glu=w_glu,
             b_glu=b_glu, gm_g_v=gm_g_v, gm_w_s=gm_w_s, gm_b_s=gm_b_s, w_in_o=w_in_o, w_out_o=w_out_o,
             da_lq1=da_lq1, da_lk1=da_lk1, da_lq2=da_lq2, da_lk2=da_lk2, da_g_sub=da_g_sub, mla_g_q=mla_g_q,
             mla_w_uq=mla_w_uq, mla_g_kv=mla_g_kv, mla_w_ukv=mla_w_ukv, peer_w_q=peer_w_q, peer_k1=peer_k1,
             peer_k2=peer_k2)
    n_even = (DEPTH + 1) // 2
    p['s5'] = [_s5_params(s5_lam_re[e], s5_lam_im[e], s5_log_dt[e], s5_b_re[e], s5_b_im[e], s5_c_re[e], s5_c_im[e])
               for e in range(n_even)]
    p['u_pack'] = [_pack_table(peer_u[layer]) for layer in range(DEPTH)]
    p['v_pack'] = [_pack_table(peer_v[layer]) for layer in range(DEPTH)]
    past = dict(s5_re=state_s5_re, s5_im=state_s5_im, da_k=cache_da_k, da_v=cache_da_v,
                mla_lat=cache_mla_latent, mla_kpe=cache_mla_kpe)
    nb = x_prompt.shape[0]
    mod = _ada(jnp.concatenate([c_prompt, c_sample], axis=0), w_ada, b_ada)
    past_len = cache_da_k.shape[2]
    y_p, o_p = _trunk(x_prompt, mod[:, :nb], 0, None, p, tl=512)
    y_s, o_s = _trunk(x_sample, mod[:, nb:], past_len, past, p, tl=x_sample.shape[1])
    st = lambda a: a[None]
    return (y_p, y_s, st(o_p['s5_re']), st(o_p['s5_im']), st(o_s['s5_re']), st(o_s['s5_im']), st(o_s['gm_v']),
            st(o_p['da_k']), st(o_p['da_v']), st(o_s['da_k']), st(o_s['da_v']),
            st(o_p['mla_lat']), st(o_p['mla_kpe']), st(o_s['mla_lat']), st(o_s['mla_kpe']))
```

```python
import functools
import math

import jax
import jax.numpy as jnp
from jax import lax
from jax.experimental import pallas as pl
from jax.experimental.pallas import tpu as pltpu
from jax.experimental.pallas import tpu_sc as plsc

F32 = jnp.float32
BF16 = jnp.bfloat16

D_MODEL = 1024
DEPTH = 2
CHUNK = 64
ROPE_THETA = 500000.0
EPS = 1e-6
NEG = -1e30

S5_WIDTH = 512
S5_GROUP = 16
S5_GROUPS = 32
S5_STATE = 64
GM_WIDTH = 512
GM_HEADS = 4
GM_CHUNK = 128
DA_HEADS = 4
DA_DH = 64
DA_DV = 128
DA_ROPE = 16
MLA_HEADS = 4
MLA_Q_RANK = 256
MLA_KV_RANK = 128
MLA_NOPE = 64
MLA_ROPE = 32
MLA_DV = 128
PEER_HEADS = 8
PEER_NKEYS = 128
PEER_TOPK = 16
PEER_HK = PEER_HEADS * PEER_TOPK

LANES = 128
VMEM_LIMIT = 48 << 20
PEER_TT = 256
PEER_HKC = 16
SC_WINDOW = 64
FLASH_RQ = 64

_HI = lax.Precision.HIGHEST


def _cparams(*sem):
    return pltpu.CompilerParams(dimension_semantics=sem, vmem_limit_bytes=VMEM_LIMIT)


def _full(shape):
    n = len(shape)
    return pl.BlockSpec(shape, lambda *_: (0,) * n)


def _ada_kernel(c_ref, w_ref, b_ref, o_ref):
    c = c_ref[...]
    s = c * jax.nn.sigmoid(c)
    o_ref[...] = jnp.dot(s, w_ref[...], precision=_HI, preferred_element_type=F32) + b_ref[...]


def _ada(c_all, w_ada, b_ada):
    r = c_all.shape[0]
    tn = 1536
    return pl.pallas_call(
        _ada_kernel,
        grid=(DEPTH, 6 * D_MODEL // tn),
        in_specs=[_full((r, D_MODEL)),
                  pl.BlockSpec((None, D_MODEL, tn), lambda l, j: (l, 0, j)),
                  pl.BlockSpec((None, 1, tn), lambda l, j: (l, 0, j))],
        out_specs=pl.BlockSpec((None, r, tn), lambda l, j: (l, 0, j)),
        out_shape=jax.ShapeDtypeStruct((DEPTH, r, 6 * D_MODEL), F32),
        compiler_params=_cparams("parallel", "parallel"),
        name="ada",
    )(c_all, w_ada, b_ada.reshape(DEPTH, 1, 6 * D_MODEL))


def _nmm_kernel(x_ref, g_ref, sc_ref, sh_ref, w_ref, o_ref, *h_ref):
    x = x_ref[...]
    ms = jnp.mean(x * x, axis=-1, keepdims=True)
    h = x * lax.rsqrt(ms + EPS) * g_ref[...] * (1.0 + sc_ref[...]) + sh_ref[...]
    o_ref[...] = jnp.dot(h.astype(BF16), w_ref[...], preferred_element_type=F32)
    if h_ref:
        h_ref[0][...] = h


def _norm_mod_matmul(x, g, sc, sh, w, tl, want_h=False):
    b, l, d = x.shape
    n = w.shape[1]
    row = pl.BlockSpec((None, 1, d), lambda bi, i: (bi, 0, 0))
    out_shape = [jax.ShapeDtypeStruct((b, l, n), F32)]
    out_specs = [pl.BlockSpec((None, tl, n), lambda bi, i: (bi, i, 0))]
    if want_h:
        out_shape.append(jax.ShapeDtypeStruct((b, l, d), F32))
        out_specs.append(pl.BlockSpec((None, tl, d), lambda bi, i: (bi, i, 0)))
    res = pl.pallas_call(
        _nmm_kernel,
        grid=(b, l // tl),
        in_specs=[pl.BlockSpec((None, tl, d), lambda bi, i: (bi, i, 0)), _full((1, d)), row, row, _full((d, n))],
        out_specs=out_specs, out_shape=out_shape,
        compiler_params=_cparams("parallel", "parallel"),
        name="norm_mod_matmul",
    )(x, g.reshape(1, d), sc, sh, w)
    return res if want_h else res[0]


def _s5_kernel(xa_ref, h0r_ref, h0i_ref, bblk_ref, ar_ref, ai_ref, cr_ref, ci_ref, d_ref, wglu_ref, bglu_ref,
               ya_ref, hr_out_ref, hi_out_ref, bur, bui, st_r, st_i, *, tl):
    nj = S5_WIDTH // LANES
    half = 8 * S5_STATE

    @pl.when(pl.program_id(1) == 0)
    def _():
        st_r[...] = h0r_ref[...]
        st_i[...] = h0i_ref[...]

    xa = xa_ref[...]
    for j in range(nj):
        bu = jnp.dot(xa[:, j * LANES:(j + 1) * LANES], bblk_ref[j], precision=_HI, preferred_element_type=F32)
        bur[j] = bu[:, :half]
        bui[j] = bu[:, half:]

    for j0 in range(0, nj, 2):
        js = (j0, j0 + 1)
        a_r = [ar_ref[j] for j in js]
        a_i = [ai_ref[j] for j in js]

        def body(k, carry, js=js, a_r=a_r, a_i=a_i):
            carry = list(carry)
            base = pl.multiple_of(k * 8, 8)
            for s in range(8):
                t = base + s
                for q, j in enumerate(js):
                    hr, hi = carry[2 * q], carry[2 * q + 1]
                    nhr = a_r[q] * hr - a_i[q] * hi + bur[j, pl.ds(t, 1), :]
                    nhi = a_r[q] * hi + a_i[q] * hr + bui[j, pl.ds(t, 1), :]
                    bur[j, pl.ds(t, 1), :] = nhr
                    bui[j, pl.ds(t, 1), :] = nhi
                    carry[2 * q], carry[2 * q + 1] = nhr, nhi
            return tuple(carry)

        init = (st_r[js[0]], st_i[js[0]], st_r[js[1]], st_i[js[1]])
        fin = lax.fori_loop(0, tl // 8, body, init)
        st_r[js[0]], st_i[js[0]], st_r[js[1]], st_i[js[1]] = fin

    hr_out_ref[...] = st_r[...]
    hi_out_ref[...] = st_i[...]

    ys = []
    for j in range(nj):
        y = (jnp.dot(bur[j].astype(BF16), cr_ref[j], preferred_element_type=F32)
             + jnp.dot(bui[j].astype(BF16), ci_ref[j], preferred_element_type=F32))
        ys.append(y + d_ref[:, j * LANES:(j + 1) * LANES] * xa[:, j * LANES:(j + 1) * LANES])
    y = jax.nn.gelu(jnp.concatenate(ys, axis=1))
    z = jnp.dot(y.astype(BF16), wglu_ref[...], preferred_element_type=F32) + bglu_ref[...]
    ya_ref[...] = z[:, :S5_WIDTH] * jax.nn.sigmoid(z[:, S5_WIDTH:])


def _s5_params(lam_re, lam_im, log_dt, b_re, b_im, c_re, c_im):
    lam = lax.complex(lam_re.astype(F32), lam_im.astype(F32))
    dt = jnp.exp(log_dt.astype(F32))[:, None]
    a_bar = jnp.exp(lam * dt)
    b_bar = ((a_bar - 1.0) / lam)[..., None] * lax.complex(b_re.astype(F32), b_im.astype(F32))
    nj = S5_GROUPS // 8
    eye = jnp.eye(8, dtype=F32)

    def blk_b(m):
        m = m.reshape(nj, 8, S5_STATE, S5_GROUP)
        return jnp.einsum('jgpc,gh->jgchp', m, eye).reshape(nj, 8 * S5_GROUP, 8 * S5_STATE)

    def blk_c(m):
        m = m.reshape(nj, 8, S5_GROUP, S5_STATE)
        return jnp.einsum('jgcp,gh->jgphc', m, eye).reshape(nj, 8 * S5_STATE, 8 * S5_GROUP)

    bblk = jnp.concatenate([blk_b(jnp.real(b_bar)), blk_b(jnp.imag(b_bar))], axis=-1)
    cr = blk_c(c_re.astype(F32)).astype(BF16)
    ci = (-blk_c(c_im.astype(F32))).astype(BF16)
    a_r = jnp.real(a_bar).reshape(nj, 1, 8 * S5_STATE)
    a_i = jnp.imag(a_bar).reshape(nj, 1, 8 * S5_STATE)
    return bblk, a_r, a_i, cr, ci


def _s5(y_in, h0_re, h0_im, sp, d_skip, w_glu, b_glu, tl):
    b, l, _ = y_in.shape
    nj = S5_GROUPS // 8
    half = 8 * S5_STATE
    bblk, a_r, a_i, cr, ci = sp
    st = pl.BlockSpec((None, nj, 1, half), lambda bi, i: (bi, 0, 0, 0))
    ya, hr, hi = pl.pallas_call(
        functools.partial(_s5_kernel, tl=tl),
        grid=(b, l // tl),
        in_specs=[pl.BlockSpec((None, tl, S5_WIDTH), lambda bi, i: (bi, i, 0)), st, st,
                  _full(bblk.shape), _full(a_r.shape), _full(a_i.shape), _full(cr.shape), _full(ci.shape),
                  _full((1, S5_WIDTH)), _full((S5_WIDTH, 2 * S5_WIDTH)), _full((1, 2 * S5_WIDTH))],
        out_specs=[pl.BlockSpec((None, tl, S5_WIDTH), lambda bi, i: (bi, i, 0)), st, st],
        out_shape=[jax.ShapeDtypeStruct((b, l, S5_WIDTH), F32),
                   jax.ShapeDtypeStruct((b, nj, 1, half), F32), jax.ShapeDtypeStruct((b, nj, 1, half), F32)],
        scratch_shapes=[pltpu.VMEM((nj, tl, half), F32), pltpu.VMEM((nj, tl, half), F32),
                        pltpu.VMEM((nj, 1, half), F32), pltpu.VMEM((nj, 1, half), F32)],
        compiler_params=_cparams("parallel", "arbitrary"),
        name="s5",
    )(y_in, h0_re.reshape(b, nj, 1, half), h0_im.reshape(b, nj, 1, half), bblk, a_r, a_i, cr, ci,
      d_skip.reshape(1, S5_WIDTH), w_glu.astype(BF16), b_glu.reshape(1, 2 * S5_WIDTH))
    return ya, hr.reshape(b, S5_GROUPS, S5_STATE), hi.reshape(b, S5_GROUPS, S5_STATE)


def _gmlp_kernel(u_ref, v_ref, gv_ref, ws_ref, bias_ref, yb_ref, *vn_ref, tl, lc):
    hd = GM_WIDTH // GM_HEADS
    gv = jax.nn.gelu(v_ref[...])
    vn = gv * lax.rsqrt(jnp.mean(gv * gv, axis=-1, keepdims=True) + EPS) * gv_ref[...]
    if vn_ref:
        vn_ref[0][...] = vn
    gu = jax.nn.gelu(u_ref[...])
    vb = vn.astype(BF16)
    for c in range(tl // lc):
        rows = slice(c * lc, (c + 1) * lc)
        for h in range(GM_HEADS):
            cols = slice(h * hd, (h + 1) * hd)
            mixed = jnp.dot(ws_ref[h], vb[rows, cols], preferred_element_type=F32) + bias_ref[:, cols]
            yb_ref[rows, cols] = gu[rows, cols] * mixed


def _gmlp(y_in, g_v, w_s, b_s, tl, want_vn):
    b, l, _ = y_in.shape
    lc = min(GM_CHUNK, l)
    hd = GM_WIDTH // GM_HEADS
    tri = jnp.tril(jnp.ones((GM_CHUNK, GM_CHUNK), dtype=bool))
    ws = jnp.where(tri[None], w_s, 0.0)[:, :lc, :lc].astype(BF16)
    bias = jnp.repeat(jnp.transpose(b_s)[:lc], hd, axis=1)
    out_shape = [jax.ShapeDtypeStruct((b, l, GM_WIDTH), F32)]
    out_specs = [pl.BlockSpec((None, tl, GM_WIDTH), lambda bi, i: (bi, i, 0))]
    if want_vn:
        out_shape.append(jax.ShapeDtypeStruct((b, l, GM_WIDTH), F32))
        out_specs.append(pl.BlockSpec((None, tl, GM_WIDTH), lambda bi, i: (bi, i, 0)))
    res = pl.pallas_call(
        functools.partial(_gmlp_kernel, tl=tl, lc=lc),
        grid=(b, l // tl),
        in_specs=[pl.BlockSpec((None, tl, GM_WIDTH), lambda bi, i: (bi, i, 1)),
                  pl.BlockSpec((None, tl, GM_WIDTH), lambda bi, i: (bi, i, 2)),
                  _full((1, GM_WIDTH)), _full(ws.shape), _full(bias.shape)],
        out_specs=out_specs, out_shape=out_shape,
        compiler_params=_cparams("parallel", "parallel"),
        name="gmlp",
    )(y_in, y_in, g_v.reshape(1, GM_WIDTH), ws, bias)
    return (res[0], res[1]) if want_vn else (res[0], None)


def _proj_res_kernel(x_ref, gt_ref, a_ref, b_ref, wa_ref, wb_ref, o_ref):
    mix = (jnp.dot(a_ref[...].astype(BF16), wa_ref[...], preferred_element_type=F32)
           + jnp.dot(b_ref[...].astype(BF16), wb_ref[...], preferred_element_type=F32))
    o_ref[...] = x_ref[...] + gt_ref[...] * mix


def _proj_residual(x, gt, a, bm, w, tl):
    b, l, d = x.shape
    ka = a.shape[-1]
    wa, wb = w[:ka].astype(BF16), w[ka:].astype(BF16)
    blk = lambda n: pl.BlockSpec((None, tl, n), lambda bi, i: (bi, i, 0))
    return pl.pallas_call(
        _proj_res_kernel,
        grid=(b, l // tl),
        in_specs=[blk(d), pl.BlockSpec((None, 1, d), lambda bi, i: (bi, 0, 0)), blk(ka), blk(bm.shape[-1]),
                  _full(wa.shape), _full(wb.shape)],
        out_specs=blk(d), out_shape=jax.ShapeDtypeStruct((b, l, d), F32),
        compiler_params=_cparams("parallel", "parallel"),
        name="proj_residual",
    )(x, gt, a, bm, wa, wb)


def _rope_tables(pos0, l):
    pos = (pos0 + jnp.arange(l, dtype=jnp.int32)).astype(F32)[:, None]
    lane = jnp.arange(LANES)

    def table(period, start, rot):
        half = rot // 2
        r = lane % period - start
        inside = (r >= 0) & (r < rot)
        k = jnp.where(inside, r % half, 0)
        inv = ROPE_THETA ** (-k.astype(F32) * 2.0 / rot)
        ang = pos * inv[None, :]
        cos = jnp.where(inside[None], jnp.cos(ang), 1.0)
        sgn = jnp.where(r < half, -1.0, 1.0)
        sin = jnp.where(inside[None], jnp.sin(ang) * sgn[None], 0.0)
        return cos, sin

    return table(DA_DH, 0, DA_ROPE), table(LANES, MLA_NOPE, MLA_ROPE), table(LANES, 0, MLA_ROPE)


def _rotate(x, cos, sin, period, start, rot):
    half = rot // 2
    n = x.shape[-1]
    reps = n // LANES
    lane = (lax.broadcasted_iota(jnp.int32, x.shape, 1) & (period - 1)) - start
    up = pltpu.roll(x, n - half, 1)
    dn = pltpu.roll(x, half, 1)
    partner = jnp.where(lane < half, up, dn)
    if reps > 1:
        cos = jnp.concatenate([cos] * reps, axis=1)
        sin = jnp.concatenate([sin] * reps, axis=1)
    return x * cos + partner * sin


def _odd_prep_kernel(q_ref, k_ref, cq_ref, ckv_ref, kpe_ref, cd_ref, sd_ref, cm_ref, sm_ref, ck_ref, sk_ref,
                     gq_ref, wuq_ref, gkv_ref, qd_ref, kd_ref, qm_ref, lat_ref, kpe_out_ref):
    cd, sd = cd_ref[...], sd_ref[...]
    qd_ref[...] = _rotate(q_ref[...], cd, sd, DA_DH, 0, DA_ROPE).astype(BF16)
    kd_ref[...] = _rotate(k_ref[...], cd, sd, DA_DH, 0, DA_ROPE)
    cq = cq_ref[...]
    cqn = cq * lax.rsqrt(jnp.mean(cq * cq, axis=-1, keepdims=True) + EPS) * gq_ref[...]
    qm = jnp.dot(cqn.astype(BF16), wuq_ref[...], preferred_element_type=F32)
    qm_ref[...] = _rotate(qm, cm_ref[...], sm_ref[...], LANES, MLA_NOPE, MLA_ROPE).astype(BF16)
    ckv = ckv_ref[...]
    lat_ref[...] = ckv * lax.rsqrt(jnp.mean(ckv * ckv, axis=-1, keepdims=True) + EPS) * gkv_ref[...]
    kpe = _rotate(kpe_ref[...], ck_ref[...], sk_ref[...], LANES, 0, MLA_ROPE)
    kpe_out_ref[...] = kpe[:, :MLA_ROPE]


def _odd_prep(y_in, pos0, g_q, w_uq, g_kv, tl):
    b, l, _ = y_in.shape
    (cd, sd), (cm, sm), (ck, sk) = _rope_tables(pos0, l)
    per = MLA_NOPE + MLA_ROPE
    wuq = jnp.pad(w_uq.reshape(MLA_Q_RANK, MLA_HEADS, per), ((0, 0), (0, 0), (0, LANES - per)))
    wuq = wuq.reshape(MLA_Q_RANK, MLA_HEADS * LANES).astype(BF16)
    col = lambda w, j: pl.BlockSpec((None, tl, w), lambda bi, i: (bi, i, j))
    tab = pl.BlockSpec((tl, LANES), lambda bi, i: (i, 0))
    out = lambda w: pl.BlockSpec((None, tl, w), lambda bi, i: (bi, i, 0))
    return pl.pallas_call(
        _odd_prep_kernel,
        grid=(b, l // tl),
        in_specs=[col(512, 0), col(512, 1), col(256, 6), col(128, 14), col(128, 15),
                  tab, tab, tab, tab, tab, tab,
                  _full((1, MLA_Q_RANK)), _full(wuq.shape), _full((1, MLA_KV_RANK))],
        out_specs=[out(512), out(512), out(512), out(MLA_KV_RANK), out(MLA_ROPE)],
        out_shape=[jax.ShapeDtypeStruct((b, l, 512), BF16), jax.ShapeDtypeStruct((b, l, 512), F32),
                   jax.ShapeDtypeStruct((b, l, 512), BF16), jax.ShapeDtypeStruct((b, l, MLA_KV_RANK), F32),
                   jax.ShapeDtypeStruct((b, l, MLA_ROPE), F32)],
        compiler_params=_cparams("parallel", "parallel"),
        name="odd_prep",
    )(y_in, y_in, y_in, y_in, y_in, cd, sd, cm, sm, ck, sk,
      g_q.reshape(1, MLA_Q_RANK), wuq, g_kv.reshape(1, MLA_KV_RANK))


def _mla_kv_kernel(lat_ref, kpe_ref, wk_ref, wv_ref, place_ref, k_ref, v_ref):
    lat = lat_ref[...].astype(BF16)
    k = (jnp.dot(lat, wk_ref[...], preferred_element_type=F32)
         + jnp.dot(kpe_ref[...].astype(BF16), place_ref[...], preferred_element_type=F32))
    k_ref[...] = k.astype(BF16)
    v_ref[...] = jnp.dot(lat, wv_ref[...], preferred_element_type=F32).astype(BF16)


def _mla_kv(lat_all, kpe_all, w_ukv, tk):
    b, k, _ = lat_all.shape
    per = MLA_NOPE + MLA_DV
    w3 = w_ukv.reshape(MLA_KV_RANK, MLA_HEADS, per)
    wk = jnp.pad(w3[:, :, :MLA_NOPE], ((0, 0), (0, 0), (0, LANES - MLA_NOPE))).reshape(MLA_KV_RANK, MLA_HEADS * LANES)
    wv = w3[:, :, MLA_NOPE:].reshape(MLA_KV_RANK, MLA_HEADS * MLA_DV)
    place = jnp.pad(jnp.eye(MLA_ROPE, dtype=F32), ((0, 0), (MLA_NOPE, LANES - MLA_NOPE - MLA_ROPE)))
    place = jnp.tile(place, (1, MLA_HEADS))
    blk = lambda w: pl.BlockSpec((None, tk, w), lambda bi, i: (bi, i, 0))
    return pl.pallas_call(
        _mla_kv_kernel,
        grid=(b, k // tk),
        in_specs=[blk(MLA_KV_RANK), blk(MLA_ROPE), _full(wk.shape), _full(wv.shape), _full(place.shape)],
        out_specs=[blk(512), blk(512)],
        out_shape=[jax.ShapeDtypeStruct((b, k, 512), BF16), jax.ShapeDtypeStruct((b, k, 512), BF16)],
        compiler_params=_cparams("parallel", "parallel"),
        name="mla_kv",
    )(lat_all, kpe_all, wk.astype(BF16), wv.astype(BF16), place.astype(BF16))


def _visible_limit(pos_last):
    shift = CHUNK.bit_length() - 1
    return ((pos_last >> shift) + 1) << shift


def _flash_kernel(lam_ref, q_ref, k_ref, v_ref, g_ref, o_ref, m_ref, l_ref, acc_ref, *,
                  nhm, dqk, maps, scale, pos0, tq, tk, nkv, out_scale, rq):
    i, j = pl.program_id(1), pl.program_id(2)

    @pl.when(j == 0)
    def _():
        m_ref[...] = jnp.full(m_ref.shape, NEG, F32)
        l_ref[...] = jnp.zeros(l_ref.shape, F32)
        acc_ref[...] = jnp.zeros(acc_ref.shape, F32)

    c2 = scale * math.log2(math.e)
    visible = j * tk < _visible_limit(pos0 + (i + 1) * tq - 1)
    unmasked = (j + 1) * tk <= _visible_limit(pos0 + i * tq)

    def process(masked):
        for hm in range(nhm):
            k = k_ref[:, hm * dqk:(hm + 1) * dqk].astype(BF16)
            h = hm // maps
            v = v_ref[:, h * DA_DV:(h + 1) * DA_DV].astype(BF16)

            def qblock(r, carry, hm=hm, k=k, v=v):
                r0 = pl.multiple_of(r * rq, rq)
                rows = pl.ds(r0, rq)
                q = q_ref[rows, hm * dqk:(hm + 1) * dqk].astype(BF16)
                s = lax.dot_general(q, k, (((1,), (1,)), ((), ())), preferred_element_type=F32)
                if masked:
                    q_pos = pos0 + i * tq + r0 + lax.broadcasted_iota(jnp.int32, (rq, tk), 0)
                    k_pos = j * tk + lax.broadcasted_iota(jnp.int32, (rq, tk), 1)
                    s = jnp.where(k_pos < _visible_limit(q_pos), s, NEG)
                m_prev = m_ref[hm, rows, :]
                m_new = jnp.maximum(m_prev, jnp.max(s, axis=-1, keepdims=True))
                p = jnp.exp2((s - m_new) * c2)
                alpha = jnp.exp2((m_prev - m_new) * c2)
                l_ref[hm, rows, :] = alpha * l_ref[hm, rows, :] + jnp.sum(p, axis=-1, keepdims=True)
                acc_ref[hm, rows, :] = (alpha * acc_ref[hm, rows, :]
                                        + jnp.dot(p.astype(BF16), v, preferred_element_type=F32))
                m_ref[hm, rows, :] = m_new
                return carry

            lax.fori_loop(0, tq // rq, qblock, 0)

    pl.when(visible & unmasked)(lambda: process(False))
    pl.when(visible & jnp.logical_not(unmasked))(lambda: process(True))

    @pl.when(j == nkv - 1)
    def _():
        for h in range(nhm // maps):
            if maps == 2:
                o = acc_ref[2 * h] / l_ref[2 * h] - lam_ref[0] * (acc_ref[2 * h + 1] / l_ref[2 * h + 1])
                o = o * lax.rsqrt(jnp.mean(o * o, axis=-1, keepdims=True) + EPS) * g_ref[...] * out_scale
            else:
                o = acc_ref[h] / l_ref[h]
            o_ref[:, h * DA_DV:(h + 1) * DA_DV] = o


def _flash(q, k, v, lam, g_sub, *, nhm, dqk, maps, scale, pos0, tq, tk, out_scale, v_col=0):
    b, l, _ = q.shape
    kk = k.shape[1]
    nkv = kk // tk
    nh = nhm // maps

    def kv_idx(bi, i, j):
        last = (_visible_limit(pos0 + (i + 1) * tq - 1) - 1) // tk
        return jnp.minimum(j, last)

    return pl.pallas_call(
        functools.partial(_flash_kernel, nhm=nhm, dqk=dqk, maps=maps, scale=scale, pos0=pos0, tq=tq, tk=tk,
                          nkv=nkv, out_scale=out_scale, rq=min(FLASH_RQ, tq)),
        grid=(b, l // tq, nkv),
        in_specs=[pl.BlockSpec(memory_space=pltpu.SMEM),
                  pl.BlockSpec((None, tq, nhm * dqk), lambda bi, i, j: (bi, i, 0)),
                  pl.BlockSpec((None, tk, nhm * dqk), lambda bi, i, j: (bi, kv_idx(bi, i, j), 0)),
                  pl.BlockSpec((None, tk, nh * DA_DV), lambda bi, i, j: (bi, kv_idx(bi, i, j), v_col)),
                  _full((1, DA_DV))],
        out_specs=pl.BlockSpec((None, tq, nh * DA_DV), lambda bi, i, j: (bi, i, 0)),
        out_shape=jax.ShapeDtypeStruct((b, l, nh * DA_DV), F32),
        scratch_shapes=[pltpu.VMEM((nhm, tq, 1), F32), pltpu.VMEM((nhm, tq, 1), F32),
                        pltpu.VMEM((nhm, tq, DA_DV), F32)],
        compiler_params=_cparams("parallel", "parallel", "arbitrary"),
        name="flash_da" if maps == 2 else "flash_mla",
    )(lam, q, k, v, g_sub)


def _top16(s, rid, payload, val_ref, pay_ref):
    big = float(s.shape[0])
    for it in range(PEER_TOPK):
        m = jnp.max(s, axis=0, keepdims=True)
        am = jnp.min(jnp.where(s == m, rid, big), axis=0, keepdims=True)
        sel = rid == am
        val_ref[it:it + 1, :] = m
        if payload is None:
            pay_ref[it:it + 1, :] = am
        else:
            pay_ref[it:it + 1, :] = jnp.max(jnp.where(sel, payload, -1.0), axis=0, keepdims=True)
        s = jnp.where(sel, -jnp.inf, s)


_PEER_PAIRS = [(a, b) for a in range(PEER_TOPK) for b in range(PEER_TOPK) if (a + 1) * (b + 1) <= PEER_TOPK]
_PEER_CAND_ROWS = -(-len(_PEER_PAIRS) // 8) * 8


def _route_kernel(q_ref, k1_ref, k2_ref, idx_ref, g_ref, v1_ref, i1_ref, v2_ref, i2_ref, vc_ref, ic_ref,
                  cand_ref, cidx_ref):
    tt = q_ref.shape[0]
    half = q_ref.shape[1] // 2
    nt = (((1,), (1,)), ((), ()))
    s1 = lax.dot_general(k1_ref[...].astype(BF16), q_ref[:, :half].astype(BF16), nt, preferred_element_type=F32)
    s2 = lax.dot_general(k2_ref[...].astype(BF16), q_ref[:, half:].astype(BF16), nt, preferred_element_type=F32)
    rid = lax.broadcasted_iota(jnp.int32, (PEER_NKEYS, tt), 0).astype(F32)
    _top16(s1, rid, None, v1_ref, i1_ref)
    _top16(s2, rid, None, v2_ref, i2_ref)
    npairs = len(_PEER_PAIRS)
    cand_ref[npairs:, :] = jnp.full((_PEER_CAND_ROWS - npairs, tt), -jnp.inf, F32)
    cidx_ref[npairs:, :] = jnp.full((_PEER_CAND_ROWS - npairs, tt), -1.0, F32)
    for r, (a, b) in enumerate(_PEER_PAIRS):
        cand_ref[r:r + 1, :] = v1_ref[a:a + 1, :] + v2_ref[b:b + 1, :]
        cidx_ref[r:r + 1, :] = i1_ref[a:a + 1, :] * float(PEER_NKEYS) + i2_ref[b:b + 1, :]
    rid2 = lax.broadcasted_iota(jnp.int32, (_PEER_CAND_ROWS, tt), 0).astype(F32)
    _top16(cand_ref[...], rid2, cidx_ref[...], vc_ref, ic_ref)
    sc = vc_ref[...]
    e = jnp.exp(sc - sc[0:1, :])
    g_ref[...] = e / jnp.sum(e, axis=0, keepdims=True)
    idx_ref[...] = ic_ref[...].astype(jnp.int32)


def _peer_route(q, k1, k2):
    t = q.shape[0]
    tt = PEER_TT
    dk = 2 * PEER_NKEYS
    kspec = pl.BlockSpec((None, PEER_NKEYS, dk // 2), lambda i, h: (h, 0, 0))
    ospec = pl.BlockSpec((None, PEER_TOPK, tt), lambda i, h: (i, h, 0))
    sm = pltpu.VMEM((PEER_TOPK, tt), F32)
    return pl.pallas_call(
        _route_kernel,
        grid=(t // tt, PEER_HEADS),
        in_specs=[pl.BlockSpec((tt, dk), lambda i, h: (i, h)), kspec, kspec],
        out_specs=[ospec, ospec],
        out_shape=[jax.ShapeDtypeStruct((t // tt, PEER_HK, tt), jnp.int32),
                   jax.ShapeDtypeStruct((t // tt, PEER_HK, tt), F32)],
        scratch_shapes=[sm, sm, sm, sm, sm, sm,
                        pltpu.VMEM((_PEER_CAND_ROWS, tt), F32), pltpu.VMEM((_PEER_CAND_ROWS, tt), F32)],
        compiler_params=_cparams("parallel", "parallel"),
        name="peer_route",
    )(q, k1, k2)


def _pack_table(tab):
    e, d = tab.shape
    tb = lax.bitcast_convert_type(tab.astype(BF16), jnp.uint16).astype(jnp.uint32)
    words = tb[:, :d // 2] | (tb[:, d // 2:] << 16)
    return lax.bitcast_convert_type(words, F32)


def _sc_gather(tab, idx):
    n = idx.shape[0]
    w = tab.shape[1]
    mesh = plsc.VectorSubcoreMesh(core_axis_name="core", subcore_axis_name="subcore")

    @pl.kernel(out_type=jax.ShapeDtypeStruct((n, w), tab.dtype), mesh=mesh)
    def gather(tab_hbm, idx_hbm, out_hbm):
        def body(idx_vmem, out_vmem):
            pltpu.sync_copy(tab_hbm.at[idx_vmem.at[0]], out_vmem)

        pltpu.emit_pipeline(
            body, grid=(n // SC_WINDOW,),
            in_specs=[pl.BlockSpec((1, SC_WINDOW), index_map=lambda i: (i, 0))],
            out_specs=[pl.BlockSpec((SC_WINDOW, w), index_map=lambda i: (i, 0))],
            core_axis_name=("core", "subcore"), dimension_semantics=(pltpu.PARALLEL,), trace_scopes=False,
        )(idx_hbm, out_hbm)

    return gather(tab, idx.reshape(n // SC_WINDOW, SC_WINDOW))


def _unpack(words):
    w = pltpu.bitcast(words, jnp.uint32)
    lo = pltpu.bitcast(w << 16, F32)
    hi = pltpu.bitcast(w & jnp.uint32(0xFFFF0000), F32)
    return lo, hi


def _peer_act_kernel(h_ref, gu_ref, g_ref, coef_ref, act_ref):
    c = pl.program_id(1)
    half = h_ref.shape[1] // 2
    h_lo, h_hi = h_ref[:, :half], h_ref[:, half:]
    lane = lax.broadcasted_iota(jnp.int32, act_ref.shape, 1)

    @pl.when(c == 0)
    def _():
        act_ref[...] = jnp.zeros(act_ref.shape, F32)

    act = act_ref[...]
    for k in range(PEER_HKC):
        lo, hi = _unpack(gu_ref[k])
        a = jnp.sum(lo * h_lo + hi * h_hi, axis=-1, keepdims=True)
        act = jnp.where(lane == c * PEER_HKC + k, a, act)
    act_ref[...] = act

    @pl.when(c == pl.num_programs(1) - 1)
    def _():
        coef_ref[...] = g_ref[...].T * jax.nn.gelu(act)


def _peer_act(h, gu, g):
    t, d = h.shape
    tt = PEER_TT
    nt = t // tt
    return pl.pallas_call(
        _peer_act_kernel,
        grid=(nt, PEER_HK // PEER_HKC),
        in_specs=[pl.BlockSpec((tt, d), lambda i, c: (i, 0)),
                  pl.BlockSpec((None, PEER_HKC, tt, d // 2), lambda i, c: (i, c, 0, 0)),
                  pl.BlockSpec((None, PEER_HK, tt), lambda i, c: (i, 0, 0))],
        out_specs=pl.BlockSpec((None, tt, PEER_HK), lambda i, c: (i, 0, 0)),
        out_shape=jax.ShapeDtypeStruct((nt, tt, PEER_HK), F32),
        scratch_shapes=[pltpu.VMEM((tt, PEER_HK), F32)],
        compiler_params=_cparams("parallel", "arbitrary"),
        name="peer_act",
    )(h, gu, g)


def _peer_mix_kernel(coef_ref, gv_ref, x_ref, gt_ref, gf_ref, o_ref, acc_ref, *, final):
    c = pl.program_id(1)
    half = x_ref.shape[1] // 2

    @pl.when(c == 0)
    def _():
        acc_ref[...] = jnp.zeros(acc_ref.shape, F32)

    coef = coef_ref[...]
    lane = lax.broadcasted_iota(jnp.int32, coef.shape, 1)
    acc_lo, acc_hi = acc_ref[:, :half], acc_ref[:, half:]
    for k in range(PEER_HKC):
        ck = jnp.sum(jnp.where(lane == c * PEER_HKC + k, coef, 0.0), axis=-1, keepdims=True)
        lo, hi = _unpack(gv_ref[k])
        acc_lo = acc_lo + ck * lo
        acc_hi = acc_hi + ck * hi
    acc_ref[:, :half] = acc_lo
    acc_ref[:, half:] = acc_hi

    @pl.when(c == pl.num_programs(1) - 1)
    def _():
        x = x_ref[...] + gt_ref[...] * acc_ref[...]
        if final:
            x = x * lax.rsqrt(jnp.mean(x * x, axis=-1, keepdims=True) + EPS) * gf_ref[...]
        o_ref[...] = x


def _peer_mix(coef, gv, x, gt, g_final, final):
    t, d = x.shape
    tt = PEER_TT
    nt = t // tt
    if gt.ndim == 3:
        per_seq = nt // gt.shape[0]
        gt_spec = pl.BlockSpec((None, 1, d), lambda i, c: (i // per_seq, 0, 0))
    else:
        gt_spec = pl.BlockSpec((tt, d), lambda i, c: (i, 0))
    return pl.pallas_call(
        functools.partial(_peer_mix_kernel, final=final),
        grid=(nt, PEER_HK // PEER_HKC),
        in_specs=[pl.BlockSpec((None, tt, PEER_HK), lambda i, c: (i, 0, 0)),
                  pl.BlockSpec((None, PEER_HKC, tt, d // 2), lambda i, c: (i, c, 0, 0)),
                  pl.BlockSpec((tt, d), lambda i, c: (i, 0)), gt_spec, _full((1, d))],
        out_specs=pl.BlockSpec((tt, d), lambda i, c: (i, 0)),
        out_shape=jax.ShapeDtypeStruct((t, d), F32),
        scratch_shapes=[pltpu.VMEM((tt, d), F32)],
        compiler_params=_cparams("parallel", "arbitrary"),
        name="peer_mix",
    )(coef, gv, x, gt, g_final.reshape(1, d))


def _peer(x, g_ffn, sc2, sh2, gt2, w_q, k1, k2, u_pack, v_pack, g_final, final, tl):
    b, l, d = x.shape
    t = b * l
    q, h = _norm_mod_matmul(x, g_ffn, sc2, sh2, w_q, tl, want_h=True)
    idx, g = _peer_route(q.reshape(t, -1), k1, k2)
    nt = t // PEER_TT
    flat = idx.reshape(-1)
    gu = _sc_gather(u_pack, flat).reshape(nt, PEER_HK, PEER_TT, d // 2)
    coef = _peer_act(h.reshape(t, d), gu, g)
    gv = _sc_gather(v_pack, flat).reshape(nt, PEER_HK, PEER_TT, d // 2)
    if l % PEER_TT:
        gt2 = jnp.broadcast_to(gt2, (b, l, d)).reshape(t, d)
    return _peer_mix(coef, gv, x.reshape(t, d), gt2, g_final, final).reshape(b, l, d)


def _trunk(x, mod, pos0, past, p, tl):
    b, l, d = x.shape
    outs = {}
    for layer in range(DEPTH):
        sh1, sc1, gt1, sh2, sc2, gt2 = [m[:, None, :] for m in jnp.split(mod[layer], 6, axis=-1)]
        if layer % 2 == 0:
            e = layer // 2
            y_in = _norm_mod_matmul(x, p['g_mix'][layer], sc1, sh1, p['w_in_e'][e].astype(BF16), tl)
            if past is None:
                h0_re = jnp.zeros((b, S5_GROUPS, S5_STATE), F32)
                h0_im = jnp.zeros((b, S5_GROUPS, S5_STATE), F32)
            else:
                h0_re, h0_im = past['s5_re'][e], past['s5_im'][e]
            ya, hr, hi = _s5(y_in, h0_re, h0_im, p['s5'][e], p['s5_d'][e], p['w_glu'][e], p['b_glu'][e], tl)
            yb, vn = _gmlp(y_in, p['gm_g_v'][e], p['gm_w_s'][e], p['gm_b_s'][e], tl, want_vn=past is not None)
            x = _proj_residual(x, gt1, ya, yb, p['w_out_e'][e], tl)
            outs.update(s5_re=hr, s5_im=hi, gm_v=vn)
        else:
            o = layer // 2
            w_in = jnp.pad(p['w_in_o'][o], ((0, 0), (0, 2048 - p['w_in_o'][o].shape[1]))).astype(BF16)
            y_in = _norm_mod_matmul(x, p['g_mix'][layer], sc1, sh1, w_in, tl)
            qd, kd, qm, lat, kpe = _odd_prep(y_in, pos0, p['mla_g_q'][o], p['mla_w_uq'][o], p['mla_g_kv'][o], tl)
            v_new = y_in[:, :, 2 * 512:3 * 512]
            lam_init = 0.8 - 0.6 * math.exp(-0.3 * layer)
            lam = (jnp.exp(jnp.sum(p['da_lq1'][o] * p['da_lk1'][o])) - jnp.exp(jnp.sum(p['da_lq2'][o] * p['da_lk2'][o]))
                   + lam_init).reshape(1).astype(F32)
            g_sub = p['da_g_sub'][o].reshape(1, DA_DV)
            if past is None:
                k_all, v_all, v_col, lat_all, kpe_all = kd, y_in, 2, lat, kpe
                tq = tk = tl
            else:
                k_all = jnp.concatenate([past['da_k'][o].reshape(b, -1, 512), kd], axis=1)
                v_all = jnp.concatenate([past['da_v'][o].reshape(b, -1, 512), v_new], axis=1)
                lat_all = jnp.concatenate([past['mla_lat'][o], lat], axis=1)
                kpe_all = jnp.concatenate([past['mla_kpe'][o], kpe], axis=1)
                v_col, tq, tk = 0, l, k_all.shape[1]
            yc = _flash(qd, k_all, v_all, lam, g_sub, nhm=2 * DA_HEADS, dqk=DA_DH, maps=2, scale=DA_DH ** -0.5,
                        pos0=pos0, tq=tq, tk=tk, out_scale=1.0 - lam_init, v_col=v_col)
            km, vm = _mla_kv(lat_all, kpe_all, p['mla_w_ukv'][o], tk)
            yd = _flash(qm, km, vm, lam, g_sub, nhm=MLA_HEADS, dqk=LANES, maps=1,
                        scale=(MLA_NOPE + MLA_ROPE) ** -0.5, pos0=pos0, tq=tq, tk=tk, out_scale=1.0)
            x = _proj_residual(x, gt1, yc, yd, p['w_out_o'][o], tl)
            outs.update(da_k=kd.reshape(b, l, DA_HEADS, 2 * DA_DH), da_v=v_new.reshape(b, l, DA_HEADS, DA_DV),
                        mla_lat=lat, mla_kpe=kpe)
        x = _peer(x, p['g_ffn'][layer], sc2, sh2, gt2, p['peer_w_q'][layer].astype(BF16), p['peer_k1'][layer],
                  p['peer_k2'][layer], p['u_pack'][layer], p['v_pack'][layer], p['g_final'],
                  final=layer == DEPTH - 1, tl=tl)
    return x, outs


def kernel(x_prompt, x_sample, c_prompt, c_sample, state_s5_re, state_s5_im, cache_da_k, cache_da_v, cache_mla_latent, cache_mla_kpe, w_ada, b_ada, g_mix, g_ffn, g_final, w_in_e, w_out_e, s5_lam_re, s5_lam_im, s5_log_dt, s5_b_re, s5_b_im, s5_c_re, s5_c_im, s5_d, w_glu, b_glu, gm_g_v, gm_w_s, gm_b_s, w_in_o, w_out_o, da_lq1, da_lk1, da_lq2, da_lk2, da_g_sub, mla_g_q, mla_w_uq, mla_g_kv, mla_w_ukv, peer_w_q, peer_k1, peer_k2, peer_u, peer_v):
    p = dict(g_mix=g_mix, g_ffn=g_ffn, g_final=g_final, w_in_e=w_in_e, w_out_e=w_out_e, s5_d=s5_d, w_glu=w_glu,
             b_glu=b_glu, gm_g_v=gm_g_v, gm_w_s=gm_w_s, gm_b_s=gm_b_s, w_in_o=w_in_o, w_out_o=w_out_o,
             da_lq1=da_lq1, da_lk1=da_lk1, da_lq2=da_lq2, da_lk2=da_lk2, da_g_sub=da_g_sub, mla_g_q=mla_g_q,
             mla_w_uq=mla_w_uq, mla_g_kv=mla_g_kv, mla_w_ukv=mla_w_ukv, peer_w_q=peer_w_q, peer_k1=peer_k1,
             peer_k2=peer_k2)
    n_even = (DEPTH + 1) // 2
    p['s5'] = [_s5_params(s5_lam_re[e], s5_lam_im[e], s5_log_dt[e], s5_b_re[e], s5_b_im[e], s5_c_re[e], s5_c_im[e])
               for e in range(n_even)]
    p['u_pack'] = [_pack_table(peer_u[layer]) for layer in range(DEPTH)]
    p['v_pack'] = [_pack_table(peer_v[layer]) for layer in range(DEPTH)]
    past = dict(s5_re=state_s5_re, s5_im=state_s5_im, da_k=cache_da_k, da_v=cache_da_v,
                mla_lat=cache_mla_latent, mla_kpe=cache_mla_kpe)
    nb = x_prompt.shape[0]
    mod = _ada(jnp.concatenate([c_prompt, c_sample], axis=0), w_ada, b_ada)
    past_len = cache_da_k.shape[2]
    parts = [_trunk(x_prompt[s:s + 1], mod[:, s:s + 1], 0, None, p, tl=512) for s in range(nb)]
    y_p = jnp.concatenate([y for y, _ in parts], axis=0)
    o_p = {k: jnp.concatenate([o[k] for _, o in parts], axis=0) for k in parts[0][1] if parts[0][1][k] is not None}
    y_s, o_s = _trunk(x_sample, mod[:, nb:], past_len, past, p, tl=x_sample.shape[1])
    st = lambda a: a[None]
    return (y_p, y_s, st(o_p['s5_re']), st(o_p['s5_im']), st(o_s['s5_re']), st(o_s['s5_im']), st(o_s['gm_v']),
            st(o_p['da_k']), st(o_p['da_v']), st(o_s['da_k']), st(o_s['da_v']),
            st(o_p['mla_lat']), st(o_p['mla_kpe']), st(o_s['mla_lat']), st(o_s['mla_kpe']))
```

```python
import functools
import math

import jax
import jax.numpy as jnp
from jax import lax
from jax.experimental import pallas as pl
from jax.experimental.pallas import tpu as pltpu
from jax.experimental.pallas import tpu_sc as plsc

F32 = jnp.float32
BF16 = jnp.bfloat16

D_MODEL = 1024
DEPTH = 2
CHUNK = 64
ROPE_THETA = 500000.0
EPS = 1e-6
NEG = -1e30

S5_WIDTH = 512
S5_GROUP = 16
S5_GROUPS = 32
S5_STATE = 64
GM_WIDTH = 512
GM_HEADS = 4
GM_CHUNK = 128
DA_HEADS = 4
DA_DH = 64
DA_DV = 128
DA_ROPE = 16
MLA_HEADS = 4
MLA_Q_RANK = 256
MLA_KV_RANK = 128
MLA_NOPE = 64
MLA_ROPE = 32
MLA_DV = 128
PEER_HEADS = 8
PEER_NKEYS = 128
PEER_TOPK = 16
PEER_HK = PEER_HEADS * PEER_TOPK

LANES = 128
VMEM_LIMIT = 48 << 20
PEER_TT = 256
PEER_HKC = 16
SC_WINDOW = 64
FLASH_RQ = 512

_HI = lax.Precision.HIGHEST


def _cparams(*sem):
    return pltpu.CompilerParams(dimension_semantics=sem, vmem_limit_bytes=VMEM_LIMIT)


def _full(shape):
    n = len(shape)
    return pl.BlockSpec(shape, lambda *_: (0,) * n)


def _ada_kernel(c_ref, w_ref, b_ref, o_ref):
    c = c_ref[...]
    s = c * jax.nn.sigmoid(c)
    o_ref[...] = jnp.dot(s, w_ref[...], precision=_HI, preferred_element_type=F32) + b_ref[...]


def _ada(c_all, w_ada, b_ada):
    r = c_all.shape[0]
    tn = 1536
    return pl.pallas_call(
        _ada_kernel,
        grid=(DEPTH, 6 * D_MODEL // tn),
        in_specs=[_full((r, D_MODEL)),
                  pl.BlockSpec((None, D_MODEL, tn), lambda l, j: (l, 0, j)),
                  pl.BlockSpec((None, 1, tn), lambda l, j: (l, 0, j))],
        out_specs=pl.BlockSpec((None, r, tn), lambda l, j: (l, 0, j)),
        out_shape=jax.ShapeDtypeStruct((DEPTH, r, 6 * D_MODEL), F32),
        compiler_params=_cparams("parallel", "parallel"),
        name="ada",
    )(c_all, w_ada, b_ada.reshape(DEPTH, 1, 6 * D_MODEL))


def _nmm_kernel(x_ref, g_ref, sc_ref, sh_ref, w_ref, o_ref, *h_ref):
    x = x_ref[...]
    ms = jnp.mean(x * x, axis=-1, keepdims=True)
    h = x * lax.rsqrt(ms + EPS) * g_ref[...] * (1.0 + sc_ref[...]) + sh_ref[...]
    o_ref[...] = jnp.dot(h.astype(BF16), w_ref[...], preferred_element_type=F32)
    if h_ref:
        h_ref[0][...] = h


def _norm_mod_matmul(x, g, sc, sh, w, tl, want_h=False):
    b, l, d = x.shape
    n = w.shape[1]
    row = pl.BlockSpec((None, 1, d), lambda bi, i: (bi, 0, 0))
    out_shape = [jax.ShapeDtypeStruct((b, l, n), F32)]
    out_specs = [pl.BlockSpec((None, tl, n), lambda bi, i: (bi, i, 0))]
    if want_h:
        out_shape.append(jax.ShapeDtypeStruct((b, l, d), F32))
        out_specs.append(pl.BlockSpec((None, tl, d), lambda bi, i: (bi, i, 0)))
    res = pl.pallas_call(
        _nmm_kernel,
        grid=(b, l // tl),
        in_specs=[pl.BlockSpec((None, tl, d), lambda bi, i: (bi, i, 0)), _full((1, d)), row, row, _full((d, n))],
        out_specs=out_specs, out_shape=out_shape,
        compiler_params=_cparams("parallel", "parallel"),
        name="norm_mod_matmul",
    )(x, g.reshape(1, d), sc, sh, w)
    return res if want_h else res[0]


def _s5_kernel(xa_ref, h0r_ref, h0i_ref, bblk_ref, ar_ref, ai_ref, cr_ref, ci_ref, d_ref, wglu_ref, bglu_ref,
               ya_ref, hr_out_ref, hi_out_ref, bur, bui, st_r, st_i, *, tl):
    nj = S5_WIDTH // LANES
    half = 8 * S5_STATE

    @pl.when(pl.program_id(1) == 0)
    def _():
        st_r[...] = h0r_ref[...]
        st_i[...] = h0i_ref[...]

    xa = xa_ref[...]
    for j in range(nj):
        bu = jnp.dot(xa[:, j * LANES:(j + 1) * LANES], bblk_ref[j], precision=_HI, preferred_element_type=F32)
        bur[j] = bu[:, :half]
        bui[j] = bu[:, half:]

    for j0 in range(0, nj, 2):
        js = (j0, j0 + 1)
        a_r = [ar_ref[j] for j in js]
        a_i = [ai_ref[j] for j in js]

        def body(k, carry, js=js, a_r=a_r, a_i=a_i):
            carry = list(carry)
            base = pl.multiple_of(k * 8, 8)
            for s in range(8):
                t = base + s
                for q, j in enumerate(js):
                    hr, hi = carry[2 * q], carry[2 * q + 1]
                    nhr = a_r[q] * hr - a_i[q] * hi + bur[j, pl.ds(t, 1), :]
                    nhi = a_r[q] * hi + a_i[q] * hr + bui[j, pl.ds(t, 1), :]
                    bur[j, pl.ds(t, 1), :] = nhr
                    bui[j, pl.ds(t, 1), :] = nhi
                    carry[2 * q], carry[2 * q + 1] = nhr, nhi
            return tuple(carry)

        init = (st_r[js[0]], st_i[js[0]], st_r[js[1]], st_i[js[1]])
        fin = lax.fori_loop(0, tl // 8, body, init)
        st_r[js[0]], st_i[js[0]], st_r[js[1]], st_i[js[1]] = fin

    hr_out_ref[...] = st_r[...]
    hi_out_ref[...] = st_i[...]

    ys = []
    for j in range(nj):
        y = (jnp.dot(bur[j].astype(BF16), cr_ref[j], preferred_element_type=F32)
             + jnp.dot(bui[j].astype(BF16), ci_ref[j], preferred_element_type=F32))
        ys.append(y + d_ref[:, j * LANES:(j + 1) * LANES] * xa[:, j * LANES:(j + 1) * LANES])
    y = jax.nn.gelu(jnp.concatenate(ys, axis=1))
    z = jnp.dot(y.astype(BF16), wglu_ref[...], preferred_element_type=F32) + bglu_ref[...]
    ya_ref[...] = z[:, :S5_WIDTH] * jax.nn.sigmoid(z[:, S5_WIDTH:])


def _s5_params(lam_re, lam_im, log_dt, b_re, b_im, c_re, c_im):
    lam = lax.complex(lam_re.astype(F32), lam_im.astype(F32))
    dt = jnp.exp(log_dt.astype(F32))[:, None]
    a_bar = jnp.exp(lam * dt)
    b_bar = ((a_bar - 1.0) / lam)[..., None] * lax.complex(b_re.astype(F32), b_im.astype(F32))
    nj = S5_GROUPS // 8
    eye = jnp.eye(8, dtype=F32)

    def blk_b(m):
        m = m.reshape(nj, 8, S5_STATE, S5_GROUP)
        return jnp.einsum('jgpc,gh->jgchp', m, eye).reshape(nj, 8 * S5_GROUP, 8 * S5_STATE)

    def blk_c(m):
        m = m.reshape(nj, 8, S5_GROUP, S5_STATE)
        return jnp.einsum('jgcp,gh->jgphc', m, eye).reshape(nj, 8 * S5_STATE, 8 * S5_GROUP)

    bblk = jnp.concatenate([blk_b(jnp.real(b_bar)), blk_b(jnp.imag(b_bar))], axis=-1)
    cr = blk_c(c_re.astype(F32)).astype(BF16)
    ci = (-blk_c(c_im.astype(F32))).astype(BF16)
    a_r = jnp.real(a_bar).reshape(nj, 1, 8 * S5_STATE)
    a_i = jnp.imag(a_bar).reshape(nj, 1, 8 * S5_STATE)
    return bblk, a_r, a_i, cr, ci


def _s5(y_in, h0_re, h0_im, sp, d_skip, w_glu, b_glu, tl):
    b, l, _ = y_in.shape
    nj = S5_GROUPS // 8
    half = 8 * S5_STATE
    bblk, a_r, a_i, cr, ci = sp
    st = pl.BlockSpec((None, nj, 1, half), lambda bi, i: (bi, 0, 0, 0))
    ya, hr, hi = pl.pallas_call(
        functools.partial(_s5_kernel, tl=tl),
        grid=(b, l // tl),
        in_specs=[pl.BlockSpec((None, tl, S5_WIDTH), lambda bi, i: (bi, i, 0)), st, st,
                  _full(bblk.shape), _full(a_r.shape), _full(a_i.shape), _full(cr.shape), _full(ci.shape),
                  _full((1, S5_WIDTH)), _full((S5_WIDTH, 2 * S5_WIDTH)), _full((1, 2 * S5_WIDTH))],
        out_specs=[pl.BlockSpec((None, tl, S5_WIDTH), lambda bi, i: (bi, i, 0)), st, st],
        out_shape=[jax.ShapeDtypeStruct((b, l, S5_WIDTH), F32),
                   jax.ShapeDtypeStruct((b, nj, 1, half), F32), jax.ShapeDtypeStruct((b, nj, 1, half), F32)],
        scratch_shapes=[pltpu.VMEM((nj, tl, half), F32), pltpu.VMEM((nj, tl, half), F32),
                        pltpu.VMEM((nj, 1, half), F32), pltpu.VMEM((nj, 1, half), F32)],
        compiler_params=_cparams("parallel", "arbitrary"),
        name="s5",
    )(y_in, h0_re.reshape(b, nj, 1, half), h0_im.reshape(b, nj, 1, half), bblk, a_r, a_i, cr, ci,
      d_skip.reshape(1, S5_WIDTH), w_glu.astype(BF16), b_glu.reshape(1, 2 * S5_WIDTH))
    return ya, hr.reshape(b, S5_GROUPS, S5_STATE), hi.reshape(b, S5_GROUPS, S5_STATE)


def _gmlp_kernel(u_ref, v_ref, gv_ref, ws_ref, bias_ref, yb_ref, *vn_ref, tl, lc):
    hd = GM_WIDTH // GM_HEADS
    gv = jax.nn.gelu(v_ref[...])
    vn = gv * lax.rsqrt(jnp.mean(gv * gv, axis=-1, keepdims=True) + EPS) * gv_ref[...]
    if vn_ref:
        vn_ref[0][...] = vn
    gu = jax.nn.gelu(u_ref[...])
    vb = vn.astype(BF16)
    for c in range(tl // lc):
        rows = slice(c * lc, (c + 1) * lc)
        for h in range(GM_HEADS):
            cols = slice(h * hd, (h + 1) * hd)
            mixed = jnp.dot(ws_ref[h], vb[rows, cols], preferred_element_type=F32) + bias_ref[:, cols]
            yb_ref[rows, cols] = gu[rows, cols] * mixed


def _gmlp(y_in, g_v, w_s, b_s, tl, want_vn):
    b, l, _ = y_in.shape
    lc = min(GM_CHUNK, l)
    hd = GM_WIDTH // GM_HEADS
    tri = jnp.tril(jnp.ones((GM_CHUNK, GM_CHUNK), dtype=bool))
    ws = jnp.where(tri[None], w_s, 0.0)[:, :lc, :lc].astype(BF16)
    bias = jnp.repeat(jnp.transpose(b_s)[:lc], hd, axis=1)
    out_shape = [jax.ShapeDtypeStruct((b, l, GM_WIDTH), F32)]
    out_specs = [pl.BlockSpec((None, tl, GM_WIDTH), lambda bi, i: (bi, i, 0))]
    if want_vn:
        out_shape.append(jax.ShapeDtypeStruct((b, l, GM_WIDTH), F32))
        out_specs.append(pl.BlockSpec((None, tl, GM_WIDTH), lambda bi, i: (bi, i, 0)))
    res = pl.pallas_call(
        functools.partial(_gmlp_kernel, tl=tl, lc=lc),
        grid=(b, l // tl),
        in_specs=[pl.BlockSpec((None, tl, GM_WIDTH), lambda bi, i: (bi, i, 1)),
                  pl.BlockSpec((None, tl, GM_WIDTH), lambda bi, i: (bi, i, 2)),
                  _full((1, GM_WIDTH)), _full(ws.shape), _full(bias.shape)],
        out_specs=out_specs, out_shape=out_shape,
        compiler_params=_cparams("parallel", "parallel"),
        name="gmlp",
    )(y_in, y_in, g_v.reshape(1, GM_WIDTH), ws, bias)
    return (res[0], res[1]) if want_vn else (res[0], None)


def _proj_res_kernel(x_ref, gt_ref, a_ref, b_ref, wa_ref, wb_ref, o_ref):
    mix = (jnp.dot(a_ref[...].astype(BF16), wa_ref[...], preferred_element_type=F32)
           + jnp.dot(b_ref[...].astype(BF16), wb_ref[...], preferred_element_type=F32))
    o_ref[...] = x_ref[...] + gt_ref[...] * mix


def _proj_residual(x, gt, a, bm, w, tl):
    b, l, d = x.shape
    ka = a.shape[-1]
    wa, wb = w[:ka].astype(BF16), w[ka:].astype(BF16)
    blk = lambda n: pl.BlockSpec((None, tl, n), lambda bi, i: (bi, i, 0))
    return pl.pallas_call(
        _proj_res_kernel,
        grid=(b, l // tl),
        in_specs=[blk(d), pl.BlockSpec((None, 1, d), lambda bi, i: (bi, 0, 0)), blk(ka), blk(bm.shape[-1]),
                  _full(wa.shape), _full(wb.shape)],
        out_specs=blk(d), out_shape=jax.ShapeDtypeStruct((b, l, d), F32),
        compiler_params=_cparams("parallel", "parallel"),
        name="proj_residual",
    )(x, gt, a, bm, wa, wb)


def _rope_tables(pos0, l):
    pos = (pos0 + jnp.arange(l, dtype=jnp.int32)).astype(F32)[:, None]
    lane = jnp.arange(LANES)

    def table(period, start, rot):
        half = rot // 2
        r = lane % period - start
        inside = (r >= 0) & (r < rot)
        k = jnp.where(inside, r % half, 0)
        inv = ROPE_THETA ** (-k.astype(F32) * 2.0 / rot)
        ang = pos * inv[None, :]
        cos = jnp.where(inside[None], jnp.cos(ang), 1.0)
        sgn = jnp.where(r < half, -1.0, 1.0)
        sin = jnp.where(inside[None], jnp.sin(ang) * sgn[None], 0.0)
        return cos, sin

    return table(DA_DH, 0, DA_ROPE), table(LANES, MLA_NOPE, MLA_ROPE), table(LANES, 0, MLA_ROPE)


def _rotate(x, cos, sin, period, start, rot):
    half = rot // 2
    n = x.shape[-1]
    reps = n // LANES
    lane = (lax.broadcasted_iota(jnp.int32, x.shape, 1) & (period - 1)) - start
    up = pltpu.roll(x, n - half, 1)
    dn = pltpu.roll(x, half, 1)
    partner = jnp.where(lane < half, up, dn)
    if reps > 1:
        cos = jnp.concatenate([cos] * reps, axis=1)
        sin = jnp.concatenate([sin] * reps, axis=1)
    return x * cos + partner * sin


def _odd_prep_kernel(q_ref, k_ref, cq_ref, ckv_ref, kpe_ref, cd_ref, sd_ref, cm_ref, sm_ref, ck_ref, sk_ref,
                     gq_ref, wuq_ref, gkv_ref, qd_ref, kd_ref, qm_ref, lat_ref, kpe_out_ref):
    cd, sd = cd_ref[...], sd_ref[...]
    qd_ref[...] = _rotate(q_ref[...], cd, sd, DA_DH, 0, DA_ROPE).astype(BF16)
    kd_ref[...] = _rotate(k_ref[...], cd, sd, DA_DH, 0, DA_ROPE)
    cq = cq_ref[...]
    cqn = cq * lax.rsqrt(jnp.mean(cq * cq, axis=-1, keepdims=True) + EPS) * gq_ref[...]
    qm = jnp.dot(cqn.astype(BF16), wuq_ref[...], preferred_element_type=F32)
    qm_ref[...] = _rotate(qm, cm_ref[...], sm_ref[...], LANES, MLA_NOPE, MLA_ROPE).astype(BF16)
    ckv = ckv_ref[...]
    lat_ref[...] = ckv * lax.rsqrt(jnp.mean(ckv * ckv, axis=-1, keepdims=True) + EPS) * gkv_ref[...]
    kpe = _rotate(kpe_ref[...], ck_ref[...], sk_ref[...], LANES, 0, MLA_ROPE)
    kpe_out_ref[...] = kpe[:, :MLA_ROPE]


def _odd_prep(y_in, pos0, g_q, w_uq, g_kv, tl):
    b, l, _ = y_in.shape
    (cd, sd), (cm, sm), (ck, sk) = _rope_tables(pos0, l)
    per = MLA_NOPE + MLA_ROPE
    wuq = jnp.pad(w_uq.reshape(MLA_Q_RANK, MLA_HEADS, per), ((0, 0), (0, 0), (0, LANES - per)))
    wuq = wuq.reshape(MLA_Q_RANK, MLA_HEADS * LANES).astype(BF16)
    col = lambda w, j: pl.BlockSpec((None, tl, w), lambda bi, i: (bi, i, j))
    tab = pl.BlockSpec((tl, LANES), lambda bi, i: (i, 0))
    out = lambda w: pl.BlockSpec((None, tl, w), lambda bi, i: (bi, i, 0))
    return pl.pallas_call(
        _odd_prep_kernel,
        grid=(b, l // tl),
        in_specs=[col(512, 0), col(512, 1), col(256, 6), col(128, 14), col(128, 15),
                  tab, tab, tab, tab, tab, tab,
                  _full((1, MLA_Q_RANK)), _full(wuq.shape), _full((1, MLA_KV_RANK))],
        out_specs=[out(512), out(512), out(512), out(MLA_KV_RANK), out(MLA_ROPE)],
        out_shape=[jax.ShapeDtypeStruct((b, l, 512), BF16), jax.ShapeDtypeStruct((b, l, 512), F32),
                   jax.ShapeDtypeStruct((b, l, 512), BF16), jax.ShapeDtypeStruct((b, l, MLA_KV_RANK), F32),
                   jax.ShapeDtypeStruct((b, l, MLA_ROPE), F32)],
        compiler_params=_cparams("parallel", "parallel"),
        name="odd_prep",
    )(y_in, y_in, y_in, y_in, y_in, cd, sd, cm, sm, ck, sk,
      g_q.reshape(1, MLA_Q_RANK), wuq, g_kv.reshape(1, MLA_KV_RANK))


def _mla_kv_kernel(lat_ref, kpe_ref, wk_ref, wv_ref, place_ref, k_ref, v_ref):
    lat = lat_ref[...].astype(BF16)
    k = (jnp.dot(lat, wk_ref[...], preferred_element_type=F32)
         + jnp.dot(kpe_ref[...].astype(BF16), place_ref[...], preferred_element_type=F32))
    k_ref[...] = k.astype(BF16)
    v_ref[...] = jnp.dot(lat, wv_ref[...], preferred_element_type=F32).astype(BF16)


def _mla_kv(lat_all, kpe_all, w_ukv, tk):
    b, k, _ = lat_all.shape
    per = MLA_NOPE + MLA_DV
    w3 = w_ukv.reshape(MLA_KV_RANK, MLA_HEADS, per)
    wk = jnp.pad(w3[:, :, :MLA_NOPE], ((0, 0), (0, 0), (0, LANES - MLA_NOPE))).reshape(MLA_KV_RANK, MLA_HEADS * LANES)
    wv = w3[:, :, MLA_NOPE:].reshape(MLA_KV_RANK, MLA_HEADS * MLA_DV)
    place = jnp.pad(jnp.eye(MLA_ROPE, dtype=F32), ((0, 0), (MLA_NOPE, LANES - MLA_NOPE - MLA_ROPE)))
    place = jnp.tile(place, (1, MLA_HEADS))
    blk = lambda w: pl.BlockSpec((None, tk, w), lambda bi, i: (bi, i, 0))
    return pl.pallas_call(
        _mla_kv_kernel,
        grid=(b, k // tk),
        in_specs=[blk(MLA_KV_RANK), blk(MLA_ROPE), _full(wk.shape), _full(wv.shape), _full(place.shape)],
        out_specs=[blk(512), blk(512)],
        out_shape=[jax.ShapeDtypeStruct((b, k, 512), BF16), jax.ShapeDtypeStruct((b, k, 512), BF16)],
        compiler_params=_cparams("parallel", "parallel"),
        name="mla_kv",
    )(lat_all, kpe_all, wk.astype(BF16), wv.astype(BF16), place.astype(BF16))


def _visible_limit(pos_last):
    shift = CHUNK.bit_length() - 1
    return ((pos_last >> shift) + 1) << shift


def _fold_lanes(x, op):
    n = x.shape[-1]
    if n % LANES:
        return x
    out = x[:, :LANES]
    for c in range(1, n // LANES):
        out = op(out, x[:, c * LANES:(c + 1) * LANES])
    return out


def _flash_kernel(lam_ref, q_ref, k_ref, v_ref, g_ref, o_ref, m_ref, l_ref, acc_ref, *,
                  nhm, dqk, maps, scale, pos0, tq, tk, nkv, out_scale, rq):
    i, j = pl.program_id(1), pl.program_id(2)

    @pl.when(j == 0)
    def _():
        m_ref[...] = jnp.full(m_ref.shape, NEG, F32)
        l_ref[...] = jnp.zeros(l_ref.shape, F32)
        acc_ref[...] = jnp.zeros(acc_ref.shape, F32)

    c2 = scale * math.log2(math.e)
    visible = j * tk < _visible_limit(pos0 + (i + 1) * tq - 1)
    unmasked = (j + 1) * tk <= _visible_limit(pos0 + i * tq)

    def process(masked):
        for hm in range(nhm):
            k = k_ref[:, hm * dqk:(hm + 1) * dqk].astype(BF16)
            h = hm // maps
            v = v_ref[:, h * DA_DV:(h + 1) * DA_DV].astype(BF16)

            def qblock(r, carry, hm=hm, k=k, v=v):
                r0 = r * rq if isinstance(r, int) else pl.multiple_of(r * rq, rq)
                rows = pl.ds(r0, rq)
                q = q_ref[rows, hm * dqk:(hm + 1) * dqk].astype(BF16)
                s = lax.dot_general(q, k, (((1,), (1,)), ((), ())), preferred_element_type=F32)
                if masked:
                    q_pos = pos0 + i * tq + r0 + lax.broadcasted_iota(jnp.int32, (rq, tk), 0)
                    k_pos = j * tk + lax.broadcasted_iota(jnp.int32, (rq, tk), 1)
                    s = jnp.where(k_pos < _visible_limit(q_pos), s, NEG)
                m_prev = m_ref[hm, rows, :]
                m_new = jnp.maximum(m_prev, jnp.max(_fold_lanes(s, jnp.maximum), axis=-1, keepdims=True))
                p = jnp.exp2((s - m_new) * c2)
                alpha = jnp.exp2((m_prev - m_new) * c2)
                l_ref[hm, rows, :] = (alpha * l_ref[hm, rows, :]
                                      + jnp.sum(_fold_lanes(p, jnp.add), axis=-1, keepdims=True))
                acc_ref[hm, rows, :] = (alpha * acc_ref[hm, rows, :]
                                        + jnp.dot(p.astype(BF16), v, preferred_element_type=F32))
                m_ref[hm, rows, :] = m_new
                return carry

            if tq == rq:
                qblock(0, 0)
            else:
                lax.fori_loop(0, tq // rq, qblock, 0)

    pl.when(visible & unmasked)(lambda: process(False))
    pl.when(visible & jnp.logical_not(unmasked))(lambda: process(True))

    @pl.when(j == nkv - 1)
    def _():
        for h in range(nhm // maps):
            if maps == 2:
                o = acc_ref[2 * h] / l_ref[2 * h] - lam_ref[0] * (acc_ref[2 * h + 1] / l_ref[2 * h + 1])
                o = o * lax.rsqrt(jnp.mean(o * o, axis=-1, keepdims=True) + EPS) * g_ref[...] * out_scale
            else:
                o = acc_ref[h] / l_ref[h]
            o_ref[:, h * DA_DV:(h + 1) * DA_DV] = o


def _flash(q, k, v, lam, g_sub, *, nhm, dqk, maps, scale, pos0, tq, tk, out_scale, v_col=0):
    b, l, _ = q.shape
    kk = k.shape[1]
    nkv = kk // tk
    nh = nhm // maps

    def kv_idx(bi, i, j):
        last = (_visible_limit(pos0 + (i + 1) * tq - 1) - 1) // tk
        return jnp.minimum(j, last)

    return pl.pallas_call(
        functools.partial(_flash_kernel, nhm=nhm, dqk=dqk, maps=maps, scale=scale, pos0=pos0, tq=tq, tk=tk,
                          nkv=nkv, out_scale=out_scale, rq=min(FLASH_RQ, tq)),
        grid=(b, l // tq, nkv),
        in_specs=[pl.BlockSpec(memory_space=pltpu.SMEM),
                  pl.BlockSpec((None, tq, nhm * dqk), lambda bi, i, j: (bi, i, 0)),
                  pl.BlockSpec((None, tk, nhm * dqk), lambda bi, i, j: (bi, kv_idx(bi, i, j), 0)),
                  pl.BlockSpec((None, tk, nh * DA_DV), lambda bi, i, j: (bi, kv_idx(bi, i, j), v_col)),
                  _full((1, DA_DV))],
        out_specs=pl.BlockSpec((None, tq, nh * DA_DV), lambda bi, i, j: (bi, i, 0)),
        out_shape=jax.ShapeDtypeStruct((b, l, nh * DA_DV), F32),
        scratch_shapes=[pltpu.VMEM((nhm, tq, 1), F32), pltpu.VMEM((nhm, tq, 1), F32),
                        pltpu.VMEM((nhm, tq, DA_DV), F32)],
        compiler_params=_cparams("parallel", "parallel", "arbitrary"),
        name="flash_da" if maps == 2 else "flash_mla",
    )(lam, q, k, v, g_sub)


def _top16(s, rid, payload, val_ref, pay_ref):
    big = float(s.shape[0])
    for it in range(PEER_TOPK):
        m = jnp.max(s, axis=0, keepdims=True)
        am = jnp.min(jnp.where(s == m, rid, big), axis=0, keepdims=True)
        sel = rid == am
        val_ref[it:it + 1, :] = m
        if payload is None:
            pay_ref[it:it + 1, :] = am
        else:
            pay_ref[it:it + 1, :] = jnp.max(jnp.where(sel, payload, -1.0), axis=0, keepdims=True)
        s = jnp.where(sel, -jnp.inf, s)


_PEER_PAIRS = [(a, b) for a in range(PEER_TOPK) for b in range(PEER_TOPK) if (a + 1) * (b + 1) <= PEER_TOPK]
_PEER_CAND_ROWS = -(-len(_PEER_PAIRS) // 8) * 8


def _route_kernel(q_ref, k1_ref, k2_ref, idx_ref, g_ref, v1_ref, i1_ref, v2_ref, i2_ref, vc_ref, ic_ref,
                  cand_ref, cidx_ref):
    tt = q_ref.shape[0]
    half = q_ref.shape[1] // 2
    nt = (((1,), (1,)), ((), ()))
    s1 = lax.dot_general(k1_ref[...].astype(BF16), q_ref[:, :half].astype(BF16), nt, preferred_element_type=F32)
    s2 = lax.dot_general(k2_ref[...].astype(BF16), q_ref[:, half:].astype(BF16), nt, preferred_element_type=F32)
    rid = lax.broadcasted_iota(jnp.int32, (PEER_NKEYS, tt), 0).astype(F32)
    _top16(s1, rid, None, v1_ref, i1_ref)
    _top16(s2, rid, None, v2_ref, i2_ref)
    npairs = len(_PEER_PAIRS)
    cand_ref[npairs:, :] = jnp.full((_PEER_CAND_ROWS - npairs, tt), -jnp.inf, F32)
    cidx_ref[npairs:, :] = jnp.full((_PEER_CAND_ROWS - npairs, tt), -1.0, F32)
    for r, (a, b) in enumerate(_PEER_PAIRS):
        cand_ref[r:r + 1, :] = v1_ref[a:a + 1, :] + v2_ref[b:b + 1, :]
        cidx_ref[r:r + 1, :] = i1_ref[a:a + 1, :] * float(PEER_NKEYS) + i2_ref[b:b + 1, :]
    rid2 = lax.broadcasted_iota(jnp.int32, (_PEER_CAND_ROWS, tt), 0).astype(F32)
    _top16(cand_ref[...], rid2, cidx_ref[...], vc_ref, ic_ref)
    sc = vc_ref[...]
    e = jnp.exp(sc - sc[0:1, :])
    g_ref[...] = e / jnp.sum(e, axis=0, keepdims=True)
    idx_ref[...] = ic_ref[...].astype(jnp.int32)


def _peer_route(q, k1, k2):
    t = q.shape[0]
    tt = PEER_TT
    dk = 2 * PEER_NKEYS
    kspec = pl.BlockSpec((None, PEER_NKEYS, dk // 2), lambda i, h: (h, 0, 0))
    ospec = pl.BlockSpec((None, PEER_TOPK, tt), lambda i, h: (i, h, 0))
    sm = pltpu.VMEM((PEER_TOPK, tt), F32)
    return pl.pallas_call(
        _route_kernel,
        grid=(t // tt, PEER_HEADS),
        in_specs=[pl.BlockSpec((tt, dk), lambda i, h: (i, h)), kspec, kspec],
        out_specs=[ospec, ospec],
        out_shape=[jax.ShapeDtypeStruct((t // tt, PEER_HK, tt), jnp.int32),
                   jax.ShapeDtypeStruct((t // tt, PEER_HK, tt), F32)],
        scratch_shapes=[sm, sm, sm, sm, sm, sm,
                        pltpu.VMEM((_PEER_CAND_ROWS, tt), F32), pltpu.VMEM((_PEER_CAND_ROWS, tt), F32)],
        compiler_params=_cparams("parallel", "parallel"),
        name="peer_route",
    )(q, k1, k2)


def _pack_table(tab):
    e, d = tab.shape
    tb = lax.bitcast_convert_type(tab.astype(BF16), jnp.uint16).astype(jnp.uint32)
    words = tb[:, :d // 2] | (tb[:, d // 2:] << 16)
    return lax.bitcast_convert_type(words, F32)


def _sc_gather(tab, idx):
    n = idx.shape[0]
    w = tab.shape[1]
    mesh = plsc.VectorSubcoreMesh(core_axis_name="core", subcore_axis_name="subcore")

    @pl.kernel(out_type=jax.ShapeDtypeStruct((n, w), tab.dtype), mesh=mesh)
    def gather(tab_hbm, idx_hbm, out_hbm):
        def body(idx_vmem, out_vmem):
            pltpu.sync_copy(tab_hbm.at[idx_vmem.at[0]], out_vmem)

        pltpu.emit_pipeline(
            body, grid=(n // SC_WINDOW,),
            in_specs=[pl.BlockSpec((1, SC_WINDOW), index_map=lambda i: (i, 0))],
            out_specs=[pl.BlockSpec((SC_WINDOW, w), index_map=lambda i: (i, 0))],
            core_axis_name=("core", "subcore"), dimension_semantics=(pltpu.PARALLEL,), trace_scopes=False,
        )(idx_hbm, out_hbm)

    return gather(tab, idx.reshape(n // SC_WINDOW, SC_WINDOW))


def _unpack(words):
    w = pltpu.bitcast(words, jnp.uint32)
    lo = pltpu.bitcast(w << 16, F32)
    hi = pltpu.bitcast(w & jnp.uint32(0xFFFF0000), F32)
    return lo, hi


def _peer_act_kernel(h_ref, gu_ref, g_ref, coef_ref, act_ref):
    c = pl.program_id(1)
    half = h_ref.shape[1] // 2
    h_lo, h_hi = h_ref[:, :half], h_ref[:, half:]
    lane = lax.broadcasted_iota(jnp.int32, act_ref.shape, 1)

    @pl.when(c == 0)
    def _():
        act_ref[...] = jnp.zeros(act_ref.shape, F32)

    act = act_ref[...]
    for k in range(PEER_HKC):
        lo, hi = _unpack(gu_ref[k])
        a = jnp.sum(lo * h_lo + hi * h_hi, axis=-1, keepdims=True)
        act = jnp.where(lane == c * PEER_HKC + k, a, act)
    act_ref[...] = act

    @pl.when(c == pl.num_programs(1) - 1)
    def _():
        coef_ref[...] = g_ref[...].T * jax.nn.gelu(act)


def _peer_act(h, gu, g):
    t, d = h.shape
    tt = PEER_TT
    nt = t // tt
    return pl.pallas_call(
        _peer_act_kernel,
        grid=(nt, PEER_HK // PEER_HKC),
        in_specs=[pl.BlockSpec((tt, d), lambda i, c: (i, 0)),
                  pl.BlockSpec((None, PEER_HKC, tt, d // 2), lambda i, c: (i, c, 0, 0)),
                  pl.BlockSpec((None, PEER_HK, tt), lambda i, c: (i, 0, 0))],
        out_specs=pl.BlockSpec((None, tt, PEER_HK), lambda i, c: (i, 0, 0)),
        out_shape=jax.ShapeDtypeStruct((nt, tt, PEER_HK), F32),
        scratch_shapes=[pltpu.VMEM((tt, PEER_HK), F32)],
        compiler_params=_cparams("parallel", "arbitrary"),
        name="peer_act",
    )(h, gu, g)


def _peer_mix_kernel(coef_ref, gv_ref, x_ref, gt_ref, gf_ref, o_ref, acc_ref, *, final):
    c = pl.program_id(1)
    half = x_ref.shape[1] // 2

    @pl.when(c == 0)
    def _():
        acc_ref[...] = jnp.zeros(acc_ref.shape, F32)

    coef = coef_ref[...]
    lane = lax.broadcasted_iota(jnp.int32, coef.shape, 1)
    acc_lo, acc_hi = acc_ref[:, :half], acc_ref[:, half:]
    for k in range(PEER_HKC):
        ck = jnp.sum(jnp.where(lane == c * PEER_HKC + k, coef, 0.0), axis=-1, keepdims=True)
        lo, hi = _unpack(gv_ref[k])
        acc_lo = acc_lo + ck * lo
        acc_hi = acc_hi + ck * hi
    acc_ref[:, :half] = acc_lo
    acc_ref[:, half:] = acc_hi

    @pl.when(c == pl.num_programs(1) - 1)
    def _():
        x = x_ref[...] + gt_ref[...] * acc_ref[...]
        if final:
            x = x * lax.rsqrt(jnp.mean(x * x, axis=-1, keepdims=True) + EPS) * gf_ref[...]
        o_ref[...] = x


def _peer_mix(coef, gv, x, gt, g_final, final):
    t, d = x.shape
    tt = PEER_TT
    nt = t // tt
    if gt.ndim == 3:
        per_seq = nt // gt.shape[0]
        gt_spec = pl.BlockSpec((None, 1, d), lambda i, c: (i // per_seq, 0, 0))
    else:
        gt_spec = pl.BlockSpec((tt, d), lambda i, c: (i, 0))
    return pl.pallas_call(
        functools.partial(_peer_mix_kernel, final=final),
        grid=(nt, PEER_HK // PEER_HKC),
        in_specs=[pl.BlockSpec((None, tt, PEER_HK), lambda i, c: (i, 0, 0)),
                  pl.BlockSpec((None, PEER_HKC, tt, d // 2), lambda i, c: (i, c, 0, 0)),
                  pl.BlockSpec((tt, d), lambda i, c: (i, 0)), gt_spec, _full((1, d))],
        out_specs=pl.BlockSpec((tt, d), lambda i, c: (i, 0)),
        out_shape=jax.ShapeDtypeStruct((t, d), F32),
        scratch_shapes=[pltpu.VMEM((tt, d), F32)],
        compiler_params=_cparams("parallel", "arbitrary"),
        name="peer_mix",
    )(coef, gv, x, gt, g_final.reshape(1, d))


def _peer(x, g_ffn, sc2, sh2, gt2, w_q, k1, k2, u_pack, v_pack, g_final, final, tl):
    b, l, d = x.shape
    t = b * l
    q, h = _norm_mod_matmul(x, g_ffn, sc2, sh2, w_q, tl, want_h=True)
    idx, g = _peer_route(q.reshape(t, -1), k1, k2)
    nt = t // PEER_TT
    flat = idx.reshape(-1)
    gu = _sc_gather(u_pack, flat).reshape(nt, PEER_HK, PEER_TT, d // 2)
    coef = _peer_act(h.reshape(t, d), gu, g)
    gv = _sc_gather(v_pack, flat).reshape(nt, PEER_HK, PEER_TT, d // 2)
    if l % PEER_TT:
        gt2 = jnp.broadcast_to(gt2, (b, l, d)).reshape(t, d)
    return _peer_mix(coef, gv, x.reshape(t, d), gt2, g_final, final).reshape(b, l, d)


def _trunk(x, mod, pos0, past, p, tl):
    b, l, d = x.shape
    outs = {}
    for layer in range(DEPTH):
        sh1, sc1, gt1, sh2, sc2, gt2 = [m[:, None, :] for m in jnp.split(mod[layer], 6, axis=-1)]
        if layer % 2 == 0:
            e = layer // 2
            y_in = _norm_mod_matmul(x, p['g_mix'][layer], sc1, sh1, p['w_in_e'][e].astype(BF16), tl)
            if past is None:
                h0_re = jnp.zeros((b, S5_GROUPS, S5_STATE), F32)
                h0_im = jnp.zeros((b, S5_GROUPS, S5_STATE), F32)
            else:
                h0_re, h0_im = past['s5_re'][e], past['s5_im'][e]
            ya, hr, hi = _s5(y_in, h0_re, h0_im, p['s5'][e], p['s5_d'][e], p['w_glu'][e], p['b_glu'][e], tl)
            yb, vn = _gmlp(y_in, p['gm_g_v'][e], p['gm_w_s'][e], p['gm_b_s'][e], tl, want_vn=past is not None)
            x = _proj_residual(x, gt1, ya, yb, p['w_out_e'][e], tl)
            outs.update(s5_re=hr, s5_im=hi, gm_v=vn)
        else:
            o = layer // 2
            w_in = jnp.pad(p['w_in_o'][o], ((0, 0), (0, 2048 - p['w_in_o'][o].shape[1]))).astype(BF16)
            y_in = _norm_mod_matmul(x, p['g_mix'][layer], sc1, sh1, w_in, tl)
            qd, kd, qm, lat, kpe = _odd_prep(y_in, pos0, p['mla_g_q'][o], p['mla_w_uq'][o], p['mla_g_kv'][o], tl)
            v_new = y_in[:, :, 2 * 512:3 * 512]
            lam_init = 0.8 - 0.6 * math.exp(-0.3 * layer)
            lam = (jnp.exp(jnp.sum(p['da_lq1'][o] * p['da_lk1'][o])) - jnp.exp(jnp.sum(p['da_lq2'][o] * p['da_lk2'][o]))
                   + lam_init).reshape(1).astype(F32)
            g_sub = p['da_g_sub'][o].reshape(1, DA_DV)
            if past is None:
                k_all, v_all, v_col, lat_all, kpe_all = kd, y_in, 2, lat, kpe
                tq = tk = tl
            else:
                k_all = jnp.concatenate([past['da_k'][o].reshape(b, -1, 512), kd], axis=1)
                v_all = jnp.concatenate([past['da_v'][o].reshape(b, -1, 512), v_new], axis=1)
                lat_all = jnp.concatenate([past['mla_lat'][o], lat], axis=1)
                kpe_all = jnp.concatenate([past['mla_kpe'][o], kpe], axis=1)
                v_col, tq, tk = 0, l, k_all.shape[1]
            yc = _flash(qd, k_all, v_all, lam, g_sub, nhm=2 * DA_HEADS, dqk=DA_DH, maps=2, scale=DA_DH ** -0.5,
                        pos0=pos0, tq=tq, tk=tk, out_scale=1.0 - lam_init, v_col=v_col)
            km, vm = _mla_kv(lat_all, kpe_all, p['mla_w_ukv'][o], tk)
            yd = _flash(qm, km, vm, lam, g_sub, nhm=MLA_HEADS, dqk=LANES, maps=1,
                        scale=(MLA_NOPE + MLA_ROPE) ** -0.5, pos0=pos0, tq=tq, tk=tk, out_scale=1.0)
            x = _proj_residual(x, gt1, yc, yd, p['w_out_o'][o], tl)
            outs.update(da_k=kd.reshape(b, l, DA_HEADS, 2 * DA_DH), da_v=v_new.reshape(b, l, DA_HEADS, DA_DV),
                        mla_lat=lat, mla_kpe=kpe)
        x = _peer(x, p['g_ffn'][layer], sc2, sh2, gt2, p['peer_w_q'][layer].astype(BF16), p['peer_k1'][layer],
                  p['peer_k2'][layer], p['u_pack'][layer], p['v_pack'][layer], p['g_final'],
                  final=layer == DEPTH - 1, tl=tl)
    return x, outs


def kernel(x_prompt, x_sample, c_prompt, c_sample, state_s5_re, state_s5_im, cache_da_k, cache_da_v, cache_mla_latent, cache_mla_kpe, w_ada, b_ada, g_mix, g_ffn, g_final, w_in_e, w_out_e, s5_lam_re, s5_lam_im, s5_log_dt, s5_b_re, s5_b_im, s5_c_re, s5_c_im, s5_d, w_glu, b_glu, gm_g_v, gm_w_s, gm_b_s, w_in_o, w_out_o, da_lq1, da_lk1, da_lq2, da_lk2, da_g_sub, mla_g_q, mla_w_uq, mla_g_kv, mla_w_ukv, peer_w_q, peer_k1, peer_k2, peer_u, peer_v):
    p = dict(g_mix=g_mix, g_ffn=g_ffn, g_final=g_final, w_in_e=w_in_e, w_out_e=w_out_e, s5_d=s5_d, w_glu=w_glu,
             b_glu=b_glu, gm_g_v=gm_g_v, gm_w_s=gm_w_s, gm_b_s=gm_b_s, w_in_o=w_in_o, w_out_o=w_out_o,
             da_lq1=da_lq1, da_lk1=da_lk1, da_lq2=da_lq2, da_lk2=da_lk2, da_g_sub=da_g_sub, mla_g_q=mla_g_q,
             mla_w_uq=mla_w_uq, mla_g_kv=mla_g_kv, mla_w_ukv=mla_w_ukv, peer_w_q=peer_w_q, peer_k1=peer_k1,
             peer_k2=peer_k2)
    n_even = (DEPTH + 1) // 2
    p['s5'] = [_s5_params(s5_lam_re[e], s5_lam_im[e], s5_log_dt[e], s5_b_re[e], s5_b_im[e], s5_c_re[e], s5_c_im[e])
               for e in range(n_even)]
    p['u_pack'] = [_pack_table(peer_u[layer]) for layer in range(DEPTH)]
    p['v_pack'] = [_pack_table(peer_v[layer]) for layer in range(DEPTH)]
    past = dict(s5_re=state_s5_re, s5_im=state_s5_im, da_k=cache_da_k, da_v=cache_da_v,
                mla_lat=cache_mla_latent, mla_kpe=cache_mla_kpe)
    nb = x_prompt.shape[0]
    mod = _ada(jnp.concatenate([c_prompt, c_sample], axis=0), w_ada, b_ada)
    past_len = cache_da_k.shape[2]
    parts = [_trunk(x_prompt[s:s + 1], mod[:, s:s + 1], 0, None, p, tl=512) for s in range(nb)]
    y_p = jnp.concatenate([y for y, _ in parts], axis=0)
    o_p = {k: jnp.concatenate([o[k] for _, o in parts], axis=0) for k in parts[0][1] if parts[0][1][k] is not None}
    y_s, o_s = _trunk(x_sample, mod[:, nb:], past_len, past, p, tl=x_sample.shape[1])
    st = lambda a: a[None]
    return (y_p, y_s, st(o_p['s5_re']), st(o_p['s5_im']), st(o_s['s5_re']), st(o_s['s5_im']), st(o_s['gm_v']),
            st(o_p['da_k']), st(o_p['da_v']), st(o_s['da_k']), st(o_s['da_v']),
            st(o_p['mla_lat']), st(o_p['mla_kpe']), st(o_s['mla_lat']), st(o_s['mla_kpe']))
```

```python
import functools
import math

import jax
import jax.numpy as jnp
from jax import lax
from jax.experimental import pallas as pl
from jax.experimental.pallas import tpu as pltpu
from jax.experimental.pallas import tpu_sc as plsc

F32 = jnp.float32
BF16 = jnp.bfloat16

D_MODEL = 1024
DEPTH = 2
CHUNK = 64
ROPE_THETA = 500000.0
EPS = 1e-6
NEG = -1e30

S5_WIDTH = 512
S5_GROUP = 16
S5_GROUPS = 32
S5_STATE = 64
GM_WIDTH = 512
GM_HEADS = 4
GM_CHUNK = 128
DA_HEADS = 4
DA_DH = 64
DA_DV = 128
DA_ROPE = 16
MLA_HEADS = 4
MLA_Q_RANK = 256
MLA_KV_RANK = 128
MLA_NOPE = 64
MLA_ROPE = 32
MLA_DV = 128
PEER_HEADS = 8
PEER_NKEYS = 128
PEER_TOPK = 16
PEER_HK = PEER_HEADS * PEER_TOPK

LANES = 128
VMEM_LIMIT = 48 << 20
PEER_TT = 256
PEER_HKC = 16
SC_WINDOW = 64
SC_LANES = 16
SC_TOKEN_BLOCK = 8
FLASH_RQ = 512

_HI = lax.Precision.HIGHEST


def _cparams(*sem):
    return pltpu.CompilerParams(dimension_semantics=sem, vmem_limit_bytes=VMEM_LIMIT)


def _full(shape):
    n = len(shape)
    return pl.BlockSpec(shape, lambda *_: (0,) * n)


def _ada_kernel(c_ref, w_ref, b_ref, o_ref):
    c = c_ref[...]
    s = c * jax.nn.sigmoid(c)
    o_ref[...] = jnp.dot(s, w_ref[...], precision=_HI, preferred_element_type=F32) + b_ref[...]


def _ada(c_all, w_ada, b_ada):
    r = c_all.shape[0]
    tn = 1536
    return pl.pallas_call(
        _ada_kernel,
        grid=(DEPTH, 6 * D_MODEL // tn),
        in_specs=[_full((r, D_MODEL)),
                  pl.BlockSpec((None, D_MODEL, tn), lambda l, j: (l, 0, j)),
                  pl.BlockSpec((None, 1, tn), lambda l, j: (l, 0, j))],
        out_specs=pl.BlockSpec((None, r, tn), lambda l, j: (l, 0, j)),
        out_shape=jax.ShapeDtypeStruct((DEPTH, r, 6 * D_MODEL), F32),
        compiler_params=_cparams("parallel", "parallel"),
        name="ada",
    )(c_all, w_ada, b_ada.reshape(DEPTH, 1, 6 * D_MODEL))


def _nmm_kernel(x_ref, g_ref, sc_ref, sh_ref, w_ref, o_ref, *h_ref):
    x = x_ref[...]
    ms = jnp.mean(x * x, axis=-1, keepdims=True)
    h = x * lax.rsqrt(ms + EPS) * g_ref[...] * (1.0 + sc_ref[...]) + sh_ref[...]
    o_ref[...] = jnp.dot(h.astype(BF16), w_ref[...], preferred_element_type=F32)
    if h_ref:
        h_ref[0][...] = h


def _norm_mod_matmul(x, g, sc, sh, w, tl, want_h=False):
    b, l, d = x.shape
    n = w.shape[1]
    row = pl.BlockSpec((None, 1, d), lambda bi, i: (bi, 0, 0))
    out_shape = [jax.ShapeDtypeStruct((b, l, n), F32)]
    out_specs = [pl.BlockSpec((None, tl, n), lambda bi, i: (bi, i, 0))]
    if want_h:
        out_shape.append(jax.ShapeDtypeStruct((b, l, d), F32))
        out_specs.append(pl.BlockSpec((None, tl, d), lambda bi, i: (bi, i, 0)))
    res = pl.pallas_call(
        _nmm_kernel,
        grid=(b, l // tl),
        in_specs=[pl.BlockSpec((None, tl, d), lambda bi, i: (bi, i, 0)), _full((1, d)), row, row, _full((d, n))],
        out_specs=out_specs, out_shape=out_shape,
        compiler_params=_cparams("parallel", "parallel"),
        name="norm_mod_matmul",
    )(x, g.reshape(1, d), sc, sh, w)
    return res if want_h else res[0]


def _s5_kernel(xa_ref, h0r_ref, h0i_ref, bblk_ref, ar_ref, ai_ref, cr_ref, ci_ref, d_ref, wglu_ref, bglu_ref,
               ya_ref, hr_out_ref, hi_out_ref, bur, bui, st_r, st_i, *, tl):
    nj = S5_WIDTH // LANES
    half = 8 * S5_STATE

    @pl.when(pl.program_id(1) == 0)
    def _():
        st_r[...] = h0r_ref[...]
        st_i[...] = h0i_ref[...]

    xa = xa_ref[...]
    for j in range(nj):
        bu = jnp.dot(xa[:, j * LANES:(j + 1) * LANES], bblk_ref[j], precision=_HI, preferred_element_type=F32)
        bur[j] = bu[:, :half]
        bui[j] = bu[:, half:]

    for j0 in range(0, nj, 2):
        js = (j0, j0 + 1)
        a_r = [ar_ref[j] for j in js]
        a_i = [ai_ref[j] for j in js]

        def body(k, carry, js=js, a_r=a_r, a_i=a_i):
            carry = list(carry)
            base = pl.multiple_of(k * 8, 8)
            for s in range(8):
                t = base + s
                for q, j in enumerate(js):
                    hr, hi = carry[2 * q], carry[2 * q + 1]
                    nhr = a_r[q] * hr - a_i[q] * hi + bur[j, pl.ds(t, 1), :]
                    nhi = a_r[q] * hi + a_i[q] * hr + bui[j, pl.ds(t, 1), :]
                    bur[j, pl.ds(t, 1), :] = nhr
                    bui[j, pl.ds(t, 1), :] = nhi
                    carry[2 * q], carry[2 * q + 1] = nhr, nhi
            return tuple(carry)

        init = (st_r[js[0]], st_i[js[0]], st_r[js[1]], st_i[js[1]])
        fin = lax.fori_loop(0, tl // 8, body, init)
        st_r[js[0]], st_i[js[0]], st_r[js[1]], st_i[js[1]] = fin

    hr_out_ref[...] = st_r[...]
    hi_out_ref[...] = st_i[...]

    ys = []
    for j in range(nj):
        y = (jnp.dot(bur[j].astype(BF16), cr_ref[j], preferred_element_type=F32)
             + jnp.dot(bui[j].astype(BF16), ci_ref[j], preferred_element_type=F32))
        ys.append(y + d_ref[:, j * LANES:(j + 1) * LANES] * xa[:, j * LANES:(j + 1) * LANES])
    y = jax.nn.gelu(jnp.concatenate(ys, axis=1))
    z = jnp.dot(y.astype(BF16), wglu_ref[...], preferred_element_type=F32) + bglu_ref[...]
    ya_ref[...] = z[:, :S5_WIDTH] * jax.nn.sigmoid(z[:, S5_WIDTH:])


def _s5_params(lam_re, lam_im, log_dt, b_re, b_im, c_re, c_im):
    lam = lax.complex(lam_re.astype(F32), lam_im.astype(F32))
    dt = jnp.exp(log_dt.astype(F32))[:, None]
    a_bar = jnp.exp(lam * dt)
    b_bar = ((a_bar - 1.0) / lam)[..., None] * lax.complex(b_re.astype(F32), b_im.astype(F32))
    nj = S5_GROUPS // 8
    eye = jnp.eye(8, dtype=F32)

    def blk_b(m):
        m = m.reshape(nj, 8, S5_STATE, S5_GROUP)
        return jnp.einsum('jgpc,gh->jgchp', m, eye).reshape(nj, 8 * S5_GROUP, 8 * S5_STATE)

    def blk_c(m):
        m = m.reshape(nj, 8, S5_GROUP, S5_STATE)
        return jnp.einsum('jgcp,gh->jgphc', m, eye).reshape(nj, 8 * S5_STATE, 8 * S5_GROUP)

    bblk = jnp.concatenate([blk_b(jnp.real(b_bar)), blk_b(jnp.imag(b_bar))], axis=-1)
    cr = blk_c(c_re.astype(F32)).astype(BF16)
    ci = (-blk_c(c_im.astype(F32))).astype(BF16)
    a_r = jnp.real(a_bar).reshape(nj, 1, 8 * S5_STATE)
    a_i = jnp.imag(a_bar).reshape(nj, 1, 8 * S5_STATE)
    return bblk, a_r, a_i, cr, ci


def _s5(y_in, h0_re, h0_im, sp, d_skip, w_glu, b_glu, tl):
    b, l, _ = y_in.shape
    nj = S5_GROUPS // 8
    half = 8 * S5_STATE
    bblk, a_r, a_i, cr, ci = sp
    st = pl.BlockSpec((None, nj, 1, half), lambda bi, i: (bi, 0, 0, 0))
    ya, hr, hi = pl.pallas_call(
        functools.partial(_s5_kernel, tl=tl),
        grid=(b, l // tl),
        in_specs=[pl.BlockSpec((None, tl, S5_WIDTH), lambda bi, i: (bi, i, 0)), st, st,
                  _full(bblk.shape), _full(a_r.shape), _full(a_i.shape), _full(cr.shape), _full(ci.shape),
                  _full((1, S5_WIDTH)), _full((S5_WIDTH, 2 * S5_WIDTH)), _full((1, 2 * S5_WIDTH))],
        out_specs=[pl.BlockSpec((None, tl, S5_WIDTH), lambda bi, i: (bi, i, 0)), st, st],
        out_shape=[jax.ShapeDtypeStruct((b, l, S5_WIDTH), F32),
                   jax.ShapeDtypeStruct((b, nj, 1, half), F32), jax.ShapeDtypeStruct((b, nj, 1, half), F32)],
        scratch_shapes=[pltpu.VMEM((nj, tl, half), F32), pltpu.VMEM((nj, tl, half), F32),
                        pltpu.VMEM((nj, 1, half), F32), pltpu.VMEM((nj, 1, half), F32)],
        compiler_params=_cparams("parallel", "arbitrary"),
        name="s5",
    )(y_in, h0_re.reshape(b, nj, 1, half), h0_im.reshape(b, nj, 1, half), bblk, a_r, a_i, cr, ci,
      d_skip.reshape(1, S5_WIDTH), w_glu.astype(BF16), b_glu.reshape(1, 2 * S5_WIDTH))
    return ya, hr.reshape(b, S5_GROUPS, S5_STATE), hi.reshape(b, S5_GROUPS, S5_STATE)


def _gmlp_kernel(u_ref, v_ref, gv_ref, ws_ref, bias_ref, yb_ref, *vn_ref, tl, lc):
    hd = GM_WIDTH // GM_HEADS
    gv = jax.nn.gelu(v_ref[...])
    vn = gv * lax.rsqrt(jnp.mean(gv * gv, axis=-1, keepdims=True) + EPS) * gv_ref[...]
    if vn_ref:
        vn_ref[0][...] = vn
    gu = jax.nn.gelu(u_ref[...])
    vb = vn.astype(BF16)
    for c in range(tl // lc):
        rows = slice(c * lc, (c + 1) * lc)
        for h in range(GM_HEADS):
            cols = slice(h * hd, (h + 1) * hd)
            mixed = jnp.dot(ws_ref[h], vb[rows, cols], preferred_element_type=F32) + bias_ref[:, cols]
            yb_ref[rows, cols] = gu[rows, cols] * mixed


def _gmlp(y_in, g_v, w_s, b_s, tl, want_vn):
    b, l, _ = y_in.shape
    lc = min(GM_CHUNK, l)
    hd = GM_WIDTH // GM_HEADS
    tri = jnp.tril(jnp.ones((GM_CHUNK, GM_CHUNK), dtype=bool))
    ws = jnp.where(tri[None], w_s, 0.0)[:, :lc, :lc].astype(BF16)
    bias = jnp.repeat(jnp.transpose(b_s)[:lc], hd, axis=1)
    out_shape = [jax.ShapeDtypeStruct((b, l, GM_WIDTH), F32)]
    out_specs = [pl.BlockSpec((None, tl, GM_WIDTH), lambda bi, i: (bi, i, 0))]
    if want_vn:
        out_shape.append(jax.ShapeDtypeStruct((b, l, GM_WIDTH), F32))
        out_specs.append(pl.BlockSpec((None, tl, GM_WIDTH), lambda bi, i: (bi, i, 0)))
    res = pl.pallas_call(
        functools.partial(_gmlp_kernel, tl=tl, lc=lc),
        grid=(b, l // tl),
        in_specs=[pl.BlockSpec((None, tl, GM_WIDTH), lambda bi, i: (bi, i, 1)),
                  pl.BlockSpec((None, tl, GM_WIDTH), lambda bi, i: (bi, i, 2)),
                  _full((1, GM_WIDTH)), _full(ws.shape), _full(bias.shape)],
        out_specs=out_specs, out_shape=out_shape,
        compiler_params=_cparams("parallel", "parallel"),
        name="gmlp",
    )(y_in, y_in, g_v.reshape(1, GM_WIDTH), ws, bias)
    return (res[0], res[1]) if want_vn else (res[0], None)


def _proj_res_kernel(x_ref, gt_ref, a_ref, b_ref, wa_ref, wb_ref, o_ref):
    mix = (jnp.dot(a_ref[...].astype(BF16), wa_ref[...], preferred_element_type=F32)
           + jnp.dot(b_ref[...].astype(BF16), wb_ref[...], preferred_element_type=F32))
    o_ref[...] = x_ref[...] + gt_ref[...] * mix


def _proj_residual(x, gt, a, bm, w, tl):
    b, l, d = x.shape
    ka = a.shape[-1]
    wa, wb = w[:ka].astype(BF16), w[ka:].astype(BF16)
    blk = lambda n: pl.BlockSpec((None, tl, n), lambda bi, i: (bi, i, 0))
    return pl.pallas_call(
        _proj_res_kernel,
        grid=(b, l // tl),
        in_specs=[blk(d), pl.BlockSpec((None, 1, d), lambda bi, i: (bi, 0, 0)), blk(ka), blk(bm.shape[-1]),
                  _full(wa.shape), _full(wb.shape)],
        out_specs=blk(d), out_shape=jax.ShapeDtypeStruct((b, l, d), F32),
        compiler_params=_cparams("parallel", "parallel"),
        name="proj_residual",
    )(x, gt, a, bm, wa, wb)


def _rope_tables(pos0, l):
    pos = (pos0 + jnp.arange(l, dtype=jnp.int32)).astype(F32)[:, None]
    lane = jnp.arange(LANES)

    def table(period, start, rot):
        half = rot // 2
        r = lane % period - start
        inside = (r >= 0) & (r < rot)
        k = jnp.where(inside, r % half, 0)
        inv = ROPE_THETA ** (-k.astype(F32) * 2.0 / rot)
        ang = pos * inv[None, :]
        cos = jnp.where(inside[None], jnp.cos(ang), 1.0)
        sgn = jnp.where(r < half, -1.0, 1.0)
        sin = jnp.where(inside[None], jnp.sin(ang) * sgn[None], 0.0)
        return cos, sin

    return table(DA_DH, 0, DA_ROPE), table(LANES, MLA_NOPE, MLA_ROPE), table(LANES, 0, MLA_ROPE)


def _rotate(x, cos, sin, period, start, rot):
    half = rot // 2
    n = x.shape[-1]
    reps = n // LANES
    lane = (lax.broadcasted_iota(jnp.int32, x.shape, 1) & (period - 1)) - start
    up = pltpu.roll(x, n - half, 1)
    dn = pltpu.roll(x, half, 1)
    partner = jnp.where(lane < half, up, dn)
    if reps > 1:
        cos = jnp.concatenate([cos] * reps, axis=1)
        sin = jnp.concatenate([sin] * reps, axis=1)
    return x * cos + partner * sin


def _odd_prep_kernel(q_ref, k_ref, cq_ref, ckv_ref, kpe_ref, cd_ref, sd_ref, cm_ref, sm_ref, ck_ref, sk_ref,
                     gq_ref, wuq_ref, gkv_ref, qd_ref, kd_ref, qm_ref, lat_ref, kpe_out_ref):
    cd, sd = cd_ref[...], sd_ref[...]
    qd_ref[...] = _rotate(q_ref[...], cd, sd, DA_DH, 0, DA_ROPE).astype(BF16)
    kd_ref[...] = _rotate(k_ref[...], cd, sd, DA_DH, 0, DA_ROPE)
    cq = cq_ref[...]
    cqn = cq * lax.rsqrt(jnp.mean(cq * cq, axis=-1, keepdims=True) + EPS) * gq_ref[...]
    qm = jnp.dot(cqn.astype(BF16), wuq_ref[...], preferred_element_type=F32)
    qm_ref[...] = _rotate(qm, cm_ref[...], sm_ref[...], LANES, MLA_NOPE, MLA_ROPE).astype(BF16)
    ckv = ckv_ref[...]
    lat_ref[...] = ckv * lax.rsqrt(jnp.mean(ckv * ckv, axis=-1, keepdims=True) + EPS) * gkv_ref[...]
    kpe = _rotate(kpe_ref[...], ck_ref[...], sk_ref[...], LANES, 0, MLA_ROPE)
    kpe_out_ref[...] = kpe[:, :MLA_ROPE]


def _odd_prep(y_in, pos0, g_q, w_uq, g_kv, tl):
    b, l, _ = y_in.shape
    (cd, sd), (cm, sm), (ck, sk) = _rope_tables(pos0, l)
    per = MLA_NOPE + MLA_ROPE
    wuq = jnp.pad(w_uq.reshape(MLA_Q_RANK, MLA_HEADS, per), ((0, 0), (0, 0), (0, LANES - per)))
    wuq = wuq.reshape(MLA_Q_RANK, MLA_HEADS * LANES).astype(BF16)
    col = lambda w, j: pl.BlockSpec((None, tl, w), lambda bi, i: (bi, i, j))
    tab = pl.BlockSpec((tl, LANES), lambda bi, i: (i, 0))
    out = lambda w: pl.BlockSpec((None, tl, w), lambda bi, i: (bi, i, 0))
    return pl.pallas_call(
        _odd_prep_kernel,
        grid=(b, l // tl),
        in_specs=[col(512, 0), col(512, 1), col(256, 6), col(128, 14), col(128, 15),
                  tab, tab, tab, tab, tab, tab,
                  _full((1, MLA_Q_RANK)), _full(wuq.shape), _full((1, MLA_KV_RANK))],
        out_specs=[out(512), out(512), out(512), out(MLA_KV_RANK), out(MLA_ROPE)],
        out_shape=[jax.ShapeDtypeStruct((b, l, 512), BF16), jax.ShapeDtypeStruct((b, l, 512), F32),
                   jax.ShapeDtypeStruct((b, l, 512), BF16), jax.ShapeDtypeStruct((b, l, MLA_KV_RANK), F32),
                   jax.ShapeDtypeStruct((b, l, MLA_ROPE), F32)],
        compiler_params=_cparams("parallel", "parallel"),
        name="odd_prep",
    )(y_in, y_in, y_in, y_in, y_in, cd, sd, cm, sm, ck, sk,
      g_q.reshape(1, MLA_Q_RANK), wuq, g_kv.reshape(1, MLA_KV_RANK))


def _mla_kv_kernel(lat_ref, kpe_ref, wk_ref, wv_ref, place_ref, k_ref, v_ref):
    lat = lat_ref[...].astype(BF16)
    k = (jnp.dot(lat, wk_ref[...], preferred_element_type=F32)
         + jnp.dot(kpe_ref[...].astype(BF16), place_ref[...], preferred_element_type=F32))
    k_ref[...] = k.astype(BF16)
    v_ref[...] = jnp.dot(lat, wv_ref[...], preferred_element_type=F32).astype(BF16)


def _mla_kv(lat_all, kpe_all, w_ukv, tk):
    b, k, _ = lat_all.shape
    per = MLA_NOPE + MLA_DV
    w3 = w_ukv.reshape(MLA_KV_RANK, MLA_HEADS, per)
    wk = jnp.pad(w3[:, :, :MLA_NOPE], ((0, 0), (0, 0), (0, LANES - MLA_NOPE))).reshape(MLA_KV_RANK, MLA_HEADS * LANES)
    wv = w3[:, :, MLA_NOPE:].reshape(MLA_KV_RANK, MLA_HEADS * MLA_DV)
    place = jnp.pad(jnp.eye(MLA_ROPE, dtype=F32), ((0, 0), (MLA_NOPE, LANES - MLA_NOPE - MLA_ROPE)))
    place = jnp.tile(place, (1, MLA_HEADS))
    blk = lambda w: pl.BlockSpec((None, tk, w), lambda bi, i: (bi, i, 0))
    return pl.pallas_call(
        _mla_kv_kernel,
        grid=(b, k // tk),
        in_specs=[blk(MLA_KV_RANK), blk(MLA_ROPE), _full(wk.shape), _full(wv.shape), _full(place.shape)],
        out_specs=[blk(512), blk(512)],
        out_shape=[jax.ShapeDtypeStruct((b, k, 512), BF16), jax.ShapeDtypeStruct((b, k, 512), BF16)],
        compiler_params=_cparams("parallel", "parallel"),
        name="mla_kv",
    )(lat_all, kpe_all, wk.astype(BF16), wv.astype(BF16), place.astype(BF16))


def _visible_limit(pos_last):
    shift = CHUNK.bit_length() - 1
    return ((pos_last >> shift) + 1) << shift


def _fold_lanes(x, op):
    n = x.shape[-1]
    if n % LANES:
        return x
    out = x[:, :LANES]
    for c in range(1, n // LANES):
        out = op(out, x[:, c * LANES:(c + 1) * LANES])
    return out


def _flash_kernel(lam_ref, q_ref, k_ref, v_ref, g_ref, o_ref, m_ref, l_ref, acc_ref, *,
                  nhm, dqk, maps, scale, pos0, tq, tk, nkv, out_scale, rq):
    i, j = pl.program_id(1), pl.program_id(2)

    @pl.when(j == 0)
    def _():
        m_ref[...] = jnp.full(m_ref.shape, NEG, F32)
        l_ref[...] = jnp.zeros(l_ref.shape, F32)
        acc_ref[...] = jnp.zeros(acc_ref.shape, F32)

    c2 = scale * math.log2(math.e)
    visible = j * tk < _visible_limit(pos0 + (i + 1) * tq - 1)
    unmasked = (j + 1) * tk <= _visible_limit(pos0 + i * tq)

    def process(masked):
        for hm in range(nhm):
            k = k_ref[:, hm * dqk:(hm + 1) * dqk].astype(BF16)
            h = hm // maps
            v = v_ref[:, h * DA_DV:(h + 1) * DA_DV].astype(BF16)

            def qblock(r, carry, hm=hm, k=k, v=v):
                r0 = r * rq if isinstance(r, int) else pl.multiple_of(r * rq, rq)
                rows = pl.ds(r0, rq)
                q = q_ref[rows, hm * dqk:(hm + 1) * dqk].astype(BF16)
                s = lax.dot_general(q, k, (((1,), (1,)), ((), ())), preferred_element_type=F32)
                if masked:
                    q_pos = pos0 + i * tq + r0 + lax.broadcasted_iota(jnp.int32, (rq, tk), 0)
                    k_pos = j * tk + lax.broadcasted_iota(jnp.int32, (rq, tk), 1)
                    s = jnp.where(k_pos < _visible_limit(q_pos), s, NEG)
                m_prev = m_ref[hm, rows, :]
                m_new = jnp.maximum(m_prev, jnp.max(_fold_lanes(s, jnp.maximum), axis=-1, keepdims=True))
                p = jnp.exp2((s - m_new) * c2)
                alpha = jnp.exp2((m_prev - m_new) * c2)
                l_ref[hm, rows, :] = (alpha * l_ref[hm, rows, :]
                                      + jnp.sum(_fold_lanes(p, jnp.add), axis=-1, keepdims=True))
                acc_ref[hm, rows, :] = (alpha * acc_ref[hm, rows, :]
                                        + jnp.dot(p.astype(BF16), v, preferred_element_type=F32))
                m_ref[hm, rows, :] = m_new
                return carry

            if tq == rq:
                qblock(0, 0)
            else:
                lax.fori_loop(0, tq // rq, qblock, 0)

    pl.when(visible & unmasked)(lambda: process(False))
    pl.when(visible & jnp.logical_not(unmasked))(lambda: process(True))

    @pl.when(j == nkv - 1)
    def _():
        for h in range(nhm // maps):
            if maps == 2:
                o = acc_ref[2 * h] / l_ref[2 * h] - lam_ref[0] * (acc_ref[2 * h + 1] / l_ref[2 * h + 1])
                o = o * lax.rsqrt(jnp.mean(o * o, axis=-1, keepdims=True) + EPS) * g_ref[...] * out_scale
            else:
                o = acc_ref[h] / l_ref[h]
            o_ref[:, h * DA_DV:(h + 1) * DA_DV] = o


def _flash(q, k, v, lam, g_sub, *, nhm, dqk, maps, scale, pos0, tq, tk, out_scale, v_col=0):
    b, l, _ = q.shape
    kk = k.shape[1]
    nkv = kk // tk
    nh = nhm // maps

    def kv_idx(bi, i, j):
        last = (_visible_limit(pos0 + (i + 1) * tq - 1) - 1) // tk
        return jnp.minimum(j, last)

    return pl.pallas_call(
        functools.partial(_flash_kernel, nhm=nhm, dqk=dqk, maps=maps, scale=scale, pos0=pos0, tq=tq, tk=tk,
                          nkv=nkv, out_scale=out_scale, rq=min(FLASH_RQ, tq)),
        grid=(b, l // tq, nkv),
        in_specs=[pl.BlockSpec(memory_space=pltpu.SMEM),
                  pl.BlockSpec((None, tq, nhm * dqk), lambda bi, i, j: (bi, i, 0)),
                  pl.BlockSpec((None, tk, nhm * dqk), lambda bi, i, j: (bi, kv_idx(bi, i, j), 0)),
                  pl.BlockSpec((None, tk, nh * DA_DV), lambda bi, i, j: (bi, kv_idx(bi, i, j), v_col)),
                  _full((1, DA_DV))],
        out_specs=pl.BlockSpec((None, tq, nh * DA_DV), lambda bi, i, j: (bi, i, 0)),
        out_shape=jax.ShapeDtypeStruct((b, l, nh * DA_DV), F32),
        scratch_shapes=[pltpu.VMEM((nhm, tq, 1), F32), pltpu.VMEM((nhm, tq, 1), F32),
                        pltpu.VMEM((nhm, tq, DA_DV), F32)],
        compiler_params=_cparams("parallel", "parallel", "arbitrary"),
        name="flash_da" if maps == 2 else "flash_mla",
    )(lam, q, k, v, g_sub)


def _top16(s, rid, payload, val_ref, pay_ref):
    big = float(s.shape[0])
    for it in range(PEER_TOPK):
        m = jnp.max(s, axis=0, keepdims=True)
        am = jnp.min(jnp.where(s == m, rid, big), axis=0, keepdims=True)
        sel = rid == am
        val_ref[it:it + 1, :] = m
        if payload is None:
            pay_ref[it:it + 1, :] = am
        else:
            pay_ref[it:it + 1, :] = jnp.max(jnp.where(sel, payload, -1.0), axis=0, keepdims=True)
        s = jnp.where(sel, -jnp.inf, s)


_PEER_PAIRS = [(a, b) for a in range(PEER_TOPK) for b in range(PEER_TOPK) if (a + 1) * (b + 1) <= PEER_TOPK]
_PEER_CAND_ROWS = -(-len(_PEER_PAIRS) // 8) * 8


def _route_kernel(q_ref, k1_ref, k2_ref, idx_ref, g_ref, v1_ref, i1_ref, v2_ref, i2_ref, vc_ref, ic_ref,
                  cand_ref, cidx_ref):
    tt = q_ref.shape[0]
    half = q_ref.shape[1] // 2
    nt = (((1,), (1,)), ((), ()))
    s1 = lax.dot_general(k1_ref[...].astype(BF16), q_ref[:, :half].astype(BF16), nt, preferred_element_type=F32)
    s2 = lax.dot_general(k2_ref[...].astype(BF16), q_ref[:, half:].astype(BF16), nt, preferred_element_type=F32)
    rid = lax.broadcasted_iota(jnp.int32, (PEER_NKEYS, tt), 0).astype(F32)
    _top16(s1, rid, None, v1_ref, i1_ref)
    _top16(s2, rid, None, v2_ref, i2_ref)
    npairs = len(_PEER_PAIRS)
    cand_ref[npairs:, :] = jnp.full((_PEER_CAND_ROWS - npairs, tt), -jnp.inf, F32)
    cidx_ref[npairs:, :] = jnp.full((_PEER_CAND_ROWS - npairs, tt), -1.0, F32)
    for r, (a, b) in enumerate(_PEER_PAIRS):
        cand_ref[r:r + 1, :] = v1_ref[a:a + 1, :] + v2_ref[b:b + 1, :]
        cidx_ref[r:r + 1, :] = i1_ref[a:a + 1, :] * float(PEER_NKEYS) + i2_ref[b:b + 1, :]
    rid2 = lax.broadcasted_iota(jnp.int32, (_PEER_CAND_ROWS, tt), 0).astype(F32)
    _top16(cand_ref[...], rid2, cidx_ref[...], vc_ref, ic_ref)
    sc = vc_ref[...]
    e = jnp.exp(sc - sc[0:1, :])
    g_ref[...] = e / jnp.sum(e, axis=0, keepdims=True)
    idx_ref[...] = ic_ref[...].astype(jnp.int32)


def _peer_route(q, k1, k2):
    t = q.shape[0]
    tt = PEER_TT
    dk = 2 * PEER_NKEYS
    kspec = pl.BlockSpec((None, PEER_NKEYS, dk // 2), lambda i, h: (h, 0, 0))
    ospec = pl.BlockSpec((None, PEER_TOPK, tt), lambda i, h: (i, h, 0))
    sm = pltpu.VMEM((PEER_TOPK, tt), F32)
    return pl.pallas_call(
        _route_kernel,
        grid=(t // tt, PEER_HEADS),
        in_specs=[pl.BlockSpec((tt, dk), lambda i, h: (i, h)), kspec, kspec],
        out_specs=[ospec, ospec],
        out_shape=[jax.ShapeDtypeStruct((t // tt, PEER_HK, tt), jnp.int32),
                   jax.ShapeDtypeStruct((t // tt, PEER_HK, tt), F32)],
        scratch_shapes=[sm, sm, sm, sm, sm, sm,
                        pltpu.VMEM((_PEER_CAND_ROWS, tt), F32), pltpu.VMEM((_PEER_CAND_ROWS, tt), F32)],
        compiler_params=_cparams("parallel", "parallel"),
        name="peer_route",
    )(q, k1, k2)


def _pack_table(tab):
    e, d = tab.shape
    tb = lax.bitcast_convert_type(tab.astype(BF16), jnp.uint16).astype(jnp.uint32)
    words = tb[:, :d // 2] | (tb[:, d // 2:] << 16)
    return lax.bitcast_convert_type(words, F32)


def _sc_gather(tab, idx):
    n = idx.shape[0]
    w = tab.shape[1]
    mesh = plsc.VectorSubcoreMesh(core_axis_name="core", subcore_axis_name="subcore")

    @pl.kernel(out_type=jax.ShapeDtypeStruct((n, w), tab.dtype), mesh=mesh)
    def gather(tab_hbm, idx_hbm, out_hbm):
        def body(idx_vmem, out_vmem):
            pltpu.sync_copy(tab_hbm.at[idx_vmem.at[0]], out_vmem)

        pltpu.emit_pipeline(
            body, grid=(n // SC_WINDOW,),
            in_specs=[pl.BlockSpec((1, SC_WINDOW), index_map=lambda i: (i, 0))],
            out_specs=[pl.BlockSpec((SC_WINDOW, w), index_map=lambda i: (i, 0))],
            core_axis_name=("core", "subcore"), dimension_semantics=(pltpu.PARALLEL,), trace_scopes=False,
        )(idx_hbm, out_hbm)

    return gather(tab, idx.reshape(n // SC_WINDOW, SC_WINDOW))


def _unpack(words):
    w = pltpu.bitcast(words, jnp.uint32)
    lo = pltpu.bitcast(w << 16, F32)
    hi = pltpu.bitcast(w & jnp.uint32(0xFFFF0000), F32)
    return lo, hi


def _sc_dot(tab, idx_tm, h):
    t, d = h.shape
    w = d // 2
    nk = idx_tm.shape[1]
    half_rows = nk // 2
    mesh = plsc.VectorSubcoreMesh(core_axis_name="core", subcore_axis_name="subcore")
    n_workers = mesh.num_cores * mesh.num_subcores
    tpw = t // n_workers
    tb = SC_TOKEN_BLOCK
    steps = 2 * tb
    assert t % n_workers == 0 and tpw % tb == 0 and half_rows % SC_LANES == 0 and w % SC_LANES == 0

    @functools.partial(
        pl.kernel, out_type=jax.ShapeDtypeStruct((t, nk), F32), mesh=mesh,
        scratch_types=[pltpu.VMEM((tb * nk,), jnp.int32), pltpu.VMEM((tb, d), F32),
                       pltpu.VMEM((2, half_rows, w), F32), pltpu.VMEM((tb, nk), F32),
                       pltpu.SemaphoreType.DMA((2,))],
        compiler_params=pltpu.CompilerParams(needs_layout_passes=False))
    def dot_kernel(tab_hbm, idx_hbm, h_hbm, act_hbm, idx_v, h_v, rows_v, act_v, sems):
        wid = lax.axis_index("subcore") * mesh.num_cores + lax.axis_index("core")
        lane = lax.iota(jnp.int32, SC_LANES)

        def gather(step, buf):
            rows = idx_v.at[pl.ds(pl.multiple_of(step * half_rows, half_rows), half_rows)]
            return pltpu.make_async_copy(tab_hbm.at[rows], rows_v.at[buf], sems.at[buf])

        def dots(step, buf):
            tok = step // 2
            col0 = (step % 2) * half_rows

            def sixteen(q, _):
                def four_rows(g, vec):
                    r0 = q * SC_LANES + g * 4

                    def chunk(c, accs):
                        off = pl.multiple_of(c * SC_LANES, SC_LANES)
                        h_lo = h_v[tok, pl.ds(off, SC_LANES)]
                        h_hi = h_v[tok, pl.ds(w + off, SC_LANES)]
                        out = []
                        for r in range(4):
                            bits = lax.bitcast_convert_type(rows_v[buf, r0 + r, pl.ds(off, SC_LANES)], jnp.int32)
                            lo = lax.bitcast_convert_type(bits << 16, F32)
                            hi = lax.bitcast_convert_type(bits & jnp.int32(-65536), F32)
                            out.append(accs[r] + lo * h_lo + hi * h_hi)
                        return tuple(out)

                    zero = jnp.zeros((SC_LANES,), F32)
                    accs = lax.fori_loop(0, w // SC_LANES, chunk, (zero, zero, zero, zero), unroll=4)
                    for r in range(4):
                        vec = jnp.where(lane == g * 4 + r, jnp.sum(accs[r]), vec)
                    return vec

                vec = lax.fori_loop(0, SC_LANES // 4, four_rows, jnp.zeros((SC_LANES,), F32))
                act_v[tok, pl.ds(pl.multiple_of(col0 + q * SC_LANES, SC_LANES), SC_LANES)] = vec
                return 0

            lax.fori_loop(0, half_rows // SC_LANES, sixteen, 0)

        @pl.loop(0, tpw // tb)
        def _(blk):
            tok0 = wid * tpw + blk * tb
            pltpu.sync_copy(idx_hbm.at[pl.ds(pl.multiple_of(tok0 * nk, nk), tb * nk)], idx_v)
            pltpu.sync_copy(h_hbm.at[pl.ds(tok0, tb)], h_v)
            gather(0, 0).start()

            @pl.loop(0, steps, step=2)
            def _(s):
                gather(s + 1, 1).start()
                gather(s, 0).wait()
                dots(s, 0)

                @pl.when(s + 2 < steps)
                def _():
                    gather(s + 2, 0).start()

                gather(s + 1, 1).wait()
                dots(s + 1, 1)

            pltpu.sync_copy(act_v, act_hbm.at[pl.ds(tok0, tb)])

    return dot_kernel(tab, idx_tm.reshape(t * nk), h)


def _peer_mix_kernel(act_ref, g_ref, gv_ref, x_ref, gt_ref, gf_ref, o_ref, acc_ref, coef_ref, *, final):
    c = pl.program_id(1)
    half = x_ref.shape[1] // 2

    @pl.when(c == 0)
    def _():
        acc_ref[...] = jnp.zeros(acc_ref.shape, F32)
        coef_ref[...] = g_ref[...].T * jax.nn.gelu(act_ref[...])

    coef = coef_ref[...]
    lane = lax.broadcasted_iota(jnp.int32, coef.shape, 1)
    acc_lo, acc_hi = acc_ref[:, :half], acc_ref[:, half:]
    for k in range(PEER_HKC):
        ck = jnp.sum(jnp.where(lane == c * PEER_HKC + k, coef, 0.0), axis=-1, keepdims=True)
        lo, hi = _unpack(gv_ref[k])
        acc_lo = acc_lo + ck * lo
        acc_hi = acc_hi + ck * hi
    acc_ref[:, :half] = acc_lo
    acc_ref[:, half:] = acc_hi

    @pl.when(c == pl.num_programs(1) - 1)
    def _():
        x = x_ref[...] + gt_ref[...] * acc_ref[...]
        if final:
            x = x * lax.rsqrt(jnp.mean(x * x, axis=-1, keepdims=True) + EPS) * gf_ref[...]
        o_ref[...] = x


def _peer_mix(act, g, gv, x, gt, g_final, final):
    t, d = x.shape
    tt = PEER_TT
    nt = t // tt
    if gt.ndim == 3:
        per_seq = nt // gt.shape[0]
        gt_spec = pl.BlockSpec((None, 1, d), lambda i, c: (i // per_seq, 0, 0))
    else:
        gt_spec = pl.BlockSpec((tt, d), lambda i, c: (i, 0))
    return pl.pallas_call(
        functools.partial(_peer_mix_kernel, final=final),
        grid=(nt, PEER_HK // PEER_HKC),
        in_specs=[pl.BlockSpec((tt, PEER_HK), lambda i, c: (i, 0)),
                  pl.BlockSpec((None, PEER_HK, tt), lambda i, c: (i, 0, 0)),
                  pl.BlockSpec((None, PEER_HKC, tt, d // 2), lambda i, c: (i, c, 0, 0)),
                  pl.BlockSpec((tt, d), lambda i, c: (i, 0)), gt_spec, _full((1, d))],
        out_specs=pl.BlockSpec((tt, d), lambda i, c: (i, 0)),
        out_shape=jax.ShapeDtypeStruct((t, d), F32),
        scratch_shapes=[pltpu.VMEM((tt, d), F32), pltpu.VMEM((tt, PEER_HK), F32)],
        compiler_params=_cparams("parallel", "arbitrary"),
        name="peer_mix",
    )(act, g, gv, x, gt, g_final.reshape(1, d))


def _peer(x, g_ffn, sc2, sh2, gt2, w_q, k1, k2, u_pack, v_pack, g_final, final, tl):
    b, l, d = x.shape
    t = b * l
    q, h = _norm_mod_matmul(x, g_ffn, sc2, sh2, w_q, tl, want_h=True)
    idx, g = _peer_route(q.reshape(t, -1), k1, k2)
    nt = t // PEER_TT
    act = _sc_dot(u_pack, jnp.transpose(idx, (0, 2, 1)).reshape(t, PEER_HK), h.reshape(t, d))
    gv = _sc_gather(v_pack, idx.reshape(-1)).reshape(nt, PEER_HK, PEER_TT, d // 2)
    if l % PEER_TT:
        gt2 = jnp.broadcast_to(gt2, (b, l, d)).reshape(t, d)
    return _peer_mix(act, g, gv, x.reshape(t, d), gt2, g_final, final).reshape(b, l, d)


def _trunk(x, mod, pos0, past, p, tl):
    b, l, d = x.shape
    outs = {}
    for layer in range(DEPTH):
        sh1, sc1, gt1, sh2, sc2, gt2 = [m[:, None, :] for m in jnp.split(mod[layer], 6, axis=-1)]
        if layer % 2 == 0:
            e = layer // 2
            y_in = _norm_mod_matmul(x, p['g_mix'][layer], sc1, sh1, p['w_in_e'][e].astype(BF16), tl)
            if past is None:
                h0_re = jnp.zeros((b, S5_GROUPS, S5_STATE), F32)
                h0_im = jnp.zeros((b, S5_GROUPS, S5_STATE), F32)
            else:
                h0_re, h0_im = past['s5_re'][e], past['s5_im'][e]
            ya, hr, hi = _s5(y_in, h0_re, h0_im, p['s5'][e], p['s5_d'][e], p['w_glu'][e], p['b_glu'][e], tl)
            yb, vn = _gmlp(y_in, p['gm_g_v'][e], p['gm_w_s'][e], p['gm_b_s'][e], tl, want_vn=past is not None)
            x = _proj_residual(x, gt1, ya, yb, p['w_out_e'][e], tl)
            outs.update(s5_re=hr, s5_im=hi, gm_v=vn)
        else:
            o = layer // 2
            w_in = jnp.pad(p['w_in_o'][o], ((0, 0), (0, 2048 - p['w_in_o'][o].shape[1]))).astype(BF16)
            y_in = _norm_mod_matmul(x, p['g_mix'][layer], sc1, sh1, w_in, tl)
            qd, kd, qm, lat, kpe = _odd_prep(y_in, pos0, p['mla_g_q'][o], p['mla_w_uq'][o], p['mla_g_kv'][o], tl)
            v_new = y_in[:, :, 2 * 512:3 * 512]
            lam_init = 0.8 - 0.6 * math.exp(-0.3 * layer)
            lam = (jnp.exp(jnp.sum(p['da_lq1'][o] * p['da_lk1'][o])) - jnp.exp(jnp.sum(p['da_lq2'][o] * p['da_lk2'][o]))
                   + lam_init).reshape(1).astype(F32)
            g_sub = p['da_g_sub'][o].reshape(1, DA_DV)
            if past is None:
                k_all, v_all, v_col, lat_all, kpe_all = kd, y_in, 2, lat, kpe
                tq = tk = tl
            else:
                k_all = jnp.concatenate([past['da_k'][o].reshape(b, -1, 512), kd], axis=1)
                v_all = jnp.concatenate([past['da_v'][o].reshape(b, -1, 512), v_new], axis=1)
                lat_all = jnp.concatenate([past['mla_lat'][o], lat], axis=1)
                kpe_all = jnp.concatenate([past['mla_kpe'][o], kpe], axis=1)
                v_col, tq, tk = 0, l, k_all.shape[1]
            yc = _flash(qd, k_all, v_all, lam, g_sub, nhm=2 * DA_HEADS, dqk=DA_DH, maps=2, scale=DA_DH ** -0.5,
                        pos0=pos0, tq=tq, tk=tk, out_scale=1.0 - lam_init, v_col=v_col)
            km, vm = _mla_kv(lat_all, kpe_all, p['mla_w_ukv'][o], tk)
            yd = _flash(qm, km, vm, lam, g_sub, nhm=MLA_HEADS, dqk=LANES, maps=1,
                        scale=(MLA_NOPE + MLA_ROPE) ** -0.5, pos0=pos0, tq=tq, tk=tk, out_scale=1.0)
            x = _proj_residual(x, gt1, yc, yd, p['w_out_o'][o], tl)
            outs.update(da_k=kd.reshape(b, l, DA_HEADS, 2 * DA_DH), da_v=v_new.reshape(b, l, DA_HEADS, DA_DV),
                        mla_lat=lat, mla_kpe=kpe)
        x = _peer(x, p['g_ffn'][layer], sc2, sh2, gt2, p['peer_w_q'][layer].astype(BF16), p['peer_k1'][layer],
                  p['peer_k2'][layer], p['u_pack'][layer], p['v_pack'][layer], p['g_final'],
                  final=layer == DEPTH - 1, tl=tl)
    return x, outs


def kernel(x_prompt, x_sample, c_prompt, c_sample, state_s5_re, state_s5_im, cache_da_k, cache_da_v, cache_mla_latent, cache_mla_kpe, w_ada, b_ada, g_mix, g_ffn, g_final, w_in_e, w_out_e, s5_lam_re, s5_lam_im, s5_log_dt, s5_b_re, s5_b_im, s5_c_re, s5_c_im, s5_d, w_glu, b_glu, gm_g_v, gm_w_s, gm_b_s, w_in_o, w_out_o, da_lq1, da_lk1, da_lq2, da_lk2, da_g_sub, mla_g_q, mla_w_uq, mla_g_kv, mla_w_ukv, peer_w_q, peer_k1, peer_k2, peer_u, peer_v):
    p = dict(g_mix=g_mix, g_ffn=g_ffn, g_final=g_final, w_in_e=w_in_e, w_out_e=w_out_e, s5_d=s5_d, w_glu=w_glu,
             b_glu=b_glu, gm_g_v=gm_g_v, gm_w_s=gm_w_s, gm_b_s=gm_b_s, w_in_o=w_in_o, w_out_o=w_out_o,
             da_lq1=da_lq1, da_lk1=da_lk1, da_lq2=da_lq2, da_lk2=da_lk2, da_g_sub=da_g_sub, mla_g_q=mla_g_q,
             mla_w_uq=mla_w_uq, mla_g_kv=mla_g_kv, mla_w_ukv=mla_w_ukv, peer_w_q=peer_w_q, peer_k1=peer_k1,
             peer_k2=peer_k2)
    n_even = (DEPTH + 1) // 2
    p['s5'] = [_s5_params(s5_lam_re[e], s5_lam_im[e], s5_log_dt[e], s5_b_re[e], s5_b_im[e], s5_c_re[e], s5_c_im[e])
               for e in range(n_even)]
    p['u_pack'] = [_pack_table(peer_u[layer]) for layer in range(DEPTH)]
    p['v_pack'] = [_pack_table(peer_v[layer]) for layer in range(DEPTH)]
    past = dict(s5_re=state_s5_re, s5_im=state_s5_im, da_k=cache_da_k, da_v=cache_da_v,
                mla_lat=cache_mla_latent, mla_kpe=cache_mla_kpe)
    nb = x_prompt.shape[0]
    mod = _ada(jnp.concatenate([c_prompt, c_sample], axis=0), w_ada, b_ada)
    past_len = cache_da_k.shape[2]
    parts = [_trunk(x_prompt[s:s + 1], mod[:, s:s + 1], 0, None, p, tl=512) for s in range(nb)]
    y_p = jnp.concatenate([y for y, _ in parts], axis=0)
    o_p = {k: jnp.concatenate([o[k] for _, o in parts], axis=0) for k in parts[0][1] if parts[0][1][k] is not None}
    y_s, o_s = _trunk(x_sample, mod[:, nb:], past_len, past, p, tl=x_sample.shape[1])
    st = lambda a: a[None]
    return (y_p, y_s, st(o_p['s5_re']), st(o_p['s5_im']), st(o_s['s5_re']), st(o_s['s5_im']), st(o_s['gm_v']),
            st(o_p['da_k']), st(o_p['da_v']), st(o_s['da_k']), st(o_s['da_v']),
            st(o_p['mla_lat']), st(o_p['mla_kpe']), st(o_s['mla_lat']), st(o_s['mla_kpe']))
```

```python
import functools
import math

import jax
import jax.numpy as jnp
from jax import lax
from jax.experimental import pallas as pl
from jax.experimental.pallas import tpu as pltpu
from jax.experimental.pallas import tpu_sc as plsc

F32 = jnp.float32
BF16 = jnp.bfloat16

D_MODEL = 1024
DEPTH = 2
CHUNK = 64
ROPE_THETA = 500000.0
EPS = 1e-6
NEG = -1e30

S5_WIDTH = 512
S5_GROUP = 16
S5_GROUPS = 32
S5_STATE = 64
GM_WIDTH = 512
GM_HEADS = 4
GM_CHUNK = 128
DA_HEADS = 4
DA_DH = 64
DA_DV = 128
DA_ROPE = 16
MLA_HEADS = 4
MLA_Q_RANK = 256
MLA_KV_RANK = 128
MLA_NOPE = 64
MLA_ROPE = 32
MLA_DV = 128
PEER_HEADS = 8
PEER_NKEYS = 128
PEER_TOPK = 16
PEER_HK = PEER_HEADS * PEER_TOPK

LANES = 128
VMEM_LIMIT = 48 << 20
PEER_TT = 256
SC_LANES = 16
SC_TOKEN_BLOCK = 8
FLASH_RQ = 512

_HI = lax.Precision.HIGHEST


def _cparams(*sem):
    return pltpu.CompilerParams(dimension_semantics=sem, vmem_limit_bytes=VMEM_LIMIT)


def _full(shape):
    n = len(shape)
    return pl.BlockSpec(shape, lambda *_: (0,) * n)


def _ada_kernel(c_ref, w_ref, b_ref, o_ref):
    c = c_ref[...]
    s = c * jax.nn.sigmoid(c)
    o_ref[...] = jnp.dot(s, w_ref[...], precision=_HI, preferred_element_type=F32) + b_ref[...]


def _ada(c_all, w_ada, b_ada):
    r = c_all.shape[0]
    tn = 1536
    return pl.pallas_call(
        _ada_kernel,
        grid=(DEPTH, 6 * D_MODEL // tn),
        in_specs=[_full((r, D_MODEL)),
                  pl.BlockSpec((None, D_MODEL, tn), lambda l, j: (l, 0, j)),
                  pl.BlockSpec((None, 1, tn), lambda l, j: (l, 0, j))],
        out_specs=pl.BlockSpec((None, r, tn), lambda l, j: (l, 0, j)),
        out_shape=jax.ShapeDtypeStruct((DEPTH, r, 6 * D_MODEL), F32),
        compiler_params=_cparams("parallel", "parallel"),
        name="ada",
    )(c_all, w_ada, b_ada.reshape(DEPTH, 1, 6 * D_MODEL))


def _nmm_kernel(x_ref, g_ref, sc_ref, sh_ref, w_ref, o_ref, *h_ref):
    x = x_ref[...]
    ms = jnp.mean(x * x, axis=-1, keepdims=True)
    h = x * lax.rsqrt(ms + EPS) * g_ref[...] * (1.0 + sc_ref[...]) + sh_ref[...]
    o_ref[...] = jnp.dot(h.astype(BF16), w_ref[...], preferred_element_type=F32)
    if h_ref:
        h_ref[0][...] = h


def _norm_mod_matmul(x, g, sc, sh, w, tl, want_h=False):
    b, l, d = x.shape
    n = w.shape[1]
    row = pl.BlockSpec((None, 1, d), lambda bi, i: (bi, 0, 0))
    out_shape = [jax.ShapeDtypeStruct((b, l, n), F32)]
    out_specs = [pl.BlockSpec((None, tl, n), lambda bi, i: (bi, i, 0))]
    if want_h:
        out_shape.append(jax.ShapeDtypeStruct((b, l, d), F32))
        out_specs.append(pl.BlockSpec((None, tl, d), lambda bi, i: (bi, i, 0)))
    res = pl.pallas_call(
        _nmm_kernel,
        grid=(b, l // tl),
        in_specs=[pl.BlockSpec((None, tl, d), lambda bi, i: (bi, i, 0)), _full((1, d)), row, row, _full((d, n))],
        out_specs=out_specs, out_shape=out_shape,
        compiler_params=_cparams("parallel", "parallel"),
        name="norm_mod_matmul",
    )(x, g.reshape(1, d), sc, sh, w)
    return res if want_h else res[0]


def _s5_kernel(xa_ref, h0r_ref, h0i_ref, bblk_ref, ar_ref, ai_ref, cr_ref, ci_ref, d_ref, wglu_ref, bglu_ref,
               ya_ref, hr_out_ref, hi_out_ref, bur, bui, st_r, st_i, *, tl):
    nj = S5_WIDTH // LANES
    half = 8 * S5_STATE

    @pl.when(pl.program_id(1) == 0)
    def _():
        st_r[...] = h0r_ref[...]
        st_i[...] = h0i_ref[...]

    xa = xa_ref[...]
    for j in range(nj):
        bu = jnp.dot(xa[:, j * LANES:(j + 1) * LANES], bblk_ref[j], precision=_HI, preferred_element_type=F32)
        bur[j] = bu[:, :half]
        bui[j] = bu[:, half:]

    for j0 in range(0, nj, 2):
        js = (j0, j0 + 1)
        a_r = [ar_ref[j] for j in js]
        a_i = [ai_ref[j] for j in js]

        def body(k, carry, js=js, a_r=a_r, a_i=a_i):
            carry = list(carry)
            base = pl.multiple_of(k * 8, 8)
            for s in range(8):
                t = base + s
                for q, j in enumerate(js):
                    hr, hi = carry[2 * q], carry[2 * q + 1]
                    nhr = a_r[q] * hr - a_i[q] * hi + bur[j, pl.ds(t, 1), :]
                    nhi = a_r[q] * hi + a_i[q] * hr + bui[j, pl.ds(t, 1), :]
                    bur[j, pl.ds(t, 1), :] = nhr
                    bui[j, pl.ds(t, 1), :] = nhi
                    carry[2 * q], carry[2 * q + 1] = nhr, nhi
            return tuple(carry)

        init = (st_r[js[0]], st_i[js[0]], st_r[js[1]], st_i[js[1]])
        fin = lax.fori_loop(0, tl // 8, body, init)
        st_r[js[0]], st_i[js[0]], st_r[js[1]], st_i[js[1]] = fin

    hr_out_ref[...] = st_r[...]
    hi_out_ref[...] = st_i[...]

    ys = []
    for j in range(nj):
        y = (jnp.dot(bur[j].astype(BF16), cr_ref[j], preferred_element_type=F32)
             + jnp.dot(bui[j].astype(BF16), ci_ref[j], preferred_element_type=F32))
        ys.append(y + d_ref[:, j * LANES:(j + 1) * LANES] * xa[:, j * LANES:(j + 1) * LANES])
    y = jax.nn.gelu(jnp.concatenate(ys, axis=1))
    z = jnp.dot(y.astype(BF16), wglu_ref[...], preferred_element_type=F32) + bglu_ref[...]
    ya_ref[...] = z[:, :S5_WIDTH] * jax.nn.sigmoid(z[:, S5_WIDTH:])


def _s5_params(lam_re, lam_im, log_dt, b_re, b_im, c_re, c_im):
    lam = lax.complex(lam_re.astype(F32), lam_im.astype(F32))
    dt = jnp.exp(log_dt.astype(F32))[:, None]
    a_bar = jnp.exp(lam * dt)
    b_bar = ((a_bar - 1.0) / lam)[..., None] * lax.complex(b_re.astype(F32), b_im.astype(F32))
    nj = S5_GROUPS // 8
    eye = jnp.eye(8, dtype=F32)

    def blk_b(m):
        m = m.reshape(nj, 8, S5_STATE, S5_GROUP)
        return jnp.einsum('jgpc,gh->jgchp', m, eye).reshape(nj, 8 * S5_GROUP, 8 * S5_STATE)

    def blk_c(m):
        m = m.reshape(nj, 8, S5_GROUP, S5_STATE)
        return jnp.einsum('jgcp,gh->jgphc', m, eye).reshape(nj, 8 * S5_STATE, 8 * S5_GROUP)

    bblk = jnp.concatenate([blk_b(jnp.real(b_bar)), blk_b(jnp.imag(b_bar))], axis=-1)
    cr = blk_c(c_re.astype(F32)).astype(BF16)
    ci = (-blk_c(c_im.astype(F32))).astype(BF16)
    a_r = jnp.real(a_bar).reshape(nj, 1, 8 * S5_STATE)
    a_i = jnp.imag(a_bar).reshape(nj, 1, 8 * S5_STATE)
    return bblk, a_r, a_i, cr, ci


def _s5(y_in, h0_re, h0_im, sp, d_skip, w_glu, b_glu, tl):
    b, l, _ = y_in.shape
    nj = S5_GROUPS // 8
    half = 8 * S5_STATE
    bblk, a_r, a_i, cr, ci = sp
    st = pl.BlockSpec((None, nj, 1, half), lambda bi, i: (bi, 0, 0, 0))
    ya, hr, hi = pl.pallas_call(
        functools.partial(_s5_kernel, tl=tl),
        grid=(b, l // tl),
        in_specs=[pl.BlockSpec((None, tl, S5_WIDTH), lambda bi, i: (bi, i, 0)), st, st,
                  _full(bblk.shape), _full(a_r.shape), _full(a_i.shape), _full(cr.shape), _full(ci.shape),
                  _full((1, S5_WIDTH)), _full((S5_WIDTH, 2 * S5_WIDTH)), _full((1, 2 * S5_WIDTH))],
        out_specs=[pl.BlockSpec((None, tl, S5_WIDTH), lambda bi, i: (bi, i, 0)), st, st],
        out_shape=[jax.ShapeDtypeStruct((b, l, S5_WIDTH), F32),
                   jax.ShapeDtypeStruct((b, nj, 1, half), F32), jax.ShapeDtypeStruct((b, nj, 1, half), F32)],
        scratch_shapes=[pltpu.VMEM((nj, tl, half), F32), pltpu.VMEM((nj, tl, half), F32),
                        pltpu.VMEM((nj, 1, half), F32), pltpu.VMEM((nj, 1, half), F32)],
        compiler_params=_cparams("parallel", "arbitrary"),
        name="s5",
    )(y_in, h0_re.reshape(b, nj, 1, half), h0_im.reshape(b, nj, 1, half), bblk, a_r, a_i, cr, ci,
      d_skip.reshape(1, S5_WIDTH), w_glu.astype(BF16), b_glu.reshape(1, 2 * S5_WIDTH))
    return ya, hr.reshape(b, S5_GROUPS, S5_STATE), hi.reshape(b, S5_GROUPS, S5_STATE)


def _gmlp_kernel(u_ref, v_ref, gv_ref, ws_ref, bias_ref, yb_ref, *vn_ref, tl, lc):
    hd = GM_WIDTH // GM_HEADS
    gv = jax.nn.gelu(v_ref[...])
    vn = gv * lax.rsqrt(jnp.mean(gv * gv, axis=-1, keepdims=True) + EPS) * gv_ref[...]
    if vn_ref:
        vn_ref[0][...] = vn
    gu = jax.nn.gelu(u_ref[...])
    vb = vn.astype(BF16)
    for c in range(tl // lc):
        rows = slice(c * lc, (c + 1) * lc)
        for h in range(GM_HEADS):
            cols = slice(h * hd, (h + 1) * hd)
            mixed = jnp.dot(ws_ref[h], vb[rows, cols], preferred_element_type=F32) + bias_ref[:, cols]
            yb_ref[rows, cols] = gu[rows, cols] * mixed


def _gmlp(y_in, g_v, w_s, b_s, tl, want_vn):
    b, l, _ = y_in.shape
    lc = min(GM_CHUNK, l)
    hd = GM_WIDTH // GM_HEADS
    tri = jnp.tril(jnp.ones((GM_CHUNK, GM_CHUNK), dtype=bool))
    ws = jnp.where(tri[None], w_s, 0.0)[:, :lc, :lc].astype(BF16)
    bias = jnp.repeat(jnp.transpose(b_s)[:lc], hd, axis=1)
    out_shape = [jax.ShapeDtypeStruct((b, l, GM_WIDTH), F32)]
    out_specs = [pl.BlockSpec((None, tl, GM_WIDTH), lambda bi, i: (bi, i, 0))]
    if want_vn:
        out_shape.append(jax.ShapeDtypeStruct((b, l, GM_WIDTH), F32))
        out_specs.append(pl.BlockSpec((None, tl, GM_WIDTH), lambda bi, i: (bi, i, 0)))
    res = pl.pallas_call(
        functools.partial(_gmlp_kernel, tl=tl, lc=lc),
        grid=(b, l // tl),
        in_specs=[pl.BlockSpec((None, tl, GM_WIDTH), lambda bi, i: (bi, i, 1)),
                  pl.BlockSpec((None, tl, GM_WIDTH), lambda bi, i: (bi, i, 2)),
                  _full((1, GM_WIDTH)), _full(ws.shape), _full(bias.shape)],
        out_specs=out_specs, out_shape=out_shape,
        compiler_params=_cparams("parallel", "parallel"),
        name="gmlp",
    )(y_in, y_in, g_v.reshape(1, GM_WIDTH), ws, bias)
    return (res[0], res[1]) if want_vn else (res[0], None)


def _proj_res_kernel(x_ref, gt_ref, a_ref, b_ref, wa_ref, wb_ref, o_ref):
    mix = (jnp.dot(a_ref[...].astype(BF16), wa_ref[...], preferred_element_type=F32)
           + jnp.dot(b_ref[...].astype(BF16), wb_ref[...], preferred_element_type=F32))
    o_ref[...] = x_ref[...] + gt_ref[...] * mix


def _proj_residual(x, gt, a, bm, w, tl):
    b, l, d = x.shape
    ka = a.shape[-1]
    wa, wb = w[:ka].astype(BF16), w[ka:].astype(BF16)
    blk = lambda n: pl.BlockSpec((None, tl, n), lambda bi, i: (bi, i, 0))
    return pl.pallas_call(
        _proj_res_kernel,
        grid=(b, l // tl),
        in_specs=[blk(d), pl.BlockSpec((None, 1, d), lambda bi, i: (bi, 0, 0)), blk(ka), blk(bm.shape[-1]),
                  _full(wa.shape), _full(wb.shape)],
        out_specs=blk(d), out_shape=jax.ShapeDtypeStruct((b, l, d), F32),
        compiler_params=_cparams("parallel", "parallel"),
        name="proj_residual",
    )(x, gt, a, bm, wa, wb)


def _rope_tables(pos0, l):
    pos = (pos0 + jnp.arange(l, dtype=jnp.int32)).astype(F32)[:, None]
    lane = jnp.arange(LANES)

    def table(period, start, rot):
        half = rot // 2
        r = lane % period - start
        inside = (r >= 0) & (r < rot)
        k = jnp.where(inside, r % half, 0)
        inv = ROPE_THETA ** (-k.astype(F32) * 2.0 / rot)
        ang = pos * inv[None, :]
        cos = jnp.where(inside[None], jnp.cos(ang), 1.0)
        sgn = jnp.where(r < half, -1.0, 1.0)
        sin = jnp.where(inside[None], jnp.sin(ang) * sgn[None], 0.0)
        return cos, sin

    return table(DA_DH, 0, DA_ROPE), table(LANES, MLA_NOPE, MLA_ROPE), table(LANES, 0, MLA_ROPE)


def _rotate(x, cos, sin, period, start, rot):
    half = rot // 2
    n = x.shape[-1]
    reps = n // LANES
    lane = (lax.broadcasted_iota(jnp.int32, x.shape, 1) & (period - 1)) - start
    up = pltpu.roll(x, n - half, 1)
    dn = pltpu.roll(x, half, 1)
    partner = jnp.where(lane < half, up, dn)
    if reps > 1:
        cos = jnp.concatenate([cos] * reps, axis=1)
        sin = jnp.concatenate([sin] * reps, axis=1)
    return x * cos + partner * sin


def _odd_prep_kernel(q_ref, k_ref, cq_ref, ckv_ref, kpe_ref, cd_ref, sd_ref, cm_ref, sm_ref, ck_ref, sk_ref,
                     gq_ref, wuq_ref, gkv_ref, qd_ref, kd_ref, qm_ref, lat_ref, kpe_out_ref):
    cd, sd = cd_ref[...], sd_ref[...]
    qd_ref[...] = _rotate(q_ref[...], cd, sd, DA_DH, 0, DA_ROPE).astype(BF16)
    kd_ref[...] = _rotate(k_ref[...], cd, sd, DA_DH, 0, DA_ROPE)
    cq = cq_ref[...]
    cqn = cq * lax.rsqrt(jnp.mean(cq * cq, axis=-1, keepdims=True) + EPS) * gq_ref[...]
    qm = jnp.dot(cqn.astype(BF16), wuq_ref[...], preferred_element_type=F32)
    qm_ref[...] = _rotate(qm, cm_ref[...], sm_ref[...], LANES, MLA_NOPE, MLA_ROPE).astype(BF16)
    ckv = ckv_ref[...]
    lat_ref[...] = ckv * lax.rsqrt(jnp.mean(ckv * ckv, axis=-1, keepdims=True) + EPS) * gkv_ref[...]
    kpe = _rotate(kpe_ref[...], ck_ref[...], sk_ref[...], LANES, 0, MLA_ROPE)
    kpe_out_ref[...] = kpe[:, :MLA_ROPE]


def _odd_prep(y_in, pos0, g_q, w_uq, g_kv, tl):
    b, l, _ = y_in.shape
    (cd, sd), (cm, sm), (ck, sk) = _rope_tables(pos0, l)
    per = MLA_NOPE + MLA_ROPE
    wuq = jnp.pad(w_uq.reshape(MLA_Q_RANK, MLA_HEADS, per), ((0, 0), (0, 0), (0, LANES - per)))
    wuq = wuq.reshape(MLA_Q_RANK, MLA_HEADS * LANES).astype(BF16)
    col = lambda w, j: pl.BlockSpec((None, tl, w), lambda bi, i: (bi, i, j))
    tab = pl.BlockSpec((tl, LANES), lambda bi, i: (i, 0))
    out = lambda w: pl.BlockSpec((None, tl, w), lambda bi, i: (bi, i, 0))
    return pl.pallas_call(
        _odd_prep_kernel,
        grid=(b, l // tl),
        in_specs=[col(512, 0), col(512, 1), col(256, 6), col(128, 14), col(128, 15),
                  tab, tab, tab, tab, tab, tab,
                  _full((1, MLA_Q_RANK)), _full(wuq.shape), _full((1, MLA_KV_RANK))],
        out_specs=[out(512), out(512), out(512), out(MLA_KV_RANK), out(MLA_ROPE)],
        out_shape=[jax.ShapeDtypeStruct((b, l, 512), BF16), jax.ShapeDtypeStruct((b, l, 512), F32),
                   jax.ShapeDtypeStruct((b, l, 512), BF16), jax.ShapeDtypeStruct((b, l, MLA_KV_RANK), F32),
                   jax.ShapeDtypeStruct((b, l, MLA_ROPE), F32)],
        compiler_params=_cparams("parallel", "parallel"),
        name="odd_prep",
    )(y_in, y_in, y_in, y_in, y_in, cd, sd, cm, sm, ck, sk,
      g_q.reshape(1, MLA_Q_RANK), wuq, g_kv.reshape(1, MLA_KV_RANK))


def _mla_kv_kernel(lat_ref, kpe_ref, wk_ref, wv_ref, place_ref, k_ref, v_ref):
    lat = lat_ref[...].astype(BF16)
    k = (jnp.dot(lat, wk_ref[...], preferred_element_type=F32)
         + jnp.dot(kpe_ref[...].astype(BF16), place_ref[...], preferred_element_type=F32))
    k_ref[...] = k.astype(BF16)
    v_ref[...] = jnp.dot(lat, wv_ref[...], preferred_element_type=F32).astype(BF16)


def _mla_kv(lat_all, kpe_all, w_ukv, tk):
    b, k, _ = lat_all.shape
    per = MLA_NOPE + MLA_DV
    w3 = w_ukv.reshape(MLA_KV_RANK, MLA_HEADS, per)
    wk = jnp.pad(w3[:, :, :MLA_NOPE], ((0, 0), (0, 0), (0, LANES - MLA_NOPE))).reshape(MLA_KV_RANK, MLA_HEADS * LANES)
    wv = w3[:, :, MLA_NOPE:].reshape(MLA_KV_RANK, MLA_HEADS * MLA_DV)
    place = jnp.pad(jnp.eye(MLA_ROPE, dtype=F32), ((0, 0), (MLA_NOPE, LANES - MLA_NOPE - MLA_ROPE)))
    place = jnp.tile(place, (1, MLA_HEADS))
    blk = lambda w: pl.BlockSpec((None, tk, w), lambda bi, i: (bi, i, 0))
    return pl.pallas_call(
        _mla_kv_kernel,
        grid=(b, k // tk),
        in_specs=[blk(MLA_KV_RANK), blk(MLA_ROPE), _full(wk.shape), _full(wv.shape), _full(place.shape)],
        out_specs=[blk(512), blk(512)],
        out_shape=[jax.ShapeDtypeStruct((b, k, 512), BF16), jax.ShapeDtypeStruct((b, k, 512), BF16)],
        compiler_params=_cparams("parallel", "parallel"),
        name="mla_kv",
    )(lat_all, kpe_all, wk.astype(BF16), wv.astype(BF16), place.astype(BF16))


def _visible_limit(pos_last):
    shift = CHUNK.bit_length() - 1
    return ((pos_last >> shift) + 1) << shift


def _fold_lanes(x, op):
    n = x.shape[-1]
    if n % LANES:
        return x
    out = x[:, :LANES]
    for c in range(1, n // LANES):
        out = op(out, x[:, c * LANES:(c + 1) * LANES])
    return out


def _flash_kernel(lam_ref, q_ref, k_ref, v_ref, g_ref, o_ref, m_ref, l_ref, acc_ref, *,
                  nhm, dqk, maps, scale, pos0, tq, tk, nkv, out_scale, rq):
    i, j = pl.program_id(1), pl.program_id(2)

    @pl.when(j == 0)
    def _():
        m_ref[...] = jnp.full(m_ref.shape, NEG, F32)
        l_ref[...] = jnp.zeros(l_ref.shape, F32)
        acc_ref[...] = jnp.zeros(acc_ref.shape, F32)

    c2 = scale * math.log2(math.e)
    visible = j * tk < _visible_limit(pos0 + (i + 1) * tq - 1)
    unmasked = (j + 1) * tk <= _visible_limit(pos0 + i * tq)

    def process(masked):
        for hm in range(nhm):
            k = k_ref[:, hm * dqk:(hm + 1) * dqk].astype(BF16)
            h = hm // maps
            v = v_ref[:, h * DA_DV:(h + 1) * DA_DV].astype(BF16)

            def qblock(r, carry, hm=hm, k=k, v=v):
                r0 = r * rq if isinstance(r, int) else pl.multiple_of(r * rq, rq)
                rows = pl.ds(r0, rq)
                q = q_ref[rows, hm * dqk:(hm + 1) * dqk].astype(BF16)
                s = lax.dot_general(q, k, (((1,), (1,)), ((), ())), preferred_element_type=F32)
                if masked:
                    q_pos = pos0 + i * tq + r0 + lax.broadcasted_iota(jnp.int32, (rq, tk), 0)
                    k_pos = j * tk + lax.broadcasted_iota(jnp.int32, (rq, tk), 1)
                    s = jnp.where(k_pos < _visible_limit(q_pos), s, NEG)
                m_prev = m_ref[hm, rows, :]
                m_new = jnp.maximum(m_prev, jnp.max(_fold_lanes(s, jnp.maximum), axis=-1, keepdims=True))
                p = jnp.exp2((s - m_new) * c2)
                alpha = jnp.exp2((m_prev - m_new) * c2)
                l_ref[hm, rows, :] = (alpha * l_ref[hm, rows, :]
                                      + jnp.sum(_fold_lanes(p, jnp.add), axis=-1, keepdims=True))
                acc_ref[hm, rows, :] = (alpha * acc_ref[hm, rows, :]
                                        + jnp.dot(p.astype(BF16), v, preferred_element_type=F32))
                m_ref[hm, rows, :] = m_new
                return carry

            if tq == rq:
                qblock(0, 0)
            else:
                lax.fori_loop(0, tq // rq, qblock, 0)

    pl.when(visible & unmasked)(lambda: process(False))
    pl.when(visible & jnp.logical_not(unmasked))(lambda: process(True))

    @pl.when(j == nkv - 1)
    def _():
        for h in range(nhm // maps):
            if maps == 2:
                o = acc_ref[2 * h] / l_ref[2 * h] - lam_ref[0] * (acc_ref[2 * h + 1] / l_ref[2 * h + 1])
                o = o * lax.rsqrt(jnp.mean(o * o, axis=-1, keepdims=True) + EPS) * g_ref[...] * out_scale
            else:
                o = acc_ref[h] / l_ref[h]
            o_ref[:, h * DA_DV:(h + 1) * DA_DV] = o


def _flash(q, k, v, lam, g_sub, *, nhm, dqk, maps, scale, pos0, tq, tk, out_scale, v_col=0):
    b, l, _ = q.shape
    kk = k.shape[1]
    nkv = kk // tk
    nh = nhm // maps

    def kv_idx(bi, i, j):
        last = (_visible_limit(pos0 + (i + 1) * tq - 1) - 1) // tk
        return jnp.minimum(j, last)

    return pl.pallas_call(
        functools.partial(_flash_kernel, nhm=nhm, dqk=dqk, maps=maps, scale=scale, pos0=pos0, tq=tq, tk=tk,
                          nkv=nkv, out_scale=out_scale, rq=min(FLASH_RQ, tq)),
        grid=(b, l // tq, nkv),
        in_specs=[pl.BlockSpec(memory_space=pltpu.SMEM),
                  pl.BlockSpec((None, tq, nhm * dqk), lambda bi, i, j: (bi, i, 0)),
                  pl.BlockSpec((None, tk, nhm * dqk), lambda bi, i, j: (bi, kv_idx(bi, i, j), 0)),
                  pl.BlockSpec((None, tk, nh * DA_DV), lambda bi, i, j: (bi, kv_idx(bi, i, j), v_col)),
                  _full((1, DA_DV))],
        out_specs=pl.BlockSpec((None, tq, nh * DA_DV), lambda bi, i, j: (bi, i, 0)),
        out_shape=jax.ShapeDtypeStruct((b, l, nh * DA_DV), F32),
        scratch_shapes=[pltpu.VMEM((nhm, tq, 1), F32), pltpu.VMEM((nhm, tq, 1), F32),
                        pltpu.VMEM((nhm, tq, DA_DV), F32)],
        compiler_params=_cparams("parallel", "parallel", "arbitrary"),
        name="flash_da" if maps == 2 else "flash_mla",
    )(lam, q, k, v, g_sub)


def _top16(s, rid, payload, val_ref, pay_ref):
    big = float(s.shape[0])
    for it in range(PEER_TOPK):
        m = jnp.max(s, axis=0, keepdims=True)
        am = jnp.min(jnp.where(s == m, rid, big), axis=0, keepdims=True)
        sel = rid == am
        val_ref[it:it + 1, :] = m
        if payload is None:
            pay_ref[it:it + 1, :] = am
        else:
            pay_ref[it:it + 1, :] = jnp.max(jnp.where(sel, payload, -1.0), axis=0, keepdims=True)
        s = jnp.where(sel, -jnp.inf, s)


_PEER_PAIRS = [(a, b) for a in range(PEER_TOPK) for b in range(PEER_TOPK) if (a + 1) * (b + 1) <= PEER_TOPK]
_PEER_CAND_ROWS = -(-len(_PEER_PAIRS) // 8) * 8


def _route_kernel(q_ref, k1_ref, k2_ref, idx_ref, g_ref, v1_ref, i1_ref, v2_ref, i2_ref, vc_ref, ic_ref,
                  cand_ref, cidx_ref):
    tt = q_ref.shape[0]
    half = q_ref.shape[1] // 2
    nt = (((1,), (1,)), ((), ()))
    s1 = lax.dot_general(k1_ref[...].astype(BF16), q_ref[:, :half].astype(BF16), nt, preferred_element_type=F32)
    s2 = lax.dot_general(k2_ref[...].astype(BF16), q_ref[:, half:].astype(BF16), nt, preferred_element_type=F32)
    rid = lax.broadcasted_iota(jnp.int32, (PEER_NKEYS, tt), 0).astype(F32)
    _top16(s1, rid, None, v1_ref, i1_ref)
    _top16(s2, rid, None, v2_ref, i2_ref)
    npairs = len(_PEER_PAIRS)
    cand_ref[npairs:, :] = jnp.full((_PEER_CAND_ROWS - npairs, tt), -jnp.inf, F32)
    cidx_ref[npairs:, :] = jnp.full((_PEER_CAND_ROWS - npairs, tt), -1.0, F32)
    for r, (a, b) in enumerate(_PEER_PAIRS):
        cand_ref[r:r + 1, :] = v1_ref[a:a + 1, :] + v2_ref[b:b + 1, :]
        cidx_ref[r:r + 1, :] = i1_ref[a:a + 1, :] * float(PEER_NKEYS) + i2_ref[b:b + 1, :]
    rid2 = lax.broadcasted_iota(jnp.int32, (_PEER_CAND_ROWS, tt), 0).astype(F32)
    _top16(cand_ref[...], rid2, cidx_ref[...], vc_ref, ic_ref)
    sc = vc_ref[...]
    e = jnp.exp(sc - sc[0:1, :])
    g_ref[...] = e / jnp.sum(e, axis=0, keepdims=True)
    idx_ref[...] = ic_ref[...].astype(jnp.int32)


def _peer_route(q, k1, k2):
    t = q.shape[0]
    tt = PEER_TT
    dk = 2 * PEER_NKEYS
    kspec = pl.BlockSpec((None, PEER_NKEYS, dk // 2), lambda i, h: (h, 0, 0))
    ospec = pl.BlockSpec((None, PEER_TOPK, tt), lambda i, h: (i, h, 0))
    sm = pltpu.VMEM((PEER_TOPK, tt), F32)
    return pl.pallas_call(
        _route_kernel,
        grid=(t // tt, PEER_HEADS),
        in_specs=[pl.BlockSpec((tt, dk), lambda i, h: (i, h)), kspec, kspec],
        out_specs=[ospec, ospec],
        out_shape=[jax.ShapeDtypeStruct((t // tt, PEER_HK, tt), jnp.int32),
                   jax.ShapeDtypeStruct((t // tt, PEER_HK, tt), F32)],
        scratch_shapes=[sm, sm, sm, sm, sm, sm,
                        pltpu.VMEM((_PEER_CAND_ROWS, tt), F32), pltpu.VMEM((_PEER_CAND_ROWS, tt), F32)],
        compiler_params=_cparams("parallel", "parallel"),
        name="peer_route",
    )(q, k1, k2)


def _pack_table(tab):
    e, d = tab.shape
    tb = lax.bitcast_convert_type(tab.astype(BF16), jnp.uint16).astype(jnp.uint32)
    words = tb[:, :d // 2] | (tb[:, d // 2:] << 16)
    return lax.bitcast_convert_type(words, F32)


def _sc_unpack(words):
    bits = lax.bitcast_convert_type(words, jnp.int32)
    return lax.bitcast_convert_type(bits << 16, F32), lax.bitcast_convert_type(bits & jnp.int32(-65536), F32)


def _sc_per_token(tab, idx_tm, side, out_cols, compute):
    t, nk = idx_tm.shape
    w = tab.shape[1]
    half_rows = nk // 2
    mesh = plsc.VectorSubcoreMesh(core_axis_name="core", subcore_axis_name="subcore")
    n_workers = mesh.num_cores * mesh.num_subcores
    tpw = t // n_workers
    tb = SC_TOKEN_BLOCK
    steps = 2 * tb
    assert t % n_workers == 0 and tpw % tb == 0 and half_rows % SC_LANES == 0 and w % SC_LANES == 0

    @functools.partial(
        pl.kernel, out_type=jax.ShapeDtypeStruct((t, out_cols), F32), mesh=mesh,
        scratch_types=[pltpu.VMEM((tb * nk,), jnp.int32), pltpu.VMEM((tb, side.shape[1]), F32),
                       pltpu.VMEM((2, half_rows, w), F32), pltpu.VMEM((tb, out_cols), F32),
                       pltpu.SemaphoreType.DMA((2,))],
        compiler_params=pltpu.CompilerParams(needs_layout_passes=False))
    def per_token_kernel(tab_hbm, idx_hbm, side_hbm, out_hbm, idx_v, side_v, rows_v, out_v, sems):
        wid = lax.axis_index("subcore") * mesh.num_cores + lax.axis_index("core")

        def gather(step, buf):
            rows = idx_v.at[pl.ds(pl.multiple_of(step * half_rows, half_rows), half_rows)]
            return pltpu.make_async_copy(tab_hbm.at[rows], rows_v.at[buf], sems.at[buf])

        def work(step, buf):
            compute(step // 2, (step % 2) * half_rows, lambda r, ds: rows_v[buf, r, ds], side_v, out_v)

        @pl.loop(0, tpw // tb)
        def _(blk):
            tok0 = wid * tpw + blk * tb
            pltpu.sync_copy(idx_hbm.at[pl.ds(pl.multiple_of(tok0 * nk, nk), tb * nk)], idx_v)
            pltpu.sync_copy(side_hbm.at[pl.ds(tok0, tb)], side_v)
            gather(0, 0).start()

            @pl.loop(0, steps, step=2)
            def _(s):
                gather(s + 1, 1).start()
                gather(s, 0).wait()
                work(s, 0)

                @pl.when(s + 2 < steps)
                def _():
                    gather(s + 2, 0).start()

                gather(s + 1, 1).wait()
                work(s + 1, 1)

            pltpu.sync_copy(out_v, out_hbm.at[pl.ds(tok0, tb)])

    return per_token_kernel(tab, idx_tm.reshape(t * nk), side)


def _sc_dot(tab, idx_tm, h):
    w = tab.shape[1]
    nk = idx_tm.shape[1]
    half_rows = nk // 2

    def compute(tok, col0, rows, h_v, act_v):
        lane = lax.iota(jnp.int32, SC_LANES)

        def sixteen(q, _):
            def four_rows(g, vec):
                r0 = q * SC_LANES + g * 4

                def chunk(c, accs):
                    ds = pl.ds(pl.multiple_of(c * SC_LANES, SC_LANES), SC_LANES)
                    h_lo = h_v[tok, ds]
                    h_hi = h_v[tok, pl.ds(pl.multiple_of(w + c * SC_LANES, SC_LANES), SC_LANES)]
                    out = []
                    for r in range(4):
                        lo, hi = _sc_unpack(rows(r0 + r, ds))
                        out.append(accs[r] + lo * h_lo + hi * h_hi)
                    return tuple(out)

                zero = jnp.zeros((SC_LANES,), F32)
                accs = lax.fori_loop(0, w // SC_LANES, chunk, (zero, zero, zero, zero), unroll=4)
                for r in range(4):
                    vec = jnp.where(lane == g * 4 + r, jnp.sum(accs[r]), vec)
                return vec

            vec = lax.fori_loop(0, SC_LANES // 4, four_rows, jnp.zeros((SC_LANES,), F32))
            act_v[tok, pl.ds(pl.multiple_of(col0 + q * SC_LANES, SC_LANES), SC_LANES)] = vec
            return 0

        lax.fori_loop(0, half_rows // SC_LANES, sixteen, 0)

    return _sc_per_token(tab, idx_tm, h, nk, compute)


def _sc_weighted_sum(tab, idx_tm, coef):
    w = tab.shape[1]
    nk = idx_tm.shape[1]
    half_rows = nk // 2
    group = 8

    def compute(tok, col0, rows, coef_v, out_v):
        lane = lax.iota(jnp.int32, SC_LANES)

        @pl.when(col0 == 0)
        def _():
            @plsc.parallel_loop(0, 2 * w // SC_LANES)
            def _(c):
                out_v[tok, pl.ds(pl.multiple_of(c * SC_LANES, SC_LANES), SC_LANES)] = jnp.zeros((SC_LANES,), F32)

        def eight_rows(g, _):
            cvec = coef_v[tok, pl.ds(pl.multiple_of(col0 + (g // 2) * SC_LANES, SC_LANES), SC_LANES)]
            base = (g % 2) * group
            cs = [jnp.sum(jnp.where(lane == base + r, cvec, 0.0)) for r in range(group)]
            r0 = g * group

            @plsc.parallel_loop(0, w // SC_LANES, unroll=2)
            def _(c):
                ds_lo = pl.ds(pl.multiple_of(c * SC_LANES, SC_LANES), SC_LANES)
                ds_hi = pl.ds(pl.multiple_of(w + c * SC_LANES, SC_LANES), SC_LANES)
                los, his = [], []
                for r in range(group):
                    lo, hi = _sc_unpack(rows(r0 + r, ds_lo))
                    los.append(cs[r] * lo)
                    his.append(cs[r] * hi)
                while len(los) > 1:
                    los = [a + b for a, b in zip(los[::2], los[1::2])]
                    his = [a + b for a, b in zip(his[::2], his[1::2])]
                out_v[tok, ds_lo] = out_v[tok, ds_lo] + los[0]
                out_v[tok, ds_hi] = out_v[tok, ds_hi] + his[0]

            return 0

        lax.fori_loop(0, half_rows // group, eight_rows, 0)

    return _sc_per_token(tab, idx_tm, coef, 2 * w, compute)


def _peer_coef_kernel(act_ref, g_ref, coef_ref):
    coef_ref[...] = g_ref[...].T * jax.nn.gelu(act_ref[...])


def _peer_coef(act, g):
    t, nk = act.shape
    tt = PEER_TT
    return pl.pallas_call(
        _peer_coef_kernel,
        grid=(t // tt,),
        in_specs=[pl.BlockSpec((tt, nk), lambda i: (i, 0)), pl.BlockSpec((None, nk, tt), lambda i: (i, 0, 0))],
        out_specs=pl.BlockSpec((tt, nk), lambda i: (i, 0)),
        out_shape=jax.ShapeDtypeStruct((t, nk), F32),
        compiler_params=_cparams("parallel"),
        name="peer_coef",
    )(act, g)


def _peer_out_kernel(x_ref, gt_ref, y_ref, gf_ref, o_ref, *, final):
    x = x_ref[...] + gt_ref[...] * y_ref[...]
    if final:
        x = x * lax.rsqrt(jnp.mean(x * x, axis=-1, keepdims=True) + EPS) * gf_ref[...]
    o_ref[...] = x


def _peer_out(x, gt, y, g_final, final):
    t, d = x.shape
    tt = PEER_TT
    nt = t // tt
    row = pl.BlockSpec((tt, d), lambda i: (i, 0))
    if gt.ndim == 3:
        per_seq = nt // gt.shape[0]
        gt_spec = pl.BlockSpec((None, 1, d), lambda i: (i // per_seq, 0, 0))
    else:
        gt_spec = row
    return pl.pallas_call(
        functools.partial(_peer_out_kernel, final=final),
        grid=(nt,),
        in_specs=[row, gt_spec, row, _full((1, d))],
        out_specs=row, out_shape=jax.ShapeDtypeStruct((t, d), F32),
        compiler_params=_cparams("parallel"),
        name="peer_out",
    )(x, gt, y, g_final.reshape(1, d))


def _peer(x, g_ffn, sc2, sh2, gt2, w_q, k1, k2, u_pack, v_pack, g_final, final, tl):
    b, l, d = x.shape
    t = b * l
    q, h = _norm_mod_matmul(x, g_ffn, sc2, sh2, w_q, tl, want_h=True)
    idx, g = _peer_route(q.reshape(t, -1), k1, k2)
    idx_tm = jnp.transpose(idx, (0, 2, 1)).reshape(t, PEER_HK)
    act = _sc_dot(u_pack, idx_tm, h.reshape(t, d))
    y = _sc_weighted_sum(v_pack, idx_tm, _peer_coef(act, g))
    if l % PEER_TT:
        gt2 = jnp.broadcast_to(gt2, (b, l, d)).reshape(t, d)
    return _peer_out(x.reshape(t, d), gt2, y, g_final, final).reshape(b, l, d)


def _trunk(x, mod, pos0, past, p, tl):
    b, l, d = x.shape
    outs = {}
    for layer in range(DEPTH):
        sh1, sc1, gt1, sh2, sc2, gt2 = [m[:, None, :] for m in jnp.split(mod[layer], 6, axis=-1)]
        if layer % 2 == 0:
            e = layer // 2
            y_in = _norm_mod_matmul(x, p['g_mix'][layer], sc1, sh1, p['w_in_e'][e].astype(BF16), tl)
            if past is None:
                h0_re = jnp.zeros((b, S5_GROUPS, S5_STATE), F32)
                h0_im = jnp.zeros((b, S5_GROUPS, S5_STATE), F32)
            else:
                h0_re, h0_im = past['s5_re'][e], past['s5_im'][e]
            ya, hr, hi = _s5(y_in, h0_re, h0_im, p['s5'][e], p['s5_d'][e], p['w_glu'][e], p['b_glu'][e], tl)
            yb, vn = _gmlp(y_in, p['gm_g_v'][e], p['gm_w_s'][e], p['gm_b_s'][e], tl, want_vn=past is not None)
            x = _proj_residual(x, gt1, ya, yb, p['w_out_e'][e], tl)
            outs.update(s5_re=hr, s5_im=hi, gm_v=vn)
        else:
            o = layer // 2
            w_in = jnp.pad(p['w_in_o'][o], ((0, 0), (0, 2048 - p['w_in_o'][o].shape[1]))).astype(BF16)
            y_in = _norm_mod_matmul(x, p['g_mix'][layer], sc1, sh1, w_in, tl)
            qd, kd, qm, lat, kpe = _odd_prep(y_in, pos0, p['mla_g_q'][o], p['mla_w_uq'][o], p['mla_g_kv'][o], tl)
            v_new = y_in[:, :, 2 * 512:3 * 512]
            lam_init = 0.8 - 0.6 * math.exp(-0.3 * layer)
            lam = (jnp.exp(jnp.sum(p['da_lq1'][o] * p['da_lk1'][o])) - jnp.exp(jnp.sum(p['da_lq2'][o] * p['da_lk2'][o]))
                   + lam_init).reshape(1).astype(F32)
            g_sub = p['da_g_sub'][o].reshape(1, DA_DV)
            if past is None:
                k_all, v_all, v_col, lat_all, kpe_all = kd, y_in, 2, lat, kpe
                tq = tk = tl
            else:
                k_all = jnp.concatenate([past['da_k'][o].reshape(b, -1, 512), kd], axis=1)
                v_all = jnp.concatenate([past['da_v'][o].reshape(b, -1, 512), v_new], axis=1)
                lat_all = jnp.concatenate([past['mla_lat'][o], lat], axis=1)
                kpe_all = jnp.concatenate([past['mla_kpe'][o], kpe], axis=1)
                v_col, tq, tk = 0, l, k_all.shape[1]
            yc = _flash(qd, k_all, v_all, lam, g_sub, nhm=2 * DA_HEADS, dqk=DA_DH, maps=2, scale=DA_DH ** -0.5,
                        pos0=pos0, tq=tq, tk=tk, out_scale=1.0 - lam_init, v_col=v_col)
            km, vm = _mla_kv(lat_all, kpe_all, p['mla_w_ukv'][o], tk)
            yd = _flash(qm, km, vm, lam, g_sub, nhm=MLA_HEADS, dqk=LANES, maps=1,
                        scale=(MLA_NOPE + MLA_ROPE) ** -0.5, pos0=pos0, tq=tq, tk=tk, out_scale=1.0)
            x = _proj_residual(x, gt1, yc, yd, p['w_out_o'][o], tl)
            outs.update(da_k=kd.reshape(b, l, DA_HEADS, 2 * DA_DH), da_v=v_new.reshape(b, l, DA_HEADS, DA_DV),
                        mla_lat=lat, mla_kpe=kpe)
        x = _peer(x, p['g_ffn'][layer], sc2, sh2, gt2, p['peer_w_q'][layer].astype(BF16), p['peer_k1'][layer],
                  p['peer_k2'][layer], p['u_pack'][layer], p['v_pack'][layer], p['g_final'],
                  final=layer == DEPTH - 1, tl=tl)
    return x, outs


def kernel(x_prompt, x_sample, c_prompt, c_sample, state_s5_re, state_s5_im, cache_da_k, cache_da_v, cache_mla_latent, cache_mla_kpe, w_ada, b_ada, g_mix, g_ffn, g_final, w_in_e, w_out_e, s5_lam_re, s5_lam_im, s5_log_dt, s5_b_re, s5_b_im, s5_c_re, s5_c_im, s5_d, w_glu, b_glu, gm_g_v, gm_w_s, gm_b_s, w_in_o, w_out_o, da_lq1, da_lk1, da_lq2, da_lk2, da_g_sub, mla_g_q, mla_w_uq, mla_g_kv, mla_w_ukv, peer_w_q, peer_k1, peer_k2, peer_u, peer_v):
    p = dict(g_mix=g_mix, g_ffn=g_ffn, g_final=g_final, w_in_e=w_in_e, w_out_e=w_out_e, s5_d=s5_d, w_glu=w_glu,
             b_glu=b_glu, gm_g_v=gm_g_v, gm_w_s=gm_w_s, gm_b_s=gm_b_s, w_in_o=w_in_o, w_out_o=w_out_o,
             da_lq1=da_lq1, da_lk1=da_lk1, da_lq2=da_lq2, da_lk2=da_lk2, da_g_sub=da_g_sub, mla_g_q=mla_g_q,
             mla_w_uq=mla_w_uq, mla_g_kv=mla_g_kv, mla_w_ukv=mla_w_ukv, peer_w_q=peer_w_q, peer_k1=peer_k1,
             peer_k2=peer_k2)
    n_even = (DEPTH + 1) // 2
    p['s5'] = [_s5_params(s5_lam_re[e], s5_lam_im[e], s5_log_dt[e], s5_b_re[e], s5_b_im[e], s5_c_re[e], s5_c_im[e])
               for e in range(n_even)]
    p['u_pack'] = [_pack_table(peer_u[layer]) for layer in range(DEPTH)]
    p['v_pack'] = [_pack_table(peer_v[layer]) for layer in range(DEPTH)]
    past = dict(s5_re=state_s5_re, s5_im=state_s5_im, da_k=cache_da_k, da_v=cache_da_v,
                mla_lat=cache_mla_latent, mla_kpe=cache_mla_kpe)
    nb = x_prompt.shape[0]
    mod = _ada(jnp.concatenate([c_prompt, c_sample], axis=0), w_ada, b_ada)
    past_len = cache_da_k.shape[2]
    parts = [_trunk(x_prompt[s:s + 1], mod[:, s:s + 1], 0, None, p, tl=512) for s in range(nb)]
    y_p = jnp.concatenate([y for y, _ in parts], axis=0)
    o_p = {k: jnp.concatenate([o[k] for _, o in parts], axis=0) for k in parts[0][1] if parts[0][1][k] is not None}
    y_s, o_s = _trunk(x_sample, mod[:, nb:], past_len, past, p, tl=x_sample.shape[1])
    st = lambda a: a[None]
    return (y_p, y_s, st(o_p['s5_re']), st(o_p['s5_im']), st(o_s['s5_re']), st(o_s['s5_im']), st(o_s['gm_v']),
            st(o_p['da_k']), st(o_p['da_v']), st(o_s['da_k']), st(o_s['da_v']),
            st(o_p['mla_lat']), st(o_p['mla_kpe']), st(o_s['mla_lat']), st(o_s['mla_kpe']))
```

```python
import functools
import math

import jax
import jax.numpy as jnp
from jax import lax
from jax.experimental import pallas as pl
from jax.experimental.pallas import tpu as pltpu
from jax.experimental.pallas import tpu_sc as plsc

F32 = jnp.float32
BF16 = jnp.bfloat16

D_MODEL = 1024
DEPTH = 2
CHUNK = 64
ROPE_THETA = 500000.0
EPS = 1e-6
NEG = -1e30

S5_WIDTH = 512
S5_GROUP = 16
S5_GROUPS = 32
S5_STATE = 64
GM_WIDTH = 512
GM_HEADS = 4
GM_CHUNK = 128
DA_HEADS = 4
DA_DH = 64
DA_DV = 128
DA_ROPE = 16
MLA_HEADS = 4
MLA_Q_RANK = 256
MLA_KV_RANK = 128
MLA_NOPE = 64
MLA_ROPE = 32
MLA_DV = 128
PEER_HEADS = 8
PEER_NKEYS = 128
PEER_TOPK = 16
PEER_HK = PEER_HEADS * PEER_TOPK

LANES = 128
VMEM_LIMIT = 48 << 20
PEER_TT = 256
SC_LANES = 16
SC_TOKEN_BLOCK = 8
FLASH_RQ = 512

_HI = lax.Precision.HIGHEST


def _cparams(*sem):
    return pltpu.CompilerParams(dimension_semantics=sem, vmem_limit_bytes=VMEM_LIMIT)


def _full(shape):
    n = len(shape)
    return pl.BlockSpec(shape, lambda *_: (0,) * n)


def _ada_kernel(c_ref, w_ref, b_ref, o_ref):
    c = c_ref[...]
    s = c * jax.nn.sigmoid(c)
    o_ref[...] = jnp.dot(s, w_ref[...], precision=_HI, preferred_element_type=F32) + b_ref[...]


def _ada(c_all, w_ada, b_ada):
    r = c_all.shape[0]
    tn = 1536
    return pl.pallas_call(
        _ada_kernel,
        grid=(DEPTH, 6 * D_MODEL // tn),
        in_specs=[_full((r, D_MODEL)),
                  pl.BlockSpec((None, D_MODEL, tn), lambda l, j: (l, 0, j)),
                  pl.BlockSpec((None, 1, tn), lambda l, j: (l, 0, j))],
        out_specs=pl.BlockSpec((None, r, tn), lambda l, j: (l, 0, j)),
        out_shape=jax.ShapeDtypeStruct((DEPTH, r, 6 * D_MODEL), F32),
        compiler_params=_cparams("parallel", "parallel"),
        name="ada",
    )(c_all, w_ada, b_ada.reshape(DEPTH, 1, 6 * D_MODEL))


def _nmm_kernel(x_ref, g_ref, sc_ref, sh_ref, w_ref, o_ref, *h_ref):
    x = x_ref[...]
    ms = jnp.mean(x * x, axis=-1, keepdims=True)
    h = x * lax.rsqrt(ms + EPS) * g_ref[...] * (1.0 + sc_ref[...]) + sh_ref[...]
    o_ref[...] = jnp.dot(h.astype(BF16), w_ref[...], preferred_element_type=F32)
    if h_ref:
        h_ref[0][...] = h


def _norm_mod_matmul(x, g, sc, sh, w, tl, want_h=False):
    b, l, d = x.shape
    n = w.shape[1]
    row = pl.BlockSpec((None, 1, d), lambda bi, i: (bi, 0, 0))
    out_shape = [jax.ShapeDtypeStruct((b, l, n), F32)]
    out_specs = [pl.BlockSpec((None, tl, n), lambda bi, i: (bi, i, 0))]
    if want_h:
        out_shape.append(jax.ShapeDtypeStruct((b, l, d), F32))
        out_specs.append(pl.BlockSpec((None, tl, d), lambda bi, i: (bi, i, 0)))
    res = pl.pallas_call(
        _nmm_kernel,
        grid=(b, l // tl),
        in_specs=[pl.BlockSpec((None, tl, d), lambda bi, i: (bi, i, 0)), _full((1, d)), row, row, _full((d, n))],
        out_specs=out_specs, out_shape=out_shape,
        compiler_params=_cparams("parallel", "parallel"),
        name="norm_mod_matmul",
    )(x, g.reshape(1, d), sc, sh, w)
    return res if want_h else res[0]


def _s5_kernel(xa_ref, h0r_ref, h0i_ref, bblk_ref, ar_ref, ai_ref, cr_ref, ci_ref, d_ref, wglu_ref, bglu_ref,
               ya_ref, hr_out_ref, hi_out_ref, bur, bui, st_r, st_i, *, tl):
    nj = S5_WIDTH // LANES
    half = 8 * S5_STATE

    @pl.when(pl.program_id(1) == 0)
    def _():
        st_r[...] = h0r_ref[...]
        st_i[...] = h0i_ref[...]

    xa = xa_ref[...]
    for j in range(nj):
        bu = jnp.dot(xa[:, j * LANES:(j + 1) * LANES], bblk_ref[j], precision=_HI, preferred_element_type=F32)
        bur[j] = bu[:, :half]
        bui[j] = bu[:, half:]

    for j0 in range(0, nj, 2):
        js = (j0, j0 + 1)
        a_r = [ar_ref[j] for j in js]
        a_i = [ai_ref[j] for j in js]

        def body(k, carry, js=js, a_r=a_r, a_i=a_i):
            carry = list(carry)
            base = pl.multiple_of(k * 8, 8)
            for s in range(8):
                t = base + s
                for q, j in enumerate(js):
                    hr, hi = carry[2 * q], carry[2 * q + 1]
                    nhr = a_r[q] * hr - a_i[q] * hi + bur[j, pl.ds(t, 1), :]
                    nhi = a_r[q] * hi + a_i[q] * hr + bui[j, pl.ds(t, 1), :]
                    bur[j, pl.ds(t, 1), :] = nhr
                    bui[j, pl.ds(t, 1), :] = nhi
                    carry[2 * q], carry[2 * q + 1] = nhr, nhi
            return tuple(carry)

        init = (st_r[js[0]], st_i[js[0]], st_r[js[1]], st_i[js[1]])
        fin = lax.fori_loop(0, tl // 8, body, init)
        st_r[js[0]], st_i[js[0]], st_r[js[1]], st_i[js[1]] = fin

    hr_out_ref[...] = st_r[...]
    hi_out_ref[...] = st_i[...]

    ys = []
    for j in range(nj):
        y = (jnp.dot(bur[j].astype(BF16), cr_ref[j], preferred_element_type=F32)
             + jnp.dot(bui[j].astype(BF16), ci_ref[j], preferred_element_type=F32))
        ys.append(y + d_ref[:, j * LANES:(j + 1) * LANES] * xa[:, j * LANES:(j + 1) * LANES])
    y = jax.nn.gelu(jnp.concatenate(ys, axis=1))
    z = jnp.dot(y.astype(BF16), wglu_ref[...], preferred_element_type=F32) + bglu_ref[...]
    ya_ref[...] = z[:, :S5_WIDTH] * jax.nn.sigmoid(z[:, S5_WIDTH:])


def _s5_params(lam_re, lam_im, log_dt, b_re, b_im, c_re, c_im):
    lam = lax.complex(lam_re.astype(F32), lam_im.astype(F32))
    dt = jnp.exp(log_dt.astype(F32))[:, None]
    a_bar = jnp.exp(lam * dt)
    b_bar = ((a_bar - 1.0) / lam)[..., None] * lax.complex(b_re.astype(F32), b_im.astype(F32))
    nj = S5_GROUPS // 8
    eye = jnp.eye(8, dtype=F32)

    def blk_b(m):
        m = m.reshape(nj, 8, S5_STATE, S5_GROUP)
        return jnp.einsum('jgpc,gh->jgchp', m, eye).reshape(nj, 8 * S5_GROUP, 8 * S5_STATE)

    def blk_c(m):
        m = m.reshape(nj, 8, S5_GROUP, S5_STATE)
        return jnp.einsum('jgcp,gh->jgphc', m, eye).reshape(nj, 8 * S5_STATE, 8 * S5_GROUP)

    bblk = jnp.concatenate([blk_b(jnp.real(b_bar)), blk_b(jnp.imag(b_bar))], axis=-1)
    cr = blk_c(c_re.astype(F32)).astype(BF16)
    ci = (-blk_c(c_im.astype(F32))).astype(BF16)
    a_r = jnp.real(a_bar).reshape(nj, 1, 8 * S5_STATE)
    a_i = jnp.imag(a_bar).reshape(nj, 1, 8 * S5_STATE)
    return bblk, a_r, a_i, cr, ci


def _s5(y_in, h0_re, h0_im, sp, d_skip, w_glu, b_glu, tl):
    b, l, _ = y_in.shape
    nj = S5_GROUPS // 8
    half = 8 * S5_STATE
    bblk, a_r, a_i, cr, ci = sp
    st = pl.BlockSpec((None, nj, 1, half), lambda bi, i: (bi, 0, 0, 0))
    ya, hr, hi = pl.pallas_call(
        functools.partial(_s5_kernel, tl=tl),
        grid=(b, l // tl),
        in_specs=[pl.BlockSpec((None, tl, S5_WIDTH), lambda bi, i: (bi, i, 0)), st, st,
                  _full(bblk.shape), _full(a_r.shape), _full(a_i.shape), _full(cr.shape), _full(ci.shape),
                  _full((1, S5_WIDTH)), _full((S5_WIDTH, 2 * S5_WIDTH)), _full((1, 2 * S5_WIDTH))],
        out_specs=[pl.BlockSpec((None, tl, S5_WIDTH), lambda bi, i: (bi, i, 0)), st, st],
        out_shape=[jax.ShapeDtypeStruct((b, l, S5_WIDTH), F32),
                   jax.ShapeDtypeStruct((b, nj, 1, half), F32), jax.ShapeDtypeStruct((b, nj, 1, half), F32)],
        scratch_shapes=[pltpu.VMEM((nj, tl, half), F32), pltpu.VMEM((nj, tl, half), F32),
                        pltpu.VMEM((nj, 1, half), F32), pltpu.VMEM((nj, 1, half), F32)],
        compiler_params=_cparams("parallel", "arbitrary"),
        name="s5",
    )(y_in, h0_re.reshape(b, nj, 1, half), h0_im.reshape(b, nj, 1, half), bblk, a_r, a_i, cr, ci,
      d_skip.reshape(1, S5_WIDTH), w_glu.astype(BF16), b_glu.reshape(1, 2 * S5_WIDTH))
    return ya, hr.reshape(b, S5_GROUPS, S5_STATE), hi.reshape(b, S5_GROUPS, S5_STATE)


def _gmlp_kernel(u_ref, v_ref, gv_ref, ws_ref, bias_ref, yb_ref, *vn_ref, tl, lc):
    hd = GM_WIDTH // GM_HEADS
    gv = jax.nn.gelu(v_ref[...])
    vn = gv * lax.rsqrt(jnp.mean(gv * gv, axis=-1, keepdims=True) + EPS) * gv_ref[...]
    if vn_ref:
        vn_ref[0][...] = vn
    gu = jax.nn.gelu(u_ref[...])
    vb = vn.astype(BF16)
    for c in range(tl // lc):
        rows = slice(c * lc, (c + 1) * lc)
        for h in range(GM_HEADS):
            cols = slice(h * hd, (h + 1) * hd)
            mixed = jnp.dot(ws_ref[h], vb[rows, cols], preferred_element_type=F32) + bias_ref[:, cols]
            yb_ref[rows, cols] = gu[rows, cols] * mixed


def _gmlp(y_in, g_v, w_s, b_s, tl, want_vn):
    b, l, _ = y_in.shape
    lc = min(GM_CHUNK, l)
    hd = GM_WIDTH // GM_HEADS
    tri = jnp.tril(jnp.ones((GM_CHUNK, GM_CHUNK), dtype=bool))
    ws = jnp.where(tri[None], w_s, 0.0)[:, :lc, :lc].astype(BF16)
    bias = jnp.repeat(jnp.transpose(b_s)[:lc], hd, axis=1)
    out_shape = [jax.ShapeDtypeStruct((b, l, GM_WIDTH), F32)]
    out_specs = [pl.BlockSpec((None, tl, GM_WIDTH), lambda bi, i: (bi, i, 0))]
    if want_vn:
        out_shape.append(jax.ShapeDtypeStruct((b, l, GM_WIDTH), F32))
        out_specs.append(pl.BlockSpec((None, tl, GM_WIDTH), lambda bi, i: (bi, i, 0)))
    res = pl.pallas_call(
        functools.partial(_gmlp_kernel, tl=tl, lc=lc),
        grid=(b, l // tl),
        in_specs=[pl.BlockSpec((None, tl, GM_WIDTH), lambda bi, i: (bi, i, 1)),
                  pl.BlockSpec((None, tl, GM_WIDTH), lambda bi, i: (bi, i, 2)),
                  _full((1, GM_WIDTH)), _full(ws.shape), _full(bias.shape)],
        out_specs=out_specs, out_shape=out_shape,
        compiler_params=_cparams("parallel", "parallel"),
        name="gmlp",
    )(y_in, y_in, g_v.reshape(1, GM_WIDTH), ws, bias)
    return (res[0], res[1]) if want_vn else (res[0], None)


def _proj_res_kernel(x_ref, gt_ref, a_ref, b_ref, wa_ref, wb_ref, o_ref):
    mix = (jnp.dot(a_ref[...].astype(BF16), wa_ref[...], preferred_element_type=F32)
           + jnp.dot(b_ref[...].astype(BF16), wb_ref[...], preferred_element_type=F32))
    o_ref[...] = x_ref[...] + gt_ref[...] * mix


def _proj_residual(x, gt, a, bm, w, tl):
    b, l, d = x.shape
    ka = a.shape[-1]
    wa, wb = w[:ka].astype(BF16), w[ka:].astype(BF16)
    blk = lambda n: pl.BlockSpec((None, tl, n), lambda bi, i: (bi, i, 0))
    return pl.pallas_call(
        _proj_res_kernel,
        grid=(b, l // tl),
        in_specs=[blk(d), pl.BlockSpec((None, 1, d), lambda bi, i: (bi, 0, 0)), blk(ka), blk(bm.shape[-1]),
                  _full(wa.shape), _full(wb.shape)],
        out_specs=blk(d), out_shape=jax.ShapeDtypeStruct((b, l, d), F32),
        compiler_params=_cparams("parallel", "parallel"),
        name="proj_residual",
    )(x, gt, a, bm, wa, wb)


def _rope_tables(pos0, l):
    pos = (pos0 + jnp.arange(l, dtype=jnp.int32)).astype(F32)[:, None]
    lane = jnp.arange(LANES)

    def table(period, start, rot):
        half = rot // 2
        r = lane % period - start
        inside = (r >= 0) & (r < rot)
        k = jnp.where(inside, r % half, 0)
        inv = ROPE_THETA ** (-k.astype(F32) * 2.0 / rot)
        ang = pos * inv[None, :]
        cos = jnp.where(inside[None], jnp.cos(ang), 1.0)
        sgn = jnp.where(r < half, -1.0, 1.0)
        sin = jnp.where(inside[None], jnp.sin(ang) * sgn[None], 0.0)
        return cos, sin

    return table(DA_DH, 0, DA_ROPE), table(LANES, MLA_NOPE, MLA_ROPE), table(LANES, 0, MLA_ROPE)


def _rotate(x, cos, sin, period, start, rot):
    half = rot // 2
    n = x.shape[-1]
    reps = n // LANES
    lane = (lax.broadcasted_iota(jnp.int32, x.shape, 1) & (period - 1)) - start
    up = pltpu.roll(x, n - half, 1)
    dn = pltpu.roll(x, half, 1)
    partner = jnp.where(lane < half, up, dn)
    if reps > 1:
        cos = jnp.concatenate([cos] * reps, axis=1)
        sin = jnp.concatenate([sin] * reps, axis=1)
    return x * cos + partner * sin


def _odd_prep_kernel(q_ref, k_ref, cq_ref, ckv_ref, kpe_ref, cd_ref, sd_ref, cm_ref, sm_ref, ck_ref, sk_ref,
                     gq_ref, wuq_ref, gkv_ref, qd_ref, kd_ref, qm_ref, lat_ref, kpe_out_ref):
    cd, sd = cd_ref[...], sd_ref[...]
    qd_ref[...] = _rotate(q_ref[...], cd, sd, DA_DH, 0, DA_ROPE).astype(BF16)
    kd_ref[...] = _rotate(k_ref[...], cd, sd, DA_DH, 0, DA_ROPE)
    cq = cq_ref[...]
    cqn = cq * lax.rsqrt(jnp.mean(cq * cq, axis=-1, keepdims=True) + EPS) * gq_ref[...]
    qm = jnp.dot(cqn.astype(BF16), wuq_ref[...], preferred_element_type=F32)
    qm_ref[...] = _rotate(qm, cm_ref[...], sm_ref[...], LANES, MLA_NOPE, MLA_ROPE).astype(BF16)
    ckv = ckv_ref[...]
    lat_ref[...] = ckv * lax.rsqrt(jnp.mean(ckv * ckv, axis=-1, keepdims=True) + EPS) * gkv_ref[...]
    kpe = _rotate(kpe_ref[...], ck_ref[...], sk_ref[...], LANES, 0, MLA_ROPE)
    kpe_out_ref[...] = kpe[:, :MLA_ROPE]


def _odd_prep(y_in, pos0, g_q, w_uq, g_kv, tl):
    b, l, _ = y_in.shape
    (cd, sd), (cm, sm), (ck, sk) = _rope_tables(pos0, l)
    per = MLA_NOPE + MLA_ROPE
    wuq = jnp.pad(w_uq.reshape(MLA_Q_RANK, MLA_HEADS, per), ((0, 0), (0, 0), (0, LANES - per)))
    wuq = wuq.reshape(MLA_Q_RANK, MLA_HEADS * LANES).astype(BF16)
    col = lambda w, j: pl.BlockSpec((None, tl, w), lambda bi, i: (bi, i, j))
    tab = pl.BlockSpec((tl, LANES), lambda bi, i: (i, 0))
    out = lambda w: pl.BlockSpec((None, tl, w), lambda bi, i: (bi, i, 0))
    return pl.pallas_call(
        _odd_prep_kernel,
        grid=(b, l // tl),
        in_specs=[col(512, 0), col(512, 1), col(256, 6), col(128, 14), col(128, 15),
                  tab, tab, tab, tab, tab, tab,
                  _full((1, MLA_Q_RANK)), _full(wuq.shape), _full((1, MLA_KV_RANK))],
        out_specs=[out(512), out(512), out(512), out(MLA_KV_RANK), out(MLA_ROPE)],
        out_shape=[jax.ShapeDtypeStruct((b, l, 512), BF16), jax.ShapeDtypeStruct((b, l, 512), F32),
                   jax.ShapeDtypeStruct((b, l, 512), BF16), jax.ShapeDtypeStruct((b, l, MLA_KV_RANK), F32),
                   jax.ShapeDtypeStruct((b, l, MLA_ROPE), F32)],
        compiler_params=_cparams("parallel", "parallel"),
        name="odd_prep",
    )(y_in, y_in, y_in, y_in, y_in, cd, sd, cm, sm, ck, sk,
      g_q.reshape(1, MLA_Q_RANK), wuq, g_kv.reshape(1, MLA_KV_RANK))


def _mla_kv_kernel(lat_ref, kpe_ref, wk_ref, wv_ref, place_ref, k_ref, v_ref):
    lat = lat_ref[...].astype(BF16)
    k = (jnp.dot(lat, wk_ref[...], preferred_element_type=F32)
         + jnp.dot(kpe_ref[...].astype(BF16), place_ref[...], preferred_element_type=F32))
    k_ref[...] = k.astype(BF16)
    v_ref[...] = jnp.dot(lat, wv_ref[...], preferred_element_type=F32).astype(BF16)


def _mla_kv(lat_all, kpe_all, w_ukv, tk):
    b, k, _ = lat_all.shape
    per = MLA_NOPE + MLA_DV
    w3 = w_ukv.reshape(MLA_KV_RANK, MLA_HEADS, per)
    wk = jnp.pad(w3[:, :, :MLA_NOPE], ((0, 0), (0, 0), (0, LANES - MLA_NOPE))).reshape(MLA_KV_RANK, MLA_HEADS * LANES)
    wv = w3[:, :, MLA_NOPE:].reshape(MLA_KV_RANK, MLA_HEADS * MLA_DV)
    place = jnp.pad(jnp.eye(MLA_ROPE, dtype=F32), ((0, 0), (MLA_NOPE, LANES - MLA_NOPE - MLA_ROPE)))
    place = jnp.tile(place, (1, MLA_HEADS))
    blk = lambda w: pl.BlockSpec((None, tk, w), lambda bi, i: (bi, i, 0))
    return pl.pallas_call(
        _mla_kv_kernel,
        grid=(b, k // tk),
        in_specs=[blk(MLA_KV_RANK), blk(MLA_ROPE), _full(wk.shape), _full(wv.shape), _full(place.shape)],
        out_specs=[blk(512), blk(512)],
        out_shape=[jax.ShapeDtypeStruct((b, k, 512), BF16), jax.ShapeDtypeStruct((b, k, 512), BF16)],
        compiler_params=_cparams("parallel", "parallel"),
        name="mla_kv",
    )(lat_all, kpe_all, wk.astype(BF16), wv.astype(BF16), place.astype(BF16))


def _visible_limit(pos_last):
    shift = CHUNK.bit_length() - 1
    return ((pos_last >> shift) + 1) << shift


def _fold_lanes(x, op):
    n = x.shape[-1]
    if n % LANES:
        return x
    out = x[:, :LANES]
    for c in range(1, n // LANES):
        out = op(out, x[:, c * LANES:(c + 1) * LANES])
    return out


def _flash_kernel(lam_ref, q_ref, k_ref, v_ref, g_ref, o_ref, m_ref, l_ref, acc_ref, *,
                  nhm, dqk, maps, scale, pos0, tq, tk, nkv, out_scale, rq):
    i, j = pl.program_id(1), pl.program_id(2)

    @pl.when(j == 0)
    def _():
        m_ref[...] = jnp.full(m_ref.shape, NEG, F32)
        l_ref[...] = jnp.zeros(l_ref.shape, F32)
        acc_ref[...] = jnp.zeros(acc_ref.shape, F32)

    c2 = scale * math.log2(math.e)
    visible = j * tk < _visible_limit(pos0 + (i + 1) * tq - 1)
    unmasked = (j + 1) * tk <= _visible_limit(pos0 + i * tq)

    def process(masked):
        for hm in range(nhm):
            k = k_ref[:, hm * dqk:(hm + 1) * dqk].astype(BF16)
            h = hm // maps
            v = v_ref[:, h * DA_DV:(h + 1) * DA_DV].astype(BF16)

            def qblock(r, carry, hm=hm, k=k, v=v):
                r0 = r * rq if isinstance(r, int) else pl.multiple_of(r * rq, rq)
                rows = pl.ds(r0, rq)
                q = q_ref[rows, hm * dqk:(hm + 1) * dqk].astype(BF16)
                s = lax.dot_general(q, k, (((1,), (1,)), ((), ())), preferred_element_type=F32)
                if masked:
                    q_pos = pos0 + i * tq + r0 + lax.broadcasted_iota(jnp.int32, (rq, tk), 0)
                    k_pos = j * tk + lax.broadcasted_iota(jnp.int32, (rq, tk), 1)
                    s = jnp.where(k_pos < _visible_limit(q_pos), s, NEG)
                m_prev = m_ref[hm, rows, :]
                m_new = jnp.maximum(m_prev, jnp.max(_fold_lanes(s, jnp.maximum), axis=-1, keepdims=True))
                p = jnp.exp2((s - m_new) * c2)
                alpha = jnp.exp2((m_prev - m_new) * c2)
                l_ref[hm, rows, :] = (alpha * l_ref[hm, rows, :]
                                      + jnp.sum(_fold_lanes(p, jnp.add), axis=-1, keepdims=True))
                acc_ref[hm, rows, :] = (alpha * acc_ref[hm, rows, :]
                                        + jnp.dot(p.astype(BF16), v, preferred_element_type=F32))
                m_ref[hm, rows, :] = m_new
                return carry

            if tq == rq:
                qblock(0, 0)
            else:
                lax.fori_loop(0, tq // rq, qblock, 0)

    pl.when(visible & unmasked)(lambda: process(False))
    pl.when(visible & jnp.logical_not(unmasked))(lambda: process(True))

    @pl.when(j == nkv - 1)
    def _():
        for h in range(nhm // maps):
            if maps == 2:
                o = acc_ref[2 * h] / l_ref[2 * h] - lam_ref[0] * (acc_ref[2 * h + 1] / l_ref[2 * h + 1])
                o = o * lax.rsqrt(jnp.mean(o * o, axis=-1, keepdims=True) + EPS) * g_ref[...] * out_scale
            else:
                o = acc_ref[h] / l_ref[h]
            o_ref[:, h * DA_DV:(h + 1) * DA_DV] = o


def _flash(q, k, v, lam, g_sub, *, nhm, dqk, maps, scale, pos0, tq, tk, out_scale, v_col=0):
    b, l, _ = q.shape
    kk = k.shape[1]
    nkv = kk // tk
    nh = nhm // maps

    def kv_idx(bi, i, j):
        last = (_visible_limit(pos0 + (i + 1) * tq - 1) - 1) // tk
        return jnp.minimum(j, last)

    return pl.pallas_call(
        functools.partial(_flash_kernel, nhm=nhm, dqk=dqk, maps=maps, scale=scale, pos0=pos0, tq=tq, tk=tk,
                          nkv=nkv, out_scale=out_scale, rq=min(FLASH_RQ, tq)),
        grid=(b, l // tq, nkv),
        in_specs=[pl.BlockSpec(memory_space=pltpu.SMEM),
                  pl.BlockSpec((None, tq, nhm * dqk), lambda bi, i, j: (bi, i, 0)),
                  pl.BlockSpec((None, tk, nhm * dqk), lambda bi, i, j: (bi, kv_idx(bi, i, j), 0)),
                  pl.BlockSpec((None, tk, nh * DA_DV), lambda bi, i, j: (bi, kv_idx(bi, i, j), v_col)),
                  _full((1, DA_DV))],
        out_specs=pl.BlockSpec((None, tq, nh * DA_DV), lambda bi, i, j: (bi, i, 0)),
        out_shape=jax.ShapeDtypeStruct((b, l, nh * DA_DV), F32),
        scratch_shapes=[pltpu.VMEM((nhm, tq, 1), F32), pltpu.VMEM((nhm, tq, 1), F32),
                        pltpu.VMEM((nhm, tq, DA_DV), F32)],
        compiler_params=_cparams("parallel", "parallel", "arbitrary"),
        name="flash_da" if maps == 2 else "flash_mla",
    )(lam, q, k, v, g_sub)


def _top16(s, rid, payload, val_ref, pay_ref):
    big = float(s.shape[0])
    for it in range(PEER_TOPK):
        m = jnp.max(s, axis=0, keepdims=True)
        am = jnp.min(jnp.where(s == m, rid, big), axis=0, keepdims=True)
        sel = rid == am
        val_ref[it:it + 1, :] = m
        if payload is None:
            pay_ref[it:it + 1, :] = am
        else:
            pay_ref[it:it + 1, :] = jnp.max(jnp.where(sel, payload, -1.0), axis=0, keepdims=True)
        s = jnp.where(sel, -jnp.inf, s)


_PEER_PAIRS = [(a, b) for a in range(PEER_TOPK) for b in range(PEER_TOPK) if (a + 1) * (b + 1) <= PEER_TOPK]
_PEER_CAND_ROWS = -(-len(_PEER_PAIRS) // 8) * 8


def _route_kernel(q_ref, k1_ref, k2_ref, idx_ref, g_ref, v1_ref, i1_ref, v2_ref, i2_ref, vc_ref, ic_ref,
                  cand_ref, cidx_ref):
    tt = q_ref.shape[0]
    half = q_ref.shape[1] // 2
    nt = (((1,), (1,)), ((), ()))
    s1 = lax.dot_general(k1_ref[...].astype(BF16), q_ref[:, :half].astype(BF16), nt, preferred_element_type=F32)
    s2 = lax.dot_general(k2_ref[...].astype(BF16), q_ref[:, half:].astype(BF16), nt, preferred_element_type=F32)
    rid = lax.broadcasted_iota(jnp.int32, (PEER_NKEYS, tt), 0).astype(F32)
    _top16(s1, rid, None, v1_ref, i1_ref)
    _top16(s2, rid, None, v2_ref, i2_ref)
    npairs = len(_PEER_PAIRS)
    cand_ref[npairs:, :] = jnp.full((_PEER_CAND_ROWS - npairs, tt), -jnp.inf, F32)
    cidx_ref[npairs:, :] = jnp.full((_PEER_CAND_ROWS - npairs, tt), -1.0, F32)
    for r, (a, b) in enumerate(_PEER_PAIRS):
        cand_ref[r:r + 1, :] = v1_ref[a:a + 1, :] + v2_ref[b:b + 1, :]
        cidx_ref[r:r + 1, :] = i1_ref[a:a + 1, :] * float(PEER_NKEYS) + i2_ref[b:b + 1, :]
    rid2 = lax.broadcasted_iota(jnp.int32, (_PEER_CAND_ROWS, tt), 0).astype(F32)
    _top16(cand_ref[...], rid2, cidx_ref[...], vc_ref, ic_ref)
    sc = vc_ref[...]
    e = jnp.exp(sc - sc[0:1, :])
    g_ref[...] = e / jnp.sum(e, axis=0, keepdims=True)
    idx_ref[...] = ic_ref[...].astype(jnp.int32)


def _peer_route(q, k1, k2):
    t = q.shape[0]
    tt = PEER_TT
    dk = 2 * PEER_NKEYS
    kspec = pl.BlockSpec((None, PEER_NKEYS, dk // 2), lambda i, h: (h, 0, 0))
    ospec = pl.BlockSpec((None, PEER_TOPK, tt), lambda i, h: (i, h, 0))
    sm = pltpu.VMEM((PEER_TOPK, tt), F32)
    return pl.pallas_call(
        _route_kernel,
        grid=(t // tt, PEER_HEADS),
        in_specs=[pl.BlockSpec((tt, dk), lambda i, h: (i, h)), kspec, kspec],
        out_specs=[ospec, ospec],
        out_shape=[jax.ShapeDtypeStruct((t // tt, PEER_HK, tt), jnp.int32),
                   jax.ShapeDtypeStruct((t // tt, PEER_HK, tt), F32)],
        scratch_shapes=[sm, sm, sm, sm, sm, sm,
                        pltpu.VMEM((_PEER_CAND_ROWS, tt), F32), pltpu.VMEM((_PEER_CAND_ROWS, tt), F32)],
        compiler_params=_cparams("parallel", "parallel"),
        name="peer_route",
    )(q, k1, k2)


def _pack_table(tab):
    e, d = tab.shape
    tb = lax.bitcast_convert_type(tab.astype(BF16), jnp.uint16).astype(jnp.uint32)
    words = tb[:, :d // 2] | (tb[:, d // 2:] << 16)
    return lax.bitcast_convert_type(words, F32)


def _sc_unpack(words):
    bits = lax.bitcast_convert_type(words, jnp.int32)
    return lax.bitcast_convert_type(bits << 16, F32), lax.bitcast_convert_type(bits & jnp.int32(-65536), F32)


def _sc_gelu(x):
    u = math.sqrt(2.0 / math.pi) * (x + 0.044715 * (x * x * x))
    tanh_u = 1.0 - 2.0 / (jnp.exp(2.0 * u) + 1.0)
    return 0.5 * x * (1.0 + tanh_u)


def _sc_peer(u_tab, v_tab, idx_tm, g_tm, h):
    t, nk = idx_tm.shape
    w = u_tab.shape[1]
    d = 2 * w
    half_rows = nk // 2
    mesh = plsc.VectorSubcoreMesh(core_axis_name="core", subcore_axis_name="subcore")
    n_workers = mesh.num_cores * mesh.num_subcores
    tpw = t // n_workers
    tb = SC_TOKEN_BLOCK
    steps = 2 * tb
    dot_rows = SC_LANES
    sum_rows = 8
    assert t % n_workers == 0 and tpw % tb == 0 and half_rows % SC_LANES == 0 and w % SC_LANES == 0

    def lanes(start):
        return pl.ds(pl.multiple_of(start, SC_LANES), SC_LANES)

    @functools.partial(
        pl.kernel, out_type=jax.ShapeDtypeStruct((t, d), F32), mesh=mesh,
        scratch_types=[pltpu.VMEM((tb * nk,), jnp.int32), pltpu.VMEM((tb, d), F32), pltpu.VMEM((tb, nk), F32),
                       pltpu.VMEM((2, half_rows, w), F32), pltpu.VMEM((tb, nk), F32), pltpu.VMEM((tb, d), F32),
                       pltpu.SemaphoreType.DMA((2,))],
        compiler_params=pltpu.CompilerParams(needs_layout_passes=False))
    def peer_kernel(u_hbm, v_hbm, idx_hbm, g_hbm, h_hbm, y_hbm, idx_v, h_v, g_v, rows_v, coef_v, y_v, sems):
        wid = lax.axis_index("subcore") * mesh.num_cores + lax.axis_index("core")
        lane = lax.iota(jnp.int32, SC_LANES)

        def gather(tab_hbm, step, buf):
            rows = idx_v.at[pl.ds(pl.multiple_of(step * half_rows, half_rows), half_rows)]
            return pltpu.make_async_copy(tab_hbm.at[rows], rows_v.at[buf], sems.at[buf])

        def run(tab_hbm, work):
            @pl.loop(0, steps, step=2)
            def _(s):
                gather(tab_hbm, s + 1, 1).start()
                gather(tab_hbm, s, 0).wait()
                work(s, 0)

                @pl.when(s + 2 < steps)
                def _():
                    gather(tab_hbm, s + 2, 0).start()

                gather(tab_hbm, s + 1, 1).wait()
                work(s + 1, 1)

        def dots(step, buf):
            tok = step // 2
            col0 = (step % 2) * half_rows

            def group(q, _):
                r0 = q * dot_rows

                def chunk(c, accs):
                    h_lo = h_v[tok, lanes(c * SC_LANES)]
                    h_hi = h_v[tok, lanes(w + c * SC_LANES)]
                    out = []
                    for r in range(dot_rows):
                        lo, hi = _sc_unpack(rows_v[buf, r0 + r, lanes(c * SC_LANES)])
                        out.append(accs[r] + lo * h_lo + hi * h_hi)
                    return tuple(out)

                accs = lax.fori_loop(0, w // SC_LANES, chunk, (jnp.zeros((SC_LANES,), F32),) * dot_rows)
                vec = jnp.zeros((SC_LANES,), F32)
                for r in range(dot_rows):
                    vec = jnp.where(lane == r, jnp.sum(accs[r]), vec)
                coef_v[tok, lanes(col0 + r0)] = vec
                return 0

            lax.fori_loop(0, half_rows // dot_rows, group, 0)

        def gates():
            @pl.loop(0, tb)
            def _(tok):
                @pl.loop(0, nk // SC_LANES)
                def _(q):
                    ds = lanes(q * SC_LANES)
                    coef_v[tok, ds] = g_v[tok, ds] * _sc_gelu(coef_v[tok, ds])

        def weighted_sum(step, buf):
            tok = step // 2
            col0 = (step % 2) * half_rows

            @pl.when(col0 == 0)
            def _():
                @plsc.parallel_loop(0, d // SC_LANES)
                def _(c):
                    y_v[tok, lanes(c * SC_LANES)] = jnp.zeros((SC_LANES,), F32)

            def group(g, _):
                cvec = coef_v[tok, lanes(col0 + (g // 2) * SC_LANES)]
                base = (g % 2) * sum_rows
                cs = [jnp.sum(jnp.where(lane == base + r, cvec, 0.0)) for r in range(sum_rows)]
                r0 = g * sum_rows

                @plsc.parallel_loop(0, w // SC_LANES, unroll=2)
                def _(c):
                    los, his = [], []
                    for r in range(sum_rows):
                        lo, hi = _sc_unpack(rows_v[buf, r0 + r, lanes(c * SC_LANES)])
                        los.append(cs[r] * lo)
                        his.append(cs[r] * hi)
                    while len(los) > 1:
                        los = [a + b for a, b in zip(los[::2], los[1::2])]
                        his = [a + b for a, b in zip(his[::2], his[1::2])]
                    y_v[tok, lanes(c * SC_LANES)] = y_v[tok, lanes(c * SC_LANES)] + los[0]
                    y_v[tok, lanes(w + c * SC_LANES)] = y_v[tok, lanes(w + c * SC_LANES)] + his[0]

                return 0

            lax.fori_loop(0, half_rows // sum_rows, group, 0)

        @pl.loop(0, tpw // tb)
        def _(blk):
            tok0 = wid * tpw + blk * tb
            pltpu.sync_copy(idx_hbm.at[pl.ds(pl.multiple_of(tok0 * nk, nk), tb * nk)], idx_v)
            gather(u_hbm, 0, 0).start()
            pltpu.sync_copy(h_hbm.at[pl.ds(tok0, tb)], h_v)
            pltpu.sync_copy(g_hbm.at[pl.ds(tok0, tb)], g_v)
            run(u_hbm, dots)
            gather(v_hbm, 0, 0).start()
            gates()
            run(v_hbm, weighted_sum)
            pltpu.sync_copy(y_v, y_hbm.at[pl.ds(tok0, tb)])

    return peer_kernel(u_tab, v_tab, idx_tm.reshape(t * nk), g_tm, h)


def _peer_out_kernel(x_ref, gt_ref, y_ref, gf_ref, o_ref, *, final):
    x = x_ref[...] + gt_ref[...] * y_ref[...]
    if final:
        x = x * lax.rsqrt(jnp.mean(x * x, axis=-1, keepdims=True) + EPS) * gf_ref[...]
    o_ref[...] = x


def _peer_out(x, gt, y, g_final, final):
    t, d = x.shape
    tt = PEER_TT
    nt = t // tt
    row = pl.BlockSpec((tt, d), lambda i: (i, 0))
    if gt.ndim == 3:
        per_seq = nt // gt.shape[0]
        gt_spec = pl.BlockSpec((None, 1, d), lambda i: (i // per_seq, 0, 0))
    else:
        gt_spec = row
    return pl.pallas_call(
        functools.partial(_peer_out_kernel, final=final),
        grid=(nt,),
        in_specs=[row, gt_spec, row, _full((1, d))],
        out_specs=row, out_shape=jax.ShapeDtypeStruct((t, d), F32),
        compiler_params=_cparams("parallel"),
        name="peer_out",
    )(x, gt, y, g_final.reshape(1, d))


def _peer(x, g_ffn, sc2, sh2, gt2, w_q, k1, k2, u_pack, v_pack, g_final, final, tl):
    b, l, d = x.shape
    t = b * l
    q, h = _norm_mod_matmul(x, g_ffn, sc2, sh2, w_q, tl, want_h=True)
    idx, g = _peer_route(q.reshape(t, -1), k1, k2)
    token_major = lambda a: jnp.transpose(a, (0, 2, 1)).reshape(t, PEER_HK)
    y = _sc_peer(u_pack, v_pack, token_major(idx), token_major(g), h.reshape(t, d))
    if l % PEER_TT:
        gt2 = jnp.broadcast_to(gt2, (b, l, d)).reshape(t, d)
    return _peer_out(x.reshape(t, d), gt2, y, g_final, final).reshape(b, l, d)


def _trunk(x, mod, pos0, past, p, tl):
    b, l, d = x.shape
    outs = {}
    for layer in range(DEPTH):
        sh1, sc1, gt1, sh2, sc2, gt2 = [m[:, None, :] for m in jnp.split(mod[layer], 6, axis=-1)]
        if layer % 2 == 0:
            e = layer // 2
            y_in = _norm_mod_matmul(x, p['g_mix'][layer], sc1, sh1, p['w_in_e'][e].astype(BF16), tl)
            if past is None:
                h0_re = jnp.zeros((b, S5_GROUPS, S5_STATE), F32)
                h0_im = jnp.zeros((b, S5_GROUPS, S5_STATE), F32)
            else:
                h0_re, h0_im = past['s5_re'][e], past['s5_im'][e]
            ya, hr, hi = _s5(y_in, h0_re, h0_im, p['s5'][e], p['s5_d'][e], p['w_glu'][e], p['b_glu'][e], tl)
            yb, vn = _gmlp(y_in, p['gm_g_v'][e], p['gm_w_s'][e], p['gm_b_s'][e], tl, want_vn=past is not None)
            x = _proj_residual(x, gt1, ya, yb, p['w_out_e'][e], tl)
            outs.update(s5_re=hr, s5_im=hi, gm_v=vn)
        else:
            o = layer // 2
            w_in = jnp.pad(p['w_in_o'][o], ((0, 0), (0, 2048 - p['w_in_o'][o].shape[1]))).astype(BF16)
            y_in = _norm_mod_matmul(x, p['g_mix'][layer], sc1, sh1, w_in, tl)
            qd, kd, qm, lat, kpe = _odd_prep(y_in, pos0, p['mla_g_q'][o], p['mla_w_uq'][o], p['mla_g_kv'][o], tl)
            v_new = y_in[:, :, 2 * 512:3 * 512]
            lam_init = 0.8 - 0.6 * math.exp(-0.3 * layer)
            lam = (jnp.exp(jnp.sum(p['da_lq1'][o] * p['da_lk1'][o])) - jnp.exp(jnp.sum(p['da_lq2'][o] * p['da_lk2'][o]))
                   + lam_init).reshape(1).astype(F32)
            g_sub = p['da_g_sub'][o].reshape(1, DA_DV)
            if past is None:
                k_all, v_all, v_col, lat_all, kpe_all = kd, y_in, 2, lat, kpe
                tq = tk = tl
            else:
                k_all = jnp.concatenate([past['da_k'][o].reshape(b, -1, 512), kd], axis=1)
                v_all = jnp.concatenate([past['da_v'][o].reshape(b, -1, 512), v_new], axis=1)
                lat_all = jnp.concatenate([past['mla_lat'][o], lat], axis=1)
                kpe_all = jnp.concatenate([past['mla_kpe'][o], kpe], axis=1)
                v_col, tq, tk = 0, l, k_all.shape[1]
            yc = _flash(qd, k_all, v_all, lam, g_sub, nhm=2 * DA_HEADS, dqk=DA_DH, maps=2, scale=DA_DH ** -0.5,
                        pos0=pos0, tq=tq, tk=tk, out_scale=1.0 - lam_init, v_col=v_col)
            km, vm = _mla_kv(lat_all, kpe_all, p['mla_w_ukv'][o], tk)
            yd = _flash(qm, km, vm, lam, g_sub, nhm=MLA_HEADS, dqk=LANES, maps=1,
                        scale=(MLA_NOPE + MLA_ROPE) ** -0.5, pos0=pos0, tq=tq, tk=tk, out_scale=1.0)
            x = _proj_residual(x, gt1, yc, yd, p['w_out_o'][o], tl)
            outs.update(da_k=kd.reshape(b, l, DA_HEADS, 2 * DA_DH), da_v=v_new.reshape(b, l, DA_HEADS, DA_DV),
                        mla_lat=lat, mla_kpe=kpe)
        x = _peer(x, p['g_ffn'][layer], sc2, sh2, gt2, p['peer_w_q'][layer].astype(BF16), p['peer_k1'][layer],
                  p['peer_k2'][layer], p['u_pack'][layer], p['v_pack'][layer], p['g_final'],
                  final=layer == DEPTH - 1, tl=tl)
    return x, outs


def kernel(x_prompt, x_sample, c_prompt, c_sample, state_s5_re, state_s5_im, cache_da_k, cache_da_v, cache_mla_latent, cache_mla_kpe, w_ada, b_ada, g_mix, g_ffn, g_final, w_in_e, w_out_e, s5_lam_re, s5_lam_im, s5_log_dt, s5_b_re, s5_b_im, s5_c_re, s5_c_im, s5_d, w_glu, b_glu, gm_g_v, gm_w_s, gm_b_s, w_in_o, w_out_o, da_lq1, da_lk1, da_lq2, da_lk2, da_g_sub, mla_g_q, mla_w_uq, mla_g_kv, mla_w_ukv, peer_w_q, peer_k1, peer_k2, peer_u, peer_v):
    p = dict(g_mix=g_mix, g_ffn=g_ffn, g_final=g_final, w_in_e=w_in_e, w_out_e=w_out_e, s5_d=s5_d, w_glu=w_glu,
             b_glu=b_glu, gm_g_v=gm_g_v, gm_w_s=gm_w_s, gm_b_s=gm_b_s, w_in_o=w_in_o, w_out_o=w_out_o,
             da_lq1=da_lq1, da_lk1=da_lk1, da_lq2=da_lq2, da_lk2=da_lk2, da_g_sub=da_g_sub, mla_g_q=mla_g_q,
             mla_w_uq=mla_w_uq, mla_g_kv=mla_g_kv, mla_w_ukv=mla_w_ukv, peer_w_q=peer_w_q, peer_k1=peer_k1,
             peer_k2=peer_k2)
    n_even = (DEPTH + 1) // 2
    p['s5'] = [_s5_params(s5_lam_re[e], s5_lam_im[e], s5_log_dt[e], s5_b_re[e], s5_b_im[e], s5_c_re[e], s5_c_im[e])
               for e in range(n_even)]
    p['u_pack'] = [_pack_table(peer_u[layer]) for layer in range(DEPTH)]
    p['v_pack'] = [_pack_table(peer_v[layer]) for layer in range(DEPTH)]
    past = dict(s5_re=state_s5_re, s5_im=state_s5_im, da_k=cache_da_k, da_v=cache_da_v,
                mla_lat=cache_mla_latent, mla_kpe=cache_mla_kpe)
    nb = x_prompt.shape[0]
    mod = _ada(jnp.concatenate([c_prompt, c_sample], axis=0), w_ada, b_ada)
    past_len = cache_da_k.shape[2]
    parts = [_trunk(x_prompt[s:s + 1], mod[:, s:s + 1], 0, None, p, tl=512) for s in range(nb)]
    y_p = jnp.concatenate([y for y, _ in parts], axis=0)
    o_p = {k: jnp.concatenate([o[k] for _, o in parts], axis=0) for k in parts[0][1] if parts[0][1][k] is not None}
    y_s, o_s = _trunk(x_sample, mod[:, nb:], past_len, past, p, tl=x_sample.shape[1])
    st = lambda a: a[None]
    return (y_p, y_s, st(o_p['s5_re']), st(o_p['s5_im']), st(o_s['s5_re']), st(o_s['s5_im']), st(o_s['gm_v']),
            st(o_p['da_k']), st(o_p['da_v']), st(o_s['da_k']), st(o_s['da_v']),
            st(o_p['mla_lat']), st(o_p['mla_kpe']), st(o_s['mla_lat']), st(o_s['mla_kpe']))
```

```python
import functools
import math

import jax
import jax.numpy as jnp
from jax import lax
from jax.experimental import pallas as pl
from jax.experimental.pallas import tpu as pltpu
from jax.experimental.pallas import tpu_sc as plsc

F32 = jnp.float32
BF16 = jnp.bfloat16

D_MODEL = 1024
DEPTH = 2
CHUNK = 64
ROPE_THETA = 500000.0
EPS = 1e-6
NEG = -1e30

S5_WIDTH = 512
S5_GROUP = 16
S5_GROUPS = 32
S5_STATE = 64
GM_WIDTH = 512
GM_HEADS = 4
GM_CHUNK = 128
DA_HEADS = 4
DA_DH = 64
DA_DV = 128
DA_ROPE = 16
MLA_HEADS = 4
MLA_Q_RANK = 256
MLA_KV_RANK = 128
MLA_NOPE = 64
MLA_ROPE = 32
MLA_DV = 128
PEER_HEADS = 8
PEER_NKEYS = 128
PEER_TOPK = 16
PEER_HK = PEER_HEADS * PEER_TOPK

LANES = 128
VMEM_LIMIT = 48 << 20
PEER_TT = 256
SC_LANES = 16
SC_TOKEN_BLOCK = 8
SC_GATHER_ROWS = 64
SC_BUFFERS = 2
SC_SIDE_ROWS = 16
PEER_TC_TILES = 6
PEER_HKC = 16
FLASH_RQ = 512

_HI = lax.Precision.HIGHEST


def _cparams(*sem):
    return pltpu.CompilerParams(dimension_semantics=sem, vmem_limit_bytes=VMEM_LIMIT)


def _full(shape):
    n = len(shape)
    return pl.BlockSpec(shape, lambda *_: (0,) * n)


def _ada_kernel(c_ref, w_ref, b_ref, o_ref):
    c = c_ref[...]
    s = c * jax.nn.sigmoid(c)
    o_ref[...] = jnp.dot(s, w_ref[...], precision=_HI, preferred_element_type=F32) + b_ref[...]


def _ada(c_all, w_ada, b_ada):
    r = c_all.shape[0]
    tn = 1536
    return pl.pallas_call(
        _ada_kernel,
        grid=(DEPTH, 6 * D_MODEL // tn),
        in_specs=[_full((r, D_MODEL)),
                  pl.BlockSpec((None, D_MODEL, tn), lambda l, j: (l, 0, j)),
                  pl.BlockSpec((None, 1, tn), lambda l, j: (l, 0, j))],
        out_specs=pl.BlockSpec((None, r, tn), lambda l, j: (l, 0, j)),
        out_shape=jax.ShapeDtypeStruct((DEPTH, r, 6 * D_MODEL), F32),
        compiler_params=_cparams("parallel", "parallel"),
        name="ada",
    )(c_all, w_ada, b_ada.reshape(DEPTH, 1, 6 * D_MODEL))


def _nmm_kernel(x_ref, g_ref, sc_ref, sh_ref, w_ref, o_ref, *h_ref):
    x = x_ref[...]
    ms = jnp.mean(x * x, axis=-1, keepdims=True)
    h = x * lax.rsqrt(ms + EPS) * g_ref[...] * (1.0 + sc_ref[...]) + sh_ref[...]
    o_ref[...] = jnp.dot(h.astype(BF16), w_ref[...], preferred_element_type=F32)
    if h_ref:
        h_ref[0][...] = h


def _norm_mod_matmul(x, g, sc, sh, w, tl, want_h=False):
    b, l, d = x.shape
    n = w.shape[1]
    row = pl.BlockSpec((None, 1, d), lambda bi, i: (bi, 0, 0))
    out_shape = [jax.ShapeDtypeStruct((b, l, n), F32)]
    out_specs = [pl.BlockSpec((None, tl, n), lambda bi, i: (bi, i, 0))]
    if want_h:
        out_shape.append(jax.ShapeDtypeStruct((b, l, d), F32))
        out_specs.append(pl.BlockSpec((None, tl, d), lambda bi, i: (bi, i, 0)))
    res = pl.pallas_call(
        _nmm_kernel,
        grid=(b, l // tl),
        in_specs=[pl.BlockSpec((None, tl, d), lambda bi, i: (bi, i, 0)), _full((1, d)), row, row, _full((d, n))],
        out_specs=out_specs, out_shape=out_shape,
        compiler_params=_cparams("parallel", "parallel"),
        name="norm_mod_matmul",
    )(x, g.reshape(1, d), sc, sh, w)
    return res if want_h else res[0]


def _s5_kernel(xa_ref, h0r_ref, h0i_ref, bblk_ref, ar_ref, ai_ref, cr_ref, ci_ref, d_ref, wglu_ref, bglu_ref,
               ya_ref, hr_out_ref, hi_out_ref, bur, bui, st_r, st_i, *, tl):
    nj = S5_WIDTH // LANES
    half = 8 * S5_STATE

    @pl.when(pl.program_id(1) == 0)
    def _():
        st_r[...] = h0r_ref[...]
        st_i[...] = h0i_ref[...]

    xa = xa_ref[...]
    for j in range(nj):
        bu = jnp.dot(xa[:, j * LANES:(j + 1) * LANES], bblk_ref[j], precision=_HI, preferred_element_type=F32)
        bur[j] = bu[:, :half]
        bui[j] = bu[:, half:]

    for j0 in range(0, nj, 2):
        js = (j0, j0 + 1)
        a_r = [ar_ref[j] for j in js]
        a_i = [ai_ref[j] for j in js]

        def body(k, carry, js=js, a_r=a_r, a_i=a_i):
            carry = list(carry)
            base = pl.multiple_of(k * 8, 8)
            for s in range(8):
                t = base + s
                for q, j in enumerate(js):
                    hr, hi = carry[2 * q], carry[2 * q + 1]
                    nhr = a_r[q] * hr - a_i[q] * hi + bur[j, pl.ds(t, 1), :]
                    nhi = a_r[q] * hi + a_i[q] * hr + bui[j, pl.ds(t, 1), :]
                    bur[j, pl.ds(t, 1), :] = nhr
                    bui[j, pl.ds(t, 1), :] = nhi
                    carry[2 * q], carry[2 * q + 1] = nhr, nhi
            return tuple(carry)

        init = (st_r[js[0]], st_i[js[0]], st_r[js[1]], st_i[js[1]])
        fin = lax.fori_loop(0, tl // 8, body, init)
        st_r[js[0]], st_i[js[0]], st_r[js[1]], st_i[js[1]] = fin

    hr_out_ref[...] = st_r[...]
    hi_out_ref[...] = st_i[...]

    ys = []
    for j in range(nj):
        y = (jnp.dot(bur[j].astype(BF16), cr_ref[j], preferred_element_type=F32)
             + jnp.dot(bui[j].astype(BF16), ci_ref[j], preferred_element_type=F32))
        ys.append(y + d_ref[:, j * LANES:(j + 1) * LANES] * xa[:, j * LANES:(j + 1) * LANES])
    y = jax.nn.gelu(jnp.concatenate(ys, axis=1))
    z = jnp.dot(y.astype(BF16), wglu_ref[...], preferred_element_type=F32) + bglu_ref[...]
    ya_ref[...] = z[:, :S5_WIDTH] * jax.nn.sigmoid(z[:, S5_WIDTH:])


def _s5_params(lam_re, lam_im, log_dt, b_re, b_im, c_re, c_im):
    lam = lax.complex(lam_re.astype(F32), lam_im.astype(F32))
    dt = jnp.exp(log_dt.astype(F32))[:, None]
    a_bar = jnp.exp(lam * dt)
    b_bar = ((a_bar - 1.0) / lam)[..., None] * lax.complex(b_re.astype(F32), b_im.astype(F32))
    nj = S5_GROUPS // 8
    eye = jnp.eye(8, dtype=F32)

    def blk_b(m):
        m = m.reshape(nj, 8, S5_STATE, S5_GROUP)
        return jnp.einsum('jgpc,gh->jgchp', m, eye).reshape(nj, 8 * S5_GROUP, 8 * S5_STATE)

    def blk_c(m):
        m = m.reshape(nj, 8, S5_GROUP, S5_STATE)
        return jnp.einsum('jgcp,gh->jgphc', m, eye).reshape(nj, 8 * S5_STATE, 8 * S5_GROUP)

    bblk = jnp.concatenate([blk_b(jnp.real(b_bar)), blk_b(jnp.imag(b_bar))], axis=-1)
    cr = blk_c(c_re.astype(F32)).astype(BF16)
    ci = (-blk_c(c_im.astype(F32))).astype(BF16)
    a_r = jnp.real(a_bar).reshape(nj, 1, 8 * S5_STATE)
    a_i = jnp.imag(a_bar).reshape(nj, 1, 8 * S5_STATE)
    return bblk, a_r, a_i, cr, ci


def _s5(y_in, h0_re, h0_im, sp, d_skip, w_glu, b_glu, tl):
    b, l, _ = y_in.shape
    nj = S5_GROUPS // 8
    half = 8 * S5_STATE
    bblk, a_r, a_i, cr, ci = sp
    st = pl.BlockSpec((None, nj, 1, half), lambda bi, i: (bi, 0, 0, 0))
    ya, hr, hi = pl.pallas_call(
        functools.partial(_s5_kernel, tl=tl),
        grid=(b, l // tl),
        in_specs=[pl.BlockSpec((None, tl, S5_WIDTH), lambda bi, i: (bi, i, 0)), st, st,
                  _full(bblk.shape), _full(a_r.shape), _full(a_i.shape), _full(cr.shape), _full(ci.shape),
                  _full((1, S5_WIDTH)), _full((S5_WIDTH, 2 * S5_WIDTH)), _full((1, 2 * S5_WIDTH))],
        out_specs=[pl.BlockSpec((None, tl, S5_WIDTH), lambda bi, i: (bi, i, 0)), st, st],
        out_shape=[jax.ShapeDtypeStruct((b, l, S5_WIDTH), F32),
                   jax.ShapeDtypeStruct((b, nj, 1, half), F32), jax.ShapeDtypeStruct((b, nj, 1, half), F32)],
        scratch_shapes=[pltpu.VMEM((nj, tl, half), F32), pltpu.VMEM((nj, tl, half), F32),
                        pltpu.VMEM((nj, 1, half), F32), pltpu.VMEM((nj, 1, half), F32)],
        compiler_params=_cparams("parallel", "arbitrary"),
        name="s5",
    )(y_in, h0_re.reshape(b, nj, 1, half), h0_im.reshape(b, nj, 1, half), bblk, a_r, a_i, cr, ci,
      d_skip.reshape(1, S5_WIDTH), w_glu.astype(BF16), b_glu.reshape(1, 2 * S5_WIDTH))
    return ya, hr.reshape(b, S5_GROUPS, S5_STATE), hi.reshape(b, S5_GROUPS, S5_STATE)


def _gmlp_kernel(u_ref, v_ref, gv_ref, ws_ref, bias_ref, yb_ref, *vn_ref, tl, lc):
    hd = GM_WIDTH // GM_HEADS
    gv = jax.nn.gelu(v_ref[...])
    vn = gv * lax.rsqrt(jnp.mean(gv * gv, axis=-1, keepdims=True) + EPS) * gv_ref[...]
    if vn_ref:
        vn_ref[0][...] = vn
    gu = jax.nn.gelu(u_ref[...])
    vb = vn.astype(BF16)
    for c in range(tl // lc):
        rows = slice(c * lc, (c + 1) * lc)
        for h in range(GM_HEADS):
            cols = slice(h * hd, (h + 1) * hd)
            mixed = jnp.dot(ws_ref[h], vb[rows, cols], preferred_element_type=F32) + bias_ref[:, cols]
            yb_ref[rows, cols] = gu[rows, cols] * mixed


def _gmlp(y_in, g_v, w_s, b_s, tl, want_vn):
    b, l, _ = y_in.shape
    lc = min(GM_CHUNK, l)
    hd = GM_WIDTH // GM_HEADS
    tri = jnp.tril(jnp.ones((GM_CHUNK, GM_CHUNK), dtype=bool))
    ws = jnp.where(tri[None], w_s, 0.0)[:, :lc, :lc].astype(BF16)
    bias = jnp.repeat(jnp.transpose(b_s)[:lc], hd, axis=1)
    out_shape = [jax.ShapeDtypeStruct((b, l, GM_WIDTH), F32)]
    out_specs = [pl.BlockSpec((None, tl, GM_WIDTH), lambda bi, i: (bi, i, 0))]
    if want_vn:
        out_shape.append(jax.ShapeDtypeStruct((b, l, GM_WIDTH), F32))
        out_specs.append(pl.BlockSpec((None, tl, GM_WIDTH), lambda bi, i: (bi, i, 0)))
    res = pl.pallas_call(
        functools.partial(_gmlp_kernel, tl=tl, lc=lc),
        grid=(b, l // tl),
        in_specs=[pl.BlockSpec((None, tl, GM_WIDTH), lambda bi, i: (bi, i, 1)),
                  pl.BlockSpec((None, tl, GM_WIDTH), lambda bi, i: (bi, i, 2)),
                  _full((1, GM_WIDTH)), _full(ws.shape), _full(bias.shape)],
        out_specs=out_specs, out_shape=out_shape,
        compiler_params=_cparams("parallel", "parallel"),
        name="gmlp",
    )(y_in, y_in, g_v.reshape(1, GM_WIDTH), ws, bias)
    return (res[0], res[1]) if want_vn else (res[0], None)


def _proj_res_kernel(x_ref, gt_ref, a_ref, b_ref, wa_ref, wb_ref, o_ref):
    mix = (jnp.dot(a_ref[...].astype(BF16), wa_ref[...], preferred_element_type=F32)
           + jnp.dot(b_ref[...].astype(BF16), wb_ref[...], preferred_element_type=F32))
    o_ref[...] = x_ref[...] + gt_ref[...] * mix


def _proj_residual(x, gt, a, bm, w, tl):
    b, l, d = x.shape
    ka = a.shape[-1]
    wa, wb = w[:ka].astype(BF16), w[ka:].astype(BF16)
    blk = lambda n: pl.BlockSpec((None, tl, n), lambda bi, i: (bi, i, 0))
    return pl.pallas_call(
        _proj_res_kernel,
        grid=(b, l // tl),
        in_specs=[blk(d), pl.BlockSpec((None, 1, d), lambda bi, i: (bi, 0, 0)), blk(ka), blk(bm.shape[-1]),
                  _full(wa.shape), _full(wb.shape)],
        out_specs=blk(d), out_shape=jax.ShapeDtypeStruct((b, l, d), F32),
        compiler_params=_cparams("parallel", "parallel"),
        name="proj_residual",
    )(x, gt, a, bm, wa, wb)


def _rope_tables(pos0, l):
    pos = (pos0 + jnp.arange(l, dtype=jnp.int32)).astype(F32)[:, None]
    lane = jnp.arange(LANES)

    def table(period, start, rot):
        half = rot // 2
        r = lane % period - start
        inside = (r >= 0) & (r < rot)
        k = jnp.where(inside, r % half, 0)
        inv = ROPE_THETA ** (-k.astype(F32) * 2.0 / rot)
        ang = pos * inv[None, :]
        cos = jnp.where(inside[None], jnp.cos(ang), 1.0)
        sgn = jnp.where(r < half, -1.0, 1.0)
        sin = jnp.where(inside[None], jnp.sin(ang) * sgn[None], 0.0)
        return cos, sin

    return table(DA_DH, 0, DA_ROPE), table(LANES, MLA_NOPE, MLA_ROPE), table(LANES, 0, MLA_ROPE)


def _rotate(x, cos, sin, period, start, rot):
    half = rot // 2
    n = x.shape[-1]
    reps = n // LANES
    lane = (lax.broadcasted_iota(jnp.int32, x.shape, 1) & (period - 1)) - start
    up = pltpu.roll(x, n - half, 1)
    dn = pltpu.roll(x, half, 1)
    partner = jnp.where(lane < half, up, dn)
    if reps > 1:
        cos = jnp.concatenate([cos] * reps, axis=1)
        sin = jnp.concatenate([sin] * reps, axis=1)
    return x * cos + partner * sin


def _odd_prep_kernel(q_ref, k_ref, cq_ref, ckv_ref, kpe_ref, cd_ref, sd_ref, cm_ref, sm_ref, ck_ref, sk_ref,
                     gq_ref, wuq_ref, gkv_ref, qd_ref, kd_ref, qm_ref, lat_ref, kpe_out_ref):
    cd, sd = cd_ref[...], sd_ref[...]
    qd_ref[...] = _rotate(q_ref[...], cd, sd, DA_DH, 0, DA_ROPE).astype(BF16)
    kd_ref[...] = _rotate(k_ref[...], cd, sd, DA_DH, 0, DA_ROPE)
    cq = cq_ref[...]
    cqn = cq * lax.rsqrt(jnp.mean(cq * cq, axis=-1, keepdims=True) + EPS) * gq_ref[...]
    qm = jnp.dot(cqn.astype(BF16), wuq_ref[...], preferred_element_type=F32)
    qm_ref[...] = _rotate(qm, cm_ref[...], sm_ref[...], LANES, MLA_NOPE, MLA_ROPE).astype(BF16)
    ckv = ckv_ref[...]
    lat_ref[...] = ckv * lax.rsqrt(jnp.mean(ckv * ckv, axis=-1, keepdims=True) + EPS) * gkv_ref[...]
    kpe = _rotate(kpe_ref[...], ck_ref[...], sk_ref[...], LANES, 0, MLA_ROPE)
    kpe_out_ref[...] = kpe[:, :MLA_ROPE]


def _odd_prep(y_in, pos0, g_q, w_uq, g_kv, tl):
    b, l, _ = y_in.shape
    (cd, sd), (cm, sm), (ck, sk) = _rope_tables(pos0, l)
    per = MLA_NOPE + MLA_ROPE
    wuq = jnp.pad(w_uq.reshape(MLA_Q_RANK, MLA_HEADS, per), ((0, 0), (0, 0), (0, LANES - per)))
    wuq = wuq.reshape(MLA_Q_RANK, MLA_HEADS * LANES).astype(BF16)
    col = lambda w, j: pl.BlockSpec((None, tl, w), lambda bi, i: (bi, i, j))
    tab = pl.BlockSpec((tl, LANES), lambda bi, i: (i, 0))
    out = lambda w: pl.BlockSpec((None, tl, w), lambda bi, i: (bi, i, 0))
    return pl.pallas_call(
        _odd_prep_kernel,
        grid=(b, l // tl),
        in_specs=[col(512, 0), col(512, 1), col(256, 6), col(128, 14), col(128, 15),
                  tab, tab, tab, tab, tab, tab,
                  _full((1, MLA_Q_RANK)), _full(wuq.shape), _full((1, MLA_KV_RANK))],
        out_specs=[out(512), out(512), out(512), out(MLA_KV_RANK), out(MLA_ROPE)],
        out_shape=[jax.ShapeDtypeStruct((b, l, 512), BF16), jax.ShapeDtypeStruct((b, l, 512), F32),
                   jax.ShapeDtypeStruct((b, l, 512), BF16), jax.ShapeDtypeStruct((b, l, MLA_KV_RANK), F32),
                   jax.ShapeDtypeStruct((b, l, MLA_ROPE), F32)],
        compiler_params=_cparams("parallel", "parallel"),
        name="odd_prep",
    )(y_in, y_in, y_in, y_in, y_in, cd, sd, cm, sm, ck, sk,
      g_q.reshape(1, MLA_Q_RANK), wuq, g_kv.reshape(1, MLA_KV_RANK))


def _mla_kv_kernel(lat_ref, kpe_ref, wk_ref, wv_ref, place_ref, k_ref, v_ref):
    lat = lat_ref[...].astype(BF16)
    k = (jnp.dot(lat, wk_ref[...], preferred_element_type=F32)
         + jnp.dot(kpe_ref[...].astype(BF16), place_ref[...], preferred_element_type=F32))
    k_ref[...] = k.astype(BF16)
    v_ref[...] = jnp.dot(lat, wv_ref[...], preferred_element_type=F32).astype(BF16)


def _mla_kv(lat_all, kpe_all, w_ukv, tk):
    b, k, _ = lat_all.shape
    per = MLA_NOPE + MLA_DV
    w3 = w_ukv.reshape(MLA_KV_RANK, MLA_HEADS, per)
    wk = jnp.pad(w3[:, :, :MLA_NOPE], ((0, 0), (0, 0), (0, LANES - MLA_NOPE))).reshape(MLA_KV_RANK, MLA_HEADS * LANES)
    wv = w3[:, :, MLA_NOPE:].reshape(MLA_KV_RANK, MLA_HEADS * MLA_DV)
    place = jnp.pad(jnp.eye(MLA_ROPE, dtype=F32), ((0, 0), (MLA_NOPE, LANES - MLA_NOPE - MLA_ROPE)))
    place = jnp.tile(place, (1, MLA_HEADS))
    blk = lambda w: pl.BlockSpec((None, tk, w), lambda bi, i: (bi, i, 0))
    return pl.pallas_call(
        _mla_kv_kernel,
        grid=(b, k // tk),
        in_specs=[blk(MLA_KV_RANK), blk(MLA_ROPE), _full(wk.shape), _full(wv.shape), _full(place.shape)],
        out_specs=[blk(512), blk(512)],
        out_shape=[jax.ShapeDtypeStruct((b, k, 512), BF16), jax.ShapeDtypeStruct((b, k, 512), BF16)],
        compiler_params=_cparams("parallel", "parallel"),
        name="mla_kv",
    )(lat_all, kpe_all, wk.astype(BF16), wv.astype(BF16), place.astype(BF16))


def _visible_limit(pos_last):
    shift = CHUNK.bit_length() - 1
    return ((pos_last >> shift) + 1) << shift


def _fold_lanes(x, op):
    n = x.shape[-1]
    if n % LANES:
        return x
    out = x[:, :LANES]
    for c in range(1, n // LANES):
        out = op(out, x[:, c * LANES:(c + 1) * LANES])
    return out


def _flash_kernel(lam_ref, q_ref, k_ref, v_ref, g_ref, o_ref, m_ref, l_ref, acc_ref, *,
                  nhm, dqk, maps, scale, pos0, tq, tk, nkv, out_scale, rq):
    i, j = pl.program_id(1), pl.program_id(2)

    @pl.when(j == 0)
    def _():
        m_ref[...] = jnp.full(m_ref.shape, NEG, F32)
        l_ref[...] = jnp.zeros(l_ref.shape, F32)
        acc_ref[...] = jnp.zeros(acc_ref.shape, F32)

    c2 = scale * math.log2(math.e)
    visible = j * tk < _visible_limit(pos0 + (i + 1) * tq - 1)
    unmasked = (j + 1) * tk <= _visible_limit(pos0 + i * tq)

    def process(masked):
        for hm in range(nhm):
            k = k_ref[:, hm * dqk:(hm + 1) * dqk].astype(BF16)
            h = hm // maps
            v = v_ref[:, h * DA_DV:(h + 1) * DA_DV].astype(BF16)

            def qblock(r, carry, hm=hm, k=k, v=v):
                r0 = r * rq if isinstance(r, int) else pl.multiple_of(r * rq, rq)
                rows = pl.ds(r0, rq)
                q = q_ref[rows, hm * dqk:(hm + 1) * dqk].astype(BF16)
                s = lax.dot_general(q, k, (((1,), (1,)), ((), ())), preferred_element_type=F32)
                if masked:
                    q_pos = pos0 + i * tq + r0 + lax.broadcasted_iota(jnp.int32, (rq, tk), 0)
                    k_pos = j * tk + lax.broadcasted_iota(jnp.int32, (rq, tk), 1)
                    s = jnp.where(k_pos < _visible_limit(q_pos), s, NEG)
                m_prev = m_ref[hm, rows, :]
                m_new = jnp.maximum(m_prev, jnp.max(_fold_lanes(s, jnp.maximum), axis=-1, keepdims=True))
                p = jnp.exp2((s - m_new) * c2)
                alpha = jnp.exp2((m_prev - m_new) * c2)
                l_ref[hm, rows, :] = (alpha * l_ref[hm, rows, :]
                                      + jnp.sum(_fold_lanes(p, jnp.add), axis=-1, keepdims=True))
                acc_ref[hm, rows, :] = (alpha * acc_ref[hm, rows, :]
                                        + jnp.dot(p.astype(BF16), v, preferred_element_type=F32))
                m_ref[hm, rows, :] = m_new
                return carry

            if tq == rq:
                qblock(0, 0)
            else:
                lax.fori_loop(0, tq // rq, qblock, 0)

    pl.when(visible & unmasked)(lambda: process(False))
    pl.when(visible & jnp.logical_not(unmasked))(lambda: process(True))

    @pl.when(j == nkv - 1)
    def _():
        for h in range(nhm // maps):
            if maps == 2:
                o = acc_ref[2 * h] / l_ref[2 * h] - lam_ref[0] * (acc_ref[2 * h + 1] / l_ref[2 * h + 1])
                o = o * lax.rsqrt(jnp.mean(o * o, axis=-1, keepdims=True) + EPS) * g_ref[...] * out_scale
            else:
                o = acc_ref[h] / l_ref[h]
            o_ref[:, h * DA_DV:(h + 1) * DA_DV] = o


def _flash(q, k, v, lam, g_sub, *, nhm, dqk, maps, scale, pos0, tq, tk, out_scale, v_col=0):
    b, l, _ = q.shape
    kk = k.shape[1]
    nkv = kk // tk
    nh = nhm // maps

    def kv_idx(bi, i, j):
        last = (_visible_limit(pos0 + (i + 1) * tq - 1) - 1) // tk
        return jnp.minimum(j, last)

    return pl.pallas_call(
        functools.partial(_flash_kernel, nhm=nhm, dqk=dqk, maps=maps, scale=scale, pos0=pos0, tq=tq, tk=tk,
                          nkv=nkv, out_scale=out_scale, rq=min(FLASH_RQ, tq)),
        grid=(b, l // tq, nkv),
        in_specs=[pl.BlockSpec(memory_space=pltpu.SMEM),
                  pl.BlockSpec((None, tq, nhm * dqk), lambda bi, i, j: (bi, i, 0)),
                  pl.BlockSpec((None, tk, nhm * dqk), lambda bi, i, j: (bi, kv_idx(bi, i, j), 0)),
                  pl.BlockSpec((None, tk, nh * DA_DV), lambda bi, i, j: (bi, kv_idx(bi, i, j), v_col)),
                  _full((1, DA_DV))],
        out_specs=pl.BlockSpec((None, tq, nh * DA_DV), lambda bi, i, j: (bi, i, 0)),
        out_shape=jax.ShapeDtypeStruct((b, l, nh * DA_DV), F32),
        scratch_shapes=[pltpu.VMEM((nhm, tq, 1), F32), pltpu.VMEM((nhm, tq, 1), F32),
                        pltpu.VMEM((nhm, tq, DA_DV), F32)],
        compiler_params=_cparams("parallel", "parallel", "arbitrary"),
        name="flash_da" if maps == 2 else "flash_mla",
    )(lam, q, k, v, g_sub)


def _top16(s, rid, payload, val_ref, pay_ref):
    big = float(s.shape[0])
    for it in range(PEER_TOPK):
        m = jnp.max(s, axis=0, keepdims=True)
        am = jnp.min(jnp.where(s == m, rid, big), axis=0, keepdims=True)
        sel = rid == am
        val_ref[it:it + 1, :] = m
        if payload is None:
            pay_ref[it:it + 1, :] = am
        else:
            pay_ref[it:it + 1, :] = jnp.max(jnp.where(sel, payload, -1.0), axis=0, keepdims=True)
        s = jnp.where(sel, -jnp.inf, s)


_PEER_PAIRS = [(a, b) for a in range(PEER_TOPK) for b in range(PEER_TOPK) if (a + 1) * (b + 1) <= PEER_TOPK]
_PEER_CAND_ROWS = -(-len(_PEER_PAIRS) // 8) * 8


def _route_kernel(q_ref, k1_ref, k2_ref, idx_ref, g_ref, v1_ref, i1_ref, v2_ref, i2_ref, vc_ref, ic_ref,
                  cand_ref, cidx_ref):
    tt = q_ref.shape[0]
    half = q_ref.shape[1] // 2
    nt = (((1,), (1,)), ((), ()))
    s1 = lax.dot_general(k1_ref[...].astype(BF16), q_ref[:, :half].astype(BF16), nt, preferred_element_type=F32)
    s2 = lax.dot_general(k2_ref[...].astype(BF16), q_ref[:, half:].astype(BF16), nt, preferred_element_type=F32)
    rid = lax.broadcasted_iota(jnp.int32, (PEER_NKEYS, tt), 0).astype(F32)
    _top16(s1, rid, None, v1_ref, i1_ref)
    _top16(s2, rid, None, v2_ref, i2_ref)
    npairs = len(_PEER_PAIRS)
    cand_ref[npairs:, :] = jnp.full((_PEER_CAND_ROWS - npairs, tt), -jnp.inf, F32)
    cidx_ref[npairs:, :] = jnp.full((_PEER_CAND_ROWS - npairs, tt), -1.0, F32)
    for r, (a, b) in enumerate(_PEER_PAIRS):
        cand_ref[r:r + 1, :] = v1_ref[a:a + 1, :] + v2_ref[b:b + 1, :]
        cidx_ref[r:r + 1, :] = i1_ref[a:a + 1, :] * float(PEER_NKEYS) + i2_ref[b:b + 1, :]
    rid2 = lax.broadcasted_iota(jnp.int32, (_PEER_CAND_ROWS, tt), 0).astype(F32)
    _top16(cand_ref[...], rid2, cidx_ref[...], vc_ref, ic_ref)
    sc = vc_ref[...]
    e = jnp.exp(sc - sc[0:1, :])
    g_ref[...] = e / jnp.sum(e, axis=0, keepdims=True)
    idx_ref[...] = ic_ref[...].astype(jnp.int32)


def _peer_route(q, k1, k2):
    t = q.shape[0]
    tt = PEER_TT
    dk = 2 * PEER_NKEYS
    kspec = pl.BlockSpec((None, PEER_NKEYS, dk // 2), lambda i, h: (h, 0, 0))
    ospec = pl.BlockSpec((None, PEER_TOPK, tt), lambda i, h: (i, h, 0))
    sm = pltpu.VMEM((PEER_TOPK, tt), F32)
    return pl.pallas_call(
        _route_kernel,
        grid=(t // tt, PEER_HEADS),
        in_specs=[pl.BlockSpec((tt, dk), lambda i, h: (i, h)), kspec, kspec],
        out_specs=[ospec, ospec],
        out_shape=[jax.ShapeDtypeStruct((t // tt, PEER_HK, tt), jnp.int32),
                   jax.ShapeDtypeStruct((t // tt, PEER_HK, tt), F32)],
        scratch_shapes=[sm, sm, sm, sm, sm, sm,
                        pltpu.VMEM((_PEER_CAND_ROWS, tt), F32), pltpu.VMEM((_PEER_CAND_ROWS, tt), F32)],
        compiler_params=_cparams("parallel", "parallel"),
        name="peer_route",
    )(q, k1, k2)


def _pack_table(tab):
    e, d = tab.shape
    tb = lax.bitcast_convert_type(tab.astype(BF16), jnp.uint16).astype(jnp.uint32)
    words = tb[:, :d // 2] | (tb[:, d // 2:] << 16)
    return lax.bitcast_convert_type(words, F32)


def _sc_unpack(words):
    bits = lax.bitcast_convert_type(words, jnp.int32)
    return lax.bitcast_convert_type(bits << 16, F32), lax.bitcast_convert_type(bits & jnp.int32(-65536), F32)


def _sc_gelu(x):
    u = math.sqrt(2.0 / math.pi) * (x + 0.044715 * (x * x * x))
    tanh_u = 1.0 - 2.0 / (jnp.exp(2.0 * u) + 1.0)
    return 0.5 * x * (1.0 + tanh_u)


def _sc_peer(u_tab, v_tab, idx_tm, g_tm, h, side_idx=None):
    t, nk = idx_tm.shape
    w = u_tab.shape[1]
    d = 2 * w
    half_rows = SC_GATHER_ROWS
    nbuf = SC_BUFFERS
    per_tok = nk // half_rows
    mesh = plsc.VectorSubcoreMesh(core_axis_name="core", subcore_axis_name="subcore")
    n_workers = mesh.num_cores * mesh.num_subcores
    tpw = t // n_workers
    tb = SC_TOKEN_BLOCK
    steps = per_tok * tb
    dot_rows = SC_LANES
    sum_rows = 8
    assert t % n_workers == 0 and tpw % tb == 0 and w % SC_LANES == 0
    assert nk % half_rows == 0 and half_rows % SC_LANES == 0 and steps % nbuf == 0

    n_side = 0 if side_idx is None else side_idx.shape[0]
    side_w = SC_SIDE_ROWS
    lw = n_side // n_workers
    n_win = lw // side_w
    blk_side = steps * side_w
    if n_side:
        assert nbuf == 2 and steps % 2 == 0 and n_side % (n_workers * blk_side) == 0
        assert n_win + 2 <= (tpw // tb) * steps

    def lanes(start):
        return pl.ds(pl.multiple_of(start, SC_LANES), SC_LANES)

    y_type = jax.ShapeDtypeStruct((t, d), F32)
    side_type = jax.ShapeDtypeStruct((n_side, w), F32)
    scratch = [pltpu.VMEM((tb * nk,), jnp.int32), pltpu.VMEM((tb, d), F32), pltpu.VMEM((tb, nk), F32),
               pltpu.VMEM((nbuf, half_rows, w), F32), pltpu.VMEM((tb, nk), F32), pltpu.VMEM((tb, d), F32),
               pltpu.SemaphoreType.DMA((nbuf,))]
    if n_side:
        scratch += [pltpu.VMEM((2 * blk_side,), jnp.int32), pltpu.VMEM((2, 2, side_w, w), F32),
                    pltpu.SemaphoreType.DMA((8,))]

    @functools.partial(
        pl.kernel, out_type=[y_type, side_type, side_type] if n_side else y_type, mesh=mesh, scratch_types=scratch,
        compiler_params=pltpu.CompilerParams(needs_layout_passes=False))
    def peer_kernel(u_hbm, v_hbm, idx_hbm, g_hbm, h_hbm, *refs):
        if n_side:
            (sidx_hbm, y_hbm, gu_hbm, gv_hbm, idx_v, h_v, g_v, rows_v, coef_v, y_v, sems,
             sidx_v, sbuf, ssems) = refs
        else:
            y_hbm, idx_v, h_v, g_v, rows_v, coef_v, y_v, sems = refs
        wid = lax.axis_index("subcore") * mesh.num_cores + lax.axis_index("core")
        lane = lax.iota(jnp.int32, SC_LANES)

        def gather(tab_hbm, step, buf):
            rows = idx_v.at[pl.ds(pl.multiple_of(step * half_rows, half_rows), half_rows)]
            return pltpu.make_async_copy(tab_hbm.at[rows], rows_v.at[buf], sems.at[buf])

        def prime(tab_hbm):
            for j in range(nbuf - 1):
                gather(tab_hbm, j, j).start()

        def side_in(tab_hbm, tab, n, par):
            slot = ((n // steps) % 2) * blk_side + (n % steps) * side_w
            rows = sidx_v.at[pl.ds(pl.multiple_of(slot, side_w), side_w)]
            return pltpu.make_async_copy(tab_hbm.at[rows], sbuf.at[tab, par], ssems.at[tab * 4 + par])

        def side_out(out_hbm, tab, n, par):
            dst = out_hbm.at[pl.ds(pl.multiple_of(wid * lw + n * side_w, side_w), side_w)]
            return pltpu.make_async_copy(sbuf.at[tab, par], dst, ssems.at[tab * 4 + 2 + par])

        def side_step(tab_hbm, out_hbm, tab, n, par):
            @pl.when((n >= 2) & (n < n_win + 2))
            def _():
                side_out(out_hbm, tab, n - 2, par).wait()

            @pl.when(n < n_win)
            def _():
                side_in(tab_hbm, tab, n, par).start()

            @pl.when((n >= 1) & (n < n_win + 1))
            def _():
                side_in(tab_hbm, tab, n - 1, 1 - par).wait()
                side_out(out_hbm, tab, n - 1, 1 - par).start()

        def run(tab_hbm, work, blk=None, side=None):
            @pl.loop(0, steps, step=nbuf)
            def _(s):
                for j in range(nbuf):
                    if side is not None:
                        side_step(tab_hbm, side[0], side[1], blk * steps + s + j, j)
                    ahead = s + j + nbuf - 1

                    @pl.when(ahead < steps)
                    def _():
                        gather(tab_hbm, ahead, (j + nbuf - 1) % nbuf).start()

                    gather(tab_hbm, s + j, j).wait()
                    work(s + j, j)

        def dots(step, buf):
            tok = step // per_tok
            col0 = (step % per_tok) * half_rows

            def group(q, _):
                r0 = q * dot_rows

                def chunk(c, accs):
                    h_lo = h_v[tok, lanes(c * SC_LANES)]
                    h_hi = h_v[tok, lanes(w + c * SC_LANES)]
                    out = []
                    for r in range(dot_rows):
                        lo, hi = _sc_unpack(rows_v[buf, r0 + r, lanes(c * SC_LANES)])
                        out.append(accs[r] + lo * h_lo + hi * h_hi)
                    return tuple(out)

                accs = lax.fori_loop(0, w // SC_LANES, chunk, (jnp.zeros((SC_LANES,), F32),) * dot_rows)
                vec = jnp.zeros((SC_LANES,), F32)
                for r in range(dot_rows):
                    vec = jnp.where(lane == r, jnp.sum(accs[r]), vec)
                coef_v[tok, lanes(col0 + r0)] = vec
                return 0

            lax.fori_loop(0, half_rows // dot_rows, group, 0)

        def gates():
            @pl.loop(0, tb)
            def _(tok):
                @pl.loop(0, nk // SC_LANES)
                def _(q):
                    ds = lanes(q * SC_LANES)
                    coef_v[tok, ds] = g_v[tok, ds] * _sc_gelu(coef_v[tok, ds])

        def weighted_sum(step, buf):
            tok = step // per_tok
            col0 = (step % per_tok) * half_rows

            @pl.when(col0 == 0)
            def _():
                @plsc.parallel_loop(0, d // SC_LANES)
                def _(c):
                    y_v[tok, lanes(c * SC_LANES)] = jnp.zeros((SC_LANES,), F32)

            def group(g, _):
                cvec = coef_v[tok, lanes(col0 + (g // 2) * SC_LANES)]
                base = (g % 2) * sum_rows
                cs = [cvec.at[jnp.full((SC_LANES,), base + r, jnp.int32)].get(mode="promise_in_bounds")
                      for r in range(sum_rows)]
                r0 = g * sum_rows

                @plsc.parallel_loop(0, w // SC_LANES, unroll=2)
                def _(c):
                    los, his = [], []
                    for r in range(sum_rows):
                        lo, hi = _sc_unpack(rows_v[buf, r0 + r, lanes(c * SC_LANES)])
                        los.append(cs[r] * lo)
                        his.append(cs[r] * hi)
                    while len(los) > 1:
                        los = [a + b for a, b in zip(los[::2], los[1::2])]
                        his = [a + b for a, b in zip(his[::2], his[1::2])]
                    y_v[tok, lanes(c * SC_LANES)] = y_v[tok, lanes(c * SC_LANES)] + los[0]
                    y_v[tok, lanes(w + c * SC_LANES)] = y_v[tok, lanes(w + c * SC_LANES)] + his[0]

                return 0

            lax.fori_loop(0, half_rows // sum_rows, group, 0)

        @pl.loop(0, tpw // tb)
        def _(blk):
            tok0 = wid * tpw + blk * tb
            pltpu.sync_copy(idx_hbm.at[pl.ds(pl.multiple_of(tok0 * nk, nk), tb * nk)], idx_v)
            prime(u_hbm)
            pltpu.sync_copy(h_hbm.at[pl.ds(tok0, tb)], h_v)
            pltpu.sync_copy(g_hbm.at[pl.ds(tok0, tb)], g_v)
            if n_side:
                @pl.when(blk * steps < n_win)
                def _():
                    src = sidx_hbm.at[pl.ds(pl.multiple_of(wid * lw + blk * blk_side, blk_side), blk_side)]
                    pltpu.sync_copy(src, sidx_v.at[pl.ds(pl.multiple_of((blk % 2) * blk_side, blk_side), blk_side)])

            run(u_hbm, dots, blk, (gu_hbm, 0) if n_side else None)
            prime(v_hbm)
            gates()
            run(v_hbm, weighted_sum, blk, (gv_hbm, 1) if n_side else None)
            pltpu.sync_copy(y_v, y_hbm.at[pl.ds(tok0, tb)])

    if n_side:
        return peer_kernel(u_tab, v_tab, idx_tm.reshape(t * nk), g_tm, h, side_idx)
    return peer_kernel(u_tab, v_tab, idx_tm.reshape(t * nk), g_tm, h)


def _unpack(words):
    w = pltpu.bitcast(words, jnp.uint32)
    lo = pltpu.bitcast(w << 16, F32)
    hi = pltpu.bitcast(w & jnp.uint32(0xFFFF0000), F32)
    return lo, hi


def _peer_act_kernel(h_ref, gu_ref, g_ref, coef_ref, act_ref):
    c = pl.program_id(1)
    half = h_ref.shape[1] // 2
    h_lo, h_hi = h_ref[:, :half], h_ref[:, half:]
    lane = lax.broadcasted_iota(jnp.int32, act_ref.shape, 1)

    @pl.when(c == 0)
    def _():
        act_ref[...] = jnp.zeros(act_ref.shape, F32)

    act = act_ref[...]
    for k in range(PEER_HKC):
        lo, hi = _unpack(gu_ref[k])
        a = jnp.sum(lo * h_lo + hi * h_hi, axis=-1, keepdims=True)
        act = jnp.where(lane == c * PEER_HKC + k, a, act)
    act_ref[...] = act

    @pl.when(c == pl.num_programs(1) - 1)
    def _():
        coef_ref[...] = g_ref[...].T * jax.nn.gelu(act)


def _peer_act(h, gu, g):
    t, d = h.shape
    tt = PEER_TT
    nt = t // tt
    return pl.pallas_call(
        _peer_act_kernel,
        grid=(nt, PEER_HK // PEER_HKC),
        in_specs=[pl.BlockSpec((tt, d), lambda i, c: (i, 0)),
                  pl.BlockSpec((None, PEER_HKC, tt, d // 2), lambda i, c: (i, c, 0, 0)),
                  pl.BlockSpec((None, PEER_HK, tt), lambda i, c: (i, 0, 0))],
        out_specs=pl.BlockSpec((None, tt, PEER_HK), lambda i, c: (i, 0, 0)),
        out_shape=jax.ShapeDtypeStruct((nt, tt, PEER_HK), F32),
        scratch_shapes=[pltpu.VMEM((tt, PEER_HK), F32)],
        compiler_params=_cparams("parallel", "arbitrary"),
        name="peer_act",
    )(h, gu, g)


def _peer_mix_kernel(coef_ref, gv_ref, y_ref):
    c = pl.program_id(1)
    half = y_ref.shape[1] // 2

    @pl.when(c == 0)
    def _():
        y_ref[...] = jnp.zeros(y_ref.shape, F32)

    coef = coef_ref[...]
    lane = lax.broadcasted_iota(jnp.int32, coef.shape, 1)
    acc_lo, acc_hi = y_ref[:, :half], y_ref[:, half:]
    for k in range(PEER_HKC):
        ck = jnp.sum(jnp.where(lane == c * PEER_HKC + k, coef, 0.0), axis=-1, keepdims=True)
        lo, hi = _unpack(gv_ref[k])
        acc_lo = acc_lo + ck * lo
        acc_hi = acc_hi + ck * hi
    y_ref[:, :half] = acc_lo
    y_ref[:, half:] = acc_hi


def _peer_mix(coef, gv):
    nt, tt, _ = coef.shape
    w = gv.shape[-1]
    return pl.pallas_call(
        _peer_mix_kernel,
        grid=(nt, PEER_HK // PEER_HKC),
        in_specs=[pl.BlockSpec((None, tt, PEER_HK), lambda i, c: (i, 0, 0)),
                  pl.BlockSpec((None, PEER_HKC, tt, w), lambda i, c: (i, c, 0, 0))],
        out_specs=pl.BlockSpec((tt, 2 * w), lambda i, c: (i, 0)),
        out_shape=jax.ShapeDtypeStruct((nt * tt, 2 * w), F32),
        compiler_params=_cparams("parallel", "arbitrary"),
        name="peer_mix",
    )(coef, gv)


def _peer_out_kernel(x_ref, gt_ref, y_ref, gf_ref, o_ref, *, final):
    x = x_ref[...] + gt_ref[...] * y_ref[...]
    if final:
        x = x * lax.rsqrt(jnp.mean(x * x, axis=-1, keepdims=True) + EPS) * gf_ref[...]
    o_ref[...] = x


def _peer_out(x, gt, y, g_final, final):
    t, d = x.shape
    tt = PEER_TT
    nt = t // tt
    row = pl.BlockSpec((tt, d), lambda i: (i, 0))
    if gt.ndim == 3:
        per_seq = nt // gt.shape[0]
        gt_spec = pl.BlockSpec((None, 1, d), lambda i: (i // per_seq, 0, 0))
    else:
        gt_spec = row
    return pl.pallas_call(
        functools.partial(_peer_out_kernel, final=final),
        grid=(nt,),
        in_specs=[row, gt_spec, row, _full((1, d))],
        out_specs=row, out_shape=jax.ShapeDtypeStruct((t, d), F32),
        compiler_params=_cparams("parallel"),
        name="peer_out",
    )(x, gt, y, g_final.reshape(1, d))


def _peer(x, g_ffn, sc2, sh2, gt2, w_q, k1, k2, u_pack, v_pack, g_final, final, tl):
    b, l, d = x.shape
    t = b * l
    q, h = _norm_mod_matmul(x, g_ffn, sc2, sh2, w_q, tl, want_h=True)
    idx, g = _peer_route(q.reshape(t, -1), k1, k2)
    h = h.reshape(t, d)
    nt = t // PEER_TT
    n_tc = nt * PEER_TC_TILES // 32
    n_sc = nt - n_tc
    t_sc = n_sc * PEER_TT
    token_major = lambda a: jnp.transpose(a, (0, 2, 1)).reshape(-1, PEER_HK)
    if n_tc:
        y_sc, gu, gv = _sc_peer(u_pack, v_pack, token_major(idx[:n_sc]), token_major(g[:n_sc]), h[:t_sc],
                                idx[n_sc:].reshape(-1))
        rows = lambda a: a.reshape(n_tc, PEER_HK, PEER_TT, d // 2)
        y_tc = _peer_mix(_peer_act(h[t_sc:], rows(gu), g[n_sc:]), rows(gv))
        y = jnp.concatenate([y_sc, y_tc], axis=0)
    else:
        y = _sc_peer(u_pack, v_pack, token_major(idx), token_major(g), h)
    if l % PEER_TT:
        gt2 = jnp.broadcast_to(gt2, (b, l, d)).reshape(t, d)
    return _peer_out(x.reshape(t, d), gt2, y, g_final, final).reshape(b, l, d)


def _trunk(x, mod, pos0, past, p, tl):
    b, l, d = x.shape
    outs = {}
    for layer in range(DEPTH):
        sh1, sc1, gt1, sh2, sc2, gt2 = [m[:, None, :] for m in jnp.split(mod[layer], 6, axis=-1)]
        if layer % 2 == 0:
            e = layer // 2
            y_in = _norm_mod_matmul(x, p['g_mix'][layer], sc1, sh1, p['w_in_e'][e].astype(BF16), tl)
            if past is None:
                h0_re = jnp.zeros((b, S5_GROUPS, S5_STATE), F32)
                h0_im = jnp.zeros((b, S5_GROUPS, S5_STATE), F32)
            else:
                h0_re, h0_im = past['s5_re'][e], past['s5_im'][e]
            ya, hr, hi = _s5(y_in, h0_re, h0_im, p['s5'][e], p['s5_d'][e], p['w_glu'][e], p['b_glu'][e], tl)
            yb, vn = _gmlp(y_in, p['gm_g_v'][e], p['gm_w_s'][e], p['gm_b_s'][e], tl, want_vn=past is not None)
            x = _proj_residual(x, gt1, ya, yb, p['w_out_e'][e], tl)
            outs.update(s5_re=hr, s5_im=hi, gm_v=vn)
        else:
            o = layer // 2
            w_in = jnp.pad(p['w_in_o'][o], ((0, 0), (0, 2048 - p['w_in_o'][o].shape[1]))).astype(BF16)
            y_in = _norm_mod_matmul(x, p['g_mix'][layer], sc1, sh1, w_in, tl)
            qd, kd, qm, lat, kpe = _odd_prep(y_in, pos0, p['mla_g_q'][o], p['mla_w_uq'][o], p['mla_g_kv'][o], tl)
            v_new = y_in[:, :, 2 * 512:3 * 512]
            lam_init = 0.8 - 0.6 * math.exp(-0.3 * layer)
            lam = (jnp.exp(jnp.sum(p['da_lq1'][o] * p['da_lk1'][o])) - jnp.exp(jnp.sum(p['da_lq2'][o] * p['da_lk2'][o]))
                   + lam_init).reshape(1).astype(F32)
            g_sub = p['da_g_sub'][o].reshape(1, DA_DV)
            if past is None:
                k_all, v_all, v_col, lat_all, kpe_all = kd, y_in, 2, lat, kpe
                tq = tk = tl
            else:
                k_all = jnp.concatenate([past['da_k'][o].reshape(b, -1, 512), kd], axis=1)
                v_all = jnp.concatenate([past['da_v'][o].reshape(b, -1, 512), v_new], axis=1)
                lat_all = jnp.concatenate([past['mla_lat'][o], lat], axis=1)
                kpe_all = jnp.concatenate([past['mla_kpe'][o], kpe], axis=1)
                v_col, tq, tk = 0, l, k_all.shape[1]
            yc = _flash(qd, k_all, v_all, lam, g_sub, nhm=2 * DA_HEADS, dqk=DA_DH, maps=2, scale=DA_DH ** -0.5,
                        pos0=pos0, tq=tq, tk=tk, out_scale=1.0 - lam_init, v_col=v_col)
            km, vm = _mla_kv(lat_all, kpe_all, p['mla_w_ukv'][o], tk)
            yd = _flash(qm, km, vm, lam, g_sub, nhm=MLA_HEADS, dqk=LANES, maps=1,
                        scale=(MLA_NOPE + MLA_ROPE) ** -0.5, pos0=pos0, tq=tq, tk=tk, out_scale=1.0)
            x = _proj_residual(x, gt1, yc, yd, p['w_out_o'][o], tl)
            outs.update(da_k=kd.reshape(b, l, DA_HEADS, 2 * DA_DH), da_v=v_new.reshape(b, l, DA_HEADS, DA_DV),
                        mla_lat=lat, mla_kpe=kpe)
        x = _peer(x, p['g_ffn'][layer], sc2, sh2, gt2, p['peer_w_q'][layer].astype(BF16), p['peer_k1'][layer],
                  p['peer_k2'][layer], p['u_pack'][layer], p['v_pack'][layer], p['g_final'],
                  final=layer == DEPTH - 1, tl=tl)
    return x, outs


def kernel(x_prompt, x_sample, c_prompt, c_sample, state_s5_re, state_s5_im, cache_da_k, cache_da_v, cache_mla_latent, cache_mla_kpe, w_ada, b_ada, g_mix, g_ffn, g_final, w_in_e, w_out_e, s5_lam_re, s5_lam_im, s5_log_dt, s5_b_re, s5_b_im, s5_c_re, s5_c_im, s5_d, w_glu, b_glu, gm_g_v, gm_w_s, gm_b_s, w_in_o, w_out_o, da_lq1, da_lk1, da_lq2, da_lk2, da_g_sub, mla_g_q, mla_w_uq, mla_g_kv, mla_w_ukv, peer_w_q, peer_k1, peer_k2, peer_u, peer_v):
    p = dict(g_mix=g_mix, g_ffn=g_ffn, g_final=g_final, w_in_e=w_in_e, w_out_e=w_out_e, s5_d=s5_d, w_glu=w_glu,
             b_glu=b_glu, gm_g_v=gm_g_v, gm_w_s=gm_w_s, gm_b_s=gm_b_s, w_in_o=w_in_o, w_out_o=w_out_o,
             da_lq1=da_lq1, da_lk1=da_lk1, da_lq2=da_lq2, da_lk2=da_lk2, da_g_sub=da_g_sub, mla_g_q=mla_g_q,
             mla_w_uq=mla_w_uq, mla_g_kv=mla_g_kv, mla_w_ukv=mla_w_ukv, peer_w_q=peer_w_q, peer_k1=peer_k1,
             peer_k2=peer_k2)
    n_even = (DEPTH + 1) // 2
    p['s5'] = [_s5_params(s5_lam_re[e], s5_lam_im[e], s5_log_dt[e], s5_b_re[e], s5_b_im[e], s5_c_re[e], s5_c_im[e])
               for e in range(n_even)]
    p['u_pack'] = [_pack_table(peer_u[layer]) for layer in range(DEPTH)]
    p['v_pack'] = [_pack_table(peer_v[layer]) for layer in range(DEPTH)]
    past = dict(s5_re=state_s5_re, s5_im=state_s5_im, da_k=cache_da_k, da_v=cache_da_v,
                mla_lat=cache_mla_latent, mla_kpe=cache_mla_kpe)
    nb = x_prompt.shape[0]
    mod = _ada(jnp.concatenate([c_prompt, c_sample], axis=0), w_ada, b_ada)
    past_len = cache_da_k.shape[2]
    parts = [_trunk(x_prompt[s:s + 1], mod[:, s:s + 1], 0, None, p, tl=512) for s in range(nb)]
    y_p = jnp.concatenate([y for y, _ in parts], axis=0)
    o_p = {k: jnp.concatenate([o[k] for _, o in parts], axis=0) for k in parts[0][1] if parts[0][1][k] is not None}
    y_s, o_s = _trunk(x_sample, mod[:, nb:], past_len, past, p, tl=x_sample.shape[1])
    st = lambda a: a[None]
    return (y_p, y_s, st(o_p['s5_re']), st(o_p['s5_im']), st(o_s['s5_re']), st(o_s['s5_im']), st(o_s['gm_v']),
            st(o_p['da_k']), st(o_p['da_v']), st(o_s['da_k']), st(o_s['da_v']),
            st(o_p['mla_lat']), st(o_p['mla_kpe']), st(o_s['mla_lat']), st(o_s['mla_kpe']))
```

```python
import functools
import math

import jax
import jax.numpy as jnp
from jax import lax
from jax.experimental import pallas as pl
from jax.experimental.pallas import tpu as pltpu
from jax.experimental.pallas import tpu_sc as plsc

F32 = jnp.float32
BF16 = jnp.bfloat16

D_MODEL = 1024
DEPTH = 2
CHUNK = 64
ROPE_THETA = 500000.0
EPS = 1e-6
NEG = -1e30

S5_WIDTH = 512
S5_GROUP = 16
S5_GROUPS = 32
S5_STATE = 64
GM_WIDTH = 512
GM_HEADS = 4
GM_CHUNK = 128
DA_HEADS = 4
DA_DH = 64
DA_DV = 128
DA_ROPE = 16
MLA_HEADS = 4
MLA_Q_RANK = 256
MLA_KV_RANK = 128
MLA_NOPE = 64
MLA_ROPE = 32
MLA_DV = 128
PEER_HEADS = 8
PEER_NKEYS = 128
PEER_TOPK = 16
PEER_HK = PEER_HEADS * PEER_TOPK

LANES = 128
VMEM_LIMIT = 48 << 20
PEER_TT = 256
SC_LANES = 16
SC_TOKEN_BLOCK = 8
SC_GATHER_ROWS = 64
SC_BUFFERS = 2
SC_SIDE_ROWS = 16
PEER_TC_TILES = 6
PEER_HKC = 16

_HI = lax.Precision.HIGHEST


def _cparams(*sem):
    return pltpu.CompilerParams(dimension_semantics=sem, vmem_limit_bytes=VMEM_LIMIT)


def _full(shape):
    n = len(shape)
    return pl.BlockSpec(shape, lambda *_: (0,) * n)


def _ada_kernel(c_ref, w_ref, b_ref, o_ref):
    c = c_ref[...]
    s = c * jax.nn.sigmoid(c)
    o_ref[...] = jnp.dot(s, w_ref[...], precision=_HI, preferred_element_type=F32) + b_ref[...]


def _ada(c_all, w_ada, b_ada):
    r = c_all.shape[0]
    tn = 1536
    return pl.pallas_call(
        _ada_kernel,
        grid=(DEPTH, 6 * D_MODEL // tn),
        in_specs=[_full((r, D_MODEL)),
                  pl.BlockSpec((None, D_MODEL, tn), lambda l, j: (l, 0, j)),
                  pl.BlockSpec((None, 1, tn), lambda l, j: (l, 0, j))],
        out_specs=pl.BlockSpec((None, r, tn), lambda l, j: (l, 0, j)),
        out_shape=jax.ShapeDtypeStruct((DEPTH, r, 6 * D_MODEL), F32),
        compiler_params=_cparams("parallel", "parallel"),
        name="ada",
    )(c_all, w_ada, b_ada.reshape(DEPTH, 1, 6 * D_MODEL))


def _nmm_kernel(x_ref, g_ref, sc_ref, sh_ref, w_ref, o_ref, *h_ref):
    x = x_ref[...]
    ms = jnp.mean(x * x, axis=-1, keepdims=True)
    h = x * lax.rsqrt(ms + EPS) * g_ref[...] * (1.0 + sc_ref[...]) + sh_ref[...]
    o_ref[...] = jnp.dot(h.astype(BF16), w_ref[...], preferred_element_type=F32)
    if h_ref:
        h_ref[0][...] = h


def _norm_mod_matmul(x, g, sc, sh, w, tl, want_h=False):
    b, l, d = x.shape
    n = w.shape[1]
    row = pl.BlockSpec((None, 1, d), lambda bi, i: (bi, 0, 0))
    out_shape = [jax.ShapeDtypeStruct((b, l, n), F32)]
    out_specs = [pl.BlockSpec((None, tl, n), lambda bi, i: (bi, i, 0))]
    if want_h:
        out_shape.append(jax.ShapeDtypeStruct((b, l, d), F32))
        out_specs.append(pl.BlockSpec((None, tl, d), lambda bi, i: (bi, i, 0)))
    res = pl.pallas_call(
        _nmm_kernel,
        grid=(b, l // tl),
        in_specs=[pl.BlockSpec((None, tl, d), lambda bi, i: (bi, i, 0)), _full((1, d)), row, row, _full((d, n))],
        out_specs=out_specs, out_shape=out_shape,
        compiler_params=_cparams("parallel", "parallel"),
        name="norm_mod_matmul",
    )(x, g.reshape(1, d), sc, sh, w)
    return res if want_h else res[0]


def _s5_kernel(xa_ref, h0r_ref, h0i_ref, bblk_ref, ar_ref, ai_ref, cr_ref, ci_ref, d_ref, wglu_ref, bglu_ref,
               ya_ref, hr_out_ref, hi_out_ref, bur, bui, st_r, st_i, *, tl):
    nj = S5_WIDTH // LANES
    half = 8 * S5_STATE

    @pl.when(pl.program_id(1) == 0)
    def _():
        st_r[...] = h0r_ref[...]
        st_i[...] = h0i_ref[...]

    xa = xa_ref[...]
    for j in range(nj):
        bu = jnp.dot(xa[:, j * LANES:(j + 1) * LANES], bblk_ref[j], precision=_HI, preferred_element_type=F32)
        bur[j] = bu[:, :half]
        bui[j] = bu[:, half:]

    for j0 in range(0, nj, 2):
        js = (j0, j0 + 1)
        a_r = [ar_ref[j] for j in js]
        a_i = [ai_ref[j] for j in js]

        def body(k, carry, js=js, a_r=a_r, a_i=a_i):
            carry = list(carry)
            base = pl.multiple_of(k * 8, 8)
            for s in range(8):
                t = base + s
                for q, j in enumerate(js):
                    hr, hi = carry[2 * q], carry[2 * q + 1]
                    nhr = a_r[q] * hr - a_i[q] * hi + bur[j, pl.ds(t, 1), :]
                    nhi = a_r[q] * hi + a_i[q] * hr + bui[j, pl.ds(t, 1), :]
                    bur[j, pl.ds(t, 1), :] = nhr
                    bui[j, pl.ds(t, 1), :] = nhi
                    carry[2 * q], carry[2 * q + 1] = nhr, nhi
            return tuple(carry)

        init = (st_r[js[0]], st_i[js[0]], st_r[js[1]], st_i[js[1]])
        fin = lax.fori_loop(0, tl // 8, body, init)
        st_r[js[0]], st_i[js[0]], st_r[js[1]], st_i[js[1]] = fin

    hr_out_ref[...] = st_r[...]
    hi_out_ref[...] = st_i[...]

    ys = []
    for j in range(nj):
        y = (jnp.dot(bur[j].astype(BF16), cr_ref[j], preferred_element_type=F32)
             + jnp.dot(bui[j].astype(BF16), ci_ref[j], preferred_element_type=F32))
        ys.append(y + d_ref[:, j * LANES:(j + 1) * LANES] * xa[:, j * LANES:(j + 1) * LANES])
    y = jax.nn.gelu(jnp.concatenate(ys, axis=1))
    z = jnp.dot(y.astype(BF16), wglu_ref[...], preferred_element_type=F32) + bglu_ref[...]
    ya_ref[...] = z[:, :S5_WIDTH] * jax.nn.sigmoid(z[:, S5_WIDTH:])


def _s5_params(lam_re, lam_im, log_dt, b_re, b_im, c_re, c_im):
    lam = lax.complex(lam_re.astype(F32), lam_im.astype(F32))
    dt = jnp.exp(log_dt.astype(F32))[:, None]
    a_bar = jnp.exp(lam * dt)
    b_bar = ((a_bar - 1.0) / lam)[..., None] * lax.complex(b_re.astype(F32), b_im.astype(F32))
    nj = S5_GROUPS // 8
    eye = jnp.eye(8, dtype=F32)

    def blk_b(m):
        m = m.reshape(nj, 8, S5_STATE, S5_GROUP)
        return jnp.einsum('jgpc,gh->jgchp', m, eye).reshape(nj, 8 * S5_GROUP, 8 * S5_STATE)

    def blk_c(m):
        m = m.reshape(nj, 8, S5_GROUP, S5_STATE)
        return jnp.einsum('jgcp,gh->jgphc', m, eye).reshape(nj, 8 * S5_STATE, 8 * S5_GROUP)

    bblk = jnp.concatenate([blk_b(jnp.real(b_bar)), blk_b(jnp.imag(b_bar))], axis=-1)
    cr = blk_c(c_re.astype(F32)).astype(BF16)
    ci = (-blk_c(c_im.astype(F32))).astype(BF16)
    a_r = jnp.real(a_bar).reshape(nj, 1, 8 * S5_STATE)
    a_i = jnp.imag(a_bar).reshape(nj, 1, 8 * S5_STATE)
    return bblk, a_r, a_i, cr, ci


def _s5(y_in, h0_re, h0_im, sp, d_skip, w_glu, b_glu, tl):
    b, l, _ = y_in.shape
    nj = S5_GROUPS // 8
    half = 8 * S5_STATE
    bblk, a_r, a_i, cr, ci = sp
    st = pl.BlockSpec((None, nj, 1, half), lambda bi, i: (bi, 0, 0, 0))
    ya, hr, hi = pl.pallas_call(
        functools.partial(_s5_kernel, tl=tl),
        grid=(b, l // tl),
        in_specs=[pl.BlockSpec((None, tl, S5_WIDTH), lambda bi, i: (bi, i, 0)), st, st,
                  _full(bblk.shape), _full(a_r.shape), _full(a_i.shape), _full(cr.shape), _full(ci.shape),
                  _full((1, S5_WIDTH)), _full((S5_WIDTH, 2 * S5_WIDTH)), _full((1, 2 * S5_WIDTH))],
        out_specs=[pl.BlockSpec((None, tl, S5_WIDTH), lambda bi, i: (bi, i, 0)), st, st],
        out_shape=[jax.ShapeDtypeStruct((b, l, S5_WIDTH), F32),
                   jax.ShapeDtypeStruct((b, nj, 1, half), F32), jax.ShapeDtypeStruct((b, nj, 1, half), F32)],
        scratch_shapes=[pltpu.VMEM((nj, tl, half), F32), pltpu.VMEM((nj, tl, half), F32),
                        pltpu.VMEM((nj, 1, half), F32), pltpu.VMEM((nj, 1, half), F32)],
        compiler_params=_cparams("parallel", "arbitrary"),
        name="s5",
    )(y_in, h0_re.reshape(b, nj, 1, half), h0_im.reshape(b, nj, 1, half), bblk, a_r, a_i, cr, ci,
      d_skip.reshape(1, S5_WIDTH), w_glu.astype(BF16), b_glu.reshape(1, 2 * S5_WIDTH))
    return ya, hr.reshape(b, S5_GROUPS, S5_STATE), hi.reshape(b, S5_GROUPS, S5_STATE)


def _gmlp_kernel(u_ref, v_ref, gv_ref, ws_ref, bias_ref, yb_ref, *vn_ref, tl, lc):
    hd = GM_WIDTH // GM_HEADS
    gv = jax.nn.gelu(v_ref[...])
    vn = gv * lax.rsqrt(jnp.mean(gv * gv, axis=-1, keepdims=True) + EPS) * gv_ref[...]
    if vn_ref:
        vn_ref[0][...] = vn
    gu = jax.nn.gelu(u_ref[...])
    vb = vn.astype(BF16)
    for c in range(tl // lc):
        rows = slice(c * lc, (c + 1) * lc)
        for h in range(GM_HEADS):
            cols = slice(h * hd, (h + 1) * hd)
            mixed = jnp.dot(ws_ref[h], vb[rows, cols], preferred_element_type=F32) + bias_ref[:, cols]
            yb_ref[rows, cols] = gu[rows, cols] * mixed


def _gmlp(y_in, g_v, w_s, b_s, tl, want_vn):
    b, l, _ = y_in.shape
    lc = min(GM_CHUNK, l)
    hd = GM_WIDTH // GM_HEADS
    tri = jnp.tril(jnp.ones((GM_CHUNK, GM_CHUNK), dtype=bool))
    ws = jnp.where(tri[None], w_s, 0.0)[:, :lc, :lc].astype(BF16)
    bias = jnp.repeat(jnp.transpose(b_s)[:lc], hd, axis=1)
    out_shape = [jax.ShapeDtypeStruct((b, l, GM_WIDTH), F32)]
    out_specs = [pl.BlockSpec((None, tl, GM_WIDTH), lambda bi, i: (bi, i, 0))]
    if want_vn:
        out_shape.append(jax.ShapeDtypeStruct((b, l, GM_WIDTH), F32))
        out_specs.append(pl.BlockSpec((None, tl, GM_WIDTH), lambda bi, i: (bi, i, 0)))
    res = pl.pallas_call(
        functools.partial(_gmlp_kernel, tl=tl, lc=lc),
        grid=(b, l // tl),
        in_specs=[pl.BlockSpec((None, tl, GM_WIDTH), lambda bi, i: (bi, i, 1)),
                  pl.BlockSpec((None, tl, GM_WIDTH), lambda bi, i: (bi, i, 2)),
                  _full((1, GM_WIDTH)), _full(ws.shape), _full(bias.shape)],
        out_specs=out_specs, out_shape=out_shape,
        compiler_params=_cparams("parallel", "parallel"),
        name="gmlp",
    )(y_in, y_in, g_v.reshape(1, GM_WIDTH), ws, bias)
    return (res[0], res[1]) if want_vn else (res[0], None)


def _proj_res_kernel(x_ref, gt_ref, a_ref, b_ref, wa_ref, wb_ref, o_ref):
    mix = (jnp.dot(a_ref[...].astype(BF16), wa_ref[...], preferred_element_type=F32)
           + jnp.dot(b_ref[...].astype(BF16), wb_ref[...], preferred_element_type=F32))
    o_ref[...] = x_ref[...] + gt_ref[...] * mix


def _proj_residual(x, gt, a, bm, w, tl):
    b, l, d = x.shape
    ka = a.shape[-1]
    wa, wb = w[:ka].astype(BF16), w[ka:].astype(BF16)
    blk = lambda n: pl.BlockSpec((None, tl, n), lambda bi, i: (bi, i, 0))
    return pl.pallas_call(
        _proj_res_kernel,
        grid=(b, l // tl),
        in_specs=[blk(d), pl.BlockSpec((None, 1, d), lambda bi, i: (bi, 0, 0)), blk(ka), blk(bm.shape[-1]),
                  _full(wa.shape), _full(wb.shape)],
        out_specs=blk(d), out_shape=jax.ShapeDtypeStruct((b, l, d), F32),
        compiler_params=_cparams("parallel", "parallel"),
        name="proj_residual",
    )(x, gt, a, bm, wa, wb)


def _rope_tables(pos0, l):
    pos = (pos0 + jnp.arange(l, dtype=jnp.int32)).astype(F32)[:, None]
    lane = jnp.arange(LANES)

    def table(period, start, rot):
        half = rot // 2
        r = lane % period - start
        inside = (r >= 0) & (r < rot)
        k = jnp.where(inside, r % half, 0)
        inv = ROPE_THETA ** (-k.astype(F32) * 2.0 / rot)
        ang = pos * inv[None, :]
        cos = jnp.where(inside[None], jnp.cos(ang), 1.0)
        sgn = jnp.where(r < half, -1.0, 1.0)
        sin = jnp.where(inside[None], jnp.sin(ang) * sgn[None], 0.0)
        return cos, sin

    return table(DA_DH, 0, DA_ROPE), table(LANES, MLA_NOPE, MLA_ROPE), table(LANES, 0, MLA_ROPE)


def _rotate(x, cos, sin, period, start, rot):
    half = rot // 2
    n = x.shape[-1]
    reps = n // LANES
    lane = (lax.broadcasted_iota(jnp.int32, x.shape, 1) & (period - 1)) - start
    up = pltpu.roll(x, n - half, 1)
    dn = pltpu.roll(x, half, 1)
    partner = jnp.where(lane < half, up, dn)
    if reps > 1:
        cos = jnp.concatenate([cos] * reps, axis=1)
        sin = jnp.concatenate([sin] * reps, axis=1)
    return x * cos + partner * sin


def _odd_prep_kernel(q_ref, k_ref, cq_ref, ckv_ref, kpe_ref, cd_ref, sd_ref, cm_ref, sm_ref, ck_ref, sk_ref,
                     gq_ref, wuq_ref, gkv_ref, qd_ref, kd_ref, qm_ref, lat_ref, kpe_out_ref):
    cd, sd = cd_ref[...], sd_ref[...]
    qd_ref[...] = _rotate(q_ref[...], cd, sd, DA_DH, 0, DA_ROPE).astype(BF16)
    kd_ref[...] = _rotate(k_ref[...], cd, sd, DA_DH, 0, DA_ROPE)
    cq = cq_ref[...]
    cqn = cq * lax.rsqrt(jnp.mean(cq * cq, axis=-1, keepdims=True) + EPS) * gq_ref[...]
    qm = jnp.dot(cqn.astype(BF16), wuq_ref[...], preferred_element_type=F32)
    qm_ref[...] = _rotate(qm, cm_ref[...], sm_ref[...], LANES, MLA_NOPE, MLA_ROPE).astype(BF16)
    ckv = ckv_ref[...]
    lat_ref[...] = ckv * lax.rsqrt(jnp.mean(ckv * ckv, axis=-1, keepdims=True) + EPS) * gkv_ref[...]
    kpe = _rotate(kpe_ref[...], ck_ref[...], sk_ref[...], LANES, 0, MLA_ROPE)
    kpe_out_ref[...] = kpe[:, :MLA_ROPE]


def _odd_prep(y_in, pos0, g_q, w_uq, g_kv, tl):
    b, l, _ = y_in.shape
    (cd, sd), (cm, sm), (ck, sk) = _rope_tables(pos0, l)
    per = MLA_NOPE + MLA_ROPE
    wuq = jnp.pad(w_uq.reshape(MLA_Q_RANK, MLA_HEADS, per), ((0, 0), (0, 0), (0, LANES - per)))
    wuq = wuq.reshape(MLA_Q_RANK, MLA_HEADS * LANES).astype(BF16)
    col = lambda w, j: pl.BlockSpec((None, tl, w), lambda bi, i: (bi, i, j))
    tab = pl.BlockSpec((tl, LANES), lambda bi, i: (i, 0))
    out = lambda w: pl.BlockSpec((None, tl, w), lambda bi, i: (bi, i, 0))
    return pl.pallas_call(
        _odd_prep_kernel,
        grid=(b, l // tl),
        in_specs=[col(512, 0), col(512, 1), col(256, 6), col(128, 14), col(128, 15),
                  tab, tab, tab, tab, tab, tab,
                  _full((1, MLA_Q_RANK)), _full(wuq.shape), _full((1, MLA_KV_RANK))],
        out_specs=[out(512), out(512), out(512), out(MLA_KV_RANK), out(MLA_ROPE)],
        out_shape=[jax.ShapeDtypeStruct((b, l, 512), BF16), jax.ShapeDtypeStruct((b, l, 512), F32),
                   jax.ShapeDtypeStruct((b, l, 512), BF16), jax.ShapeDtypeStruct((b, l, MLA_KV_RANK), F32),
                   jax.ShapeDtypeStruct((b, l, MLA_ROPE), F32)],
        compiler_params=_cparams("parallel", "parallel"),
        name="odd_prep",
    )(y_in, y_in, y_in, y_in, y_in, cd, sd, cm, sm, ck, sk,
      g_q.reshape(1, MLA_Q_RANK), wuq, g_kv.reshape(1, MLA_KV_RANK))


def _mla_kv_kernel(lat_ref, kpe_ref, wk_ref, wv_ref, place_ref, k_ref, v_ref):
    lat = lat_ref[...].astype(BF16)
    k = (jnp.dot(lat, wk_ref[...], preferred_element_type=F32)
         + jnp.dot(kpe_ref[...].astype(BF16), place_ref[...], preferred_element_type=F32))
    k_ref[...] = k.astype(BF16)
    v_ref[...] = jnp.dot(lat, wv_ref[...], preferred_element_type=F32).astype(BF16)


def _mla_kv(lat_all, kpe_all, w_ukv, tk):
    b, k, _ = lat_all.shape
    per = MLA_NOPE + MLA_DV
    w3 = w_ukv.reshape(MLA_KV_RANK, MLA_HEADS, per)
    wk = jnp.pad(w3[:, :, :MLA_NOPE], ((0, 0), (0, 0), (0, LANES - MLA_NOPE))).reshape(MLA_KV_RANK, MLA_HEADS * LANES)
    wv = w3[:, :, MLA_NOPE:].reshape(MLA_KV_RANK, MLA_HEADS * MLA_DV)
    place = jnp.pad(jnp.eye(MLA_ROPE, dtype=F32), ((0, 0), (MLA_NOPE, LANES - MLA_NOPE - MLA_ROPE)))
    place = jnp.tile(place, (1, MLA_HEADS))
    blk = lambda w: pl.BlockSpec((None, tk, w), lambda bi, i: (bi, i, 0))
    return pl.pallas_call(
        _mla_kv_kernel,
        grid=(b, k // tk),
        in_specs=[blk(MLA_KV_RANK), blk(MLA_ROPE), _full(wk.shape), _full(wv.shape), _full(place.shape)],
        out_specs=[blk(512), blk(512)],
        out_shape=[jax.ShapeDtypeStruct((b, k, 512), BF16), jax.ShapeDtypeStruct((b, k, 512), BF16)],
        compiler_params=_cparams("parallel", "parallel"),
        name="mla_kv",
    )(lat_all, kpe_all, wk.astype(BF16), wv.astype(BF16), place.astype(BF16))


def _visible_limit(pos_last):
    shift = CHUNK.bit_length() - 1
    return ((pos_last >> shift) + 1) << shift


def _fold_lanes(x, op):
    n = x.shape[-1]
    if n % LANES:
        return x
    out = x[:, :LANES]
    for c in range(1, n // LANES):
        out = op(out, x[:, c * LANES:(c + 1) * LANES])
    return out


def _flash_kernel(lam_ref, q_ref, k_ref, v_ref, g_ref, o_ref, m_ref, l_ref, acc_ref, *,
                  nhm, dqk, maps, scale, pos0, tq, tk, nkv, out_scale):
    i, j = pl.program_id(1), pl.program_id(2)

    @pl.when(j == 0)
    def _():
        m_ref[...] = jnp.full(m_ref.shape, NEG, F32)
        l_ref[...] = jnp.zeros(l_ref.shape, F32)
        acc_ref[...] = jnp.zeros(acc_ref.shape, F32)

    c2 = scale * math.log2(math.e)
    visible = j * tk < _visible_limit(pos0 + (i + 1) * tq - 1)
    unmasked = (j + 1) * tk <= _visible_limit(pos0 + i * tq)

    def scores(hm):
        q = q_ref[:, hm * dqk:(hm + 1) * dqk].astype(BF16)
        k = k_ref[:, hm * dqk:(hm + 1) * dqk].astype(BF16)
        return lax.dot_general(q, k, (((1,), (1,)), ((), ())), preferred_element_type=F32)

    def process(masked):
        if masked:
            q_pos = pos0 + i * tq + lax.broadcasted_iota(jnp.int32, (tq, tk), 0)
            k_pos = j * tk + lax.broadcasted_iota(jnp.int32, (tq, tk), 1)
            mask = k_pos < _visible_limit(q_pos)
        s_next = scores(0)
        for hm in range(nhm):
            s = s_next
            if hm + 1 < nhm:
                s_next = scores(hm + 1)
            if masked:
                s = jnp.where(mask, s, NEG)
            v = v_ref[:, (hm // maps) * DA_DV:(hm // maps + 1) * DA_DV].astype(BF16)
            m_prev = m_ref[hm]
            m_new = jnp.maximum(m_prev, jnp.max(_fold_lanes(s, jnp.maximum), axis=-1, keepdims=True))
            p = jnp.exp2((s - m_new) * c2)
            alpha = jnp.exp2((m_prev - m_new) * c2)
            l_ref[hm] = alpha * l_ref[hm] + jnp.sum(_fold_lanes(p, jnp.add), axis=-1, keepdims=True)
            acc_ref[hm] = alpha * acc_ref[hm] + jnp.dot(p.astype(BF16), v, preferred_element_type=F32)
            m_ref[hm] = m_new

    pl.when(visible & unmasked)(lambda: process(False))
    pl.when(visible & jnp.logical_not(unmasked))(lambda: process(True))

    @pl.when(j == nkv - 1)
    def _():
        for h in range(nhm // maps):
            if maps == 2:
                o = acc_ref[2 * h] / l_ref[2 * h] - lam_ref[0] * (acc_ref[2 * h + 1] / l_ref[2 * h + 1])
                o = o * lax.rsqrt(jnp.mean(o * o, axis=-1, keepdims=True) + EPS) * g_ref[...] * out_scale
            else:
                o = acc_ref[h] / l_ref[h]
            o_ref[:, h * DA_DV:(h + 1) * DA_DV] = o


def _flash(q, k, v, lam, g_sub, *, nhm, dqk, maps, scale, pos0, tq, tk, out_scale, v_col=0):
    b, l, _ = q.shape
    kk = k.shape[1]
    nkv = kk // tk
    nh = nhm // maps

    def kv_idx(bi, i, j):
        last = (_visible_limit(pos0 + (i + 1) * tq - 1) - 1) // tk
        return jnp.minimum(j, last)

    return pl.pallas_call(
        functools.partial(_flash_kernel, nhm=nhm, dqk=dqk, maps=maps, scale=scale, pos0=pos0, tq=tq, tk=tk,
                          nkv=nkv, out_scale=out_scale),
        grid=(b, l // tq, nkv),
        in_specs=[pl.BlockSpec(memory_space=pltpu.SMEM),
                  pl.BlockSpec((None, tq, nhm * dqk), lambda bi, i, j: (bi, i, 0)),
                  pl.BlockSpec((None, tk, nhm * dqk), lambda bi, i, j: (bi, kv_idx(bi, i, j), 0)),
                  pl.BlockSpec((None, tk, nh * DA_DV), lambda bi, i, j: (bi, kv_idx(bi, i, j), v_col)),
                  _full((1, DA_DV))],
        out_specs=pl.BlockSpec((None, tq, nh * DA_DV), lambda bi, i, j: (bi, i, 0)),
        out_shape=jax.ShapeDtypeStruct((b, l, nh * DA_DV), F32),
        scratch_shapes=[pltpu.VMEM((nhm, tq, 1), F32), pltpu.VMEM((nhm, tq, 1), F32),
                        pltpu.VMEM((nhm, tq, DA_DV), F32)],
        compiler_params=_cparams("parallel", "parallel", "arbitrary"),
        name="flash_da" if maps == 2 else "flash_mla",
    )(lam, q, k, v, g_sub)


def _top16(s, rid, payload, val_ref, pay_ref):
    big = float(s.shape[0])
    for it in range(PEER_TOPK):
        m = jnp.max(s, axis=0, keepdims=True)
        am = jnp.min(jnp.where(s == m, rid, big), axis=0, keepdims=True)
        sel = rid == am
        val_ref[it:it + 1, :] = m
        if payload is None:
            pay_ref[it:it + 1, :] = am
        else:
            pay_ref[it:it + 1, :] = jnp.max(jnp.where(sel, payload, -1.0), axis=0, keepdims=True)
        s = jnp.where(sel, -jnp.inf, s)


_PEER_PAIRS = [(a, b) for a in range(PEER_TOPK) for b in range(PEER_TOPK) if (a + 1) * (b + 1) <= PEER_TOPK]
_PEER_CAND_ROWS = -(-len(_PEER_PAIRS) // 8) * 8


def _route_kernel(q_ref, k1_ref, k2_ref, idx_ref, g_ref, v1_ref, i1_ref, v2_ref, i2_ref, vc_ref, ic_ref,
                  cand_ref, cidx_ref):
    tt = q_ref.shape[0]
    half = q_ref.shape[1] // 2
    nt = (((1,), (1,)), ((), ()))
    s1 = lax.dot_general(k1_ref[...].astype(BF16), q_ref[:, :half].astype(BF16), nt, preferred_element_type=F32)
    s2 = lax.dot_general(k2_ref[...].astype(BF16), q_ref[:, half:].astype(BF16), nt, preferred_element_type=F32)
    rid = lax.broadcasted_iota(jnp.int32, (PEER_NKEYS, tt), 0).astype(F32)
    _top16(s1, rid, None, v1_ref, i1_ref)
    _top16(s2, rid, None, v2_ref, i2_ref)
    npairs = len(_PEER_PAIRS)
    cand_ref[npairs:, :] = jnp.full((_PEER_CAND_ROWS - npairs, tt), -jnp.inf, F32)
    cidx_ref[npairs:, :] = jnp.full((_PEER_CAND_ROWS - npairs, tt), -1.0, F32)
    for r, (a, b) in enumerate(_PEER_PAIRS):
        cand_ref[r:r + 1, :] = v1_ref[a:a + 1, :] + v2_ref[b:b + 1, :]
        cidx_ref[r:r + 1, :] = i1_ref[a:a + 1, :] * float(PEER_NKEYS) + i2_ref[b:b + 1, :]
    rid2 = lax.broadcasted_iota(jnp.int32, (_PEER_CAND_ROWS, tt), 0).astype(F32)
    _top16(cand_ref[...], rid2, cidx_ref[...], vc_ref, ic_ref)
    sc = vc_ref[...]
    e = jnp.exp(sc - sc[0:1, :])
    g_ref[...] = e / jnp.sum(e, axis=0, keepdims=True)
    idx_ref[...] = ic_ref[...].astype(jnp.int32)


def _peer_route(q, k1, k2):
    t = q.shape[0]
    tt = PEER_TT
    dk = 2 * PEER_NKEYS
    kspec = pl.BlockSpec((None, PEER_NKEYS, dk // 2), lambda i, h: (h, 0, 0))
    ospec = pl.BlockSpec((None, PEER_TOPK, tt), lambda i, h: (i, h, 0))
    sm = pltpu.VMEM((PEER_TOPK, tt), F32)
    return pl.pallas_call(
        _route_kernel,
        grid=(t // tt, PEER_HEADS),
        in_specs=[pl.BlockSpec((tt, dk), lambda i, h: (i, h)), kspec, kspec],
        out_specs=[ospec, ospec],
        out_shape=[jax.ShapeDtypeStruct((t // tt, PEER_HK, tt), jnp.int32),
                   jax.ShapeDtypeStruct((t // tt, PEER_HK, tt), F32)],
        scratch_shapes=[sm, sm, sm, sm, sm, sm,
                        pltpu.VMEM((_PEER_CAND_ROWS, tt), F32), pltpu.VMEM((_PEER_CAND_ROWS, tt), F32)],
        compiler_params=_cparams("parallel", "parallel"),
        name="peer_route",
    )(q, k1, k2)


def _pack_table(tab):
    e, d = tab.shape
    tb = lax.bitcast_convert_type(tab.astype(BF16), jnp.uint16).astype(jnp.uint32)
    words = tb[:, :d // 2] | (tb[:, d // 2:] << 16)
    return lax.bitcast_convert_type(words, F32)


def _sc_unpack(words):
    bits = lax.bitcast_convert_type(words, jnp.int32)
    return lax.bitcast_convert_type(bits << 16, F32), lax.bitcast_convert_type(bits & jnp.int32(-65536), F32)


def _sc_gelu(x):
    u = math.sqrt(2.0 / math.pi) * (x + 0.044715 * (x * x * x))
    tanh_u = 1.0 - 2.0 / (jnp.exp(2.0 * u) + 1.0)
    return 0.5 * x * (1.0 + tanh_u)


def _sc_peer(u_tab, v_tab, idx_tm, g_tm, h, side_idx=None):
    t, nk = idx_tm.shape
    w = u_tab.shape[1]
    d = 2 * w
    half_rows = SC_GATHER_ROWS
    nbuf = SC_BUFFERS
    per_tok = nk // half_rows
    mesh = plsc.VectorSubcoreMesh(core_axis_name="core", subcore_axis_name="subcore")
    n_workers = mesh.num_cores * mesh.num_subcores
    tpw = t // n_workers
    tb = SC_TOKEN_BLOCK
    steps = per_tok * tb
    dot_rows = SC_LANES
    sum_rows = 8
    assert t % n_workers == 0 and tpw % tb == 0 and w % SC_LANES == 0
    assert nk % half_rows == 0 and half_rows % SC_LANES == 0 and steps % nbuf == 0

    n_side = 0 if side_idx is None else side_idx.shape[0]
    side_w = SC_SIDE_ROWS
    lw = n_side // n_workers
    n_win = lw // side_w
    blk_side = steps * side_w
    if n_side:
        assert nbuf == 2 and steps % 2 == 0 and n_side % (n_workers * blk_side) == 0
        assert n_win + 2 <= (tpw // tb) * steps

    def lanes(start):
        return pl.ds(pl.multiple_of(start, SC_LANES), SC_LANES)

    y_type = jax.ShapeDtypeStruct((t, d), F32)
    side_type = jax.ShapeDtypeStruct((n_side, w), F32)
    scratch = [pltpu.VMEM((tb * nk,), jnp.int32), pltpu.VMEM((tb, d), F32), pltpu.VMEM((tb, nk), F32),
               pltpu.VMEM((nbuf, half_rows, w), F32), pltpu.VMEM((tb, nk), F32), pltpu.VMEM((tb, d), F32),
               pltpu.SemaphoreType.DMA((nbuf,))]
    if n_side:
        scratch += [pltpu.VMEM((2 * blk_side,), jnp.int32), pltpu.VMEM((2, 2, side_w, w), F32),
                    pltpu.SemaphoreType.DMA((8,))]

    @functools.partial(
        pl.kernel, out_type=[y_type, side_type, side_type] if n_side else y_type, mesh=mesh, scratch_types=scratch,
        compiler_params=pltpu.CompilerParams(needs_layout_passes=False))
    def peer_kernel(u_hbm, v_hbm, idx_hbm, g_hbm, h_hbm, *refs):
        if n_side:
            (sidx_hbm, y_hbm, gu_hbm, gv_hbm, idx_v, h_v, g_v, rows_v, coef_v, y_v, sems,
             sidx_v, sbuf, ssems) = refs
        else:
            y_hbm, idx_v, h_v, g_v, rows_v, coef_v, y_v, sems = refs
        wid = lax.axis_index("subcore") * mesh.num_cores + lax.axis_index("core")
        lane = lax.iota(jnp.int32, SC_LANES)

        def gather(tab_hbm, step, buf):
            rows = idx_v.at[pl.ds(pl.multiple_of(step * half_rows, half_rows), half_rows)]
            return pltpu.make_async_copy(tab_hbm.at[rows], rows_v.at[buf], sems.at[buf])

        def prime(tab_hbm):
            for j in range(nbuf - 1):
                gather(tab_hbm, j, j).start()

        def side_in(tab_hbm, tab, n, par):
            slot = ((n // steps) % 2) * blk_side + (n % steps) * side_w
            rows = sidx_v.at[pl.ds(pl.multiple_of(slot, side_w), side_w)]
            return pltpu.make_async_copy(tab_hbm.at[rows], sbuf.at[tab, par], ssems.at[tab * 4 + par])

        def side_out(out_hbm, tab, n, par):
            dst = out_hbm.at[pl.ds(pl.multiple_of(wid * lw + n * side_w, side_w), side_w)]
            return pltpu.make_async_copy(sbuf.at[tab, par], dst, ssems.at[tab * 4 + 2 + par])

        def side_step(tab_hbm, out_hbm, tab, n, par):
            @pl.when((n >= 2) & (n < n_win + 2))
            def _():
                side_out(out_hbm, tab, n - 2, par).wait()

            @pl.when(n < n_win)
            def _():
                side_in(tab_hbm, tab, n, par).start()

            @pl.when((n >= 1) & (n < n_win + 1))
            def _():
                side_in(tab_hbm, tab, n - 1, 1 - par).wait()
                side_out(out_hbm, tab, n - 1, 1 - par).start()

        def run(tab_hbm, work, blk=None, side=None):
            @pl.loop(0, steps, step=nbuf)
            def _(s):
                for j in range(nbuf):
                    if side is not None:
                        side_step(tab_hbm, side[0], side[1], blk * steps + s + j, j)
                    ahead = s + j + nbuf - 1

                    @pl.when(ahead < steps)
                    def _():
                        gather(tab_hbm, ahead, (j + nbuf - 1) % nbuf).start()

                    gather(tab_hbm, s + j, j).wait()
                    work(s + j, j)

        def dots(step, buf):
            tok = step // per_tok
            col0 = (step % per_tok) * half_rows

            def group(q, _):
                r0 = q * dot_rows

                def chunk(c, accs):
                    h_lo = h_v[tok, lanes(c * SC_LANES)]
                    h_hi = h_v[tok, lanes(w + c * SC_LANES)]
                    out = []
                    for r in range(dot_rows):
                        lo, hi = _sc_unpack(rows_v[buf, r0 + r, lanes(c * SC_LANES)])
                        out.append(accs[r] + lo * h_lo + hi * h_hi)
                    return tuple(out)

                accs = lax.fori_loop(0, w // SC_LANES, chunk, (jnp.zeros((SC_LANES,), F32),) * dot_rows)
                vec = jnp.zeros((SC_LANES,), F32)
                for r in range(dot_rows):
                    vec = jnp.where(lane == r, jnp.sum(accs[r]), vec)
                coef_v[tok, lanes(col0 + r0)] = vec
                return 0

            lax.fori_loop(0, half_rows // dot_rows, group, 0)

        def gates():
            @pl.loop(0, tb)
            def _(tok):
                @pl.loop(0, nk // SC_LANES)
                def _(q):
                    ds = lanes(q * SC_LANES)
                    coef_v[tok, ds] = g_v[tok, ds] * _sc_gelu(coef_v[tok, ds])

        def weighted_sum(step, buf):
            tok = step // per_tok
            col0 = (step % per_tok) * half_rows

            @pl.when(col0 == 0)
            def _():
                @plsc.parallel_loop(0, d // SC_LANES)
                def _(c):
                    y_v[tok, lanes(c * SC_LANES)] = jnp.zeros((SC_LANES,), F32)

            def group(g, _):
                cvec = coef_v[tok, lanes(col0 + (g // 2) * SC_LANES)]
                base = (g % 2) * sum_rows
                cs = [cvec.at[jnp.full((SC_LANES,), base + r, jnp.int32)].get(mode="promise_in_bounds")
                      for r in range(sum_rows)]
                r0 = g * sum_rows

                @plsc.parallel_loop(0, w // SC_LANES, unroll=2)
                def _(c):
                    los, his = [], []
                    for r in range(sum_rows):
                        lo, hi = _sc_unpack(rows_v[buf, r0 + r, lanes(c * SC_LANES)])
                        los.append(cs[r] * lo)
                        his.append(cs[r] * hi)
                    while len(los) > 1:
                        los = [a + b for a, b in zip(los[::2], los[1::2])]
                        his = [a + b for a, b in zip(his[::2], his[1::2])]
                    y_v[tok, lanes(c * SC_LANES)] = y_v[tok, lanes(c * SC_LANES)] + los[0]
                    y_v[tok, lanes(w + c * SC_LANES)] = y_v[tok, lanes(w + c * SC_LANES)] + his[0]

                return 0

            lax.fori_loop(0, half_rows // sum_rows, group, 0)

        @pl.loop(0, tpw // tb)
        def _(blk):
            tok0 = wid * tpw + blk * tb
            pltpu.sync_copy(idx_hbm.at[pl.ds(pl.multiple_of(tok0 * nk, nk), tb * nk)], idx_v)
            prime(u_hbm)
            pltpu.sync_copy(h_hbm.at[pl.ds(tok0, tb)], h_v)
            pltpu.sync_copy(g_hbm.at[pl.ds(tok0, tb)], g_v)
            if n_side:
                @pl.when(blk * steps < n_win)
                def _():
                    src = sidx_hbm.at[pl.ds(pl.multiple_of(wid * lw + blk * blk_side, blk_side), blk_side)]
                    pltpu.sync_copy(src, sidx_v.at[pl.ds(pl.multiple_of((blk % 2) * blk_side, blk_side), blk_side)])

            run(u_hbm, dots, blk, (gu_hbm, 0) if n_side else None)
            prime(v_hbm)
            gates()
            run(v_hbm, weighted_sum, blk, (gv_hbm, 1) if n_side else None)
            pltpu.sync_copy(y_v, y_hbm.at[pl.ds(tok0, tb)])

    if n_side:
        return peer_kernel(u_tab, v_tab, idx_tm.reshape(t * nk), g_tm, h, side_idx)
    return peer_kernel(u_tab, v_tab, idx_tm.reshape(t * nk), g_tm, h)


def _unpack(words):
    w = pltpu.bitcast(words, jnp.uint32)
    lo = pltpu.bitcast(w << 16, F32)
    hi = pltpu.bitcast(w & jnp.uint32(0xFFFF0000), F32)
    return lo, hi


def _peer_act_kernel(h_ref, gu_ref, g_ref, coef_ref, act_ref):
    c = pl.program_id(1)
    half = h_ref.shape[1] // 2
    h_lo, h_hi = h_ref[:, :half], h_ref[:, half:]
    lane = lax.broadcasted_iota(jnp.int32, act_ref.shape, 1)

    @pl.when(c == 0)
    def _():
        act_ref[...] = jnp.zeros(act_ref.shape, F32)

    act = act_ref[...]
    for k in range(PEER_HKC):
        lo, hi = _unpack(gu_ref[k])
        a = jnp.sum(lo * h_lo + hi * h_hi, axis=-1, keepdims=True)
        act = jnp.where(lane == c * PEER_HKC + k, a, act)
    act_ref[...] = act

    @pl.when(c == pl.num_programs(1) - 1)
    def _():
        coef_ref[...] = g_ref[...].T * jax.nn.gelu(act)


def _peer_act(h, gu, g):
    t, d = h.shape
    tt = PEER_TT
    nt = t // tt
    return pl.pallas_call(
        _peer_act_kernel,
        grid=(nt, PEER_HK // PEER_HKC),
        in_specs=[pl.BlockSpec((tt, d), lambda i, c: (i, 0)),
                  pl.BlockSpec((None, PEER_HKC, tt, d // 2), lambda i, c: (i, c, 0, 0)),
                  pl.BlockSpec((None, PEER_HK, tt), lambda i, c: (i, 0, 0))],
        out_specs=pl.BlockSpec((None, tt, PEER_HK), lambda i, c: (i, 0, 0)),
        out_shape=jax.ShapeDtypeStruct((nt, tt, PEER_HK), F32),
        scratch_shapes=[pltpu.VMEM((tt, PEER_HK), F32)],
        compiler_params=_cparams("parallel", "arbitrary"),
        name="peer_act",
    )(h, gu, g)


def _peer_mix_kernel(coef_ref, gv_ref, y_ref):
    c = pl.program_id(1)
    half = y_ref.shape[1] // 2

    @pl.when(c == 0)
    def _():
        y_ref[...] = jnp.zeros(y_ref.shape, F32)

    coef = coef_ref[...]
    lane = lax.broadcasted_iota(jnp.int32, coef.shape, 1)
    acc_lo, acc_hi = y_ref[:, :half], y_ref[:, half:]
    for k in range(PEER_HKC):
        ck = jnp.sum(jnp.where(lane == c * PEER_HKC + k, coef, 0.0), axis=-1, keepdims=True)
        lo, hi = _unpack(gv_ref[k])
        acc_lo = acc_lo + ck * lo
        acc_hi = acc_hi + ck * hi
    y_ref[:, :half] = acc_lo
    y_ref[:, half:] = acc_hi


def _peer_mix(coef, gv):
    nt, tt, _ = coef.shape
    w = gv.shape[-1]
    return pl.pallas_call(
        _peer_mix_kernel,
        grid=(nt, PEER_HK // PEER_HKC),
        in_specs=[pl.BlockSpec((None, tt, PEER_HK), lambda i, c: (i, 0, 0)),
                  pl.BlockSpec((None, PEER_HKC, tt, w), lambda i, c: (i, c, 0, 0))],
        out_specs=pl.BlockSpec((tt, 2 * w), lambda i, c: (i, 0)),
        out_shape=jax.ShapeDtypeStruct((nt * tt, 2 * w), F32),
        compiler_params=_cparams("parallel", "arbitrary"),
        name="peer_mix",
    )(coef, gv)


def _peer_out_kernel(x_ref, gt_ref, y_ref, gf_ref, o_ref, *, final):
    x = x_ref[...] + gt_ref[...] * y_ref[...]
    if final:
        x = x * lax.rsqrt(jnp.mean(x * x, axis=-1, keepdims=True) + EPS) * gf_ref[...]
    o_ref[...] = x


def _peer_out(x, gt, y, g_final, final):
    t, d = x.shape
    tt = PEER_TT
    nt = t // tt
    row = pl.BlockSpec((tt, d), lambda i: (i, 0))
    if gt.ndim == 3:
        per_seq = nt // gt.shape[0]
        gt_spec = pl.BlockSpec((None, 1, d), lambda i: (i // per_seq, 0, 0))
    else:
        gt_spec = row
    return pl.pallas_call(
        functools.partial(_peer_out_kernel, final=final),
        grid=(nt,),
        in_specs=[row, gt_spec, row, _full((1, d))],
        out_specs=row, out_shape=jax.ShapeDtypeStruct((t, d), F32),
        compiler_params=_cparams("parallel"),
        name="peer_out",
    )(x, gt, y, g_final.reshape(1, d))


def _peer(x, g_ffn, sc2, sh2, gt2, w_q, k1, k2, u_pack, v_pack, g_final, final, tl):
    b, l, d = x.shape
    t = b * l
    q, h = _norm_mod_matmul(x, g_ffn, sc2, sh2, w_q, tl, want_h=True)
    idx, g = _peer_route(q.reshape(t, -1), k1, k2)
    h = h.reshape(t, d)
    nt = t // PEER_TT
    n_tc = nt * PEER_TC_TILES // 32
    n_sc = nt - n_tc
    t_sc = n_sc * PEER_TT
    token_major = lambda a: jnp.transpose(a, (0, 2, 1)).reshape(-1, PEER_HK)
    if n_tc:
        y_sc, gu, gv = _sc_peer(u_pack, v_pack, token_major(idx[:n_sc]), token_major(g[:n_sc]), h[:t_sc],
                                idx[n_sc:].reshape(-1))
        rows = lambda a: a.reshape(n_tc, PEER_HK, PEER_TT, d // 2)
        y_tc = _peer_mix(_peer_act(h[t_sc:], rows(gu), g[n_sc:]), rows(gv))
        y = jnp.concatenate([y_sc, y_tc], axis=0)
    else:
        y = _sc_peer(u_pack, v_pack, token_major(idx), token_major(g), h)
    if l % PEER_TT:
        gt2 = jnp.broadcast_to(gt2, (b, l, d)).reshape(t, d)
    return _peer_out(x.reshape(t, d), gt2, y, g_final, final).reshape(b, l, d)


def _trunk(x, mod, pos0, past, p, tl):
    b, l, d = x.shape
    outs = {}
    for layer in range(DEPTH):
        sh1, sc1, gt1, sh2, sc2, gt2 = [m[:, None, :] for m in jnp.split(mod[layer], 6, axis=-1)]
        if layer % 2 == 0:
            e = layer // 2
            y_in = _norm_mod_matmul(x, p['g_mix'][layer], sc1, sh1, p['w_in_e'][e].astype(BF16), tl)
            if past is None:
                h0_re = jnp.zeros((b, S5_GROUPS, S5_STATE), F32)
                h0_im = jnp.zeros((b, S5_GROUPS, S5_STATE), F32)
            else:
                h0_re, h0_im = past['s5_re'][e], past['s5_im'][e]
            ya, hr, hi = _s5(y_in, h0_re, h0_im, p['s5'][e], p['s5_d'][e], p['w_glu'][e], p['b_glu'][e], tl)
            yb, vn = _gmlp(y_in, p['gm_g_v'][e], p['gm_w_s'][e], p['gm_b_s'][e], tl, want_vn=past is not None)
            x = _proj_residual(x, gt1, ya, yb, p['w_out_e'][e], tl)
            outs.update(s5_re=hr, s5_im=hi, gm_v=vn)
        else:
            o = layer // 2
            w_in = jnp.pad(p['w_in_o'][o], ((0, 0), (0, 2048 - p['w_in_o'][o].shape[1]))).astype(BF16)
            y_in = _norm_mod_matmul(x, p['g_mix'][layer], sc1, sh1, w_in, tl)
            qd, kd, qm, lat, kpe = _odd_prep(y_in, pos0, p['mla_g_q'][o], p['mla_w_uq'][o], p['mla_g_kv'][o], tl)
            v_new = y_in[:, :, 2 * 512:3 * 512]
            lam_init = 0.8 - 0.6 * math.exp(-0.3 * layer)
            lam = (jnp.exp(jnp.sum(p['da_lq1'][o] * p['da_lk1'][o])) - jnp.exp(jnp.sum(p['da_lq2'][o] * p['da_lk2'][o]))
                   + lam_init).reshape(1).astype(F32)
            g_sub = p['da_g_sub'][o].reshape(1, DA_DV)
            if past is None:
                k_all, v_all, v_col, lat_all, kpe_all = kd, y_in, 2, lat, kpe
                tq = tk = tl
            else:
                k_all = jnp.concatenate([past['da_k'][o].reshape(b, -1, 512), kd], axis=1)
                v_all = jnp.concatenate([past['da_v'][o].reshape(b, -1, 512), v_new], axis=1)
                lat_all = jnp.concatenate([past['mla_lat'][o], lat], axis=1)
                kpe_all = jnp.concatenate([past['mla_kpe'][o], kpe], axis=1)
                v_col, tq, tk = 0, l, k_all.shape[1]
            yc = _flash(qd, k_all, v_all, lam, g_sub, nhm=2 * DA_HEADS, dqk=DA_DH, maps=2, scale=DA_DH ** -0.5,
                        pos0=pos0, tq=tq, tk=tk, out_scale=1.0 - lam_init, v_col=v_col)
            km, vm = _mla_kv(lat_all, kpe_all, p['mla_w_ukv'][o], tk)
            yd = _flash(qm, km, vm, lam, g_sub, nhm=MLA_HEADS, dqk=LANES, maps=1,
                        scale=(MLA_NOPE + MLA_ROPE) ** -0.5, pos0=pos0, tq=tq, tk=tk, out_scale=1.0)
            x = _proj_residual(x, gt1, yc, yd, p['w_out_o'][o], tl)
            outs.update(da_k=kd.reshape(b, l, DA_HEADS, 2 * DA_DH), da_v=v_new.reshape(b, l, DA_HEADS, DA_DV),
                        mla_lat=lat, mla_kpe=kpe)
        x = _peer(x, p['g_ffn'][layer], sc2, sh2, gt2, p['peer_w_q'][layer].astype(BF16), p['peer_k1'][layer],
                  p['peer_k2'][layer], p['u_pack'][layer], p['v_pack'][layer], p['g_final'],
                  final=layer == DEPTH - 1, tl=tl)
    return x, outs


def kernel(x_prompt, x_sample, c_prompt, c_sample, state_s5_re, state_s5_im, cache_da_k, cache_da_v, cache_mla_latent, cache_mla_kpe, w_ada, b_ada, g_mix, g_ffn, g_final, w_in_e, w_out_e, s5_lam_re, s5_lam_im, s5_log_dt, s5_b_re, s5_b_im, s5_c_re, s5_c_im, s5_d, w_glu, b_glu, gm_g_v, gm_w_s, gm_b_s, w_in_o, w_out_o, da_lq1, da_lk1, da_lq2, da_lk2, da_g_sub, mla_g_q, mla_w_uq, mla_g_kv, mla_w_ukv, peer_w_q, peer_k1, peer_k2, peer_u, peer_v):
    p = dict(g_mix=g_mix, g_ffn=g_ffn, g_final=g_final, w_in_e=w_in_e, w_out_e=w_out_e, s5_d=s5_d, w_glu=w_glu,
             b_glu=b_glu, gm_g_v=gm_g_v, gm_w_s=gm_w_s, gm_b_s=gm_b_s, w_in_o=w_in_o, w_out_o=w_out_o,
             da_lq1=da_lq1, da_lk1=da_lk1, da_lq2=da_lq2, da_lk2=da_lk2, da_g_sub=da_g_sub, mla_g_q=mla_g_q,
             mla_w_uq=mla_w_uq, mla_g_kv=mla_g_kv, mla_w_ukv=mla_w_ukv, peer_w_q=peer_w_q, peer_k1=peer_k1,
             peer_k2=peer_k2)
    n_even = (DEPTH + 1) // 2
    p['s5'] = [_s5_params(s5_lam_re[e], s5_lam_im[e], s5_log_dt[e], s5_b_re[e], s5_b_im[e], s5_c_re[e], s5_c_im[e])
               for e in range(n_even)]
    p['u_pack'] = [_pack_table(peer_u[layer]) for layer in range(DEPTH)]
    p['v_pack'] = [_pack_table(peer_v[layer]) for layer in range(DEPTH)]
    past = dict(s5_re=state_s5_re, s5_im=state_s5_im, da_k=cache_da_k, da_v=cache_da_v,
                mla_lat=cache_mla_latent, mla_kpe=cache_mla_kpe)
    nb = x_prompt.shape[0]
    mod = _ada(jnp.concatenate([c_prompt, c_sample], axis=0), w_ada, b_ada)
    past_len = cache_da_k.shape[2]
    parts = [_trunk(x_prompt[s:s + 1], mod[:, s:s + 1], 0, None, p, tl=512) for s in range(nb)]
    y_p = jnp.concatenate([y for y, _ in parts], axis=0)
    o_p = {k: jnp.concatenate([o[k] for _, o in parts], axis=0) for k in parts[0][1] if parts[0][1][k] is not None}
    y_s, o_s = _trunk(x_sample, mod[:, nb:], past_len, past, p, tl=x_sample.shape[1])
    st = lambda a: a[None]
    return (y_p, y_s, st(o_p['s5_re']), st(o_p['s5_im']), st(o_s['s5_re']), st(o_s['s5_im']), st(o_s['gm_v']),
            st(o_p['da_k']), st(o_p['da_v']), st(o_s['da_k']), st(o_s['da_v']),
            st(o_p['mla_lat']), st(o_p['mla_kpe']), st(o_s['mla_lat']), st(o_s['mla_kpe']))
```

```python
import functools
import math

import jax
import jax.numpy as jnp
from jax import lax
from jax.experimental import pallas as pl
from jax.experimental.pallas import tpu as pltpu
from jax.experimental.pallas import tpu_sc as plsc

F32 = jnp.float32
BF16 = jnp.bfloat16

D_MODEL = 1024
DEPTH = 2
CHUNK = 64
ROPE_THETA = 500000.0
EPS = 1e-6
NEG = -1e30

S5_WIDTH = 512
S5_GROUP = 16
S5_GROUPS = 32
S5_STATE = 64
GM_WIDTH = 512
GM_HEADS = 4
GM_CHUNK = 128
DA_HEADS = 4
DA_DH = 64
DA_DV = 128
DA_ROPE = 16
MLA_HEADS = 4
MLA_Q_RANK = 256
MLA_KV_RANK = 128
MLA_NOPE = 64
MLA_ROPE = 32
MLA_DV = 128
PEER_HEADS = 8
PEER_NKEYS = 128
PEER_TOPK = 16
PEER_HK = PEER_HEADS * PEER_TOPK

LANES = 128
VMEM_LIMIT = 48 << 20
PROMPT_TILE = 512
ODD_IN_PAD = 2048
PEER_TT = 256
SC_LANES = 16
SC_TOKEN_BLOCK = 8
SC_GATHER_ROWS = 64
SC_BUFFERS = 2
SC_SIDE_ROWS = 16
PEER_TC_TILES = 6
PEER_HKC = 16

_HI = lax.Precision.HIGHEST


def _cparams(*sem):
    return pltpu.CompilerParams(dimension_semantics=sem, vmem_limit_bytes=VMEM_LIMIT)


def _full(shape):
    n = len(shape)
    return pl.BlockSpec(shape, lambda *_: (0,) * n)


def _ada_kernel(c_ref, w_ref, b_ref, o_ref):
    c = c_ref[...]
    s = c * jax.nn.sigmoid(c)
    o_ref[...] = jnp.dot(s, w_ref[...], precision=_HI, preferred_element_type=F32) + b_ref[...]


def _ada(c_all, w_ada, b_ada):
    r = c_all.shape[0]
    tn = 1536
    return pl.pallas_call(
        _ada_kernel,
        grid=(DEPTH, 6 * D_MODEL // tn),
        in_specs=[_full((r, D_MODEL)),
                  pl.BlockSpec((None, D_MODEL, tn), lambda l, j: (l, 0, j)),
                  pl.BlockSpec((None, 1, tn), lambda l, j: (l, 0, j))],
        out_specs=pl.BlockSpec((None, r, tn), lambda l, j: (l, 0, j)),
        out_shape=jax.ShapeDtypeStruct((DEPTH, r, 6 * D_MODEL), F32),
        compiler_params=_cparams("parallel", "parallel"),
        name="ada",
    )(c_all, w_ada, b_ada.reshape(DEPTH, 1, 6 * D_MODEL))


def _nmm_kernel(x_ref, g_ref, sc_ref, sh_ref, w_ref, o_ref, *h_ref):
    x = x_ref[...]
    ms = jnp.mean(x * x, axis=-1, keepdims=True)
    h = x * lax.rsqrt(ms + EPS) * g_ref[...] * (1.0 + sc_ref[...]) + sh_ref[...]
    o_ref[...] = jnp.dot(h.astype(BF16), w_ref[...], preferred_element_type=F32)
    if h_ref:
        h_ref[0][...] = h


def _norm_mod_matmul(x, g, sc, sh, w, tl, want_h=False):
    b, l, d = x.shape
    n = w.shape[1]
    row = pl.BlockSpec((None, 1, d), lambda bi, i: (bi, 0, 0))
    out_shape = [jax.ShapeDtypeStruct((b, l, n), F32)]
    out_specs = [pl.BlockSpec((None, tl, n), lambda bi, i: (bi, i, 0))]
    if want_h:
        out_shape.append(jax.ShapeDtypeStruct((b, l, d), F32))
        out_specs.append(pl.BlockSpec((None, tl, d), lambda bi, i: (bi, i, 0)))
    res = pl.pallas_call(
        _nmm_kernel,
        grid=(b, l // tl),
        in_specs=[pl.BlockSpec((None, tl, d), lambda bi, i: (bi, i, 0)), _full((1, d)), row, row, _full((d, n))],
        out_specs=out_specs, out_shape=out_shape,
        compiler_params=_cparams("parallel", "parallel"),
        name="norm_mod_matmul",
    )(x, g.reshape(1, d), sc, sh, w)
    return res if want_h else res[0]


def _s5_kernel(xa_ref, h0r_ref, h0i_ref, bblk_ref, ar_ref, ai_ref, cr_ref, ci_ref, d_ref, wglu_ref, bglu_ref,
               ya_ref, hr_out_ref, hi_out_ref, bur, bui, st_r, st_i, *, tl):
    nj = S5_WIDTH // LANES
    half = 8 * S5_STATE

    @pl.when(pl.program_id(1) == 0)
    def _():
        st_r[...] = h0r_ref[...]
        st_i[...] = h0i_ref[...]

    xa = xa_ref[...]
    for j in range(nj):
        bu = jnp.dot(xa[:, j * LANES:(j + 1) * LANES], bblk_ref[j], precision=_HI, preferred_element_type=F32)
        bur[j] = bu[:, :half]
        bui[j] = bu[:, half:]

    for j0 in range(0, nj, 2):
        js = (j0, j0 + 1)
        a_r = [ar_ref[j] for j in js]
        a_i = [ai_ref[j] for j in js]

        def body(k, carry, js=js, a_r=a_r, a_i=a_i):
            carry = list(carry)
            base = pl.multiple_of(k * 8, 8)
            for s in range(8):
                t = base + s
                for q, j in enumerate(js):
                    hr, hi = carry[2 * q], carry[2 * q + 1]
                    nhr = a_r[q] * hr - a_i[q] * hi + bur[j, pl.ds(t, 1), :]
                    nhi = a_r[q] * hi + a_i[q] * hr + bui[j, pl.ds(t, 1), :]
                    bur[j, pl.ds(t, 1), :] = nhr
                    bui[j, pl.ds(t, 1), :] = nhi
                    carry[2 * q], carry[2 * q + 1] = nhr, nhi
            return tuple(carry)

        init = (st_r[js[0]], st_i[js[0]], st_r[js[1]], st_i[js[1]])
        fin = lax.fori_loop(0, tl // 8, body, init)
        st_r[js[0]], st_i[js[0]], st_r[js[1]], st_i[js[1]] = fin

    hr_out_ref[...] = st_r[...]
    hi_out_ref[...] = st_i[...]

    ys = []
    for j in range(nj):
        y = (jnp.dot(bur[j].astype(BF16), cr_ref[j], preferred_element_type=F32)
             + jnp.dot(bui[j].astype(BF16), ci_ref[j], preferred_element_type=F32))
        ys.append(y + d_ref[:, j * LANES:(j + 1) * LANES] * xa[:, j * LANES:(j + 1) * LANES])
    y = jax.nn.gelu(jnp.concatenate(ys, axis=1))
    z = jnp.dot(y.astype(BF16), wglu_ref[...], preferred_element_type=F32) + bglu_ref[...]
    ya_ref[...] = z[:, :S5_WIDTH] * jax.nn.sigmoid(z[:, S5_WIDTH:])


def _s5_params(lam_re, lam_im, log_dt, b_re, b_im, c_re, c_im):
    lam = lax.complex(lam_re.astype(F32), lam_im.astype(F32))
    dt = jnp.exp(log_dt.astype(F32))[:, None]
    a_bar = jnp.exp(lam * dt)
    b_bar = ((a_bar - 1.0) / lam)[..., None] * lax.complex(b_re.astype(F32), b_im.astype(F32))
    nj = S5_GROUPS // 8
    eye = jnp.eye(8, dtype=F32)

    def blk_b(m):
        m = m.reshape(nj, 8, S5_STATE, S5_GROUP)
        return jnp.einsum('jgpc,gh->jgchp', m, eye).reshape(nj, 8 * S5_GROUP, 8 * S5_STATE)

    def blk_c(m):
        m = m.reshape(nj, 8, S5_GROUP, S5_STATE)
        return jnp.einsum('jgcp,gh->jgphc', m, eye).reshape(nj, 8 * S5_STATE, 8 * S5_GROUP)

    bblk = jnp.concatenate([blk_b(jnp.real(b_bar)), blk_b(jnp.imag(b_bar))], axis=-1)
    cr = blk_c(c_re.astype(F32)).astype(BF16)
    ci = (-blk_c(c_im.astype(F32))).astype(BF16)
    a_r = jnp.real(a_bar).reshape(nj, 1, 8 * S5_STATE)
    a_i = jnp.imag(a_bar).reshape(nj, 1, 8 * S5_STATE)
    return bblk, a_r, a_i, cr, ci


def _s5(y_in, h0_re, h0_im, sp, d_skip, w_glu, b_glu, tl):
    b, l, _ = y_in.shape
    nj = S5_GROUPS // 8
    half = 8 * S5_STATE
    bblk, a_r, a_i, cr, ci = sp
    st = pl.BlockSpec((None, nj, 1, half), lambda bi, i: (bi, 0, 0, 0))
    ya, hr, hi = pl.pallas_call(
        functools.partial(_s5_kernel, tl=tl),
        grid=(b, l // tl),
        in_specs=[pl.BlockSpec((None, tl, S5_WIDTH), lambda bi, i: (bi, i, 0)), st, st,
                  _full(bblk.shape), _full(a_r.shape), _full(a_i.shape), _full(cr.shape), _full(ci.shape),
                  _full((1, S5_WIDTH)), _full((S5_WIDTH, 2 * S5_WIDTH)), _full((1, 2 * S5_WIDTH))],
        out_specs=[pl.BlockSpec((None, tl, S5_WIDTH), lambda bi, i: (bi, i, 0)), st, st],
        out_shape=[jax.ShapeDtypeStruct((b, l, S5_WIDTH), F32),
                   jax.ShapeDtypeStruct((b, nj, 1, half), F32), jax.ShapeDtypeStruct((b, nj, 1, half), F32)],
        scratch_shapes=[pltpu.VMEM((nj, tl, half), F32), pltpu.VMEM((nj, tl, half), F32),
                        pltpu.VMEM((nj, 1, half), F32), pltpu.VMEM((nj, 1, half), F32)],
        compiler_params=_cparams("parallel", "arbitrary"),
        name="s5",
    )(y_in, h0_re.reshape(b, nj, 1, half), h0_im.reshape(b, nj, 1, half), bblk, a_r, a_i, cr, ci,
      d_skip.reshape(1, S5_WIDTH), w_glu.astype(BF16), b_glu.reshape(1, 2 * S5_WIDTH))
    return ya, hr.reshape(b, S5_GROUPS, S5_STATE), hi.reshape(b, S5_GROUPS, S5_STATE)


def _gmlp_kernel(u_ref, v_ref, gv_ref, ws_ref, bias_ref, yb_ref, *vn_ref, tl, lc):
    hd = GM_WIDTH // GM_HEADS
    gv = jax.nn.gelu(v_ref[...])
    vn = gv * lax.rsqrt(jnp.mean(gv * gv, axis=-1, keepdims=True) + EPS) * gv_ref[...]
    if vn_ref:
        vn_ref[0][...] = vn
    gu = jax.nn.gelu(u_ref[...])
    vb = vn.astype(BF16)
    for c in range(tl // lc):
        rows = slice(c * lc, (c + 1) * lc)
        for h in range(GM_HEADS):
            cols = slice(h * hd, (h + 1) * hd)
            mixed = jnp.dot(ws_ref[h], vb[rows, cols], preferred_element_type=F32) + bias_ref[:, cols]
            yb_ref[rows, cols] = gu[rows, cols] * mixed


def _gmlp(y_in, g_v, w_s, b_s, tl, want_vn):
    b, l, _ = y_in.shape
    lc = min(GM_CHUNK, l)
    hd = GM_WIDTH // GM_HEADS
    tri = jnp.tril(jnp.ones((GM_CHUNK, GM_CHUNK), dtype=bool))
    ws = jnp.where(tri[None], w_s, 0.0)[:, :lc, :lc].astype(BF16)
    bias = jnp.repeat(jnp.transpose(b_s)[:lc], hd, axis=1)
    out_shape = [jax.ShapeDtypeStruct((b, l, GM_WIDTH), F32)]
    out_specs = [pl.BlockSpec((None, tl, GM_WIDTH), lambda bi, i: (bi, i, 0))]
    if want_vn:
        out_shape.append(jax.ShapeDtypeStruct((b, l, GM_WIDTH), F32))
        out_specs.append(pl.BlockSpec((None, tl, GM_WIDTH), lambda bi, i: (bi, i, 0)))
    res = pl.pallas_call(
        functools.partial(_gmlp_kernel, tl=tl, lc=lc),
        grid=(b, l // tl),
        in_specs=[pl.BlockSpec((None, tl, GM_WIDTH), lambda bi, i: (bi, i, 1)),
                  pl.BlockSpec((None, tl, GM_WIDTH), lambda bi, i: (bi, i, 2)),
                  _full((1, GM_WIDTH)), _full(ws.shape), _full(bias.shape)],
        out_specs=out_specs, out_shape=out_shape,
        compiler_params=_cparams("parallel", "parallel"),
        name="gmlp",
    )(y_in, y_in, g_v.reshape(1, GM_WIDTH), ws, bias)
    return (res[0], res[1]) if want_vn else (res[0], None)


def _proj_res_kernel(x_ref, gt_ref, a_ref, b_ref, wa_ref, wb_ref, o_ref):
    mix = (jnp.dot(a_ref[...].astype(BF16), wa_ref[...], preferred_element_type=F32)
           + jnp.dot(b_ref[...].astype(BF16), wb_ref[...], preferred_element_type=F32))
    o_ref[...] = x_ref[...] + gt_ref[...] * mix


def _proj_residual(x, gt, a, bm, w, tl):
    b, l, d = x.shape
    ka = a.shape[-1]
    wa, wb = w[:ka].astype(BF16), w[ka:].astype(BF16)
    blk = lambda n: pl.BlockSpec((None, tl, n), lambda bi, i: (bi, i, 0))
    return pl.pallas_call(
        _proj_res_kernel,
        grid=(b, l // tl),
        in_specs=[blk(d), pl.BlockSpec((None, 1, d), lambda bi, i: (bi, 0, 0)), blk(ka), blk(bm.shape[-1]),
                  _full(wa.shape), _full(wb.shape)],
        out_specs=blk(d), out_shape=jax.ShapeDtypeStruct((b, l, d), F32),
        compiler_params=_cparams("parallel", "parallel"),
        name="proj_residual",
    )(x, gt, a, bm, wa, wb)


def _rope_tables(pos0, l):
    pos = (pos0 + jnp.arange(l, dtype=jnp.int32)).astype(F32)[:, None]
    lane = jnp.arange(LANES)

    def table(period, start, rot):
        half = rot // 2
        r = lane % period - start
        inside = (r >= 0) & (r < rot)
        k = jnp.where(inside, r % half, 0)
        inv = ROPE_THETA ** (-k.astype(F32) * 2.0 / rot)
        ang = pos * inv[None, :]
        cos = jnp.where(inside[None], jnp.cos(ang), 1.0)
        sgn = jnp.where(r < half, -1.0, 1.0)
        sin = jnp.where(inside[None], jnp.sin(ang) * sgn[None], 0.0)
        return cos, sin

    return table(DA_DH, 0, DA_ROPE), table(LANES, MLA_NOPE, MLA_ROPE), table(LANES, 0, MLA_ROPE)


def _rotate(x, cos, sin, period, start, rot):
    half = rot // 2
    n = x.shape[-1]
    reps = n // LANES
    lane = (lax.broadcasted_iota(jnp.int32, x.shape, 1) & (period - 1)) - start
    up = pltpu.roll(x, n - half, 1)
    dn = pltpu.roll(x, half, 1)
    partner = jnp.where(lane < half, up, dn)
    if reps > 1:
        cos = jnp.concatenate([cos] * reps, axis=1)
        sin = jnp.concatenate([sin] * reps, axis=1)
    return x * cos + partner * sin


def _odd_prep_kernel(q_ref, k_ref, cq_ref, ckv_ref, kpe_ref, cd_ref, sd_ref, cm_ref, sm_ref, ck_ref, sk_ref,
                     gq_ref, wuq_ref, gkv_ref, qd_ref, kd_ref, qm_ref, lat_ref, kpe_out_ref):
    cd, sd = cd_ref[...], sd_ref[...]
    qd_ref[...] = _rotate(q_ref[...], cd, sd, DA_DH, 0, DA_ROPE).astype(BF16)
    kd_ref[...] = _rotate(k_ref[...], cd, sd, DA_DH, 0, DA_ROPE)
    cq = cq_ref[...]
    cqn = cq * lax.rsqrt(jnp.mean(cq * cq, axis=-1, keepdims=True) + EPS) * gq_ref[...]
    qm = jnp.dot(cqn.astype(BF16), wuq_ref[...], preferred_element_type=F32)
    qm_ref[...] = _rotate(qm, cm_ref[...], sm_ref[...], LANES, MLA_NOPE, MLA_ROPE).astype(BF16)
    ckv = ckv_ref[...]
    lat_ref[...] = ckv * lax.rsqrt(jnp.mean(ckv * ckv, axis=-1, keepdims=True) + EPS) * gkv_ref[...]
    kpe = _rotate(kpe_ref[...], ck_ref[...], sk_ref[...], LANES, 0, MLA_ROPE)
    kpe_out_ref[...] = kpe[:, :MLA_ROPE]


def _odd_prep(y_in, pos0, g_q, w_uq, g_kv, tl):
    b, l, _ = y_in.shape
    (cd, sd), (cm, sm), (ck, sk) = _rope_tables(pos0, l)
    per = MLA_NOPE + MLA_ROPE
    wuq = jnp.pad(w_uq.reshape(MLA_Q_RANK, MLA_HEADS, per), ((0, 0), (0, 0), (0, LANES - per)))
    wuq = wuq.reshape(MLA_Q_RANK, MLA_HEADS * LANES).astype(BF16)
    col = lambda w, j: pl.BlockSpec((None, tl, w), lambda bi, i: (bi, i, j))
    tab = pl.BlockSpec((tl, LANES), lambda bi, i: (i, 0))
    out = lambda w: pl.BlockSpec((None, tl, w), lambda bi, i: (bi, i, 0))
    return pl.pallas_call(
        _odd_prep_kernel,
        grid=(b, l // tl),
        in_specs=[col(512, 0), col(512, 1), col(256, 6), col(128, 14), col(128, 15),
                  tab, tab, tab, tab, tab, tab,
                  _full((1, MLA_Q_RANK)), _full(wuq.shape), _full((1, MLA_KV_RANK))],
        out_specs=[out(512), out(512), out(512), out(MLA_KV_RANK), out(MLA_ROPE)],
        out_shape=[jax.ShapeDtypeStruct((b, l, 512), BF16), jax.ShapeDtypeStruct((b, l, 512), F32),
                   jax.ShapeDtypeStruct((b, l, 512), BF16), jax.ShapeDtypeStruct((b, l, MLA_KV_RANK), F32),
                   jax.ShapeDtypeStruct((b, l, MLA_ROPE), F32)],
        compiler_params=_cparams("parallel", "parallel"),
        name="odd_prep",
    )(y_in, y_in, y_in, y_in, y_in, cd, sd, cm, sm, ck, sk,
      g_q.reshape(1, MLA_Q_RANK), wuq, g_kv.reshape(1, MLA_KV_RANK))


def _mla_kv_kernel(lat_ref, kpe_ref, wk_ref, wv_ref, place_ref, k_ref, v_ref):
    lat = lat_ref[...].astype(BF16)
    k = (jnp.dot(lat, wk_ref[...], preferred_element_type=F32)
         + jnp.dot(kpe_ref[...].astype(BF16), place_ref[...], preferred_element_type=F32))
    k_ref[...] = k.astype(BF16)
    v_ref[...] = jnp.dot(lat, wv_ref[...], preferred_element_type=F32).astype(BF16)


def _mla_kv(lat_all, kpe_all, w_ukv, tk):
    b, k, _ = lat_all.shape
    per = MLA_NOPE + MLA_DV
    w3 = w_ukv.reshape(MLA_KV_RANK, MLA_HEADS, per)
    wk = jnp.pad(w3[:, :, :MLA_NOPE], ((0, 0), (0, 0), (0, LANES - MLA_NOPE))).reshape(MLA_KV_RANK, MLA_HEADS * LANES)
    wv = w3[:, :, MLA_NOPE:].reshape(MLA_KV_RANK, MLA_HEADS * MLA_DV)
    place = jnp.pad(jnp.eye(MLA_ROPE, dtype=F32), ((0, 0), (MLA_NOPE, LANES - MLA_NOPE - MLA_ROPE)))
    place = jnp.tile(place, (1, MLA_HEADS))
    blk = lambda w: pl.BlockSpec((None, tk, w), lambda bi, i: (bi, i, 0))
    return pl.pallas_call(
        _mla_kv_kernel,
        grid=(b, k // tk),
        in_specs=[blk(MLA_KV_RANK), blk(MLA_ROPE), _full(wk.shape), _full(wv.shape), _full(place.shape)],
        out_specs=[blk(512), blk(512)],
        out_shape=[jax.ShapeDtypeStruct((b, k, 512), BF16), jax.ShapeDtypeStruct((b, k, 512), BF16)],
        compiler_params=_cparams("parallel", "parallel"),
        name="mla_kv",
    )(lat_all, kpe_all, wk.astype(BF16), wv.astype(BF16), place.astype(BF16))


def _visible_limit(pos_last):
    shift = CHUNK.bit_length() - 1
    return ((pos_last >> shift) + 1) << shift


def _fold_lanes(x, op):
    n = x.shape[-1]
    if n % LANES:
        return x
    out = x[:, :LANES]
    for c in range(1, n // LANES):
        out = op(out, x[:, c * LANES:(c + 1) * LANES])
    return out


def _flash_kernel(lam_ref, q_ref, k_ref, v_ref, g_ref, o_ref, m_ref, l_ref, acc_ref, *,
                  nhm, dqk, maps, scale, pos0, tq, tk, nkv, out_scale):
    i, j = pl.program_id(1), pl.program_id(2)

    @pl.when(j == 0)
    def _():
        m_ref[...] = jnp.full(m_ref.shape, NEG, F32)
        l_ref[...] = jnp.zeros(l_ref.shape, F32)
        acc_ref[...] = jnp.zeros(acc_ref.shape, F32)

    c2 = scale * math.log2(math.e)
    visible = j * tk < _visible_limit(pos0 + (i + 1) * tq - 1)
    unmasked = (j + 1) * tk <= _visible_limit(pos0 + i * tq)

    def scores(hm):
        q = q_ref[:, hm * dqk:(hm + 1) * dqk].astype(BF16)
        k = k_ref[:, hm * dqk:(hm + 1) * dqk].astype(BF16)
        return lax.dot_general(q, k, (((1,), (1,)), ((), ())), preferred_element_type=F32)

    def process(masked):
        if masked:
            q_pos = pos0 + i * tq + lax.broadcasted_iota(jnp.int32, (tq, tk), 0)
            k_pos = j * tk + lax.broadcasted_iota(jnp.int32, (tq, tk), 1)
            mask = k_pos < _visible_limit(q_pos)
        s_next = scores(0)
        for hm in range(nhm):
            s = s_next
            if hm + 1 < nhm:
                s_next = scores(hm + 1)
            if masked:
                s = jnp.where(mask, s, NEG)
            v = v_ref[:, (hm // maps) * DA_DV:(hm // maps + 1) * DA_DV].astype(BF16)
            m_prev = m_ref[hm]
            m_new = jnp.maximum(m_prev, jnp.max(_fold_lanes(s, jnp.maximum), axis=-1, keepdims=True))
            p = jnp.exp2((s - m_new) * c2)
            alpha = jnp.exp2((m_prev - m_new) * c2)
            l_ref[hm] = alpha * l_ref[hm] + jnp.sum(_fold_lanes(p, jnp.add), axis=-1, keepdims=True)
            acc_ref[hm] = alpha * acc_ref[hm] + jnp.dot(p.astype(BF16), v, preferred_element_type=F32)
            m_ref[hm] = m_new

    pl.when(visible & unmasked)(lambda: process(False))
    pl.when(visible & jnp.logical_not(unmasked))(lambda: process(True))

    @pl.when(j == nkv - 1)
    def _():
        for h in range(nhm // maps):
            if maps == 2:
                o = acc_ref[2 * h] / l_ref[2 * h] - lam_ref[0] * (acc_ref[2 * h + 1] / l_ref[2 * h + 1])
                o = o * lax.rsqrt(jnp.mean(o * o, axis=-1, keepdims=True) + EPS) * g_ref[...] * out_scale
            else:
                o = acc_ref[h] / l_ref[h]
            o_ref[:, h * DA_DV:(h + 1) * DA_DV] = o


def _flash(q, k, v, lam, g_sub, *, nhm, dqk, maps, scale, pos0, tq, tk, out_scale, v_col=0):
    b, l, _ = q.shape
    kk = k.shape[1]
    nkv = kk // tk
    nh = nhm // maps

    def kv_idx(bi, i, j):
        last = (_visible_limit(pos0 + (i + 1) * tq - 1) - 1) // tk
        return jnp.minimum(j, last)

    return pl.pallas_call(
        functools.partial(_flash_kernel, nhm=nhm, dqk=dqk, maps=maps, scale=scale, pos0=pos0, tq=tq, tk=tk,
                          nkv=nkv, out_scale=out_scale),
        grid=(b, l // tq, nkv),
        in_specs=[pl.BlockSpec(memory_space=pltpu.SMEM),
                  pl.BlockSpec((None, tq, nhm * dqk), lambda bi, i, j: (bi, i, 0)),
                  pl.BlockSpec((None, tk, nhm * dqk), lambda bi, i, j: (bi, kv_idx(bi, i, j), 0)),
                  pl.BlockSpec((None, tk, nh * DA_DV), lambda bi, i, j: (bi, kv_idx(bi, i, j), v_col)),
                  _full((1, DA_DV))],
        out_specs=pl.BlockSpec((None, tq, nh * DA_DV), lambda bi, i, j: (bi, i, 0)),
        out_shape=jax.ShapeDtypeStruct((b, l, nh * DA_DV), F32),
        scratch_shapes=[pltpu.VMEM((nhm, tq, 1), F32), pltpu.VMEM((nhm, tq, 1), F32),
                        pltpu.VMEM((nhm, tq, DA_DV), F32)],
        compiler_params=_cparams("parallel", "parallel", "arbitrary"),
        name="flash_da" if maps == 2 else "flash_mla",
    )(lam, q, k, v, g_sub)


def _top16(s, rid, payload, val_ref, pay_ref):
    big = float(s.shape[0])
    for it in range(PEER_TOPK):
        m = jnp.max(s, axis=0, keepdims=True)
        am = jnp.min(jnp.where(s == m, rid, big), axis=0, keepdims=True)
        sel = rid == am
        val_ref[it:it + 1, :] = m
        if payload is None:
            pay_ref[it:it + 1, :] = am
        else:
            pay_ref[it:it + 1, :] = jnp.max(jnp.where(sel, payload, -1.0), axis=0, keepdims=True)
        s = jnp.where(sel, -jnp.inf, s)


_PEER_PAIRS = [(a, b) for a in range(PEER_TOPK) for b in range(PEER_TOPK) if (a + 1) * (b + 1) <= PEER_TOPK]
_PEER_CAND_ROWS = -(-len(_PEER_PAIRS) // 8) * 8


def _route_kernel(q_ref, k1_ref, k2_ref, idx_ref, g_ref, v1_ref, i1_ref, v2_ref, i2_ref, vc_ref, ic_ref,
                  cand_ref, cidx_ref):
    tt = q_ref.shape[0]
    half = q_ref.shape[1] // 2
    nt = (((1,), (1,)), ((), ()))
    s1 = lax.dot_general(k1_ref[...].astype(BF16), q_ref[:, :half].astype(BF16), nt, preferred_element_type=F32)
    s2 = lax.dot_general(k2_ref[...].astype(BF16), q_ref[:, half:].astype(BF16), nt, preferred_element_type=F32)
    rid = lax.broadcasted_iota(jnp.int32, (PEER_NKEYS, tt), 0).astype(F32)
    _top16(s1, rid, None, v1_ref, i1_ref)
    _top16(s2, rid, None, v2_ref, i2_ref)
    npairs = len(_PEER_PAIRS)
    cand_ref[npairs:, :] = jnp.full((_PEER_CAND_ROWS - npairs, tt), -jnp.inf, F32)
    cidx_ref[npairs:, :] = jnp.full((_PEER_CAND_ROWS - npairs, tt), -1.0, F32)
    for r, (a, b) in enumerate(_PEER_PAIRS):
        cand_ref[r:r + 1, :] = v1_ref[a:a + 1, :] + v2_ref[b:b + 1, :]
        cidx_ref[r:r + 1, :] = i1_ref[a:a + 1, :] * float(PEER_NKEYS) + i2_ref[b:b + 1, :]
    rid2 = lax.broadcasted_iota(jnp.int32, (_PEER_CAND_ROWS, tt), 0).astype(F32)
    _top16(cand_ref[...], rid2, cidx_ref[...], vc_ref, ic_ref)
    sc = vc_ref[...]
    e = jnp.exp(sc - sc[0:1, :])
    g_ref[...] = e / jnp.sum(e, axis=0, keepdims=True)
    idx_ref[...] = ic_ref[...].astype(jnp.int32)


def _peer_route(q, k1, k2):
    t = q.shape[0]
    tt = PEER_TT
    dk = 2 * PEER_NKEYS
    kspec = pl.BlockSpec((None, PEER_NKEYS, dk // 2), lambda i, h: (h, 0, 0))
    ospec = pl.BlockSpec((None, PEER_TOPK, tt), lambda i, h: (i, h, 0))
    sm = pltpu.VMEM((PEER_TOPK, tt), F32)
    return pl.pallas_call(
        _route_kernel,
        grid=(t // tt, PEER_HEADS),
        in_specs=[pl.BlockSpec((tt, dk), lambda i, h: (i, h)), kspec, kspec],
        out_specs=[ospec, ospec],
        out_shape=[jax.ShapeDtypeStruct((t // tt, PEER_HK, tt), jnp.int32),
                   jax.ShapeDtypeStruct((t // tt, PEER_HK, tt), F32)],
        scratch_shapes=[sm, sm, sm, sm, sm, sm,
                        pltpu.VMEM((_PEER_CAND_ROWS, tt), F32), pltpu.VMEM((_PEER_CAND_ROWS, tt), F32)],
        compiler_params=_cparams("parallel", "parallel"),
        name="peer_route",
    )(q, k1, k2)


def _pack_table(tab):
    e, d = tab.shape
    tb = lax.bitcast_convert_type(tab.astype(BF16), jnp.uint16).astype(jnp.uint32)
    words = tb[:, :d // 2] | (tb[:, d // 2:] << 16)
    return lax.bitcast_convert_type(words, F32)


def _sc_unpack(words):
    bits = lax.bitcast_convert_type(words, jnp.int32)
    return lax.bitcast_convert_type(bits << 16, F32), lax.bitcast_convert_type(bits & jnp.int32(-65536), F32)


def _sc_gelu(x):
    u = math.sqrt(2.0 / math.pi) * (x + 0.044715 * (x * x * x))
    tanh_u = 1.0 - 2.0 / (jnp.exp(2.0 * u) + 1.0)
    return 0.5 * x * (1.0 + tanh_u)


def _sc_peer(u_tab, v_tab, idx_tm, g_tm, h, side_idx=None):
    t, nk = idx_tm.shape
    w = u_tab.shape[1]
    d = 2 * w
    half_rows = SC_GATHER_ROWS
    nbuf = SC_BUFFERS
    per_tok = nk // half_rows
    mesh = plsc.VectorSubcoreMesh(core_axis_name="core", subcore_axis_name="subcore")
    n_workers = mesh.num_cores * mesh.num_subcores
    tpw = t // n_workers
    tb = SC_TOKEN_BLOCK
    steps = per_tok * tb
    dot_rows = SC_LANES
    sum_rows = 8
    assert t % n_workers == 0 and tpw % tb == 0 and w % SC_LANES == 0
    assert nk % half_rows == 0 and half_rows % SC_LANES == 0 and steps % nbuf == 0

    n_side = 0 if side_idx is None else side_idx.shape[0]
    side_w = SC_SIDE_ROWS
    lw = n_side // n_workers
    n_win = lw // side_w
    blk_side = steps * side_w
    if n_side:
        assert nbuf == 2 and steps % 2 == 0 and n_side % (n_workers * blk_side) == 0
        assert n_win + 2 <= (tpw // tb) * steps

    def lanes(start):
        return pl.ds(pl.multiple_of(start, SC_LANES), SC_LANES)

    y_type = jax.ShapeDtypeStruct((t, d), F32)
    side_type = jax.ShapeDtypeStruct((n_side, w), F32)
    scratch = [pltpu.VMEM((tb * nk,), jnp.int32), pltpu.VMEM((tb, d), F32), pltpu.VMEM((tb, nk), F32),
               pltpu.VMEM((nbuf, half_rows, w), F32), pltpu.VMEM((tb, nk), F32), pltpu.VMEM((tb, d), F32),
               pltpu.SemaphoreType.DMA((nbuf,))]
    if n_side:
        scratch += [pltpu.VMEM((2 * blk_side,), jnp.int32), pltpu.VMEM((2, 2, side_w, w), F32),
                    pltpu.SemaphoreType.DMA((8,))]

    @functools.partial(
        pl.kernel, out_type=[y_type, side_type, side_type] if n_side else y_type, mesh=mesh, scratch_types=scratch,
        compiler_params=pltpu.CompilerParams(needs_layout_passes=False))
    def peer_kernel(u_hbm, v_hbm, idx_hbm, g_hbm, h_hbm, *refs):
        if n_side:
            (sidx_hbm, y_hbm, gu_hbm, gv_hbm, idx_v, h_v, g_v, rows_v, coef_v, y_v, sems,
             sidx_v, sbuf, ssems) = refs
        else:
            y_hbm, idx_v, h_v, g_v, rows_v, coef_v, y_v, sems = refs
        wid = lax.axis_index("subcore") * mesh.num_cores + lax.axis_index("core")
        lane = lax.iota(jnp.int32, SC_LANES)

        def gather(tab_hbm, step, buf):
            rows = idx_v.at[pl.ds(pl.multiple_of(step * half_rows, half_rows), half_rows)]
            return pltpu.make_async_copy(tab_hbm.at[rows], rows_v.at[buf], sems.at[buf])

        def prime(tab_hbm):
            for j in range(nbuf - 1):
                gather(tab_hbm, j, j).start()

        def side_in(tab_hbm, tab, n, par):
            slot = ((n // steps) % 2) * blk_side + (n % steps) * side_w
            rows = sidx_v.at[pl.ds(pl.multiple_of(slot, side_w), side_w)]
            return pltpu.make_async_copy(tab_hbm.at[rows], sbuf.at[tab, par], ssems.at[tab * 4 + par])

        def side_out(out_hbm, tab, n, par):
            dst = out_hbm.at[pl.ds(pl.multiple_of(wid * lw + n * side_w, side_w), side_w)]
            return pltpu.make_async_copy(sbuf.at[tab, par], dst, ssems.at[tab * 4 + 2 + par])

        def side_step(tab_hbm, out_hbm, tab, n, par):
            @pl.when((n >= 2) & (n < n_win + 2))
            def _():
                side_out(out_hbm, tab, n - 2, par).wait()

            @pl.when(n < n_win)
            def _():
                side_in(tab_hbm, tab, n, par).start()

            @pl.when((n >= 1) & (n < n_win + 1))
            def _():
                side_in(tab_hbm, tab, n - 1, 1 - par).wait()
                side_out(out_hbm, tab, n - 1, 1 - par).start()

        def run(tab_hbm, work, blk=None, side=None):
            @pl.loop(0, steps, step=nbuf)
            def _(s):
                for j in range(nbuf):
                    if side is not None:
                        side_step(tab_hbm, side[0], side[1], blk * steps + s + j, j)
                    ahead = s + j + nbuf - 1

                    @pl.when(ahead < steps)
                    def _():
                        gather(tab_hbm, ahead, (j + nbuf - 1) % nbuf).start()

                    gather(tab_hbm, s + j, j).wait()
                    work(s + j, j)

        def dots(step, buf):
            tok = step // per_tok
            col0 = (step % per_tok) * half_rows

            def group(q, _):
                r0 = q * dot_rows

                def chunk(c, accs):
                    h_lo = h_v[tok, lanes(c * SC_LANES)]
                    h_hi = h_v[tok, lanes(w + c * SC_LANES)]
                    out = []
                    for r in range(dot_rows):
                        lo, hi = _sc_unpack(rows_v[buf, r0 + r, lanes(c * SC_LANES)])
                        out.append(accs[r] + lo * h_lo + hi * h_hi)
                    return tuple(out)

                accs = lax.fori_loop(0, w // SC_LANES, chunk, (jnp.zeros((SC_LANES,), F32),) * dot_rows)
                vec = jnp.zeros((SC_LANES,), F32)
                for r in range(dot_rows):
                    vec = jnp.where(lane == r, jnp.sum(accs[r]), vec)
                coef_v[tok, lanes(col0 + r0)] = vec
                return 0

            lax.fori_loop(0, half_rows // dot_rows, group, 0)

        def gates():
            @pl.loop(0, tb)
            def _(tok):
                @pl.loop(0, nk // SC_LANES)
                def _(q):
                    ds = lanes(q * SC_LANES)
                    coef_v[tok, ds] = g_v[tok, ds] * _sc_gelu(coef_v[tok, ds])

        def weighted_sum(step, buf):
            tok = step // per_tok
            col0 = (step % per_tok) * half_rows

            @pl.when(col0 == 0)
            def _():
                @plsc.parallel_loop(0, d // SC_LANES)
                def _(c):
                    y_v[tok, lanes(c * SC_LANES)] = jnp.zeros((SC_LANES,), F32)

            def group(g, _):
                cvec = coef_v[tok, lanes(col0 + (g // 2) * SC_LANES)]
                base = (g % 2) * sum_rows
                cs = [cvec.at[jnp.full((SC_LANES,), base + r, jnp.int32)].get(mode="promise_in_bounds")
                      for r in range(sum_rows)]
                r0 = g * sum_rows

                @plsc.parallel_loop(0, w // SC_LANES, unroll=2)
                def _(c):
                    los, his = [], []
                    for r in range(sum_rows):
                        lo, hi = _sc_unpack(rows_v[buf, r0 + r, lanes(c * SC_LANES)])
                        los.append(cs[r] * lo)
                        his.append(cs[r] * hi)
                    while len(los) > 1:
                        los = [a + b for a, b in zip(los[::2], los[1::2])]
                        his = [a + b for a, b in zip(his[::2], his[1::2])]
                    y_v[tok, lanes(c * SC_LANES)] = y_v[tok, lanes(c * SC_LANES)] + los[0]
                    y_v[tok, lanes(w + c * SC_LANES)] = y_v[tok, lanes(w + c * SC_LANES)] + his[0]

                return 0

            lax.fori_loop(0, half_rows // sum_rows, group, 0)

        @pl.loop(0, tpw // tb)
        def _(blk):
            tok0 = wid * tpw + blk * tb
            pltpu.sync_copy(idx_hbm.at[pl.ds(pl.multiple_of(tok0 * nk, nk), tb * nk)], idx_v)
            prime(u_hbm)
            pltpu.sync_copy(h_hbm.at[pl.ds(tok0, tb)], h_v)
            pltpu.sync_copy(g_hbm.at[pl.ds(tok0, tb)], g_v)
            if n_side:
                @pl.when(blk * steps < n_win)
                def _():
                    src = sidx_hbm.at[pl.ds(pl.multiple_of(wid * lw + blk * blk_side, blk_side), blk_side)]
                    pltpu.sync_copy(src, sidx_v.at[pl.ds(pl.multiple_of((blk % 2) * blk_side, blk_side), blk_side)])

            run(u_hbm, dots, blk, (gu_hbm, 0) if n_side else None)
            prime(v_hbm)
            gates()
            run(v_hbm, weighted_sum, blk, (gv_hbm, 1) if n_side else None)
            pltpu.sync_copy(y_v, y_hbm.at[pl.ds(tok0, tb)])

    if n_side:
        return peer_kernel(u_tab, v_tab, idx_tm.reshape(t * nk), g_tm, h, side_idx)
    return peer_kernel(u_tab, v_tab, idx_tm.reshape(t * nk), g_tm, h)


def _unpack(words):
    w = pltpu.bitcast(words, jnp.uint32)
    lo = pltpu.bitcast(w << 16, F32)
    hi = pltpu.bitcast(w & jnp.uint32(0xFFFF0000), F32)
    return lo, hi


def _peer_act_kernel(h_ref, gu_ref, g_ref, coef_ref, act_ref):
    c = pl.program_id(1)
    half = h_ref.shape[1] // 2
    h_lo, h_hi = h_ref[:, :half], h_ref[:, half:]
    lane = lax.broadcasted_iota(jnp.int32, act_ref.shape, 1)

    @pl.when(c == 0)
    def _():
        act_ref[...] = jnp.zeros(act_ref.shape, F32)

    act = act_ref[...]
    for k in range(PEER_HKC):
        lo, hi = _unpack(gu_ref[k])
        a = jnp.sum(lo * h_lo + hi * h_hi, axis=-1, keepdims=True)
        act = jnp.where(lane == c * PEER_HKC + k, a, act)
    act_ref[...] = act

    @pl.when(c == pl.num_programs(1) - 1)
    def _():
        coef_ref[...] = g_ref[...].T * jax.nn.gelu(act)


def _peer_act(h, gu, g):
    t, d = h.shape
    tt = PEER_TT
    nt = t // tt
    return pl.pallas_call(
        _peer_act_kernel,
        grid=(nt, PEER_HK // PEER_HKC),
        in_specs=[pl.BlockSpec((tt, d), lambda i, c: (i, 0)),
                  pl.BlockSpec((None, PEER_HKC, tt, d // 2), lambda i, c: (i, c, 0, 0)),
                  pl.BlockSpec((None, PEER_HK, tt), lambda i, c: (i, 0, 0))],
        out_specs=pl.BlockSpec((None, tt, PEER_HK), lambda i, c: (i, 0, 0)),
        out_shape=jax.ShapeDtypeStruct((nt, tt, PEER_HK), F32),
        scratch_shapes=[pltpu.VMEM((tt, PEER_HK), F32)],
        compiler_params=_cparams("parallel", "arbitrary"),
        name="peer_act",
    )(h, gu, g)


def _peer_mix_kernel(coef_ref, gv_ref, y_ref):
    c = pl.program_id(1)
    half = y_ref.shape[1] // 2

    @pl.when(c == 0)
    def _():
        y_ref[...] = jnp.zeros(y_ref.shape, F32)

    coef = coef_ref[...]
    lane = lax.broadcasted_iota(jnp.int32, coef.shape, 1)
    acc_lo, acc_hi = y_ref[:, :half], y_ref[:, half:]
    for k in range(PEER_HKC):
        ck = jnp.sum(jnp.where(lane == c * PEER_HKC + k, coef, 0.0), axis=-1, keepdims=True)
        lo, hi = _unpack(gv_ref[k])
        acc_lo = acc_lo + ck * lo
        acc_hi = acc_hi + ck * hi
    y_ref[:, :half] = acc_lo
    y_ref[:, half:] = acc_hi


def _peer_mix(coef, gv):
    nt, tt, _ = coef.shape
    w = gv.shape[-1]
    return pl.pallas_call(
        _peer_mix_kernel,
        grid=(nt, PEER_HK // PEER_HKC),
        in_specs=[pl.BlockSpec((None, tt, PEER_HK), lambda i, c: (i, 0, 0)),
                  pl.BlockSpec((None, PEER_HKC, tt, w), lambda i, c: (i, c, 0, 0))],
        out_specs=pl.BlockSpec((tt, 2 * w), lambda i, c: (i, 0)),
        out_shape=jax.ShapeDtypeStruct((nt * tt, 2 * w), F32),
        compiler_params=_cparams("parallel", "arbitrary"),
        name="peer_mix",
    )(coef, gv)


def _peer_out_kernel(x_ref, gt_ref, y_ref, gf_ref, o_ref, *, final):
    x = x_ref[...] + gt_ref[...] * y_ref[...]
    if final:
        x = x * lax.rsqrt(jnp.mean(x * x, axis=-1, keepdims=True) + EPS) * gf_ref[...]
    o_ref[...] = x


def _peer_out(x, gt, y, g_final, final):
    t, d = x.shape
    tt = PEER_TT
    nt = t // tt
    row = pl.BlockSpec((tt, d), lambda i: (i, 0))
    if gt.ndim == 3:
        per_seq = nt // gt.shape[0]
        gt_spec = pl.BlockSpec((None, 1, d), lambda i: (i // per_seq, 0, 0))
    else:
        gt_spec = row
    return pl.pallas_call(
        functools.partial(_peer_out_kernel, final=final),
        grid=(nt,),
        in_specs=[row, gt_spec, row, _full((1, d))],
        out_specs=row, out_shape=jax.ShapeDtypeStruct((t, d), F32),
        compiler_params=_cparams("parallel"),
        name="peer_out",
    )(x, gt, y, g_final.reshape(1, d))


def _peer(x, g_ffn, sc2, sh2, gt2, w_q, k1, k2, u_pack, v_pack, g_final, final, tl):
    b, l, d = x.shape
    t = b * l
    q, h = _norm_mod_matmul(x, g_ffn, sc2, sh2, w_q, tl, want_h=True)
    idx, g = _peer_route(q.reshape(t, -1), k1, k2)
    h = h.reshape(t, d)
    nt = t // PEER_TT
    n_tc = nt * PEER_TC_TILES // 32
    n_sc = nt - n_tc
    t_sc = n_sc * PEER_TT
    token_major = lambda a: jnp.transpose(a, (0, 2, 1)).reshape(-1, PEER_HK)
    if n_tc:
        y_sc, gu, gv = _sc_peer(u_pack, v_pack, token_major(idx[:n_sc]), token_major(g[:n_sc]), h[:t_sc],
                                idx[n_sc:].reshape(-1))
        rows = lambda a: a.reshape(n_tc, PEER_HK, PEER_TT, d // 2)
        y_tc = _peer_mix(_peer_act(h[t_sc:], rows(gu), g[n_sc:]), rows(gv))
        y = jnp.concatenate([y_sc, y_tc], axis=0)
    else:
        y = _sc_peer(u_pack, v_pack, token_major(idx), token_major(g), h)
    if l % PEER_TT:
        gt2 = jnp.broadcast_to(gt2, (b, l, d)).reshape(t, d)
    return _peer_out(x.reshape(t, d), gt2, y, g_final, final).reshape(b, l, d)


def _trunk(x, mod, pos0, past, p, tl):
    b, l, d = x.shape
    outs = {}
    for layer in range(DEPTH):
        sh1, sc1, gt1, sh2, sc2, gt2 = [m[:, None, :] for m in jnp.split(mod[layer], 6, axis=-1)]
        if layer % 2 == 0:
            e = layer // 2
            y_in = _norm_mod_matmul(x, p['g_mix'][layer], sc1, sh1, p['w_in_e'][e].astype(BF16), tl)
            if past is None:
                h0_re = jnp.zeros((b, S5_GROUPS, S5_STATE), F32)
                h0_im = jnp.zeros((b, S5_GROUPS, S5_STATE), F32)
            else:
                h0_re, h0_im = past['s5_re'][e], past['s5_im'][e]
            ya, hr, hi = _s5(y_in, h0_re, h0_im, p['s5'][e], p['s5_d'][e], p['w_glu'][e], p['b_glu'][e], tl)
            yb, vn = _gmlp(y_in, p['gm_g_v'][e], p['gm_w_s'][e], p['gm_b_s'][e], tl, want_vn=past is not None)
            x = _proj_residual(x, gt1, ya, yb, p['w_out_e'][e], tl)
            outs.update(s5_re=hr, s5_im=hi, gm_v=vn)
        else:
            o = layer // 2
            w_in = jnp.pad(p['w_in_o'][o], ((0, 0), (0, ODD_IN_PAD - p['w_in_o'][o].shape[1]))).astype(BF16)
            y_in = _norm_mod_matmul(x, p['g_mix'][layer], sc1, sh1, w_in, tl)
            qd, kd, qm, lat, kpe = _odd_prep(y_in, pos0, p['mla_g_q'][o], p['mla_w_uq'][o], p['mla_g_kv'][o], tl)
            v_new = y_in[:, :, 2 * 512:3 * 512]
            lam_init = 0.8 - 0.6 * math.exp(-0.3 * layer)
            lam = (jnp.exp(jnp.sum(p['da_lq1'][o] * p['da_lk1'][o])) - jnp.exp(jnp.sum(p['da_lq2'][o] * p['da_lk2'][o]))
                   + lam_init).reshape(1).astype(F32)
            g_sub = p['da_g_sub'][o].reshape(1, DA_DV)
            if past is None:
                k_all, v_all, v_col, lat_all, kpe_all = kd, y_in, 2, lat, kpe
                tq = tk = tl
            else:
                k_all = jnp.concatenate([past['da_k'][o].reshape(b, -1, 512), kd], axis=1)
                v_all = jnp.concatenate([past['da_v'][o].reshape(b, -1, 512), v_new], axis=1)
                lat_all = jnp.concatenate([past['mla_lat'][o], lat], axis=1)
                kpe_all = jnp.concatenate([past['mla_kpe'][o], kpe], axis=1)
                v_col, tq, tk = 0, l, k_all.shape[1]
            yc = _flash(qd, k_all, v_all, lam, g_sub, nhm=2 * DA_HEADS, dqk=DA_DH, maps=2, scale=DA_DH ** -0.5,
                        pos0=pos0, tq=tq, tk=tk, out_scale=1.0 - lam_init, v_col=v_col)
            km, vm = _mla_kv(lat_all, kpe_all, p['mla_w_ukv'][o], tk)
            yd = _flash(qm, km, vm, lam, g_sub, nhm=MLA_HEADS, dqk=LANES, maps=1,
                        scale=(MLA_NOPE + MLA_ROPE) ** -0.5, pos0=pos0, tq=tq, tk=tk, out_scale=1.0)
            x = _proj_residual(x, gt1, yc, yd, p['w_out_o'][o], tl)
            outs.update(da_k=kd.reshape(b, l, DA_HEADS, 2 * DA_DH), da_v=v_new.reshape(b, l, DA_HEADS, DA_DV),
                        mla_lat=lat, mla_kpe=kpe)
        x = _peer(x, p['g_ffn'][layer], sc2, sh2, gt2, p['peer_w_q'][layer].astype(BF16), p['peer_k1'][layer],
                  p['peer_k2'][layer], p['u_pack'][layer], p['v_pack'][layer], p['g_final'],
                  final=layer == DEPTH - 1, tl=tl)
    return x, outs


def kernel(x_prompt, x_sample, c_prompt, c_sample, state_s5_re, state_s5_im, cache_da_k, cache_da_v, cache_mla_latent, cache_mla_kpe, w_ada, b_ada, g_mix, g_ffn, g_final, w_in_e, w_out_e, s5_lam_re, s5_lam_im, s5_log_dt, s5_b_re, s5_b_im, s5_c_re, s5_c_im, s5_d, w_glu, b_glu, gm_g_v, gm_w_s, gm_b_s, w_in_o, w_out_o, da_lq1, da_lk1, da_lq2, da_lk2, da_g_sub, mla_g_q, mla_w_uq, mla_g_kv, mla_w_ukv, peer_w_q, peer_k1, peer_k2, peer_u, peer_v):
    p = dict(g_mix=g_mix, g_ffn=g_ffn, g_final=g_final, w_in_e=w_in_e, w_out_e=w_out_e, s5_d=s5_d, w_glu=w_glu,
             b_glu=b_glu, gm_g_v=gm_g_v, gm_w_s=gm_w_s, gm_b_s=gm_b_s, w_in_o=w_in_o, w_out_o=w_out_o,
             da_lq1=da_lq1, da_lk1=da_lk1, da_lq2=da_lq2, da_lk2=da_lk2, da_g_sub=da_g_sub, mla_g_q=mla_g_q,
             mla_w_uq=mla_w_uq, mla_g_kv=mla_g_kv, mla_w_ukv=mla_w_ukv, peer_w_q=peer_w_q, peer_k1=peer_k1,
             peer_k2=peer_k2)
    n_even = (DEPTH + 1) // 2
    p['s5'] = [_s5_params(s5_lam_re[e], s5_lam_im[e], s5_log_dt[e], s5_b_re[e], s5_b_im[e], s5_c_re[e], s5_c_im[e])
               for e in range(n_even)]
    p['u_pack'] = [_pack_table(peer_u[layer]) for layer in range(DEPTH)]
    p['v_pack'] = [_pack_table(peer_v[layer]) for layer in range(DEPTH)]
    past = dict(s5_re=state_s5_re, s5_im=state_s5_im, da_k=cache_da_k, da_v=cache_da_v,
                mla_lat=cache_mla_latent, mla_kpe=cache_mla_kpe)
    nb = x_prompt.shape[0]
    mod = _ada(jnp.concatenate([c_prompt, c_sample], axis=0), w_ada, b_ada)
    past_len = cache_da_k.shape[2]
    parts = [_trunk(x_prompt[s:s + 1], mod[:, s:s + 1], 0, None, p, tl=PROMPT_TILE) for s in range(nb)]
    y_p = jnp.concatenate([y for y, _ in parts], axis=0)
    o_p = {k: jnp.concatenate([o[k] for _, o in parts], axis=0) for k in parts[0][1] if parts[0][1][k] is not None}
    y_s, o_s = _trunk(x_sample, mod[:, nb:], past_len, past, p, tl=x_sample.shape[1])
    st = lambda a: a[None]
    return (y_p, y_s, st(o_p['s5_re']), st(o_p['s5_im']), st(o_s['s5_re']), st(o_s['s5_im']), st(o_s['gm_v']),
            st(o_p['da_k']), st(o_p['da_v']), st(o_s['da_k']), st(o_s['da_v']),
            st(o_p['mla_lat']), st(o_p['mla_kpe']), st(o_s['mla_lat']), st(o_s['mla_kpe']))
```

```python
import functools
import math

import jax
import jax.numpy as jnp
from jax import lax
from jax.experimental import pallas as pl
from jax.experimental.pallas import tpu as pltpu
from jax.experimental.pallas import tpu_sc as plsc

F32 = jnp.float32
BF16 = jnp.bfloat16

D_MODEL = 1024
DEPTH = 2
CHUNK = 64
ROPE_THETA = 500000.0
EPS = 1e-6
NEG = -1e30

S5_WIDTH = 512
S5_GROUP = 16
S5_GROUPS = 32
S5_STATE = 64
GM_WIDTH = 512
GM_HEADS = 4
GM_CHUNK = 128
DA_HEADS = 4
DA_DH = 64
DA_DV = 128
DA_ROPE = 16
MLA_HEADS = 4
MLA_Q_RANK = 256
MLA_KV_RANK = 128
MLA_NOPE = 64
MLA_ROPE = 32
MLA_DV = 128
PEER_HEADS = 8
PEER_NKEYS = 128
PEER_TOPK = 16
PEER_HK = PEER_HEADS * PEER_TOPK

LANES = 128
VMEM_LIMIT = 48 << 20
PROMPT_TILE = 512
ODD_IN_PAD = 2048
EVEN_SEGMENTS = 2
PEER_TT = 256
SC_LANES = 16
SC_TOKEN_BLOCK = 8
SC_GATHER_ROWS = 64
SC_BUFFERS = 2
SC_SIDE_ROWS = 16
PEER_TC_TILES = 6
PEER_HKC = 16

_HI = lax.Precision.HIGHEST


def _cparams(*sem):
    return pltpu.CompilerParams(dimension_semantics=sem, vmem_limit_bytes=VMEM_LIMIT)


def _full(shape):
    n = len(shape)
    return pl.BlockSpec(shape, lambda *_: (0,) * n)


def _ada_kernel(c_ref, w_ref, b_ref, o_ref):
    c = c_ref[...]
    s = c * jax.nn.sigmoid(c)
    o_ref[...] = jnp.dot(s, w_ref[...], precision=_HI, preferred_element_type=F32) + b_ref[...]


def _ada(c_all, w_ada, b_ada):
    r = c_all.shape[0]
    tn = 1536
    return pl.pallas_call(
        _ada_kernel,
        grid=(DEPTH, 6 * D_MODEL // tn),
        in_specs=[_full((r, D_MODEL)),
                  pl.BlockSpec((None, D_MODEL, tn), lambda l, j: (l, 0, j)),
                  pl.BlockSpec((None, 1, tn), lambda l, j: (l, 0, j))],
        out_specs=pl.BlockSpec((None, r, tn), lambda l, j: (l, 0, j)),
        out_shape=jax.ShapeDtypeStruct((DEPTH, r, 6 * D_MODEL), F32),
        compiler_params=_cparams("parallel", "parallel"),
        name="ada",
    )(c_all, w_ada, b_ada.reshape(DEPTH, 1, 6 * D_MODEL))


def _nmm_kernel(x_ref, g_ref, sc_ref, sh_ref, w_ref, o_ref, *h_ref):
    x = x_ref[...]
    ms = jnp.mean(x * x, axis=-1, keepdims=True)
    h = x * lax.rsqrt(ms + EPS) * g_ref[...] * (1.0 + sc_ref[...]) + sh_ref[...]
    o_ref[...] = jnp.dot(h.astype(BF16), w_ref[...], preferred_element_type=F32)
    if h_ref:
        h_ref[0][...] = h


def _norm_mod_matmul(x, g, sc, sh, w, tl, want_h=False):
    b, l, d = x.shape
    n = w.shape[1]
    row = pl.BlockSpec((None, 1, d), lambda bi, i: (bi, 0, 0))
    out_shape = [jax.ShapeDtypeStruct((b, l, n), F32)]
    out_specs = [pl.BlockSpec((None, tl, n), lambda bi, i: (bi, i, 0))]
    if want_h:
        out_shape.append(jax.ShapeDtypeStruct((b, l, d), F32))
        out_specs.append(pl.BlockSpec((None, tl, d), lambda bi, i: (bi, i, 0)))
    res = pl.pallas_call(
        _nmm_kernel,
        grid=(b, l // tl),
        in_specs=[pl.BlockSpec((None, tl, d), lambda bi, i: (bi, i, 0)), _full((1, d)), row, row, _full((d, n))],
        out_specs=out_specs, out_shape=out_shape,
        compiler_params=_cparams("parallel", "parallel"),
        name="norm_mod_matmul",
    )(x, g.reshape(1, d), sc, sh, w)
    return res if want_h else res[0]


def _s5_kernel(xa_ref, h0r_ref, h0i_ref, bblk_ref, ar_ref, ai_ref, cr_ref, ci_ref, d_ref, wglu_ref, bglu_ref,
               ya_ref, hr_out_ref, hi_out_ref, bur, bui, st_r, st_i, *, tl):
    nj = S5_WIDTH // LANES
    half = 8 * S5_STATE

    @pl.when(pl.program_id(1) == 0)
    def _():
        st_r[...] = h0r_ref[...]
        st_i[...] = h0i_ref[...]

    xa = xa_ref[...]
    for j in range(nj):
        bu = jnp.dot(xa[:, j * LANES:(j + 1) * LANES], bblk_ref[j], precision=_HI, preferred_element_type=F32)
        bur[j] = bu[:, :half]
        bui[j] = bu[:, half:]

    for j0 in range(0, nj, 2):
        js = (j0, j0 + 1)
        a_r = [ar_ref[j] for j in js]
        a_i = [ai_ref[j] for j in js]

        def body(k, carry, js=js, a_r=a_r, a_i=a_i):
            carry = list(carry)
            base = pl.multiple_of(k * 8, 8)
            for s in range(8):
                t = base + s
                for q, j in enumerate(js):
                    hr, hi = carry[2 * q], carry[2 * q + 1]
                    nhr = a_r[q] * hr - a_i[q] * hi + bur[j, pl.ds(t, 1), :]
                    nhi = a_r[q] * hi + a_i[q] * hr + bui[j, pl.ds(t, 1), :]
                    bur[j, pl.ds(t, 1), :] = nhr
                    bui[j, pl.ds(t, 1), :] = nhi
                    carry[2 * q], carry[2 * q + 1] = nhr, nhi
            return tuple(carry)

        init = (st_r[js[0]], st_i[js[0]], st_r[js[1]], st_i[js[1]])
        fin = lax.fori_loop(0, tl // 8, body, init)
        st_r[js[0]], st_i[js[0]], st_r[js[1]], st_i[js[1]] = fin

    hr_out_ref[...] = st_r[...]
    hi_out_ref[...] = st_i[...]

    ys = []
    for j in range(nj):
        y = (jnp.dot(bur[j].astype(BF16), cr_ref[j], preferred_element_type=F32)
             + jnp.dot(bui[j].astype(BF16), ci_ref[j], preferred_element_type=F32))
        ys.append(y + d_ref[:, j * LANES:(j + 1) * LANES] * xa[:, j * LANES:(j + 1) * LANES])
    y = jax.nn.gelu(jnp.concatenate(ys, axis=1))
    z = jnp.dot(y.astype(BF16), wglu_ref[...], preferred_element_type=F32) + bglu_ref[...]
    ya_ref[...] = z[:, :S5_WIDTH] * jax.nn.sigmoid(z[:, S5_WIDTH:])


def _s5_params(lam_re, lam_im, log_dt, b_re, b_im, c_re, c_im):
    lam = lax.complex(lam_re.astype(F32), lam_im.astype(F32))
    dt = jnp.exp(log_dt.astype(F32))[:, None]
    a_bar = jnp.exp(lam * dt)
    b_bar = ((a_bar - 1.0) / lam)[..., None] * lax.complex(b_re.astype(F32), b_im.astype(F32))
    nj = S5_GROUPS // 8
    eye = jnp.eye(8, dtype=F32)

    def blk_b(m):
        m = m.reshape(nj, 8, S5_STATE, S5_GROUP)
        return jnp.einsum('jgpc,gh->jgchp', m, eye).reshape(nj, 8 * S5_GROUP, 8 * S5_STATE)

    def blk_c(m):
        m = m.reshape(nj, 8, S5_GROUP, S5_STATE)
        return jnp.einsum('jgcp,gh->jgphc', m, eye).reshape(nj, 8 * S5_STATE, 8 * S5_GROUP)

    bblk = jnp.concatenate([blk_b(jnp.real(b_bar)), blk_b(jnp.imag(b_bar))], axis=-1)
    cr = blk_c(c_re.astype(F32)).astype(BF16)
    ci = (-blk_c(c_im.astype(F32))).astype(BF16)
    a_r = jnp.real(a_bar).reshape(nj, 1, 8 * S5_STATE)
    a_i = jnp.imag(a_bar).reshape(nj, 1, 8 * S5_STATE)
    return bblk, a_r, a_i, cr, ci


def _s5(y_in, h0_re, h0_im, sp, d_skip, w_glu, b_glu, tl):
    b, l, _ = y_in.shape
    nj = S5_GROUPS // 8
    half = 8 * S5_STATE
    bblk, a_r, a_i, cr, ci = sp
    st = pl.BlockSpec((None, nj, 1, half), lambda bi, i: (bi, 0, 0, 0))
    ya, hr, hi = pl.pallas_call(
        functools.partial(_s5_kernel, tl=tl),
        grid=(b, l // tl),
        in_specs=[pl.BlockSpec((None, tl, S5_WIDTH), lambda bi, i: (bi, i, 0)), st, st,
                  _full(bblk.shape), _full(a_r.shape), _full(a_i.shape), _full(cr.shape), _full(ci.shape),
                  _full((1, S5_WIDTH)), _full((S5_WIDTH, 2 * S5_WIDTH)), _full((1, 2 * S5_WIDTH))],
        out_specs=[pl.BlockSpec((None, tl, S5_WIDTH), lambda bi, i: (bi, i, 0)), st, st],
        out_shape=[jax.ShapeDtypeStruct((b, l, S5_WIDTH), F32),
                   jax.ShapeDtypeStruct((b, nj, 1, half), F32), jax.ShapeDtypeStruct((b, nj, 1, half), F32)],
        scratch_shapes=[pltpu.VMEM((nj, tl, half), F32), pltpu.VMEM((nj, tl, half), F32),
                        pltpu.VMEM((nj, 1, half), F32), pltpu.VMEM((nj, 1, half), F32)],
        compiler_params=_cparams("parallel", "arbitrary"),
        name="s5",
    )(y_in, h0_re.reshape(b, nj, 1, half), h0_im.reshape(b, nj, 1, half), bblk, a_r, a_i, cr, ci,
      d_skip.reshape(1, S5_WIDTH), w_glu.astype(BF16), b_glu.reshape(1, 2 * S5_WIDTH))
    return ya, hr.reshape(b, S5_GROUPS, S5_STATE), hi.reshape(b, S5_GROUPS, S5_STATE)


def _gmlp_kernel(u_ref, v_ref, gv_ref, ws_ref, bias_ref, yb_ref, *vn_ref, tl, lc):
    hd = GM_WIDTH // GM_HEADS
    gv = jax.nn.gelu(v_ref[...])
    vn = gv * lax.rsqrt(jnp.mean(gv * gv, axis=-1, keepdims=True) + EPS) * gv_ref[...]
    if vn_ref:
        vn_ref[0][...] = vn
    gu = jax.nn.gelu(u_ref[...])
    vb = vn.astype(BF16)
    for c in range(tl // lc):
        rows = slice(c * lc, (c + 1) * lc)
        for h in range(GM_HEADS):
            cols = slice(h * hd, (h + 1) * hd)
            mixed = jnp.dot(ws_ref[h], vb[rows, cols], preferred_element_type=F32) + bias_ref[:, cols]
            yb_ref[rows, cols] = gu[rows, cols] * mixed


def _gmlp(y_in, g_v, w_s, b_s, tl, want_vn):
    b, l, _ = y_in.shape
    lc = min(GM_CHUNK, l)
    hd = GM_WIDTH // GM_HEADS
    tri = jnp.tril(jnp.ones((GM_CHUNK, GM_CHUNK), dtype=bool))
    ws = jnp.where(tri[None], w_s, 0.0)[:, :lc, :lc].astype(BF16)
    bias = jnp.repeat(jnp.transpose(b_s)[:lc], hd, axis=1)
    out_shape = [jax.ShapeDtypeStruct((b, l, GM_WIDTH), F32)]
    out_specs = [pl.BlockSpec((None, tl, GM_WIDTH), lambda bi, i: (bi, i, 0))]
    if want_vn:
        out_shape.append(jax.ShapeDtypeStruct((b, l, GM_WIDTH), F32))
        out_specs.append(pl.BlockSpec((None, tl, GM_WIDTH), lambda bi, i: (bi, i, 0)))
    res = pl.pallas_call(
        functools.partial(_gmlp_kernel, tl=tl, lc=lc),
        grid=(b, l // tl),
        in_specs=[pl.BlockSpec((None, tl, GM_WIDTH), lambda bi, i: (bi, i, 1)),
                  pl.BlockSpec((None, tl, GM_WIDTH), lambda bi, i: (bi, i, 2)),
                  _full((1, GM_WIDTH)), _full(ws.shape), _full(bias.shape)],
        out_specs=out_specs, out_shape=out_shape,
        compiler_params=_cparams("parallel", "parallel"),
        name="gmlp",
    )(y_in, y_in, g_v.reshape(1, GM_WIDTH), ws, bias)
    return (res[0], res[1]) if want_vn else (res[0], None)


def _proj_res_kernel(x_ref, gt_ref, a_ref, b_ref, wa_ref, wb_ref, o_ref):
    mix = (jnp.dot(a_ref[...].astype(BF16), wa_ref[...], preferred_element_type=F32)
           + jnp.dot(b_ref[...].astype(BF16), wb_ref[...], preferred_element_type=F32))
    o_ref[...] = x_ref[...] + gt_ref[...] * mix


def _proj_residual(x, gt, a, bm, w, tl):
    b, l, d = x.shape
    ka = a.shape[-1]
    wa, wb = w[:ka].astype(BF16), w[ka:].astype(BF16)
    blk = lambda n: pl.BlockSpec((None, tl, n), lambda bi, i: (bi, i, 0))
    return pl.pallas_call(
        _proj_res_kernel,
        grid=(b, l // tl),
        in_specs=[blk(d), pl.BlockSpec((None, 1, d), lambda bi, i: (bi, 0, 0)), blk(ka), blk(bm.shape[-1]),
                  _full(wa.shape), _full(wb.shape)],
        out_specs=blk(d), out_shape=jax.ShapeDtypeStruct((b, l, d), F32),
        compiler_params=_cparams("parallel", "parallel"),
        name="proj_residual",
    )(x, gt, a, bm, wa, wb)


def _rope_tables(pos0, l):
    pos = (pos0 + jnp.arange(l, dtype=jnp.int32)).astype(F32)[:, None]
    lane = jnp.arange(LANES)

    def table(period, start, rot):
        half = rot // 2
        r = lane % period - start
        inside = (r >= 0) & (r < rot)
        k = jnp.where(inside, r % half, 0)
        inv = ROPE_THETA ** (-k.astype(F32) * 2.0 / rot)
        ang = pos * inv[None, :]
        cos = jnp.where(inside[None], jnp.cos(ang), 1.0)
        sgn = jnp.where(r < half, -1.0, 1.0)
        sin = jnp.where(inside[None], jnp.sin(ang) * sgn[None], 0.0)
        return cos, sin

    return table(DA_DH, 0, DA_ROPE), table(LANES, MLA_NOPE, MLA_ROPE), table(LANES, 0, MLA_ROPE)


def _rotate(x, cos, sin, period, start, rot):
    half = rot // 2
    n = x.shape[-1]
    reps = n // LANES
    lane = (lax.broadcasted_iota(jnp.int32, x.shape, 1) & (period - 1)) - start
    up = pltpu.roll(x, n - half, 1)
    dn = pltpu.roll(x, half, 1)
    partner = jnp.where(lane < half, up, dn)
    if reps > 1:
        cos = jnp.concatenate([cos] * reps, axis=1)
        sin = jnp.concatenate([sin] * reps, axis=1)
    return x * cos + partner * sin


def _odd_prep_kernel(q_ref, k_ref, cq_ref, ckv_ref, kpe_ref, cd_ref, sd_ref, cm_ref, sm_ref, ck_ref, sk_ref,
                     gq_ref, wuq_ref, gkv_ref, qd_ref, kd_ref, qm_ref, lat_ref, kpe_out_ref):
    cd, sd = cd_ref[...], sd_ref[...]
    qd_ref[...] = _rotate(q_ref[...], cd, sd, DA_DH, 0, DA_ROPE).astype(BF16)
    kd_ref[...] = _rotate(k_ref[...], cd, sd, DA_DH, 0, DA_ROPE)
    cq = cq_ref[...]
    cqn = cq * lax.rsqrt(jnp.mean(cq * cq, axis=-1, keepdims=True) + EPS) * gq_ref[...]
    qm = jnp.dot(cqn.astype(BF16), wuq_ref[...], preferred_element_type=F32)
    qm_ref[...] = _rotate(qm, cm_ref[...], sm_ref[...], LANES, MLA_NOPE, MLA_ROPE).astype(BF16)
    ckv = ckv_ref[...]
    lat_ref[...] = ckv * lax.rsqrt(jnp.mean(ckv * ckv, axis=-1, keepdims=True) + EPS) * gkv_ref[...]
    kpe = _rotate(kpe_ref[...], ck_ref[...], sk_ref[...], LANES, 0, MLA_ROPE)
    kpe_out_ref[...] = kpe[:, :MLA_ROPE]


def _odd_prep(y_in, pos0, g_q, w_uq, g_kv, tl):
    b, l, _ = y_in.shape
    (cd, sd), (cm, sm), (ck, sk) = _rope_tables(pos0, l)
    per = MLA_NOPE + MLA_ROPE
    wuq = jnp.pad(w_uq.reshape(MLA_Q_RANK, MLA_HEADS, per), ((0, 0), (0, 0), (0, LANES - per)))
    wuq = wuq.reshape(MLA_Q_RANK, MLA_HEADS * LANES).astype(BF16)
    col = lambda w, j: pl.BlockSpec((None, tl, w), lambda bi, i: (bi, i, j))
    tab = pl.BlockSpec((tl, LANES), lambda bi, i: (i, 0))
    out = lambda w: pl.BlockSpec((None, tl, w), lambda bi, i: (bi, i, 0))
    return pl.pallas_call(
        _odd_prep_kernel,
        grid=(b, l // tl),
        in_specs=[col(512, 0), col(512, 1), col(256, 6), col(128, 14), col(128, 15),
                  tab, tab, tab, tab, tab, tab,
                  _full((1, MLA_Q_RANK)), _full(wuq.shape), _full((1, MLA_KV_RANK))],
        out_specs=[out(512), out(512), out(512), out(MLA_KV_RANK), out(MLA_ROPE)],
        out_shape=[jax.ShapeDtypeStruct((b, l, 512), BF16), jax.ShapeDtypeStruct((b, l, 512), F32),
                   jax.ShapeDtypeStruct((b, l, 512), BF16), jax.ShapeDtypeStruct((b, l, MLA_KV_RANK), F32),
                   jax.ShapeDtypeStruct((b, l, MLA_ROPE), F32)],
        compiler_params=_cparams("parallel", "parallel"),
        name="odd_prep",
    )(y_in, y_in, y_in, y_in, y_in, cd, sd, cm, sm, ck, sk,
      g_q.reshape(1, MLA_Q_RANK), wuq, g_kv.reshape(1, MLA_KV_RANK))


def _mla_kv_kernel(lat_ref, kpe_ref, wk_ref, wv_ref, place_ref, k_ref, v_ref):
    lat = lat_ref[...].astype(BF16)
    k = (jnp.dot(lat, wk_ref[...], preferred_element_type=F32)
         + jnp.dot(kpe_ref[...].astype(BF16), place_ref[...], preferred_element_type=F32))
    k_ref[...] = k.astype(BF16)
    v_ref[...] = jnp.dot(lat, wv_ref[...], preferred_element_type=F32).astype(BF16)


def _mla_kv(lat_all, kpe_all, w_ukv, tk):
    b, k, _ = lat_all.shape
    per = MLA_NOPE + MLA_DV
    w3 = w_ukv.reshape(MLA_KV_RANK, MLA_HEADS, per)
    wk = jnp.pad(w3[:, :, :MLA_NOPE], ((0, 0), (0, 0), (0, LANES - MLA_NOPE))).reshape(MLA_KV_RANK, MLA_HEADS * LANES)
    wv = w3[:, :, MLA_NOPE:].reshape(MLA_KV_RANK, MLA_HEADS * MLA_DV)
    place = jnp.pad(jnp.eye(MLA_ROPE, dtype=F32), ((0, 0), (MLA_NOPE, LANES - MLA_NOPE - MLA_ROPE)))
    place = jnp.tile(place, (1, MLA_HEADS))
    blk = lambda w: pl.BlockSpec((None, tk, w), lambda bi, i: (bi, i, 0))
    return pl.pallas_call(
        _mla_kv_kernel,
        grid=(b, k // tk),
        in_specs=[blk(MLA_KV_RANK), blk(MLA_ROPE), _full(wk.shape), _full(wv.shape), _full(place.shape)],
        out_specs=[blk(512), blk(512)],
        out_shape=[jax.ShapeDtypeStruct((b, k, 512), BF16), jax.ShapeDtypeStruct((b, k, 512), BF16)],
        compiler_params=_cparams("parallel", "parallel"),
        name="mla_kv",
    )(lat_all, kpe_all, wk.astype(BF16), wv.astype(BF16), place.astype(BF16))


def _visible_limit(pos_last):
    shift = CHUNK.bit_length() - 1
    return ((pos_last >> shift) + 1) << shift


def _fold_lanes(x, op):
    n = x.shape[-1]
    if n % LANES:
        return x
    out = x[:, :LANES]
    for c in range(1, n // LANES):
        out = op(out, x[:, c * LANES:(c + 1) * LANES])
    return out


def _flash_kernel(lam_ref, q_ref, k_ref, v_ref, g_ref, o_ref, m_ref, l_ref, acc_ref, *,
                  nhm, dqk, maps, scale, pos0, tq, tk, nkv, out_scale):
    i, j = pl.program_id(1), pl.program_id(2)

    @pl.when(j == 0)
    def _():
        m_ref[...] = jnp.full(m_ref.shape, NEG, F32)
        l_ref[...] = jnp.zeros(l_ref.shape, F32)
        acc_ref[...] = jnp.zeros(acc_ref.shape, F32)

    c2 = scale * math.log2(math.e)
    visible = j * tk < _visible_limit(pos0 + (i + 1) * tq - 1)
    unmasked = (j + 1) * tk <= _visible_limit(pos0 + i * tq)

    def scores(hm):
        q = q_ref[:, hm * dqk:(hm + 1) * dqk].astype(BF16)
        k = k_ref[:, hm * dqk:(hm + 1) * dqk].astype(BF16)
        return lax.dot_general(q, k, (((1,), (1,)), ((), ())), preferred_element_type=F32)

    def process(masked):
        if masked:
            q_pos = pos0 + i * tq + lax.broadcasted_iota(jnp.int32, (tq, tk), 0)
            k_pos = j * tk + lax.broadcasted_iota(jnp.int32, (tq, tk), 1)
            mask = k_pos < _visible_limit(q_pos)
        s_next = scores(0)
        for hm in range(nhm):
            s = s_next
            if hm + 1 < nhm:
                s_next = scores(hm + 1)
            if masked:
                s = jnp.where(mask, s, NEG)
            v = v_ref[:, (hm // maps) * DA_DV:(hm // maps + 1) * DA_DV].astype(BF16)
            m_prev = m_ref[hm]
            m_new = jnp.maximum(m_prev, jnp.max(_fold_lanes(s, jnp.maximum), axis=-1, keepdims=True))
            p = jnp.exp2((s - m_new) * c2)
            alpha = jnp.exp2((m_prev - m_new) * c2)
            l_ref[hm] = alpha * l_ref[hm] + jnp.sum(_fold_lanes(p, jnp.add), axis=-1, keepdims=True)
            acc_ref[hm] = alpha * acc_ref[hm] + jnp.dot(p.astype(BF16), v, preferred_element_type=F32)
            m_ref[hm] = m_new

    pl.when(visible & unmasked)(lambda: process(False))
    pl.when(visible & jnp.logical_not(unmasked))(lambda: process(True))

    @pl.when(j == nkv - 1)
    def _():
        for h in range(nhm // maps):
            if maps == 2:
                o = acc_ref[2 * h] / l_ref[2 * h] - lam_ref[0] * (acc_ref[2 * h + 1] / l_ref[2 * h + 1])
                o = o * lax.rsqrt(jnp.mean(o * o, axis=-1, keepdims=True) + EPS) * g_ref[...] * out_scale
            else:
                o = acc_ref[h] / l_ref[h]
            o_ref[:, h * DA_DV:(h + 1) * DA_DV] = o


def _flash(q, k, v, lam, g_sub, *, nhm, dqk, maps, scale, pos0, tq, tk, out_scale, v_col=0):
    b, l, _ = q.shape
    kk = k.shape[1]
    nkv = kk // tk
    nh = nhm // maps

    def kv_idx(bi, i, j):
        last = (_visible_limit(pos0 + (i + 1) * tq - 1) - 1) // tk
        return jnp.minimum(j, last)

    return pl.pallas_call(
        functools.partial(_flash_kernel, nhm=nhm, dqk=dqk, maps=maps, scale=scale, pos0=pos0, tq=tq, tk=tk,
                          nkv=nkv, out_scale=out_scale),
        grid=(b, l // tq, nkv),
        in_specs=[pl.BlockSpec(memory_space=pltpu.SMEM),
                  pl.BlockSpec((None, tq, nhm * dqk), lambda bi, i, j: (bi, i, 0)),
                  pl.BlockSpec((None, tk, nhm * dqk), lambda bi, i, j: (bi, kv_idx(bi, i, j), 0)),
                  pl.BlockSpec((None, tk, nh * DA_DV), lambda bi, i, j: (bi, kv_idx(bi, i, j), v_col)),
                  _full((1, DA_DV))],
        out_specs=pl.BlockSpec((None, tq, nh * DA_DV), lambda bi, i, j: (bi, i, 0)),
        out_shape=jax.ShapeDtypeStruct((b, l, nh * DA_DV), F32),
        scratch_shapes=[pltpu.VMEM((nhm, tq, 1), F32), pltpu.VMEM((nhm, tq, 1), F32),
                        pltpu.VMEM((nhm, tq, DA_DV), F32)],
        compiler_params=_cparams("parallel", "parallel", "arbitrary"),
        name="flash_da" if maps == 2 else "flash_mla",
    )(lam, q, k, v, g_sub)


def _top16(s, rid, payload, val_ref, pay_ref):
    big = float(s.shape[0])
    for it in range(PEER_TOPK):
        m = jnp.max(s, axis=0, keepdims=True)
        am = jnp.min(jnp.where(s == m, rid, big), axis=0, keepdims=True)
        sel = rid == am
        val_ref[it:it + 1, :] = m
        if payload is None:
            pay_ref[it:it + 1, :] = am
        else:
            pay_ref[it:it + 1, :] = jnp.max(jnp.where(sel, payload, -1.0), axis=0, keepdims=True)
        s = jnp.where(sel, -jnp.inf, s)


_PEER_PAIRS = [(a, b) for a in range(PEER_TOPK) for b in range(PEER_TOPK) if (a + 1) * (b + 1) <= PEER_TOPK]
_PEER_CAND_ROWS = -(-len(_PEER_PAIRS) // 8) * 8


def _route_kernel(q_ref, k1_ref, k2_ref, idx_ref, g_ref, v1_ref, i1_ref, v2_ref, i2_ref, vc_ref, ic_ref,
                  cand_ref, cidx_ref):
    tt = q_ref.shape[0]
    half = q_ref.shape[1] // 2
    nt = (((1,), (1,)), ((), ()))
    s1 = lax.dot_general(k1_ref[...].astype(BF16), q_ref[:, :half].astype(BF16), nt, preferred_element_type=F32)
    s2 = lax.dot_general(k2_ref[...].astype(BF16), q_ref[:, half:].astype(BF16), nt, preferred_element_type=F32)
    rid = lax.broadcasted_iota(jnp.int32, (PEER_NKEYS, tt), 0).astype(F32)
    _top16(s1, rid, None, v1_ref, i1_ref)
    _top16(s2, rid, None, v2_ref, i2_ref)
    npairs = len(_PEER_PAIRS)
    cand_ref[npairs:, :] = jnp.full((_PEER_CAND_ROWS - npairs, tt), -jnp.inf, F32)
    cidx_ref[npairs:, :] = jnp.full((_PEER_CAND_ROWS - npairs, tt), -1.0, F32)
    for r, (a, b) in enumerate(_PEER_PAIRS):
        cand_ref[r:r + 1, :] = v1_ref[a:a + 1, :] + v2_ref[b:b + 1, :]
        cidx_ref[r:r + 1, :] = i1_ref[a:a + 1, :] * float(PEER_NKEYS) + i2_ref[b:b + 1, :]
    rid2 = lax.broadcasted_iota(jnp.int32, (_PEER_CAND_ROWS, tt), 0).astype(F32)
    _top16(cand_ref[...], rid2, cidx_ref[...], vc_ref, ic_ref)
    sc = vc_ref[...]
    e = jnp.exp(sc - sc[0:1, :])
    g_ref[...] = e / jnp.sum(e, axis=0, keepdims=True)
    idx_ref[...] = ic_ref[...].astype(jnp.int32)


def _peer_route(q, k1, k2):
    t = q.shape[0]
    tt = PEER_TT
    dk = 2 * PEER_NKEYS
    kspec = pl.BlockSpec((None, PEER_NKEYS, dk // 2), lambda i, h: (h, 0, 0))
    ospec = pl.BlockSpec((None, PEER_TOPK, tt), lambda i, h: (i, h, 0))
    sm = pltpu.VMEM((PEER_TOPK, tt), F32)
    return pl.pallas_call(
        _route_kernel,
        grid=(t // tt, PEER_HEADS),
        in_specs=[pl.BlockSpec((tt, dk), lambda i, h: (i, h)), kspec, kspec],
        out_specs=[ospec, ospec],
        out_shape=[jax.ShapeDtypeStruct((t // tt, PEER_HK, tt), jnp.int32),
                   jax.ShapeDtypeStruct((t // tt, PEER_HK, tt), F32)],
        scratch_shapes=[sm, sm, sm, sm, sm, sm,
                        pltpu.VMEM((_PEER_CAND_ROWS, tt), F32), pltpu.VMEM((_PEER_CAND_ROWS, tt), F32)],
        compiler_params=_cparams("parallel", "parallel"),
        name="peer_route",
    )(q, k1, k2)


def _pack_table(tab):
    e, d = tab.shape
    tb = lax.bitcast_convert_type(tab.astype(BF16), jnp.uint16).astype(jnp.uint32)
    words = tb[:, :d // 2] | (tb[:, d // 2:] << 16)
    return lax.bitcast_convert_type(words, F32)


def _sc_unpack(words):
    bits = lax.bitcast_convert_type(words, jnp.int32)
    return lax.bitcast_convert_type(bits << 16, F32), lax.bitcast_convert_type(bits & jnp.int32(-65536), F32)


def _sc_gelu(x):
    u = math.sqrt(2.0 / math.pi) * (x + 0.044715 * (x * x * x))
    tanh_u = 1.0 - 2.0 / (jnp.exp(2.0 * u) + 1.0)
    return 0.5 * x * (1.0 + tanh_u)


def _sc_peer(u_tab, v_tab, idx_tm, g_tm, h, side_idx=None):
    t, nk = idx_tm.shape
    w = u_tab.shape[1]
    d = 2 * w
    half_rows = SC_GATHER_ROWS
    nbuf = SC_BUFFERS
    per_tok = nk // half_rows
    mesh = plsc.VectorSubcoreMesh(core_axis_name="core", subcore_axis_name="subcore")
    n_workers = mesh.num_cores * mesh.num_subcores
    tpw = t // n_workers
    tb = SC_TOKEN_BLOCK
    steps = per_tok * tb
    dot_rows = SC_LANES
    sum_rows = 8
    assert t % n_workers == 0 and tpw % tb == 0 and w % SC_LANES == 0
    assert nk % half_rows == 0 and half_rows % SC_LANES == 0 and steps % nbuf == 0

    n_side = 0 if side_idx is None else side_idx.shape[0]
    side_w = SC_SIDE_ROWS
    lw = n_side // n_workers
    n_win = lw // side_w
    blk_side = steps * side_w
    if n_side:
        assert nbuf == 2 and steps % 2 == 0 and n_side % (n_workers * blk_side) == 0
        assert n_win + 2 <= (tpw // tb) * steps

    def lanes(start):
        return pl.ds(pl.multiple_of(start, SC_LANES), SC_LANES)

    y_type = jax.ShapeDtypeStruct((t, d), F32)
    side_type = jax.ShapeDtypeStruct((n_side, w), F32)
    scratch = [pltpu.VMEM((tb * nk,), jnp.int32), pltpu.VMEM((tb, d), F32), pltpu.VMEM((tb, nk), F32),
               pltpu.VMEM((nbuf, half_rows, w), F32), pltpu.VMEM((tb, nk), F32), pltpu.VMEM((tb, d), F32),
               pltpu.SemaphoreType.DMA((nbuf,))]
    if n_side:
        scratch += [pltpu.VMEM((2 * blk_side,), jnp.int32), pltpu.VMEM((2, 2, side_w, w), F32),
                    pltpu.SemaphoreType.DMA((8,))]

    @functools.partial(
        pl.kernel, out_type=[y_type, side_type, side_type] if n_side else y_type, mesh=mesh, scratch_types=scratch,
        compiler_params=pltpu.CompilerParams(needs_layout_passes=False))
    def peer_kernel(u_hbm, v_hbm, idx_hbm, g_hbm, h_hbm, *refs):
        if n_side:
            (sidx_hbm, y_hbm, gu_hbm, gv_hbm, idx_v, h_v, g_v, rows_v, coef_v, y_v, sems,
             sidx_v, sbuf, ssems) = refs
        else:
            y_hbm, idx_v, h_v, g_v, rows_v, coef_v, y_v, sems = refs
        wid = lax.axis_index("subcore") * mesh.num_cores + lax.axis_index("core")
        lane = lax.iota(jnp.int32, SC_LANES)

        def gather(tab_hbm, step, buf):
            rows = idx_v.at[pl.ds(pl.multiple_of(step * half_rows, half_rows), half_rows)]
            return pltpu.make_async_copy(tab_hbm.at[rows], rows_v.at[buf], sems.at[buf])

        def prime(tab_hbm):
            for j in range(nbuf - 1):
                gather(tab_hbm, j, j).start()

        def side_in(tab_hbm, tab, n, par):
            slot = ((n // steps) % 2) * blk_side + (n % steps) * side_w
            rows = sidx_v.at[pl.ds(pl.multiple_of(slot, side_w), side_w)]
            return pltpu.make_async_copy(tab_hbm.at[rows], sbuf.at[tab, par], ssems.at[tab * 4 + par])

        def side_out(out_hbm, tab, n, par):
            dst = out_hbm.at[pl.ds(pl.multiple_of(wid * lw + n * side_w, side_w), side_w)]
            return pltpu.make_async_copy(sbuf.at[tab, par], dst, ssems.at[tab * 4 + 2 + par])

        def side_step(tab_hbm, out_hbm, tab, n, par):
            @pl.when((n >= 2) & (n < n_win + 2))
            def _():
                side_out(out_hbm, tab, n - 2, par).wait()

            @pl.when(n < n_win)
            def _():
                side_in(tab_hbm, tab, n, par).start()

            @pl.when((n >= 1) & (n < n_win + 1))
            def _():
                side_in(tab_hbm, tab, n - 1, 1 - par).wait()
                side_out(out_hbm, tab, n - 1, 1 - par).start()

        def run(tab_hbm, work, blk=None, side=None):
            @pl.loop(0, steps, step=nbuf)
            def _(s):
                for j in range(nbuf):
                    if side is not None:
                        side_step(tab_hbm, side[0], side[1], blk * steps + s + j, j)
                    ahead = s + j + nbuf - 1

                    @pl.when(ahead < steps)
                    def _():
                        gather(tab_hbm, ahead, (j + nbuf - 1) % nbuf).start()

                    gather(tab_hbm, s + j, j).wait()
                    work(s + j, j)

        def dots(step, buf):
            tok = step // per_tok
            col0 = (step % per_tok) * half_rows

            def group(q, _):
                r0 = q * dot_rows

                def chunk(c, accs):
                    h_lo = h_v[tok, lanes(c * SC_LANES)]
                    h_hi = h_v[tok, lanes(w + c * SC_LANES)]
                    out = []
                    for r in range(dot_rows):
                        lo, hi = _sc_unpack(rows_v[buf, r0 + r, lanes(c * SC_LANES)])
                        out.append(accs[r] + lo * h_lo + hi * h_hi)
                    return tuple(out)

                accs = lax.fori_loop(0, w // SC_LANES, chunk, (jnp.zeros((SC_LANES,), F32),) * dot_rows)
                vec = jnp.zeros((SC_LANES,), F32)
                for r in range(dot_rows):
                    vec = jnp.where(lane == r, jnp.sum(accs[r]), vec)
                coef_v[tok, lanes(col0 + r0)] = vec
                return 0

            lax.fori_loop(0, half_rows // dot_rows, group, 0)

        def gates():
            @pl.loop(0, tb)
            def _(tok):
                @pl.loop(0, nk // SC_LANES)
                def _(q):
                    ds = lanes(q * SC_LANES)
                    coef_v[tok, ds] = g_v[tok, ds] * _sc_gelu(coef_v[tok, ds])

        def weighted_sum(step, buf):
            tok = step // per_tok
            col0 = (step % per_tok) * half_rows

            @pl.when(col0 == 0)
            def _():
                @plsc.parallel_loop(0, d // SC_LANES)
                def _(c):
                    y_v[tok, lanes(c * SC_LANES)] = jnp.zeros((SC_LANES,), F32)

            def group(g, _):
                cvec = coef_v[tok, lanes(col0 + (g // 2) * SC_LANES)]
                base = (g % 2) * sum_rows
                cs = [cvec.at[jnp.full((SC_LANES,), base + r, jnp.int32)].get(mode="promise_in_bounds")
                      for r in range(sum_rows)]
                r0 = g * sum_rows

                @plsc.parallel_loop(0, w // SC_LANES, unroll=2)
                def _(c):
                    los, his = [], []
                    for r in range(sum_rows):
                        lo, hi = _sc_unpack(rows_v[buf, r0 + r, lanes(c * SC_LANES)])
                        los.append(cs[r] * lo)
                        his.append(cs[r] * hi)
                    while len(los) > 1:
                        los = [a + b for a, b in zip(los[::2], los[1::2])]
                        his = [a + b for a, b in zip(his[::2], his[1::2])]
                    y_v[tok, lanes(c * SC_LANES)] = y_v[tok, lanes(c * SC_LANES)] + los[0]
                    y_v[tok, lanes(w + c * SC_LANES)] = y_v[tok, lanes(w + c * SC_LANES)] + his[0]

                return 0

            lax.fori_loop(0, half_rows // sum_rows, group, 0)

        @pl.loop(0, tpw // tb)
        def _(blk):
            tok0 = wid * tpw + blk * tb
            pltpu.sync_copy(idx_hbm.at[pl.ds(pl.multiple_of(tok0 * nk, nk), tb * nk)], idx_v)
            prime(u_hbm)
            pltpu.sync_copy(h_hbm.at[pl.ds(tok0, tb)], h_v)
            pltpu.sync_copy(g_hbm.at[pl.ds(tok0, tb)], g_v)
            if n_side:
                @pl.when(blk * steps < n_win)
                def _():
                    src = sidx_hbm.at[pl.ds(pl.multiple_of(wid * lw + blk * blk_side, blk_side), blk_side)]
                    pltpu.sync_copy(src, sidx_v.at[pl.ds(pl.multiple_of((blk % 2) * blk_side, blk_side), blk_side)])

            run(u_hbm, dots, blk, (gu_hbm, 0) if n_side else None)
            prime(v_hbm)
            gates()
            run(v_hbm, weighted_sum, blk, (gv_hbm, 1) if n_side else None)
            pltpu.sync_copy(y_v, y_hbm.at[pl.ds(tok0, tb)])

    if n_side:
        return peer_kernel(u_tab, v_tab, idx_tm.reshape(t * nk), g_tm, h, side_idx)
    return peer_kernel(u_tab, v_tab, idx_tm.reshape(t * nk), g_tm, h)


def _unpack(words):
    w = pltpu.bitcast(words, jnp.uint32)
    lo = pltpu.bitcast(w << 16, F32)
    hi = pltpu.bitcast(w & jnp.uint32(0xFFFF0000), F32)
    return lo, hi


def _peer_act_kernel(h_ref, gu_ref, g_ref, coef_ref, act_ref):
    c = pl.program_id(1)
    half = h_ref.shape[1] // 2
    h_lo, h_hi = h_ref[:, :half], h_ref[:, half:]
    lane = lax.broadcasted_iota(jnp.int32, act_ref.shape, 1)

    @pl.when(c == 0)
    def _():
        act_ref[...] = jnp.zeros(act_ref.shape, F32)

    act = act_ref[...]
    for k in range(PEER_HKC):
        lo, hi = _unpack(gu_ref[k])
        a = jnp.sum(lo * h_lo + hi * h_hi, axis=-1, keepdims=True)
        act = jnp.where(lane == c * PEER_HKC + k, a, act)
    act_ref[...] = act

    @pl.when(c == pl.num_programs(1) - 1)
    def _():
        coef_ref[...] = g_ref[...].T * jax.nn.gelu(act)


def _peer_act(h, gu, g):
    t, d = h.shape
    tt = PEER_TT
    nt = t // tt
    return pl.pallas_call(
        _peer_act_kernel,
        grid=(nt, PEER_HK // PEER_HKC),
        in_specs=[pl.BlockSpec((tt, d), lambda i, c: (i, 0)),
                  pl.BlockSpec((None, PEER_HKC, tt, d // 2), lambda i, c: (i, c, 0, 0)),
                  pl.BlockSpec((None, PEER_HK, tt), lambda i, c: (i, 0, 0))],
        out_specs=pl.BlockSpec((None, tt, PEER_HK), lambda i, c: (i, 0, 0)),
        out_shape=jax.ShapeDtypeStruct((nt, tt, PEER_HK), F32),
        scratch_shapes=[pltpu.VMEM((tt, PEER_HK), F32)],
        compiler_params=_cparams("parallel", "arbitrary"),
        name="peer_act",
    )(h, gu, g)


def _peer_mix_kernel(coef_ref, gv_ref, y_ref):
    c = pl.program_id(1)
    half = y_ref.shape[1] // 2

    @pl.when(c == 0)
    def _():
        y_ref[...] = jnp.zeros(y_ref.shape, F32)

    coef = coef_ref[...]
    lane = lax.broadcasted_iota(jnp.int32, coef.shape, 1)
    acc_lo, acc_hi = y_ref[:, :half], y_ref[:, half:]
    for k in range(PEER_HKC):
        ck = jnp.sum(jnp.where(lane == c * PEER_HKC + k, coef, 0.0), axis=-1, keepdims=True)
        lo, hi = _unpack(gv_ref[k])
        acc_lo = acc_lo + ck * lo
        acc_hi = acc_hi + ck * hi
    y_ref[:, :half] = acc_lo
    y_ref[:, half:] = acc_hi


def _peer_mix(coef, gv):
    nt, tt, _ = coef.shape
    w = gv.shape[-1]
    return pl.pallas_call(
        _peer_mix_kernel,
        grid=(nt, PEER_HK // PEER_HKC),
        in_specs=[pl.BlockSpec((None, tt, PEER_HK), lambda i, c: (i, 0, 0)),
                  pl.BlockSpec((None, PEER_HKC, tt, w), lambda i, c: (i, c, 0, 0))],
        out_specs=pl.BlockSpec((tt, 2 * w), lambda i, c: (i, 0)),
        out_shape=jax.ShapeDtypeStruct((nt * tt, 2 * w), F32),
        compiler_params=_cparams("parallel", "arbitrary"),
        name="peer_mix",
    )(coef, gv)


def _peer_out_kernel(x_ref, gt_ref, y_ref, gf_ref, o_ref, *, final):
    x = x_ref[...] + gt_ref[...] * y_ref[...]
    if final:
        x = x * lax.rsqrt(jnp.mean(x * x, axis=-1, keepdims=True) + EPS) * gf_ref[...]
    o_ref[...] = x


def _peer_out(x, gt, y, g_final, final):
    t, d = x.shape
    tt = PEER_TT
    nt = t // tt
    row = pl.BlockSpec((tt, d), lambda i: (i, 0))
    if gt.ndim == 3:
        per_seq = nt // gt.shape[0]
        gt_spec = pl.BlockSpec((None, 1, d), lambda i: (i // per_seq, 0, 0))
    else:
        gt_spec = row
    return pl.pallas_call(
        functools.partial(_peer_out_kernel, final=final),
        grid=(nt,),
        in_specs=[row, gt_spec, row, _full((1, d))],
        out_specs=row, out_shape=jax.ShapeDtypeStruct((t, d), F32),
        compiler_params=_cparams("parallel"),
        name="peer_out",
    )(x, gt, y, g_final.reshape(1, d))


def _peer(x, g_ffn, sc2, sh2, gt2, w_q, k1, k2, u_pack, v_pack, g_final, final, tl):
    b, l, d = x.shape
    t = b * l
    q, h = _norm_mod_matmul(x, g_ffn, sc2, sh2, w_q, tl, want_h=True)
    idx, g = _peer_route(q.reshape(t, -1), k1, k2)
    h = h.reshape(t, d)
    nt = t // PEER_TT
    n_tc = nt * PEER_TC_TILES // 32
    n_sc = nt - n_tc
    t_sc = n_sc * PEER_TT
    token_major = lambda a: jnp.transpose(a, (0, 2, 1)).reshape(-1, PEER_HK)
    if n_tc:
        y_sc, gu, gv = _sc_peer(u_pack, v_pack, token_major(idx[:n_sc]), token_major(g[:n_sc]), h[:t_sc],
                                idx[n_sc:].reshape(-1))
        rows = lambda a: a.reshape(n_tc, PEER_HK, PEER_TT, d // 2)
        y_tc = _peer_mix(_peer_act(h[t_sc:], rows(gu), g[n_sc:]), rows(gv))
        y = jnp.concatenate([y_sc, y_tc], axis=0)
    else:
        y = _sc_peer(u_pack, v_pack, token_major(idx), token_major(g), h)
    if l % PEER_TT:
        gt2 = jnp.broadcast_to(gt2, (b, l, d)).reshape(t, d)
    return _peer_out(x.reshape(t, d), gt2, y, g_final, final).reshape(b, l, d)


def _trunk(x, mod, pos0, past, p, tl):
    b, l, d = x.shape
    outs = {}
    for layer in range(DEPTH):
        sh1, sc1, gt1, sh2, sc2, gt2 = [m[:, None, :] for m in jnp.split(mod[layer], 6, axis=-1)]
        peer = functools.partial(
            _peer, g_ffn=p['g_ffn'][layer], sc2=sc2, sh2=sh2, gt2=gt2, w_q=p['peer_w_q'][layer].astype(BF16),
            k1=p['peer_k1'][layer], k2=p['peer_k2'][layer], u_pack=p['u_pack'][layer], v_pack=p['v_pack'][layer],
            g_final=p['g_final'], final=layer == DEPTH - 1, tl=tl)
        if layer % 2 == 0:
            e = layer // 2
            if past is None:
                hr = jnp.zeros((b, S5_GROUPS, S5_STATE), F32)
                hi = jnp.zeros((b, S5_GROUPS, S5_STATE), F32)
            else:
                hr, hi = past['s5_re'][e], past['s5_im'][e]
            n_seg = EVEN_SEGMENTS if l % (EVEN_SEGMENTS * max(tl, GM_CHUNK, PEER_TT)) == 0 else 1
            segs = []
            for xs in jnp.split(x, n_seg, axis=1):
                y_in = _norm_mod_matmul(xs, p['g_mix'][layer], sc1, sh1, p['w_in_e'][e].astype(BF16), tl)
                ya, hr, hi = _s5(y_in, hr, hi, p['s5'][e], p['s5_d'][e], p['w_glu'][e], p['b_glu'][e], tl)
                yb, vn = _gmlp(y_in, p['gm_g_v'][e], p['gm_w_s'][e], p['gm_b_s'][e], tl, want_vn=past is not None)
                segs.append(peer(_proj_residual(xs, gt1, ya, yb, p['w_out_e'][e], tl)))
            x = segs[0] if n_seg == 1 else jnp.concatenate(segs, axis=1)
            outs.update(s5_re=hr, s5_im=hi, gm_v=vn)
        else:
            o = layer // 2
            w_in = jnp.pad(p['w_in_o'][o], ((0, 0), (0, ODD_IN_PAD - p['w_in_o'][o].shape[1]))).astype(BF16)
            y_in = _norm_mod_matmul(x, p['g_mix'][layer], sc1, sh1, w_in, tl)
            qd, kd, qm, lat, kpe = _odd_prep(y_in, pos0, p['mla_g_q'][o], p['mla_w_uq'][o], p['mla_g_kv'][o], tl)
            v_new = y_in[:, :, 2 * 512:3 * 512]
            lam_init = 0.8 - 0.6 * math.exp(-0.3 * layer)
            lam = (jnp.exp(jnp.sum(p['da_lq1'][o] * p['da_lk1'][o])) - jnp.exp(jnp.sum(p['da_lq2'][o] * p['da_lk2'][o]))
                   + lam_init).reshape(1).astype(F32)
            g_sub = p['da_g_sub'][o].reshape(1, DA_DV)
            if past is None:
                k_all, v_all, v_col, lat_all, kpe_all = kd, y_in, 2, lat, kpe
                tq = tk = tl
            else:
                k_all = jnp.concatenate([past['da_k'][o].reshape(b, -1, 512), kd], axis=1)
                v_all = jnp.concatenate([past['da_v'][o].reshape(b, -1, 512), v_new], axis=1)
                lat_all = jnp.concatenate([past['mla_lat'][o], lat], axis=1)
                kpe_all = jnp.concatenate([past['mla_kpe'][o], kpe], axis=1)
                v_col, tq, tk = 0, l, k_all.shape[1]
            yc = _flash(qd, k_all, v_all, lam, g_sub, nhm=2 * DA_HEADS, dqk=DA_DH, maps=2, scale=DA_DH ** -0.5,
                        pos0=pos0, tq=tq, tk=tk, out_scale=1.0 - lam_init, v_col=v_col)
            km, vm = _mla_kv(lat_all, kpe_all, p['mla_w_ukv'][o], tk)
            yd = _flash(qm, km, vm, lam, g_sub, nhm=MLA_HEADS, dqk=LANES, maps=1,
                        scale=(MLA_NOPE + MLA_ROPE) ** -0.5, pos0=pos0, tq=tq, tk=tk, out_scale=1.0)
            x = peer(_proj_residual(x, gt1, yc, yd, p['w_out_o'][o], tl))
            outs.update(da_k=kd.reshape(b, l, DA_HEADS, 2 * DA_DH), da_v=v_new.reshape(b, l, DA_HEADS, DA_DV),
                        mla_lat=lat, mla_kpe=kpe)
    return x, outs


def kernel(x_prompt, x_sample, c_prompt, c_sample, state_s5_re, state_s5_im, cache_da_k, cache_da_v, cache_mla_latent, cache_mla_kpe, w_ada, b_ada, g_mix, g_ffn, g_final, w_in_e, w_out_e, s5_lam_re, s5_lam_im, s5_log_dt, s5_b_re, s5_b_im, s5_c_re, s5_c_im, s5_d, w_glu, b_glu, gm_g_v, gm_w_s, gm_b_s, w_in_o, w_out_o, da_lq1, da_lk1, da_lq2, da_lk2, da_g_sub, mla_g_q, mla_w_uq, mla_g_kv, mla_w_ukv, peer_w_q, peer_k1, peer_k2, peer_u, peer_v):
    p = dict(g_mix=g_mix, g_ffn=g_ffn, g_final=g_final, w_in_e=w_in_e, w_out_e=w_out_e, s5_d=s5_d, w_glu=w_glu,
             b_glu=b_glu, gm_g_v=gm_g_v, gm_w_s=gm_w_s, gm_b_s=gm_b_s, w_in_o=w_in_o, w_out_o=w_out_o,
             da_lq1=da_lq1, da_lk1=da_lk1, da_lq2=da_lq2, da_lk2=da_lk2, da_g_sub=da_g_sub, mla_g_q=mla_g_q,
             mla_w_uq=mla_w_uq, mla_g_kv=mla_g_kv, mla_w_ukv=mla_w_ukv, peer_w_q=peer_w_q, peer_k1=peer_k1,
             peer_k2=peer_k2)
    n_even = (DEPTH + 1) // 2
    p['s5'] = [_s5_params(s5_lam_re[e], s5_lam_im[e], s5_log_dt[e], s5_b_re[e], s5_b_im[e], s5_c_re[e], s5_c_im[e])
               for e in range(n_even)]
    p['u_pack'] = [_pack_table(peer_u[layer]) for layer in range(DEPTH)]
    p['v_pack'] = [_pack_table(peer_v[layer]) for layer in range(DEPTH)]
    past = dict(s5_re=state_s5_re, s5_im=state_s5_im, da_k=cache_da_k, da_v=cache_da_v,
                mla_lat=cache_mla_latent, mla_kpe=cache_mla_kpe)
    nb = x_prompt.shape[0]
    mod = _ada(jnp.concatenate([c_prompt, c_sample], axis=0), w_ada, b_ada)
    past_len = cache_da_k.shape[2]
    parts = [_trunk(x_prompt[s:s + 1], mod[:, s:s + 1], 0, None, p, tl=PROMPT_TILE) for s in range(nb)]
    y_p = jnp.concatenate([y for y, _ in parts], axis=0)
    o_p = {k: jnp.concatenate([o[k] for _, o in parts], axis=0) for k in parts[0][1] if parts[0][1][k] is not None}
    y_s, o_s = _trunk(x_sample, mod[:, nb:], past_len, past, p, tl=x_sample.shape[1])
    st = lambda a: a[None]
    return (y_p, y_s, st(o_p['s5_re']), st(o_p['s5_im']), st(o_s['s5_re']), st(o_s['s5_im']), st(o_s['gm_v']),
            st(o_p['da_k']), st(o_p['da_v']), st(o_s['da_k']), st(o_s['da_v']),
            st(o_p['mla_lat']), st(o_p['mla_kpe']), st(o_s['mla_lat']), st(o_s['mla_kpe']))
```

```python
import functools
import math

import jax
import jax.numpy as jnp
from jax import lax
from jax.experimental import pallas as pl
from jax.experimental.pallas import tpu as pltpu
from jax.experimental.pallas import tpu_sc as plsc

F32 = jnp.float32
BF16 = jnp.bfloat16

D_MODEL = 1024
DEPTH = 2
CHUNK = 64
ROPE_THETA = 500000.0
EPS = 1e-6
NEG = -1e30

S5_WIDTH = 512
S5_GROUP = 16
S5_GROUPS = 32
S5_STATE = 64
GM_WIDTH = 512
GM_HEADS = 4
GM_CHUNK = 128
DA_HEADS = 4
DA_DH = 64
DA_DV = 128
DA_ROPE = 16
MLA_HEADS = 4
MLA_Q_RANK = 256
MLA_KV_RANK = 128
MLA_NOPE = 64
MLA_ROPE = 32
MLA_DV = 128
PEER_HEADS = 8
PEER_NKEYS = 128
PEER_TOPK = 16
PEER_HK = PEER_HEADS * PEER_TOPK

LANES = 128
VMEM_LIMIT = 48 << 20
PROMPT_TILE = 512
ODD_IN_PAD = 2048
EVEN_SEGMENTS = 2
PEER_TT = 256
SC_LANES = 16
SC_TOKEN_BLOCK = 8
SC_GATHER_ROWS = 64
SC_BUFFERS = 2
SC_SIDE_ROWS = 16
PEER_TC_TILES = 6
PEER_HKC = 16

_HI = lax.Precision.HIGHEST


def _cparams(*sem):
    return pltpu.CompilerParams(dimension_semantics=sem, vmem_limit_bytes=VMEM_LIMIT)


def _full(shape):
    n = len(shape)
    return pl.BlockSpec(shape, lambda *_: (0,) * n)


def _ada_kernel(c_ref, w_ref, b_ref, o_ref):
    c = c_ref[...]
    s = c * jax.nn.sigmoid(c)
    o_ref[...] = jnp.dot(s, w_ref[...], precision=_HI, preferred_element_type=F32) + b_ref[...]


def _ada(c_all, w_ada, b_ada):
    r = c_all.shape[0]
    tn = 1536
    return pl.pallas_call(
        _ada_kernel,
        grid=(DEPTH, 6 * D_MODEL // tn),
        in_specs=[_full((r, D_MODEL)),
                  pl.BlockSpec((None, D_MODEL, tn), lambda l, j: (l, 0, j)),
                  pl.BlockSpec((None, 1, tn), lambda l, j: (l, 0, j))],
        out_specs=pl.BlockSpec((None, r, tn), lambda l, j: (l, 0, j)),
        out_shape=jax.ShapeDtypeStruct((DEPTH, r, 6 * D_MODEL), F32),
        compiler_params=_cparams("parallel", "parallel"),
        name="ada",
    )(c_all, w_ada, b_ada.reshape(DEPTH, 1, 6 * D_MODEL))


def _nmm_kernel(x_ref, g_ref, sc_ref, sh_ref, w_ref, o_ref, *h_ref):
    x = x_ref[...]
    ms = jnp.mean(x * x, axis=-1, keepdims=True)
    h = x * lax.rsqrt(ms + EPS) * g_ref[...] * (1.0 + sc_ref[...]) + sh_ref[...]
    o_ref[...] = jnp.dot(h.astype(BF16), w_ref[...], preferred_element_type=F32)
    if h_ref:
        h_ref[0][...] = h


def _norm_mod_matmul(x, g, sc, sh, w, tl, want_h=False):
    b, l, d = x.shape
    n = w.shape[1]
    row = pl.BlockSpec((None, 1, d), lambda bi, i: (bi, 0, 0))
    out_shape = [jax.ShapeDtypeStruct((b, l, n), F32)]
    out_specs = [pl.BlockSpec((None, tl, n), lambda bi, i: (bi, i, 0))]
    if want_h:
        out_shape.append(jax.ShapeDtypeStruct((b, l, d), F32))
        out_specs.append(pl.BlockSpec((None, tl, d), lambda bi, i: (bi, i, 0)))
    res = pl.pallas_call(
        _nmm_kernel,
        grid=(b, l // tl),
        in_specs=[pl.BlockSpec((None, tl, d), lambda bi, i: (bi, i, 0)), _full((1, d)), row, row, _full((d, n))],
        out_specs=out_specs, out_shape=out_shape,
        compiler_params=_cparams("parallel", "parallel"),
        name="norm_mod_matmul",
    )(x, g.reshape(1, d), sc, sh, w)
    return res if want_h else res[0]


def _s5_kernel(xa_ref, h0r_ref, h0i_ref, bblk_ref, ar_ref, ai_ref, cr_ref, ci_ref, d_ref, wglu_ref, bglu_ref,
               ya_ref, hr_out_ref, hi_out_ref, bur, bui, st_r, st_i, *, tl):
    nj = S5_WIDTH // LANES
    half = 8 * S5_STATE

    @pl.when(pl.program_id(1) == 0)
    def _():
        st_r[...] = h0r_ref[...]
        st_i[...] = h0i_ref[...]

    xa = xa_ref[...]
    for j in range(nj):
        bu = jnp.dot(xa[:, j * LANES:(j + 1) * LANES], bblk_ref[j], precision=_HI, preferred_element_type=F32)
        bur[j] = bu[:, :half]
        bui[j] = bu[:, half:]

    for j0 in range(0, nj, 2):
        js = (j0, j0 + 1)
        a_r = [ar_ref[j] for j in js]
        a_i = [ai_ref[j] for j in js]

        def body(k, carry, js=js, a_r=a_r, a_i=a_i):
            carry = list(carry)
            base = pl.multiple_of(k * 8, 8)
            for s in range(8):
                t = base + s
                for q, j in enumerate(js):
                    hr, hi = carry[2 * q], carry[2 * q + 1]
                    nhr = a_r[q] * hr - a_i[q] * hi + bur[j, pl.ds(t, 1), :]
                    nhi = a_r[q] * hi + a_i[q] * hr + bui[j, pl.ds(t, 1), :]
                    bur[j, pl.ds(t, 1), :] = nhr
                    bui[j, pl.ds(t, 1), :] = nhi
                    carry[2 * q], carry[2 * q + 1] = nhr, nhi
            return tuple(carry)

        init = (st_r[js[0]], st_i[js[0]], st_r[js[1]], st_i[js[1]])
        fin = lax.fori_loop(0, tl // 8, body, init)
        st_r[js[0]], st_i[js[0]], st_r[js[1]], st_i[js[1]] = fin

    hr_out_ref[...] = st_r[...]
    hi_out_ref[...] = st_i[...]

    ys = []
    for j in range(nj):
        y = (jnp.dot(bur[j].astype(BF16), cr_ref[j], preferred_element_type=F32)
             + jnp.dot(bui[j].astype(BF16), ci_ref[j], preferred_element_type=F32))
        ys.append(y + d_ref[:, j * LANES:(j + 1) * LANES] * xa[:, j * LANES:(j + 1) * LANES])
    y = jax.nn.gelu(jnp.concatenate(ys, axis=1))
    z = jnp.dot(y.astype(BF16), wglu_ref[...], preferred_element_type=F32) + bglu_ref[...]
    ya_ref[...] = z[:, :S5_WIDTH] * jax.nn.sigmoid(z[:, S5_WIDTH:])


def _s5_params(lam_re, lam_im, log_dt, b_re, b_im, c_re, c_im):
    lam = lax.complex(lam_re.astype(F32), lam_im.astype(F32))
    dt = jnp.exp(log_dt.astype(F32))[:, None]
    a_bar = jnp.exp(lam * dt)
    b_bar = ((a_bar - 1.0) / lam)[..., None] * lax.complex(b_re.astype(F32), b_im.astype(F32))
    nj = S5_GROUPS // 8
    eye = jnp.eye(8, dtype=F32)

    def blk_b(m):
        m = m.reshape(nj, 8, S5_STATE, S5_GROUP)
        return jnp.einsum('jgpc,gh->jgchp', m, eye).reshape(nj, 8 * S5_GROUP, 8 * S5_STATE)

    def blk_c(m):
        m = m.reshape(nj, 8, S5_GROUP, S5_STATE)
        return jnp.einsum('jgcp,gh->jgphc', m, eye).reshape(nj, 8 * S5_STATE, 8 * S5_GROUP)

    bblk = jnp.concatenate([blk_b(jnp.real(b_bar)), blk_b(jnp.imag(b_bar))], axis=-1)
    cr = blk_c(c_re.astype(F32)).astype(BF16)
    ci = (-blk_c(c_im.astype(F32))).astype(BF16)
    a_r = jnp.real(a_bar).reshape(nj, 1, 8 * S5_STATE)
    a_i = jnp.imag(a_bar).reshape(nj, 1, 8 * S5_STATE)
    return bblk, a_r, a_i, cr, ci


def _s5(y_in, h0_re, h0_im, sp, d_skip, w_glu, b_glu, tl):
    b, l, _ = y_in.shape
    nj = S5_GROUPS // 8
    half = 8 * S5_STATE
    bblk, a_r, a_i, cr, ci = sp
    st = pl.BlockSpec((None, nj, 1, half), lambda bi, i: (bi, 0, 0, 0))
    ya, hr, hi = pl.pallas_call(
        functools.partial(_s5_kernel, tl=tl),
        grid=(b, l // tl),
        in_specs=[pl.BlockSpec((None, tl, S5_WIDTH), lambda bi, i: (bi, i, 0)), st, st,
                  _full(bblk.shape), _full(a_r.shape), _full(a_i.shape), _full(cr.shape), _full(ci.shape),
                  _full((1, S5_WIDTH)), _full((S5_WIDTH, 2 * S5_WIDTH)), _full((1, 2 * S5_WIDTH))],
        out_specs=[pl.BlockSpec((None, tl, S5_WIDTH), lambda bi, i: (bi, i, 0)), st, st],
        out_shape=[jax.ShapeDtypeStruct((b, l, S5_WIDTH), F32),
                   jax.ShapeDtypeStruct((b, nj, 1, half), F32), jax.ShapeDtypeStruct((b, nj, 1, half), F32)],
        scratch_shapes=[pltpu.VMEM((nj, tl, half), F32), pltpu.VMEM((nj, tl, half), F32),
                        pltpu.VMEM((nj, 1, half), F32), pltpu.VMEM((nj, 1, half), F32)],
        compiler_params=_cparams("parallel", "arbitrary"),
        name="s5",
    )(y_in, h0_re.reshape(b, nj, 1, half), h0_im.reshape(b, nj, 1, half), bblk, a_r, a_i, cr, ci,
      d_skip.reshape(1, S5_WIDTH), w_glu.astype(BF16), b_glu.reshape(1, 2 * S5_WIDTH))
    return ya, hr.reshape(b, S5_GROUPS, S5_STATE), hi.reshape(b, S5_GROUPS, S5_STATE)


def _gmlp_kernel(u_ref, v_ref, gv_ref, ws_ref, bias_ref, yb_ref, *vn_ref, tl, lc):
    hd = GM_WIDTH // GM_HEADS
    gv = jax.nn.gelu(v_ref[...])
    vn = gv * lax.rsqrt(jnp.mean(gv * gv, axis=-1, keepdims=True) + EPS) * gv_ref[...]
    if vn_ref:
        vn_ref[0][...] = vn
    gu = jax.nn.gelu(u_ref[...])
    vb = vn.astype(BF16)
    for c in range(tl // lc):
        rows = slice(c * lc, (c + 1) * lc)
        for h in range(GM_HEADS):
            cols = slice(h * hd, (h + 1) * hd)
            mixed = jnp.dot(ws_ref[h], vb[rows, cols], preferred_element_type=F32) + bias_ref[:, cols]
            yb_ref[rows, cols] = gu[rows, cols] * mixed


def _gmlp(y_in, g_v, w_s, b_s, tl, want_vn):
    b, l, _ = y_in.shape
    lc = min(GM_CHUNK, l)
    hd = GM_WIDTH // GM_HEADS
    tri = jnp.tril(jnp.ones((GM_CHUNK, GM_CHUNK), dtype=bool))
    ws = jnp.where(tri[None], w_s, 0.0)[:, :lc, :lc].astype(BF16)
    bias = jnp.repeat(jnp.transpose(b_s)[:lc], hd, axis=1)
    out_shape = [jax.ShapeDtypeStruct((b, l, GM_WIDTH), F32)]
    out_specs = [pl.BlockSpec((None, tl, GM_WIDTH), lambda bi, i: (bi, i, 0))]
    if want_vn:
        out_shape.append(jax.ShapeDtypeStruct((b, l, GM_WIDTH), F32))
        out_specs.append(pl.BlockSpec((None, tl, GM_WIDTH), lambda bi, i: (bi, i, 0)))
    res = pl.pallas_call(
        functools.partial(_gmlp_kernel, tl=tl, lc=lc),
        grid=(b, l // tl),
        in_specs=[pl.BlockSpec((None, tl, GM_WIDTH), lambda bi, i: (bi, i, 1)),
                  pl.BlockSpec((None, tl, GM_WIDTH), lambda bi, i: (bi, i, 2)),
                  _full((1, GM_WIDTH)), _full(ws.shape), _full(bias.shape)],
        out_specs=out_specs, out_shape=out_shape,
        compiler_params=_cparams("parallel", "parallel"),
        name="gmlp",
    )(y_in, y_in, g_v.reshape(1, GM_WIDTH), ws, bias)
    return (res[0], res[1]) if want_vn else (res[0], None)


def _proj_res_kernel(x_ref, gt_ref, a_ref, b_ref, wa_ref, wb_ref, o_ref):
    mix = (jnp.dot(a_ref[...].astype(BF16), wa_ref[...], preferred_element_type=F32)
           + jnp.dot(b_ref[...].astype(BF16), wb_ref[...], preferred_element_type=F32))
    o_ref[...] = x_ref[...] + gt_ref[...] * mix


def _proj_residual(x, gt, a, bm, w, tl):
    b, l, d = x.shape
    ka = a.shape[-1]
    wa, wb = w[:ka].astype(BF16), w[ka:].astype(BF16)
    blk = lambda n: pl.BlockSpec((None, tl, n), lambda bi, i: (bi, i, 0))
    return pl.pallas_call(
        _proj_res_kernel,
        grid=(b, l // tl),
        in_specs=[blk(d), pl.BlockSpec((None, 1, d), lambda bi, i: (bi, 0, 0)), blk(ka), blk(bm.shape[-1]),
                  _full(wa.shape), _full(wb.shape)],
        out_specs=blk(d), out_shape=jax.ShapeDtypeStruct((b, l, d), F32),
        compiler_params=_cparams("parallel", "parallel"),
        name="proj_residual",
    )(x, gt, a, bm, wa, wb)


def _rope_tables(pos0, l):
    pos = (pos0 + jnp.arange(l, dtype=jnp.int32)).astype(F32)[:, None]
    lane = jnp.arange(LANES)

    def table(period, start, rot):
        half = rot // 2
        r = lane % period - start
        inside = (r >= 0) & (r < rot)
        k = jnp.where(inside, r % half, 0)
        inv = ROPE_THETA ** (-k.astype(F32) * 2.0 / rot)
        ang = pos * inv[None, :]
        cos = jnp.where(inside[None], jnp.cos(ang), 1.0)
        sgn = jnp.where(r < half, -1.0, 1.0)
        sin = jnp.where(inside[None], jnp.sin(ang) * sgn[None], 0.0)
        return cos, sin

    return table(DA_DH, 0, DA_ROPE), table(LANES, MLA_NOPE, MLA_ROPE), table(LANES, 0, MLA_ROPE)


def _rotate(x, cos, sin, period, start, rot):
    half = rot // 2
    n = x.shape[-1]
    reps = n // LANES
    lane = (lax.broadcasted_iota(jnp.int32, x.shape, 1) & (period - 1)) - start
    up = pltpu.roll(x, n - half, 1)
    dn = pltpu.roll(x, half, 1)
    partner = jnp.where(lane < half, up, dn)
    if reps > 1:
        cos = jnp.concatenate([cos] * reps, axis=1)
        sin = jnp.concatenate([sin] * reps, axis=1)
    return x * cos + partner * sin


def _odd_prep_kernel(q_ref, k_ref, cq_ref, ckv_ref, kpe_ref, cd_ref, sd_ref, cm_ref, sm_ref, ck_ref, sk_ref,
                     gq_ref, wuq_ref, gkv_ref, qd_ref, kd_ref, qm_ref, lat_ref, kpe_out_ref):
    cd, sd = cd_ref[...], sd_ref[...]
    qd_ref[...] = _rotate(q_ref[...], cd, sd, DA_DH, 0, DA_ROPE).astype(BF16)
    kd_ref[...] = _rotate(k_ref[...], cd, sd, DA_DH, 0, DA_ROPE)
    cq = cq_ref[...]
    cqn = cq * lax.rsqrt(jnp.mean(cq * cq, axis=-1, keepdims=True) + EPS) * gq_ref[...]
    qm = jnp.dot(cqn.astype(BF16), wuq_ref[...], preferred_element_type=F32)
    qm_ref[...] = _rotate(qm, cm_ref[...], sm_ref[...], LANES, MLA_NOPE, MLA_ROPE).astype(BF16)
    ckv = ckv_ref[...]
    lat_ref[...] = ckv * lax.rsqrt(jnp.mean(ckv * ckv, axis=-1, keepdims=True) + EPS) * gkv_ref[...]
    kpe = _rotate(kpe_ref[...], ck_ref[...], sk_ref[...], LANES, 0, MLA_ROPE)
    kpe_out_ref[...] = kpe[:, :MLA_ROPE]


def _odd_prep(y_in, pos0, g_q, w_uq, g_kv, tl):
    b, l, _ = y_in.shape
    (cd, sd), (cm, sm), (ck, sk) = _rope_tables(pos0, l)
    per = MLA_NOPE + MLA_ROPE
    wuq = jnp.pad(w_uq.reshape(MLA_Q_RANK, MLA_HEADS, per), ((0, 0), (0, 0), (0, LANES - per)))
    wuq = wuq.reshape(MLA_Q_RANK, MLA_HEADS * LANES).astype(BF16)
    col = lambda w, j: pl.BlockSpec((None, tl, w), lambda bi, i: (bi, i, j))
    tab = pl.BlockSpec((tl, LANES), lambda bi, i: (i, 0))
    out = lambda w: pl.BlockSpec((None, tl, w), lambda bi, i: (bi, i, 0))
    return pl.pallas_call(
        _odd_prep_kernel,
        grid=(b, l // tl),
        in_specs=[col(512, 0), col(512, 1), col(256, 6), col(128, 14), col(128, 15),
                  tab, tab, tab, tab, tab, tab,
                  _full((1, MLA_Q_RANK)), _full(wuq.shape), _full((1, MLA_KV_RANK))],
        out_specs=[out(512), out(512), out(512), out(MLA_KV_RANK), out(MLA_ROPE)],
        out_shape=[jax.ShapeDtypeStruct((b, l, 512), BF16), jax.ShapeDtypeStruct((b, l, 512), F32),
                   jax.ShapeDtypeStruct((b, l, 512), BF16), jax.ShapeDtypeStruct((b, l, MLA_KV_RANK), F32),
                   jax.ShapeDtypeStruct((b, l, MLA_ROPE), F32)],
        compiler_params=_cparams("parallel", "parallel"),
        name="odd_prep",
    )(y_in, y_in, y_in, y_in, y_in, cd, sd, cm, sm, ck, sk,
      g_q.reshape(1, MLA_Q_RANK), wuq, g_kv.reshape(1, MLA_KV_RANK))


def _mla_kv_kernel(lat_ref, kpe_ref, wk_ref, wv_ref, place_ref, k_ref, v_ref):
    lat = lat_ref[...].astype(BF16)
    k = (jnp.dot(lat, wk_ref[...], preferred_element_type=F32)
         + jnp.dot(kpe_ref[...].astype(BF16), place_ref[...], preferred_element_type=F32))
    k_ref[...] = k.astype(BF16)
    v_ref[...] = jnp.dot(lat, wv_ref[...], preferred_element_type=F32).astype(BF16)


def _mla_kv(lat_all, kpe_all, w_ukv, tk):
    b, k, _ = lat_all.shape
    per = MLA_NOPE + MLA_DV
    w3 = w_ukv.reshape(MLA_KV_RANK, MLA_HEADS, per)
    wk = jnp.pad(w3[:, :, :MLA_NOPE], ((0, 0), (0, 0), (0, LANES - MLA_NOPE))).reshape(MLA_KV_RANK, MLA_HEADS * LANES)
    wv = w3[:, :, MLA_NOPE:].reshape(MLA_KV_RANK, MLA_HEADS * MLA_DV)
    place = jnp.pad(jnp.eye(MLA_ROPE, dtype=F32), ((0, 0), (MLA_NOPE, LANES - MLA_NOPE - MLA_ROPE)))
    place = jnp.tile(place, (1, MLA_HEADS))
    blk = lambda w: pl.BlockSpec((None, tk, w), lambda bi, i: (bi, i, 0))
    return pl.pallas_call(
        _mla_kv_kernel,
        grid=(b, k // tk),
        in_specs=[blk(MLA_KV_RANK), blk(MLA_ROPE), _full(wk.shape), _full(wv.shape), _full(place.shape)],
        out_specs=[blk(512), blk(512)],
        out_shape=[jax.ShapeDtypeStruct((b, k, 512), BF16), jax.ShapeDtypeStruct((b, k, 512), BF16)],
        compiler_params=_cparams("parallel", "parallel"),
        name="mla_kv",
    )(lat_all, kpe_all, wk.astype(BF16), wv.astype(BF16), place.astype(BF16))


def _visible_limit(pos_last):
    shift = CHUNK.bit_length() - 1
    return ((pos_last >> shift) + 1) << shift


def _fold_lanes(x, op):
    n = x.shape[-1]
    if n % LANES:
        return x
    out = x[:, :LANES]
    for c in range(1, n // LANES):
        out = op(out, x[:, c * LANES:(c + 1) * LANES])
    return out


def _flash_kernel(lam_ref, q_ref, k_ref, v_ref, g_ref, o_ref, m_ref, l_ref, acc_ref, *,
                  nhm, dqk, maps, scale, pos0, tq, tk, nkv, out_scale):
    i, j = pl.program_id(1), pl.program_id(2)

    @pl.when(j == 0)
    def _():
        m_ref[...] = jnp.full(m_ref.shape, NEG, F32)
        l_ref[...] = jnp.zeros(l_ref.shape, F32)
        acc_ref[...] = jnp.zeros(acc_ref.shape, F32)

    c2 = scale * math.log2(math.e)
    visible = j * tk < _visible_limit(pos0 + (i + 1) * tq - 1)
    unmasked = (j + 1) * tk <= _visible_limit(pos0 + i * tq)

    def scores(hm):
        q = q_ref[:, hm * dqk:(hm + 1) * dqk].astype(BF16)
        k = k_ref[:, hm * dqk:(hm + 1) * dqk].astype(BF16)
        return lax.dot_general(q, k, (((1,), (1,)), ((), ())), preferred_element_type=F32)

    def process(masked):
        if masked:
            q_pos = pos0 + i * tq + lax.broadcasted_iota(jnp.int32, (tq, tk), 0)
            k_pos = j * tk + lax.broadcasted_iota(jnp.int32, (tq, tk), 1)
            mask = k_pos < _visible_limit(q_pos)
        s_next = scores(0)
        for hm in range(nhm):
            s = s_next
            if hm + 1 < nhm:
                s_next = scores(hm + 1)
            if masked:
                s = jnp.where(mask, s, NEG)
            v = v_ref[:, (hm // maps) * DA_DV:(hm // maps + 1) * DA_DV].astype(BF16)
            m_prev = m_ref[hm]
            m_new = jnp.maximum(m_prev, jnp.max(_fold_lanes(s, jnp.maximum), axis=-1, keepdims=True))
            p = jnp.exp2((s - m_new) * c2)
            alpha = jnp.exp2((m_prev - m_new) * c2)
            l_ref[hm] = alpha * l_ref[hm] + jnp.sum(_fold_lanes(p, jnp.add), axis=-1, keepdims=True)
            acc_ref[hm] = alpha * acc_ref[hm] + jnp.dot(p.astype(BF16), v, preferred_element_type=F32)
            m_ref[hm] = m_new

    pl.when(visible & unmasked)(lambda: process(False))
    pl.when(visible & jnp.logical_not(unmasked))(lambda: process(True))

    @pl.when(j == nkv - 1)
    def _():
        for h in range(nhm // maps):
            if maps == 2:
                o = acc_ref[2 * h] / l_ref[2 * h] - lam_ref[0] * (acc_ref[2 * h + 1] / l_ref[2 * h + 1])
                o = o * lax.rsqrt(jnp.mean(o * o, axis=-1, keepdims=True) + EPS) * g_ref[...] * out_scale
            else:
                o = acc_ref[h] / l_ref[h]
            o_ref[:, h * DA_DV:(h + 1) * DA_DV] = o


def _flash(q, k, v, lam, g_sub, *, nhm, dqk, maps, scale, pos0, tq, tk, out_scale, v_col=0):
    b, l, _ = q.shape
    kk = k.shape[1]
    nkv = kk // tk
    nh = nhm // maps

    def kv_idx(bi, i, j):
        last = (_visible_limit(pos0 + (i + 1) * tq - 1) - 1) // tk
        return jnp.minimum(j, last)

    return pl.pallas_call(
        functools.partial(_flash_kernel, nhm=nhm, dqk=dqk, maps=maps, scale=scale, pos0=pos0, tq=tq, tk=tk,
                          nkv=nkv, out_scale=out_scale),
        grid=(b, l // tq, nkv),
        in_specs=[pl.BlockSpec(memory_space=pltpu.SMEM),
                  pl.BlockSpec((None, tq, nhm * dqk), lambda bi, i, j: (bi, i, 0)),
                  pl.BlockSpec((None, tk, nhm * dqk), lambda bi, i, j: (bi, kv_idx(bi, i, j), 0)),
                  pl.BlockSpec((None, tk, nh * DA_DV), lambda bi, i, j: (bi, kv_idx(bi, i, j), v_col)),
                  _full((1, DA_DV))],
        out_specs=pl.BlockSpec((None, tq, nh * DA_DV), lambda bi, i, j: (bi, i, 0)),
        out_shape=jax.ShapeDtypeStruct((b, l, nh * DA_DV), F32),
        scratch_shapes=[pltpu.VMEM((nhm, tq, 1), F32), pltpu.VMEM((nhm, tq, 1), F32),
                        pltpu.VMEM((nhm, tq, DA_DV), F32)],
        compiler_params=_cparams("parallel", "parallel", "arbitrary"),
        name="flash_da" if maps == 2 else "flash_mla",
    )(lam, q, k, v, g_sub)


def _top16(s, rid, payload, val_ref, pay_ref):
    big = float(s.shape[0])
    for it in range(PEER_TOPK):
        m = jnp.max(s, axis=0, keepdims=True)
        am = jnp.min(jnp.where(s == m, rid, big), axis=0, keepdims=True)
        sel = rid == am
        val_ref[it:it + 1, :] = m
        if payload is None:
            pay_ref[it:it + 1, :] = am
        else:
            pay_ref[it:it + 1, :] = jnp.max(jnp.where(sel, payload, -1.0), axis=0, keepdims=True)
        s = jnp.where(sel, -jnp.inf, s)


_PEER_PAIRS = [(a, b) for a in range(PEER_TOPK) for b in range(PEER_TOPK) if (a + 1) * (b + 1) <= PEER_TOPK]
_PEER_CAND_ROWS = -(-len(_PEER_PAIRS) // 8) * 8


def _route_kernel(q_ref, k1_ref, k2_ref, idx_ref, g_ref, v1_ref, i1_ref, v2_ref, i2_ref, vc_ref, ic_ref,
                  cand_ref, cidx_ref):
    tt = q_ref.shape[0]
    half = q_ref.shape[1] // 2
    nt = (((1,), (1,)), ((), ()))
    s1 = lax.dot_general(k1_ref[...].astype(BF16), q_ref[:, :half].astype(BF16), nt, preferred_element_type=F32)
    s2 = lax.dot_general(k2_ref[...].astype(BF16), q_ref[:, half:].astype(BF16), nt, preferred_element_type=F32)
    rid = lax.broadcasted_iota(jnp.int32, (PEER_NKEYS, tt), 0).astype(F32)
    _top16(s1, rid, None, v1_ref, i1_ref)
    _top16(s2, rid, None, v2_ref, i2_ref)
    npairs = len(_PEER_PAIRS)
    cand_ref[npairs:, :] = jnp.full((_PEER_CAND_ROWS - npairs, tt), -jnp.inf, F32)
    cidx_ref[npairs:, :] = jnp.full((_PEER_CAND_ROWS - npairs, tt), -1.0, F32)
    for r, (a, b) in enumerate(_PEER_PAIRS):
        cand_ref[r:r + 1, :] = v1_ref[a:a + 1, :] + v2_ref[b:b + 1, :]
        cidx_ref[r:r + 1, :] = i1_ref[a:a + 1, :] * float(PEER_NKEYS) + i2_ref[b:b + 1, :]
    rid2 = lax.broadcasted_iota(jnp.int32, (_PEER_CAND_ROWS, tt), 0).astype(F32)
    _top16(cand_ref[...], rid2, cidx_ref[...], vc_ref, ic_ref)
    sc = vc_ref[...]
    e = jnp.exp(sc - sc[0:1, :])
    g_ref[...] = e / jnp.sum(e, axis=0, keepdims=True)
    idx_ref[...] = ic_ref[...].astype(jnp.int32)


def _peer_route(q, k1, k2):
    t = q.shape[0]
    tt = PEER_TT
    dk = 2 * PEER_NKEYS
    kspec = pl.BlockSpec((None, PEER_NKEYS, dk // 2), lambda i, h: (h, 0, 0))
    ospec = pl.BlockSpec((None, PEER_TOPK, tt), lambda i, h: (i, h, 0))
    sm = pltpu.VMEM((PEER_TOPK, tt), F32)
    return pl.pallas_call(
        _route_kernel,
        grid=(t // tt, PEER_HEADS),
        in_specs=[pl.BlockSpec((tt, dk), lambda i, h: (i, h)), kspec, kspec],
        out_specs=[ospec, ospec],
        out_shape=[jax.ShapeDtypeStruct((t // tt, PEER_HK, tt), jnp.int32),
                   jax.ShapeDtypeStruct((t // tt, PEER_HK, tt), F32)],
        scratch_shapes=[sm, sm, sm, sm, sm, sm,
                        pltpu.VMEM((_PEER_CAND_ROWS, tt), F32), pltpu.VMEM((_PEER_CAND_ROWS, tt), F32)],
        compiler_params=_cparams("parallel", "parallel"),
        name="peer_route",
    )(q, k1, k2)


def _pack_table(tab):
    e, d = tab.shape
    tb = lax.bitcast_convert_type(tab.astype(BF16), jnp.uint16).astype(jnp.uint32)
    words = tb[:, :d // 2] | (tb[:, d // 2:] << 16)
    return lax.bitcast_convert_type(words, F32)


def _sc_unpack(words):
    bits = lax.bitcast_convert_type(words, jnp.int32)
    return lax.bitcast_convert_type(bits << 16, F32), lax.bitcast_convert_type(bits & jnp.int32(-65536), F32)


def _sc_gelu(x):
    u = math.sqrt(2.0 / math.pi) * (x + 0.044715 * (x * x * x))
    tanh_u = 1.0 - 2.0 / (jnp.exp(2.0 * u) + 1.0)
    return 0.5 * x * (1.0 + tanh_u)


def _sc_peer(u_tab, v_tab, idx_tm, g_tm, h, side_idx=None):
    t, nk = idx_tm.shape
    w = u_tab.shape[1]
    d = 2 * w
    half_rows = SC_GATHER_ROWS
    nbuf = SC_BUFFERS
    per_tok = nk // half_rows
    mesh = plsc.VectorSubcoreMesh(core_axis_name="core", subcore_axis_name="subcore")
    n_workers = mesh.num_cores * mesh.num_subcores
    tpw = t // n_workers
    tb = SC_TOKEN_BLOCK
    steps = per_tok * tb
    dot_rows = SC_LANES
    sum_rows = 16
    assert t % n_workers == 0 and tpw % tb == 0 and w % SC_LANES == 0
    assert nk % half_rows == 0 and half_rows % SC_LANES == 0 and steps % nbuf == 0

    n_side = 0 if side_idx is None else side_idx.shape[0]
    side_w = SC_SIDE_ROWS
    lw = n_side // n_workers
    n_win = lw // side_w
    blk_side = steps * side_w
    if n_side:
        assert nbuf == 2 and steps % 2 == 0 and n_side % (n_workers * blk_side) == 0
        assert n_win + 2 <= (tpw // tb) * steps

    def lanes(start):
        return pl.ds(pl.multiple_of(start, SC_LANES), SC_LANES)

    y_type = jax.ShapeDtypeStruct((t, d), F32)
    side_type = jax.ShapeDtypeStruct((n_side, w), F32)
    scratch = [pltpu.VMEM((tb * nk,), jnp.int32), pltpu.VMEM((tb, d), F32), pltpu.VMEM((tb, nk), F32),
               pltpu.VMEM((nbuf, half_rows, w), F32), pltpu.VMEM((tb, nk), F32), pltpu.VMEM((tb, d), F32),
               pltpu.SemaphoreType.DMA((nbuf,))]
    if n_side:
        scratch += [pltpu.VMEM((2 * blk_side,), jnp.int32), pltpu.VMEM((2, 2, side_w, w), F32),
                    pltpu.SemaphoreType.DMA((8,))]

    @functools.partial(
        pl.kernel, out_type=[y_type, side_type, side_type] if n_side else y_type, mesh=mesh, scratch_types=scratch,
        compiler_params=pltpu.CompilerParams(needs_layout_passes=False))
    def peer_kernel(u_hbm, v_hbm, idx_hbm, g_hbm, h_hbm, *refs):
        if n_side:
            (sidx_hbm, y_hbm, gu_hbm, gv_hbm, idx_v, h_v, g_v, rows_v, coef_v, y_v, sems,
             sidx_v, sbuf, ssems) = refs
        else:
            y_hbm, idx_v, h_v, g_v, rows_v, coef_v, y_v, sems = refs
        wid = lax.axis_index("subcore") * mesh.num_cores + lax.axis_index("core")
        lane = lax.iota(jnp.int32, SC_LANES)

        def gather(tab_hbm, step, buf):
            rows = idx_v.at[pl.ds(pl.multiple_of(step * half_rows, half_rows), half_rows)]
            return pltpu.make_async_copy(tab_hbm.at[rows], rows_v.at[buf], sems.at[buf])

        def prime(tab_hbm):
            for j in range(nbuf - 1):
                gather(tab_hbm, j, j).start()

        def side_in(tab_hbm, tab, n, par):
            slot = ((n // steps) % 2) * blk_side + (n % steps) * side_w
            rows = sidx_v.at[pl.ds(pl.multiple_of(slot, side_w), side_w)]
            return pltpu.make_async_copy(tab_hbm.at[rows], sbuf.at[tab, par], ssems.at[tab * 4 + par])

        def side_out(out_hbm, tab, n, par):
            dst = out_hbm.at[pl.ds(pl.multiple_of(wid * lw + n * side_w, side_w), side_w)]
            return pltpu.make_async_copy(sbuf.at[tab, par], dst, ssems.at[tab * 4 + 2 + par])

        def side_step(tab_hbm, out_hbm, tab, n, par):
            @pl.when((n >= 2) & (n < n_win + 2))
            def _():
                side_out(out_hbm, tab, n - 2, par).wait()

            @pl.when(n < n_win)
            def _():
                side_in(tab_hbm, tab, n, par).start()

            @pl.when((n >= 1) & (n < n_win + 1))
            def _():
                side_in(tab_hbm, tab, n - 1, 1 - par).wait()
                side_out(out_hbm, tab, n - 1, 1 - par).start()

        def run(tab_hbm, work, blk=None, side=None):
            @pl.loop(0, steps, step=nbuf)
            def _(s):
                for j in range(nbuf):
                    if side is not None:
                        side_step(tab_hbm, side[0], side[1], blk * steps + s + j, j)
                    ahead = s + j + nbuf - 1

                    @pl.when(ahead < steps)
                    def _():
                        gather(tab_hbm, ahead, (j + nbuf - 1) % nbuf).start()

                    gather(tab_hbm, s + j, j).wait()
                    work(s + j, j)

        def dots(step, buf):
            tok = step // per_tok
            col0 = (step % per_tok) * half_rows

            def group(q, _):
                r0 = q * dot_rows

                def chunk(c, accs):
                    h_lo = h_v[tok, lanes(c * SC_LANES)]
                    h_hi = h_v[tok, lanes(w + c * SC_LANES)]
                    out = []
                    for r in range(dot_rows):
                        lo, hi = _sc_unpack(rows_v[buf, r0 + r, lanes(c * SC_LANES)])
                        out.append(accs[r] + lo * h_lo + hi * h_hi)
                    return tuple(out)

                accs = lax.fori_loop(0, w // SC_LANES, chunk, (jnp.zeros((SC_LANES,), F32),) * dot_rows)
                vec = jnp.zeros((SC_LANES,), F32)
                for r in range(dot_rows):
                    vec = jnp.where(lane == r, jnp.sum(accs[r]), vec)
                coef_v[tok, lanes(col0 + r0)] = vec
                return 0

            lax.fori_loop(0, half_rows // dot_rows, group, 0)

        def gates():
            @pl.loop(0, tb)
            def _(tok):
                @pl.loop(0, nk // SC_LANES)
                def _(q):
                    ds = lanes(q * SC_LANES)
                    coef_v[tok, ds] = g_v[tok, ds] * _sc_gelu(coef_v[tok, ds])

        def weighted_sum(step, buf):
            tok = step // per_tok
            col0 = (step % per_tok) * half_rows

            @pl.when(col0 == 0)
            def _():
                @plsc.parallel_loop(0, d // SC_LANES)
                def _(c):
                    y_v[tok, lanes(c * SC_LANES)] = jnp.zeros((SC_LANES,), F32)

            def group(g, _):
                r0 = g * sum_rows
                cvec = coef_v[tok, lanes(col0 + (r0 // SC_LANES) * SC_LANES)]
                base = r0 % SC_LANES
                cs = [cvec.at[jnp.full((SC_LANES,), base + r, jnp.int32)].get(mode="promise_in_bounds")
                      for r in range(sum_rows)]

                @plsc.parallel_loop(0, w // SC_LANES, unroll=SC_LANES // sum_rows)
                def _(c):
                    los, his = [], []
                    for r in range(sum_rows):
                        lo, hi = _sc_unpack(rows_v[buf, r0 + r, lanes(c * SC_LANES)])
                        los.append(cs[r] * lo)
                        his.append(cs[r] * hi)
                    while len(los) > 1:
                        los = [a + b for a, b in zip(los[::2], los[1::2])]
                        his = [a + b for a, b in zip(his[::2], his[1::2])]
                    y_v[tok, lanes(c * SC_LANES)] = y_v[tok, lanes(c * SC_LANES)] + los[0]
                    y_v[tok, lanes(w + c * SC_LANES)] = y_v[tok, lanes(w + c * SC_LANES)] + his[0]

                return 0

            lax.fori_loop(0, half_rows // sum_rows, group, 0)

        @pl.loop(0, tpw // tb)
        def _(blk):
            tok0 = wid * tpw + blk * tb
            pltpu.sync_copy(idx_hbm.at[pl.ds(pl.multiple_of(tok0 * nk, nk), tb * nk)], idx_v)
            prime(u_hbm)
            pltpu.sync_copy(h_hbm.at[pl.ds(tok0, tb)], h_v)
            pltpu.sync_copy(g_hbm.at[pl.ds(tok0, tb)], g_v)
            if n_side:
                @pl.when(blk * steps < n_win)
                def _():
                    src = sidx_hbm.at[pl.ds(pl.multiple_of(wid * lw + blk * blk_side, blk_side), blk_side)]
                    pltpu.sync_copy(src, sidx_v.at[pl.ds(pl.multiple_of((blk % 2) * blk_side, blk_side), blk_side)])

            run(u_hbm, dots, blk, (gu_hbm, 0) if n_side else None)
            prime(v_hbm)
            gates()
            run(v_hbm, weighted_sum, blk, (gv_hbm, 1) if n_side else None)
            pltpu.sync_copy(y_v, y_hbm.at[pl.ds(tok0, tb)])

    if n_side:
        return peer_kernel(u_tab, v_tab, idx_tm.reshape(t * nk), g_tm, h, side_idx)
    return peer_kernel(u_tab, v_tab, idx_tm.reshape(t * nk), g_tm, h)


def _unpack(words):
    w = pltpu.bitcast(words, jnp.uint32)
    lo = pltpu.bitcast(w << 16, F32)
    hi = pltpu.bitcast(w & jnp.uint32(0xFFFF0000), F32)
    return lo, hi


def _peer_act_kernel(h_ref, gu_ref, g_ref, coef_ref, act_ref):
    c = pl.program_id(1)
    half = h_ref.shape[1] // 2
    h_lo, h_hi = h_ref[:, :half], h_ref[:, half:]
    lane = lax.broadcasted_iota(jnp.int32, act_ref.shape, 1)

    @pl.when(c == 0)
    def _():
        act_ref[...] = jnp.zeros(act_ref.shape, F32)

    act = act_ref[...]
    for k in range(PEER_HKC):
        lo, hi = _unpack(gu_ref[k])
        a = jnp.sum(lo * h_lo + hi * h_hi, axis=-1, keepdims=True)
        act = jnp.where(lane == c * PEER_HKC + k, a, act)
    act_ref[...] = act

    @pl.when(c == pl.num_programs(1) - 1)
    def _():
        coef_ref[...] = g_ref[...].T * jax.nn.gelu(act)


def _peer_act(h, gu, g):
    t, d = h.shape
    tt = PEER_TT
    nt = t // tt
    return pl.pallas_call(
        _peer_act_kernel,
        grid=(nt, PEER_HK // PEER_HKC),
        in_specs=[pl.BlockSpec((tt, d), lambda i, c: (i, 0)),
                  pl.BlockSpec((None, PEER_HKC, tt, d // 2), lambda i, c: (i, c, 0, 0)),
                  pl.BlockSpec((None, PEER_HK, tt), lambda i, c: (i, 0, 0))],
        out_specs=pl.BlockSpec((None, tt, PEER_HK), lambda i, c: (i, 0, 0)),
        out_shape=jax.ShapeDtypeStruct((nt, tt, PEER_HK), F32),
        scratch_shapes=[pltpu.VMEM((tt, PEER_HK), F32)],
        compiler_params=_cparams("parallel", "arbitrary"),
        name="peer_act",
    )(h, gu, g)


def _peer_mix_kernel(coef_ref, gv_ref, y_ref):
    c = pl.program_id(1)
    half = y_ref.shape[1] // 2

    @pl.when(c == 0)
    def _():
        y_ref[...] = jnp.zeros(y_ref.shape, F32)

    coef = coef_ref[...]
    lane = lax.broadcasted_iota(jnp.int32, coef.shape, 1)
    acc_lo, acc_hi = y_ref[:, :half], y_ref[:, half:]
    for k in range(PEER_HKC):
        ck = jnp.sum(jnp.where(lane == c * PEER_HKC + k, coef, 0.0), axis=-1, keepdims=True)
        lo, hi = _unpack(gv_ref[k])
        acc_lo = acc_lo + ck * lo
        acc_hi = acc_hi + ck * hi
    y_ref[:, :half] = acc_lo
    y_ref[:, half:] = acc_hi


def _peer_mix(coef, gv):
    nt, tt, _ = coef.shape
    w = gv.shape[-1]
    return pl.pallas_call(
        _peer_mix_kernel,
        grid=(nt, PEER_HK // PEER_HKC),
        in_specs=[pl.BlockSpec((None, tt, PEER_HK), lambda i, c: (i, 0, 0)),
                  pl.BlockSpec((None, PEER_HKC, tt, w), lambda i, c: (i, c, 0, 0))],
        out_specs=pl.BlockSpec((tt, 2 * w), lambda i, c: (i, 0)),
        out_shape=jax.ShapeDtypeStruct((nt * tt, 2 * w), F32),
        compiler_params=_cparams("parallel", "arbitrary"),
        name="peer_mix",
    )(coef, gv)


def _peer_out_kernel(x_ref, gt_ref, y_ref, gf_ref, o_ref, *, final):
    x = x_ref[...] + gt_ref[...] * y_ref[...]
    if final:
        x = x * lax.rsqrt(jnp.mean(x * x, axis=-1, keepdims=True) + EPS) * gf_ref[...]
    o_ref[...] = x


def _peer_out(x, gt, y, g_final, final):
    t, d = x.shape
    tt = PEER_TT
    nt = t // tt
    row = pl.BlockSpec((tt, d), lambda i: (i, 0))
    if gt.ndim == 3:
        per_seq = nt // gt.shape[0]
        gt_spec = pl.BlockSpec((None, 1, d), lambda i: (i // per_seq, 0, 0))
    else:
        gt_spec = row
    return pl.pallas_call(
        functools.partial(_peer_out_kernel, final=final),
        grid=(nt,),
        in_specs=[row, gt_spec, row, _full((1, d))],
        out_specs=row, out_shape=jax.ShapeDtypeStruct((t, d), F32),
        compiler_params=_cparams("parallel"),
        name="peer_out",
    )(x, gt, y, g_final.reshape(1, d))


def _peer(x, g_ffn, sc2, sh2, gt2, w_q, k1, k2, u_pack, v_pack, g_final, final, tl):
    b, l, d = x.shape
    t = b * l
    q, h = _norm_mod_matmul(x, g_ffn, sc2, sh2, w_q, tl, want_h=True)
    idx, g = _peer_route(q.reshape(t, -1), k1, k2)
    h = h.reshape(t, d)
    nt = t // PEER_TT
    n_tc = nt * PEER_TC_TILES // 32
    n_sc = nt - n_tc
    t_sc = n_sc * PEER_TT
    token_major = lambda a: jnp.transpose(a, (0, 2, 1)).reshape(-1, PEER_HK)
    if n_tc:
        y_sc, gu, gv = _sc_peer(u_pack, v_pack, token_major(idx[:n_sc]), token_major(g[:n_sc]), h[:t_sc],
                                idx[n_sc:].reshape(-1))
        rows = lambda a: a.reshape(n_tc, PEER_HK, PEER_TT, d // 2)
        y_tc = _peer_mix(_peer_act(h[t_sc:], rows(gu), g[n_sc:]), rows(gv))
        y = jnp.concatenate([y_sc, y_tc], axis=0)
    else:
        y = _sc_peer(u_pack, v_pack, token_major(idx), token_major(g), h)
    if l % PEER_TT:
        gt2 = jnp.broadcast_to(gt2, (b, l, d)).reshape(t, d)
    return _peer_out(x.reshape(t, d), gt2, y, g_final, final).reshape(b, l, d)


def _trunk(x, mod, pos0, past, p, tl):
    b, l, d = x.shape
    outs = {}
    for layer in range(DEPTH):
        sh1, sc1, gt1, sh2, sc2, gt2 = [m[:, None, :] for m in jnp.split(mod[layer], 6, axis=-1)]
        peer = functools.partial(
            _peer, g_ffn=p['g_ffn'][layer], sc2=sc2, sh2=sh2, gt2=gt2, w_q=p['peer_w_q'][layer].astype(BF16),
            k1=p['peer_k1'][layer], k2=p['peer_k2'][layer], u_pack=p['u_pack'][layer], v_pack=p['v_pack'][layer],
            g_final=p['g_final'], final=layer == DEPTH - 1, tl=tl)
        if layer % 2 == 0:
            e = layer // 2
            if past is None:
                hr = jnp.zeros((b, S5_GROUPS, S5_STATE), F32)
                hi = jnp.zeros((b, S5_GROUPS, S5_STATE), F32)
            else:
                hr, hi = past['s5_re'][e], past['s5_im'][e]
            n_seg = EVEN_SEGMENTS if l % (EVEN_SEGMENTS * max(tl, GM_CHUNK, PEER_TT)) == 0 else 1
            segs = []
            for xs in jnp.split(x, n_seg, axis=1):
                y_in = _norm_mod_matmul(xs, p['g_mix'][layer], sc1, sh1, p['w_in_e'][e].astype(BF16), tl)
                ya, hr, hi = _s5(y_in, hr, hi, p['s5'][e], p['s5_d'][e], p['w_glu'][e], p['b_glu'][e], tl)
                yb, vn = _gmlp(y_in, p['gm_g_v'][e], p['gm_w_s'][e], p['gm_b_s'][e], tl, want_vn=past is not None)
                segs.append(peer(_proj_residual(xs, gt1, ya, yb, p['w_out_e'][e], tl)))
            x = segs[0] if n_seg == 1 else jnp.concatenate(segs, axis=1)
            outs.update(s5_re=hr, s5_im=hi, gm_v=vn)
        else:
            o = layer // 2
            w_in = jnp.pad(p['w_in_o'][o], ((0, 0), (0, ODD_IN_PAD - p['w_in_o'][o].shape[1]))).astype(BF16)
            y_in = _norm_mod_matmul(x, p['g_mix'][layer], sc1, sh1, w_in, tl)
            qd, kd, qm, lat, kpe = _odd_prep(y_in, pos0, p['mla_g_q'][o], p['mla_w_uq'][o], p['mla_g_kv'][o], tl)
            v_new = y_in[:, :, 2 * 512:3 * 512]
            lam_init = 0.8 - 0.6 * math.exp(-0.3 * layer)
            lam = (jnp.exp(jnp.sum(p['da_lq1'][o] * p['da_lk1'][o])) - jnp.exp(jnp.sum(p['da_lq2'][o] * p['da_lk2'][o]))
                   + lam_init).reshape(1).astype(F32)
            g_sub = p['da_g_sub'][o].reshape(1, DA_DV)
            if past is None:
                k_all, v_all, v_col, lat_all, kpe_all = kd, y_in, 2, lat, kpe
                tq = tk = tl
            else:
                k_all = jnp.concatenate([past['da_k'][o].reshape(b, -1, 512), kd], axis=1)
                v_all = jnp.concatenate([past['da_v'][o].reshape(b, -1, 512), v_new], axis=1)
                lat_all = jnp.concatenate([past['mla_lat'][o], lat], axis=1)
                kpe_all = jnp.concatenate([past['mla_kpe'][o], kpe], axis=1)
                v_col, tq, tk = 0, l, k_all.shape[1]
            yc = _flash(qd, k_all, v_all, lam, g_sub, nhm=2 * DA_HEADS, dqk=DA_DH, maps=2, scale=DA_DH ** -0.5,
                        pos0=pos0, tq=tq, tk=tk, out_scale=1.0 - lam_init, v_col=v_col)
            km, vm = _mla_kv(lat_all, kpe_all, p['mla_w_ukv'][o], tk)
            yd = _flash(qm, km, vm, lam, g_sub, nhm=MLA_HEADS, dqk=LANES, maps=1,
                        scale=(MLA_NOPE + MLA_ROPE) ** -0.5, pos0=pos0, tq=tq, tk=tk, out_scale=1.0)
            x = peer(_proj_residual(x, gt1, yc, yd, p['w_out_o'][o], tl))
            outs.update(da_k=kd.reshape(b, l, DA_HEADS, 2 * DA_DH), da_v=v_new.reshape(b, l, DA_HEADS, DA_DV),
                        mla_lat=lat, mla_kpe=kpe)
    return x, outs


def kernel(x_prompt, x_sample, c_prompt, c_sample, state_s5_re, state_s5_im, cache_da_k, cache_da_v, cache_mla_latent, cache_mla_kpe, w_ada, b_ada, g_mix, g_ffn, g_final, w_in_e, w_out_e, s5_lam_re, s5_lam_im, s5_log_dt, s5_b_re, s5_b_im, s5_c_re, s5_c_im, s5_d, w_glu, b_glu, gm_g_v, gm_w_s, gm_b_s, w_in_o, w_out_o, da_lq1, da_lk1, da_lq2, da_lk2, da_g_sub, mla_g_q, mla_w_uq, mla_g_kv, mla_w_ukv, peer_w_q, peer_k1, peer_k2, peer_u, peer_v):
    p = dict(g_mix=g_mix, g_ffn=g_ffn, g_final=g_final, w_in_e=w_in_e, w_out_e=w_out_e, s5_d=s5_d, w_glu=w_glu,
             b_glu=b_glu, gm_g_v=gm_g_v, gm_w_s=gm_w_s, gm_b_s=gm_b_s, w_in_o=w_in_o, w_out_o=w_out_o,
             da_lq1=da_lq1, da_lk1=da_lk1, da_lq2=da_lq2, da_lk2=da_lk2, da_g_sub=da_g_sub, mla_g_q=mla_g_q,
             mla_w_uq=mla_w_uq, mla_g_kv=mla_g_kv, mla_w_ukv=mla_w_ukv, peer_w_q=peer_w_q, peer_k1=peer_k1,
             peer_k2=peer_k2)
    n_even = (DEPTH + 1) // 2
    p['s5'] = [_s5_params(s5_lam_re[e], s5_lam_im[e], s5_log_dt[e], s5_b_re[e], s5_b_im[e], s5_c_re[e], s5_c_im[e])
               for e in range(n_even)]
    p['u_pack'] = [_pack_table(peer_u[layer]) for layer in range(DEPTH)]
    p['v_pack'] = [_pack_table(peer_v[layer]) for layer in range(DEPTH)]
    past = dict(s5_re=state_s5_re, s5_im=state_s5_im, da_k=cache_da_k, da_v=cache_da_v,
                mla_lat=cache_mla_latent, mla_kpe=cache_mla_kpe)
    nb = x_prompt.shape[0]
    mod = _ada(jnp.concatenate([c_prompt, c_sample], axis=0), w_ada, b_ada)
    past_len = cache_da_k.shape[2]
    parts = [_trunk(x_prompt[s:s + 1], mod[:, s:s + 1], 0, None, p, tl=PROMPT_TILE) for s in range(nb)]
    y_p = jnp.concatenate([y for y, _ in parts], axis=0)
    o_p = {k: jnp.concatenate([o[k] for _, o in parts], axis=0) for k in parts[0][1] if parts[0][1][k] is not None}
    y_s, o_s = _trunk(x_sample, mod[:, nb:], past_len, past, p, tl=x_sample.shape[1])
    st = lambda a: a[None]
    return (y_p, y_s, st(o_p['s5_re']), st(o_p['s5_im']), st(o_s['s5_re']), st(o_s['s5_im']), st(o_s['gm_v']),
            st(o_p['da_k']), st(o_p['da_v']), st(o_s['da_k']), st(o_s['da_v']),
            st(o_p['mla_lat']), st(o_p['mla_kpe']), st(o_s['mla_lat']), st(o_s['mla_kpe']))
```

```python
import functools
import math

import jax
import jax.numpy as jnp
from jax import lax
from jax.experimental import pallas as pl
from jax.experimental.pallas import tpu as pltpu
from jax.experimental.pallas import tpu_sc as plsc

F32 = jnp.float32
BF16 = jnp.bfloat16

D_MODEL = 1024
DEPTH = 2
CHUNK = 64
ROPE_THETA = 500000.0
EPS = 1e-6
NEG = -1e30

S5_WIDTH = 512
S5_GROUP = 16
S5_GROUPS = 32
S5_STATE = 64
GM_WIDTH = 512
GM_HEADS = 4
GM_CHUNK = 128
DA_HEADS = 4
DA_DH = 64
DA_DV = 128
DA_ROPE = 16
MLA_HEADS = 4
MLA_Q_RANK = 256
MLA_KV_RANK = 128
MLA_NOPE = 64
MLA_ROPE = 32
MLA_DV = 128
PEER_HEADS = 8
PEER_NKEYS = 128
PEER_TOPK = 16
PEER_HK = PEER_HEADS * PEER_TOPK

LANES = 128
VMEM_LIMIT = 48 << 20
PROMPT_TILE = 512
ODD_IN_PAD = 2048
EVEN_SEGMENTS = 2
PEER_TT = 256
SC_LANES = 16
SC_TOKEN_BLOCK = 8
SC_GATHER_ROWS = 64
SC_BUFFERS = 2
SC_SIDE_ROWS = 16
PEER_TC_TILES = 6
PEER_HKC = 16

_HI = lax.Precision.HIGHEST


def _cparams(*sem):
    return pltpu.CompilerParams(dimension_semantics=sem, vmem_limit_bytes=VMEM_LIMIT)


def _full(shape):
    n = len(shape)
    return pl.BlockSpec(shape, lambda *_: (0,) * n)


def _ada_kernel(c_ref, w_ref, b_ref, o_ref):
    c = c_ref[...]
    s = c * jax.nn.sigmoid(c)
    o_ref[...] = jnp.dot(s, w_ref[...], precision=_HI, preferred_element_type=F32) + b_ref[...]


def _ada(c_all, w_ada, b_ada):
    r = c_all.shape[0]
    tn = 1536
    return pl.pallas_call(
        _ada_kernel,
        grid=(DEPTH, 6 * D_MODEL // tn),
        in_specs=[_full((r, D_MODEL)),
                  pl.BlockSpec((None, D_MODEL, tn), lambda l, j: (l, 0, j)),
                  pl.BlockSpec((None, 1, tn), lambda l, j: (l, 0, j))],
        out_specs=pl.BlockSpec((None, r, tn), lambda l, j: (l, 0, j)),
        out_shape=jax.ShapeDtypeStruct((DEPTH, r, 6 * D_MODEL), F32),
        compiler_params=_cparams("parallel", "parallel"),
        name="ada",
    )(c_all, w_ada, b_ada.reshape(DEPTH, 1, 6 * D_MODEL))


def _nmm_kernel(x_ref, g_ref, sc_ref, sh_ref, w_ref, o_ref, *h_ref):
    x = x_ref[...]
    ms = jnp.mean(x * x, axis=-1, keepdims=True)
    h = x * lax.rsqrt(ms + EPS) * g_ref[...] * (1.0 + sc_ref[...]) + sh_ref[...]
    o_ref[...] = jnp.dot(h.astype(BF16), w_ref[...], preferred_element_type=F32)
    if h_ref:
        h_ref[0][...] = h


def _norm_mod_matmul(x, g, sc, sh, w, tl, want_h=False):
    b, l, d = x.shape
    n = w.shape[1]
    row = pl.BlockSpec((None, 1, d), lambda bi, i: (bi, 0, 0))
    out_shape = [jax.ShapeDtypeStruct((b, l, n), F32)]
    out_specs = [pl.BlockSpec((None, tl, n), lambda bi, i: (bi, i, 0))]
    if want_h:
        out_shape.append(jax.ShapeDtypeStruct((b, l, d), F32))
        out_specs.append(pl.BlockSpec((None, tl, d), lambda bi, i: (bi, i, 0)))
    res = pl.pallas_call(
        _nmm_kernel,
        grid=(b, l // tl),
        in_specs=[pl.BlockSpec((None, tl, d), lambda bi, i: (bi, i, 0)), _full((1, d)), row, row, _full((d, n))],
        out_specs=out_specs, out_shape=out_shape,
        compiler_params=_cparams("parallel", "parallel"),
        name="norm_mod_matmul",
    )(x, g.reshape(1, d), sc, sh, w)
    return res if want_h else res[0]


def _s5_kernel(xa_ref, h0r_ref, h0i_ref, bblk_ref, ar_ref, ai_ref, cr_ref, ci_ref, d_ref, wglu_ref, bglu_ref,
               ya_ref, hr_out_ref, hi_out_ref, bur, bui, st_r, st_i, *, tl):
    nj = S5_WIDTH // LANES
    half = 8 * S5_STATE

    @pl.when(pl.program_id(1) == 0)
    def _():
        st_r[...] = h0r_ref[...]
        st_i[...] = h0i_ref[...]

    xa = xa_ref[...]
    for j in range(nj):
        bu = jnp.dot(xa[:, j * LANES:(j + 1) * LANES], bblk_ref[j], precision=_HI, preferred_element_type=F32)
        bur[j] = bu[:, :half]
        bui[j] = bu[:, half:]

    for j0 in range(0, nj, 2):
        js = (j0, j0 + 1)
        a_r = [ar_ref[j] for j in js]
        a_i = [ai_ref[j] for j in js]

        def body(k, carry, js=js, a_r=a_r, a_i=a_i):
            carry = list(carry)
            base = pl.multiple_of(k * 8, 8)
            for s in range(8):
                t = base + s
                for q, j in enumerate(js):
                    hr, hi = carry[2 * q], carry[2 * q + 1]
                    nhr = a_r[q] * hr - a_i[q] * hi + bur[j, pl.ds(t, 1), :]
                    nhi = a_r[q] * hi + a_i[q] * hr + bui[j, pl.ds(t, 1), :]
                    bur[j, pl.ds(t, 1), :] = nhr
                    bui[j, pl.ds(t, 1), :] = nhi
                    carry[2 * q], carry[2 * q + 1] = nhr, nhi
            return tuple(carry)

        init = (st_r[js[0]], st_i[js[0]], st_r[js[1]], st_i[js[1]])
        fin = lax.fori_loop(0, tl // 8, body, init)
        st_r[js[0]], st_i[js[0]], st_r[js[1]], st_i[js[1]] = fin

    hr_out_ref[...] = st_r[...]
    hi_out_ref[...] = st_i[...]

    ys = []
    for j in range(nj):
        y = (jnp.dot(bur[j].astype(BF16), cr_ref[j], preferred_element_type=F32)
             + jnp.dot(bui[j].astype(BF16), ci_ref[j], preferred_element_type=F32))
        ys.append(y + d_ref[:, j * LANES:(j + 1) * LANES] * xa[:, j * LANES:(j + 1) * LANES])
    y = jax.nn.gelu(jnp.concatenate(ys, axis=1))
    z = jnp.dot(y.astype(BF16), wglu_ref[...], preferred_element_type=F32) + bglu_ref[...]
    ya_ref[...] = z[:, :S5_WIDTH] * jax.nn.sigmoid(z[:, S5_WIDTH:])


def _s5_params(lam_re, lam_im, log_dt, b_re, b_im, c_re, c_im):
    lam = lax.complex(lam_re.astype(F32), lam_im.astype(F32))
    dt = jnp.exp(log_dt.astype(F32))[:, None]
    a_bar = jnp.exp(lam * dt)
    b_bar = ((a_bar - 1.0) / lam)[..., None] * lax.complex(b_re.astype(F32), b_im.astype(F32))
    nj = S5_GROUPS // 8
    eye = jnp.eye(8, dtype=F32)

    def blk_b(m):
        m = m.reshape(nj, 8, S5_STATE, S5_GROUP)
        return jnp.einsum('jgpc,gh->jgchp', m, eye).reshape(nj, 8 * S5_GROUP, 8 * S5_STATE)

    def blk_c(m):
        m = m.reshape(nj, 8, S5_GROUP, S5_STATE)
        return jnp.einsum('jgcp,gh->jgphc', m, eye).reshape(nj, 8 * S5_STATE, 8 * S5_GROUP)

    bblk = jnp.concatenate([blk_b(jnp.real(b_bar)), blk_b(jnp.imag(b_bar))], axis=-1)
    cr = blk_c(c_re.astype(F32)).astype(BF16)
    ci = (-blk_c(c_im.astype(F32))).astype(BF16)
    a_r = jnp.real(a_bar).reshape(nj, 1, 8 * S5_STATE)
    a_i = jnp.imag(a_bar).reshape(nj, 1, 8 * S5_STATE)
    return bblk, a_r, a_i, cr, ci


def _s5(y_in, h0_re, h0_im, sp, d_skip, w_glu, b_glu, tl):
    b, l, _ = y_in.shape
    nj = S5_GROUPS // 8
    half = 8 * S5_STATE
    bblk, a_r, a_i, cr, ci = sp
    st = pl.BlockSpec((None, nj, 1, half), lambda bi, i: (bi, 0, 0, 0))
    ya, hr, hi = pl.pallas_call(
        functools.partial(_s5_kernel, tl=tl),
        grid=(b, l // tl),
        in_specs=[pl.BlockSpec((None, tl, S5_WIDTH), lambda bi, i: (bi, i, 0)), st, st,
                  _full(bblk.shape), _full(a_r.shape), _full(a_i.shape), _full(cr.shape), _full(ci.shape),
                  _full((1, S5_WIDTH)), _full((S5_WIDTH, 2 * S5_WIDTH)), _full((1, 2 * S5_WIDTH))],
        out_specs=[pl.BlockSpec((None, tl, S5_WIDTH), lambda bi, i: (bi, i, 0)), st, st],
        out_shape=[jax.ShapeDtypeStruct((b, l, S5_WIDTH), F32),
                   jax.ShapeDtypeStruct((b, nj, 1, half), F32), jax.ShapeDtypeStruct((b, nj, 1, half), F32)],
        scratch_shapes=[pltpu.VMEM((nj, tl, half), F32), pltpu.VMEM((nj, tl, half), F32),
                        pltpu.VMEM((nj, 1, half), F32), pltpu.VMEM((nj, 1, half), F32)],
        compiler_params=_cparams("parallel", "arbitrary"),
        name="s5",
    )(y_in, h0_re.reshape(b, nj, 1, half), h0_im.reshape(b, nj, 1, half), bblk, a_r, a_i, cr, ci,
      d_skip.reshape(1, S5_WIDTH), w_glu.astype(BF16), b_glu.reshape(1, 2 * S5_WIDTH))
    return ya, hr.reshape(b, S5_GROUPS, S5_STATE), hi.reshape(b, S5_GROUPS, S5_STATE)


def _gmlp_kernel(u_ref, v_ref, gv_ref, ws_ref, bias_ref, yb_ref, *vn_ref, tl, lc):
    hd = GM_WIDTH // GM_HEADS
    gv = jax.nn.gelu(v_ref[...])
    vn = gv * lax.rsqrt(jnp.mean(gv * gv, axis=-1, keepdims=True) + EPS) * gv_ref[...]
    if vn_ref:
        vn_ref[0][...] = vn
    gu = jax.nn.gelu(u_ref[...])
    vb = vn.astype(BF16)
    for c in range(tl // lc):
        rows = slice(c * lc, (c + 1) * lc)
        for h in range(GM_HEADS):
            cols = slice(h * hd, (h + 1) * hd)
            mixed = jnp.dot(ws_ref[h], vb[rows, cols], preferred_element_type=F32) + bias_ref[:, cols]
            yb_ref[rows, cols] = gu[rows, cols] * mixed


def _gmlp(y_in, g_v, w_s, b_s, tl, want_vn):
    b, l, _ = y_in.shape
    lc = min(GM_CHUNK, l)
    hd = GM_WIDTH // GM_HEADS
    tri = jnp.tril(jnp.ones((GM_CHUNK, GM_CHUNK), dtype=bool))
    ws = jnp.where(tri[None], w_s, 0.0)[:, :lc, :lc].astype(BF16)
    bias = jnp.repeat(jnp.transpose(b_s)[:lc], hd, axis=1)
    out_shape = [jax.ShapeDtypeStruct((b, l, GM_WIDTH), F32)]
    out_specs = [pl.BlockSpec((None, tl, GM_WIDTH), lambda bi, i: (bi, i, 0))]
    if want_vn:
        out_shape.append(jax.ShapeDtypeStruct((b, l, GM_WIDTH), F32))
        out_specs.append(pl.BlockSpec((None, tl, GM_WIDTH), lambda bi, i: (bi, i, 0)))
    res = pl.pallas_call(
        functools.partial(_gmlp_kernel, tl=tl, lc=lc),
        grid=(b, l // tl),
        in_specs=[pl.BlockSpec((None, tl, GM_WIDTH), lambda bi, i: (bi, i, 1)),
                  pl.BlockSpec((None, tl, GM_WIDTH), lambda bi, i: (bi, i, 2)),
                  _full((1, GM_WIDTH)), _full(ws.shape), _full(bias.shape)],
        out_specs=out_specs, out_shape=out_shape,
        compiler_params=_cparams("parallel", "parallel"),
        name="gmlp",
    )(y_in, y_in, g_v.reshape(1, GM_WIDTH), ws, bias)
    return (res[0], res[1]) if want_vn else (res[0], None)


def _proj_res_kernel(x_ref, gt_ref, a_ref, b_ref, wa_ref, wb_ref, o_ref):
    mix = (jnp.dot(a_ref[...].astype(BF16), wa_ref[...], preferred_element_type=F32)
           + jnp.dot(b_ref[...].astype(BF16), wb_ref[...], preferred_element_type=F32))
    o_ref[...] = x_ref[...] + gt_ref[...] * mix


def _proj_residual(x, gt, a, bm, w, tl):
    b, l, d = x.shape
    ka = a.shape[-1]
    wa, wb = w[:ka].astype(BF16), w[ka:].astype(BF16)
    blk = lambda n: pl.BlockSpec((None, tl, n), lambda bi, i: (bi, i, 0))
    return pl.pallas_call(
        _proj_res_kernel,
        grid=(b, l // tl),
        in_specs=[blk(d), pl.BlockSpec((None, 1, d), lambda bi, i: (bi, 0, 0)), blk(ka), blk(bm.shape[-1]),
                  _full(wa.shape), _full(wb.shape)],
        out_specs=blk(d), out_shape=jax.ShapeDtypeStruct((b, l, d), F32),
        compiler_params=_cparams("parallel", "parallel"),
        name="proj_residual",
    )(x, gt, a, bm, wa, wb)


def _rope_tables(pos0, l):
    pos = (pos0 + jnp.arange(l, dtype=jnp.int32)).astype(F32)[:, None]
    lane = jnp.arange(LANES)

    def table(period, start, rot):
        half = rot // 2
        r = lane % period - start
        inside = (r >= 0) & (r < rot)
        k = jnp.where(inside, r % half, 0)
        inv = ROPE_THETA ** (-k.astype(F32) * 2.0 / rot)
        ang = pos * inv[None, :]
        cos = jnp.where(inside[None], jnp.cos(ang), 1.0)
        sgn = jnp.where(r < half, -1.0, 1.0)
        sin = jnp.where(inside[None], jnp.sin(ang) * sgn[None], 0.0)
        return cos, sin

    return table(DA_DH, 0, DA_ROPE), table(LANES, MLA_NOPE, MLA_ROPE), table(LANES, 0, MLA_ROPE)


def _rotate(x, cos, sin, period, start, rot):
    half = rot // 2
    n = x.shape[-1]
    reps = n // LANES
    lane = (lax.broadcasted_iota(jnp.int32, x.shape, 1) & (period - 1)) - start
    up = pltpu.roll(x, n - half, 1)
    dn = pltpu.roll(x, half, 1)
    partner = jnp.where(lane < half, up, dn)
    if reps > 1:
        cos = jnp.concatenate([cos] * reps, axis=1)
        sin = jnp.concatenate([sin] * reps, axis=1)
    return x * cos + partner * sin


def _odd_prep_kernel(q_ref, k_ref, cq_ref, ckv_ref, kpe_ref, cd_ref, sd_ref, cm_ref, sm_ref, ck_ref, sk_ref,
                     gq_ref, wuq_ref, gkv_ref, qd_ref, kd_ref, qm_ref, lat_ref, kpe_out_ref):
    cd, sd = cd_ref[...], sd_ref[...]
    qd_ref[...] = _rotate(q_ref[...], cd, sd, DA_DH, 0, DA_ROPE).astype(BF16)
    kd_ref[...] = _rotate(k_ref[...], cd, sd, DA_DH, 0, DA_ROPE)
    cq = cq_ref[...]
    cqn = cq * lax.rsqrt(jnp.mean(cq * cq, axis=-1, keepdims=True) + EPS) * gq_ref[...]
    qm = jnp.dot(cqn.astype(BF16), wuq_ref[...], preferred_element_type=F32)
    qm_ref[...] = _rotate(qm, cm_ref[...], sm_ref[...], LANES, MLA_NOPE, MLA_ROPE).astype(BF16)
    ckv = ckv_ref[...]
    lat_ref[...] = ckv * lax.rsqrt(jnp.mean(ckv * ckv, axis=-1, keepdims=True) + EPS) * gkv_ref[...]
    kpe = _rotate(kpe_ref[...], ck_ref[...], sk_ref[...], LANES, 0, MLA_ROPE)
    kpe_out_ref[...] = kpe[:, :MLA_ROPE]


def _odd_prep(y_in, pos0, g_q, w_uq, g_kv, tl):
    b, l, _ = y_in.shape
    (cd, sd), (cm, sm), (ck, sk) = _rope_tables(pos0, l)
    per = MLA_NOPE + MLA_ROPE
    wuq = jnp.pad(w_uq.reshape(MLA_Q_RANK, MLA_HEADS, per), ((0, 0), (0, 0), (0, LANES - per)))
    wuq = wuq.reshape(MLA_Q_RANK, MLA_HEADS * LANES).astype(BF16)
    col = lambda w, j: pl.BlockSpec((None, tl, w), lambda bi, i: (bi, i, j))
    tab = pl.BlockSpec((tl, LANES), lambda bi, i: (i, 0))
    out = lambda w: pl.BlockSpec((None, tl, w), lambda bi, i: (bi, i, 0))
    return pl.pallas_call(
        _odd_prep_kernel,
        grid=(b, l // tl),
        in_specs=[col(512, 0), col(512, 1), col(256, 6), col(128, 14), col(128, 15),
                  tab, tab, tab, tab, tab, tab,
                  _full((1, MLA_Q_RANK)), _full(wuq.shape), _full((1, MLA_KV_RANK))],
        out_specs=[out(512), out(512), out(512), out(MLA_KV_RANK), out(MLA_ROPE)],
        out_shape=[jax.ShapeDtypeStruct((b, l, 512), BF16), jax.ShapeDtypeStruct((b, l, 512), F32),
                   jax.ShapeDtypeStruct((b, l, 512), BF16), jax.ShapeDtypeStruct((b, l, MLA_KV_RANK), F32),
                   jax.ShapeDtypeStruct((b, l, MLA_ROPE), F32)],
        compiler_params=_cparams("parallel", "parallel"),
        name="odd_prep",
    )(y_in, y_in, y_in, y_in, y_in, cd, sd, cm, sm, ck, sk,
      g_q.reshape(1, MLA_Q_RANK), wuq, g_kv.reshape(1, MLA_KV_RANK))


def _mla_kv_kernel(lat_ref, kpe_ref, wk_ref, wv_ref, place_ref, k_ref, v_ref):
    lat = lat_ref[...].astype(BF16)
    k = (jnp.dot(lat, wk_ref[...], preferred_element_type=F32)
         + jnp.dot(kpe_ref[...].astype(BF16), place_ref[...], preferred_element_type=F32))
    k_ref[...] = k.astype(BF16)
    v_ref[...] = jnp.dot(lat, wv_ref[...], preferred_element_type=F32).astype(BF16)


def _mla_kv(lat_all, kpe_all, w_ukv, tk):
    b, k, _ = lat_all.shape
    per = MLA_NOPE + MLA_DV
    w3 = w_ukv.reshape(MLA_KV_RANK, MLA_HEADS, per)
    wk = jnp.pad(w3[:, :, :MLA_NOPE], ((0, 0), (0, 0), (0, LANES - MLA_NOPE))).reshape(MLA_KV_RANK, MLA_HEADS * LANES)
    wv = w3[:, :, MLA_NOPE:].reshape(MLA_KV_RANK, MLA_HEADS * MLA_DV)
    place = jnp.pad(jnp.eye(MLA_ROPE, dtype=F32), ((0, 0), (MLA_NOPE, LANES - MLA_NOPE - MLA_ROPE)))
    place = jnp.tile(place, (1, MLA_HEADS))
    blk = lambda w: pl.BlockSpec((None, tk, w), lambda bi, i: (bi, i, 0))
    return pl.pallas_call(
        _mla_kv_kernel,
        grid=(b, k // tk),
        in_specs=[blk(MLA_KV_RANK), blk(MLA_ROPE), _full(wk.shape), _full(wv.shape), _full(place.shape)],
        out_specs=[blk(512), blk(512)],
        out_shape=[jax.ShapeDtypeStruct((b, k, 512), BF16), jax.ShapeDtypeStruct((b, k, 512), BF16)],
        compiler_params=_cparams("parallel", "parallel"),
        name="mla_kv",
    )(lat_all, kpe_all, wk.astype(BF16), wv.astype(BF16), place.astype(BF16))


def _visible_limit(pos_last):
    shift = CHUNK.bit_length() - 1
    return ((pos_last >> shift) + 1) << shift


def _fold_lanes(x, op):
    n = x.shape[-1]
    if n % LANES:
        return x
    out = x[:, :LANES]
    for c in range(1, n // LANES):
        out = op(out, x[:, c * LANES:(c + 1) * LANES])
    return out


def _flash_kernel(lam_ref, q_ref, k_ref, v_ref, g_ref, o_ref, m_ref, l_ref, acc_ref, *,
                  nhm, dqk, maps, scale, pos0, tq, tk, nkv, out_scale):
    i, j = pl.program_id(1), pl.program_id(2)

    @pl.when(j == 0)
    def _():
        m_ref[...] = jnp.full(m_ref.shape, NEG, F32)
        l_ref[...] = jnp.zeros(l_ref.shape, F32)
        acc_ref[...] = jnp.zeros(acc_ref.shape, F32)

    c2 = scale * math.log2(math.e)
    visible = j * tk < _visible_limit(pos0 + (i + 1) * tq - 1)
    unmasked = (j + 1) * tk <= _visible_limit(pos0 + i * tq)

    def scores(hm):
        q = q_ref[:, hm * dqk:(hm + 1) * dqk].astype(BF16)
        k = k_ref[:, hm * dqk:(hm + 1) * dqk].astype(BF16)
        return lax.dot_general(q, k, (((1,), (1,)), ((), ())), preferred_element_type=F32)

    def process(masked):
        if masked:
            q_pos = pos0 + i * tq + lax.broadcasted_iota(jnp.int32, (tq, tk), 0)
            k_pos = j * tk + lax.broadcasted_iota(jnp.int32, (tq, tk), 1)
            mask = k_pos < _visible_limit(q_pos)
        s_next = scores(0)
        for hm in range(nhm):
            s = s_next
            if hm + 1 < nhm:
                s_next = scores(hm + 1)
            if masked:
                s = jnp.where(mask, s, NEG)
            v = v_ref[:, (hm // maps) * DA_DV:(hm // maps + 1) * DA_DV].astype(BF16)
            m_prev = m_ref[hm]
            m_new = jnp.maximum(m_prev, jnp.max(_fold_lanes(s, jnp.maximum), axis=-1, keepdims=True))
            p = jnp.exp2((s - m_new) * c2)
            alpha = jnp.exp2((m_prev - m_new) * c2)
            l_ref[hm] = alpha * l_ref[hm] + jnp.sum(_fold_lanes(p, jnp.add), axis=-1, keepdims=True)
            acc_ref[hm] = alpha * acc_ref[hm] + jnp.dot(p.astype(BF16), v, preferred_element_type=F32)
            m_ref[hm] = m_new

    pl.when(visible & unmasked)(lambda: process(False))
    pl.when(visible & jnp.logical_not(unmasked))(lambda: process(True))

    @pl.when(j == nkv - 1)
    def _():
        for h in range(nhm // maps):
            if maps == 2:
                o = acc_ref[2 * h] / l_ref[2 * h] - lam_ref[0] * (acc_ref[2 * h + 1] / l_ref[2 * h + 1])
                o = o * lax.rsqrt(jnp.mean(o * o, axis=-1, keepdims=True) + EPS) * g_ref[...] * out_scale
            else:
                o = acc_ref[h] / l_ref[h]
            o_ref[:, h * DA_DV:(h + 1) * DA_DV] = o


def _flash(q, k, v, lam, g_sub, *, nhm, dqk, maps, scale, pos0, tq, tk, out_scale, v_col=0):
    b, l, _ = q.shape
    kk = k.shape[1]
    nkv = kk // tk
    nh = nhm // maps

    def kv_idx(bi, i, j):
        last = (_visible_limit(pos0 + (i + 1) * tq - 1) - 1) // tk
        return jnp.minimum(j, last)

    return pl.pallas_call(
        functools.partial(_flash_kernel, nhm=nhm, dqk=dqk, maps=maps, scale=scale, pos0=pos0, tq=tq, tk=tk,
                          nkv=nkv, out_scale=out_scale),
        grid=(b, l // tq, nkv),
        in_specs=[pl.BlockSpec(memory_space=pltpu.SMEM),
                  pl.BlockSpec((None, tq, nhm * dqk), lambda bi, i, j: (bi, i, 0)),
                  pl.BlockSpec((None, tk, nhm * dqk), lambda bi, i, j: (bi, kv_idx(bi, i, j), 0)),
                  pl.BlockSpec((None, tk, nh * DA_DV), lambda bi, i, j: (bi, kv_idx(bi, i, j), v_col)),
                  _full((1, DA_DV))],
        out_specs=pl.BlockSpec((None, tq, nh * DA_DV), lambda bi, i, j: (bi, i, 0)),
        out_shape=jax.ShapeDtypeStruct((b, l, nh * DA_DV), F32),
        scratch_shapes=[pltpu.VMEM((nhm, tq, 1), F32), pltpu.VMEM((nhm, tq, 1), F32),
                        pltpu.VMEM((nhm, tq, DA_DV), F32)],
        compiler_params=_cparams("parallel", "parallel", "arbitrary"),
        name="flash_da" if maps == 2 else "flash_mla",
    )(lam, q, k, v, g_sub)


def _top16(s, rid, payload, val_ref, pay_ref):
    big = float(s.shape[0])
    for it in range(PEER_TOPK):
        m = jnp.max(s, axis=0, keepdims=True)
        am = jnp.min(jnp.where(s == m, rid, big), axis=0, keepdims=True)
        sel = rid == am
        val_ref[it:it + 1, :] = m
        if payload is None:
            pay_ref[it:it + 1, :] = am
        else:
            pay_ref[it:it + 1, :] = jnp.max(jnp.where(sel, payload, -1.0), axis=0, keepdims=True)
        s = jnp.where(sel, -jnp.inf, s)


_PEER_PAIRS = [(a, b) for a in range(PEER_TOPK) for b in range(PEER_TOPK) if (a + 1) * (b + 1) <= PEER_TOPK]
_PEER_CAND_ROWS = -(-len(_PEER_PAIRS) // 8) * 8


def _route_kernel(q_ref, k1_ref, k2_ref, idx_ref, g_ref, v1_ref, i1_ref, v2_ref, i2_ref, vc_ref, ic_ref,
                  cand_ref, cidx_ref):
    tt = q_ref.shape[0]
    half = q_ref.shape[1] // 2
    nt = (((1,), (1,)), ((), ()))
    s1 = lax.dot_general(k1_ref[...].astype(BF16), q_ref[:, :half].astype(BF16), nt, preferred_element_type=F32)
    s2 = lax.dot_general(k2_ref[...].astype(BF16), q_ref[:, half:].astype(BF16), nt, preferred_element_type=F32)
    rid = lax.broadcasted_iota(jnp.int32, (PEER_NKEYS, tt), 0).astype(F32)
    _top16(s1, rid, None, v1_ref, i1_ref)
    _top16(s2, rid, None, v2_ref, i2_ref)
    npairs = len(_PEER_PAIRS)
    cand_ref[npairs:, :] = jnp.full((_PEER_CAND_ROWS - npairs, tt), -jnp.inf, F32)
    cidx_ref[npairs:, :] = jnp.full((_PEER_CAND_ROWS - npairs, tt), -1.0, F32)
    for r, (a, b) in enumerate(_PEER_PAIRS):
        cand_ref[r:r + 1, :] = v1_ref[a:a + 1, :] + v2_ref[b:b + 1, :]
        cidx_ref[r:r + 1, :] = i1_ref[a:a + 1, :] * float(PEER_NKEYS) + i2_ref[b:b + 1, :]
    rid2 = lax.broadcasted_iota(jnp.int32, (_PEER_CAND_ROWS, tt), 0).astype(F32)
    _top16(cand_ref[...], rid2, cidx_ref[...], vc_ref, ic_ref)
    sc = vc_ref[...]
    e = jnp.exp(sc - sc[0:1, :])
    g_ref[...] = e / jnp.sum(e, axis=0, keepdims=True)
    idx_ref[...] = ic_ref[...].astype(jnp.int32)


def _peer_route(q, k1, k2):
    t = q.shape[0]
    tt = PEER_TT
    dk = 2 * PEER_NKEYS
    kspec = pl.BlockSpec((None, PEER_NKEYS, dk // 2), lambda i, h: (h, 0, 0))
    ospec = pl.BlockSpec((None, PEER_TOPK, tt), lambda i, h: (i, h, 0))
    sm = pltpu.VMEM((PEER_TOPK, tt), F32)
    return pl.pallas_call(
        _route_kernel,
        grid=(t // tt, PEER_HEADS),
        in_specs=[pl.BlockSpec((tt, dk), lambda i, h: (i, h)), kspec, kspec],
        out_specs=[ospec, ospec],
        out_shape=[jax.ShapeDtypeStruct((t // tt, PEER_HK, tt), jnp.int32),
                   jax.ShapeDtypeStruct((t // tt, PEER_HK, tt), F32)],
        scratch_shapes=[sm, sm, sm, sm, sm, sm,
                        pltpu.VMEM((_PEER_CAND_ROWS, tt), F32), pltpu.VMEM((_PEER_CAND_ROWS, tt), F32)],
        compiler_params=_cparams("parallel", "parallel"),
        name="peer_route",
    )(q, k1, k2)


def _pack_table(tab):
    e, d = tab.shape
    tb = lax.bitcast_convert_type(tab.astype(BF16), jnp.uint16).astype(jnp.uint32)
    words = tb[:, :d // 2] | (tb[:, d // 2:] << 16)
    return lax.bitcast_convert_type(words, F32)


def _sc_unpack(words):
    bits = lax.bitcast_convert_type(words, jnp.int32)
    return lax.bitcast_convert_type(bits << 16, F32), lax.bitcast_convert_type(bits & jnp.int32(-65536), F32)


def _sc_gelu(x):
    u = math.sqrt(2.0 / math.pi) * (x + 0.044715 * (x * x * x))
    tanh_u = 1.0 - 2.0 / (jnp.exp(2.0 * u) + 1.0)
    return 0.5 * x * (1.0 + tanh_u)


def _sc_peer(u_tab, v_tab, idx_tm, g_tm, h, side_idx=None):
    t, nk = idx_tm.shape
    w = u_tab.shape[1]
    d = 2 * w
    half_rows = SC_GATHER_ROWS
    nbuf = SC_BUFFERS
    per_tok = nk // half_rows
    mesh = plsc.VectorSubcoreMesh(core_axis_name="core", subcore_axis_name="subcore")
    n_workers = mesh.num_cores * mesh.num_subcores
    tpw = t // n_workers
    tb = SC_TOKEN_BLOCK
    steps = per_tok * tb
    dot_rows = SC_LANES
    sum_rows = 16
    assert t % n_workers == 0 and tpw % tb == 0 and w % SC_LANES == 0
    assert nk % half_rows == 0 and half_rows % SC_LANES == 0 and steps % nbuf == 0

    n_side = 0 if side_idx is None else side_idx.shape[0]
    side_w = SC_SIDE_ROWS
    lw = n_side // n_workers
    n_win = lw // side_w
    blk_side = steps * side_w
    if n_side:
        assert nbuf == 2 and steps % 2 == 0 and n_side % (n_workers * blk_side) == 0
        assert n_win + 2 <= (tpw // tb) * steps

    def lanes(start):
        return pl.ds(pl.multiple_of(start, SC_LANES), SC_LANES)

    y_type = jax.ShapeDtypeStruct((t, d), F32)
    side_type = jax.ShapeDtypeStruct((n_side, w), F32)
    scratch = [pltpu.VMEM((tb * nk,), jnp.int32), pltpu.VMEM((tb, d), F32), pltpu.VMEM((tb, nk), F32),
               pltpu.VMEM((nbuf, half_rows, w), F32), pltpu.VMEM((tb, nk), F32), pltpu.VMEM((tb, d), F32),
               pltpu.SemaphoreType.DMA((nbuf,))]
    if n_side:
        scratch += [pltpu.VMEM((2 * blk_side,), jnp.int32), pltpu.VMEM((2, 2, side_w, w), F32),
                    pltpu.SemaphoreType.DMA((8,))]

    @functools.partial(
        pl.kernel, out_type=[y_type, side_type, side_type] if n_side else y_type, mesh=mesh, scratch_types=scratch,
        compiler_params=pltpu.CompilerParams(needs_layout_passes=False))
    def peer_kernel(u_hbm, v_hbm, idx_hbm, g_hbm, h_hbm, *refs):
        if n_side:
            (sidx_hbm, y_hbm, gu_hbm, gv_hbm, idx_v, h_v, g_v, rows_v, coef_v, y_v, sems,
             sidx_v, sbuf, ssems) = refs
        else:
            y_hbm, idx_v, h_v, g_v, rows_v, coef_v, y_v, sems = refs
        wid = lax.axis_index("subcore") * mesh.num_cores + lax.axis_index("core")
        lane = lax.iota(jnp.int32, SC_LANES)

        def gather(tab_hbm, step, buf):
            rows = idx_v.at[pl.ds(pl.multiple_of(step * half_rows, half_rows), half_rows)]
            return pltpu.make_async_copy(tab_hbm.at[rows], rows_v.at[buf], sems.at[buf])

        def prime(tab_hbm):
            for j in range(nbuf - 1):
                gather(tab_hbm, j, j).start()

        def side_in(tab_hbm, tab, n, par):
            slot = ((n // steps) % 2) * blk_side + (n % steps) * side_w
            rows = sidx_v.at[pl.ds(pl.multiple_of(slot, side_w), side_w)]
            return pltpu.make_async_copy(tab_hbm.at[rows], sbuf.at[tab, par], ssems.at[tab * 4 + par])

        def side_out(out_hbm, tab, n, par):
            dst = out_hbm.at[pl.ds(pl.multiple_of(wid * lw + n * side_w, side_w), side_w)]
            return pltpu.make_async_copy(sbuf.at[tab, par], dst, ssems.at[tab * 4 + 2 + par])

        def side_step(tab_hbm, out_hbm, tab, n, par):
            @pl.when((n >= 2) & (n < n_win + 2))
            def _():
                side_out(out_hbm, tab, n - 2, par).wait()

            @pl.when(n < n_win)
            def _():
                side_in(tab_hbm, tab, n, par).start()

            @pl.when((n >= 1) & (n < n_win + 1))
            def _():
                side_in(tab_hbm, tab, n - 1, 1 - par).wait()
                side_out(out_hbm, tab, n - 1, 1 - par).start()

        def run(tab_hbm, work, blk=None, side=None):
            @pl.loop(0, steps, step=nbuf)
            def _(s):
                for j in range(nbuf):
                    if side is not None:
                        side_step(tab_hbm, side[0], side[1], blk * steps + s + j, j)
                    ahead = s + j + nbuf - 1

                    @pl.when(ahead < steps)
                    def _():
                        gather(tab_hbm, ahead, (j + nbuf - 1) % nbuf).start()

                    gather(tab_hbm, s + j, j).wait()
                    work(s + j, j)

        def dots(step, buf):
            tok = step // per_tok
            col0 = (step % per_tok) * half_rows

            def group(q, _):
                r0 = q * dot_rows

                def chunk(c, accs):
                    h_lo = h_v[tok, lanes(c * SC_LANES)]
                    h_hi = h_v[tok, lanes(w + c * SC_LANES)]
                    out = []
                    for r in range(dot_rows):
                        lo, hi = _sc_unpack(rows_v[buf, r0 + r, lanes(c * SC_LANES)])
                        out.append(accs[r] + lo * h_lo + hi * h_hi)
                    return tuple(out)

                accs = lax.fori_loop(0, w // SC_LANES, chunk, (jnp.zeros((SC_LANES,), F32),) * dot_rows)
                vec = jnp.zeros((SC_LANES,), F32)
                for r in range(dot_rows):
                    vec = jnp.where(lane == r, jnp.sum(accs[r]), vec)
                coef_v[tok, lanes(col0 + r0)] = vec
                return 0

            lax.fori_loop(0, half_rows // dot_rows, group, 0)

        def gates():
            @pl.loop(0, tb)
            def _(tok):
                @pl.loop(0, nk // SC_LANES)
                def _(q):
                    ds = lanes(q * SC_LANES)
                    coef_v[tok, ds] = g_v[tok, ds] * _sc_gelu(coef_v[tok, ds])

        def weighted_sum(step, buf):
            tok = step // per_tok
            col0 = (step % per_tok) * half_rows

            @pl.when(col0 == 0)
            def _():
                @plsc.parallel_loop(0, d // SC_LANES)
                def _(c):
                    y_v[tok, lanes(c * SC_LANES)] = jnp.zeros((SC_LANES,), F32)

            def group(g, _):
                r0 = g * sum_rows
                cvec = coef_v[tok, lanes(col0 + (r0 // SC_LANES) * SC_LANES)]
                base = r0 % SC_LANES
                cs = [cvec.at[jnp.full((SC_LANES,), base + r, jnp.int32)].get(mode="promise_in_bounds")
                      for r in range(sum_rows)]

                @plsc.parallel_loop(0, w // SC_LANES, unroll=SC_LANES // sum_rows)
                def _(c):
                    los, his = [], []
                    for r in range(sum_rows):
                        lo, hi = _sc_unpack(rows_v[buf, r0 + r, lanes(c * SC_LANES)])
                        los.append(cs[r] * lo)
                        his.append(cs[r] * hi)
                    while len(los) > 1:
                        los = [a + b for a, b in zip(los[::2], los[1::2])]
                        his = [a + b for a, b in zip(his[::2], his[1::2])]
                    y_v[tok, lanes(c * SC_LANES)] = y_v[tok, lanes(c * SC_LANES)] + los[0]
                    y_v[tok, lanes(w + c * SC_LANES)] = y_v[tok, lanes(w + c * SC_LANES)] + his[0]

                return 0

            lax.fori_loop(0, half_rows // sum_rows, group, 0)

        @pl.loop(0, tpw // tb)
        def _(blk):
            tok0 = wid * tpw + blk * tb
            pltpu.sync_copy(idx_hbm.at[pl.ds(pl.multiple_of(tok0 * nk, nk), tb * nk)], idx_v)
            prime(u_hbm)
            pltpu.sync_copy(h_hbm.at[pl.ds(tok0, tb)], h_v)
            pltpu.sync_copy(g_hbm.at[pl.ds(tok0, tb)], g_v)
            if n_side:
                @pl.when(blk * steps < n_win)
                def _():
                    src = sidx_hbm.at[pl.ds(pl.multiple_of(wid * lw + blk * blk_side, blk_side), blk_side)]
                    pltpu.sync_copy(src, sidx_v.at[pl.ds(pl.multiple_of((blk % 2) * blk_side, blk_side), blk_side)])

            run(u_hbm, dots, blk, (gu_hbm, 0) if n_side else None)
            prime(v_hbm)
            gates()
            run(v_hbm, weighted_sum, blk, (gv_hbm, 1) if n_side else None)
            pltpu.sync_copy(y_v, y_hbm.at[pl.ds(tok0, tb)])

    if n_side:
        return peer_kernel(u_tab, v_tab, idx_tm.reshape(t * nk), g_tm, h, side_idx)
    return peer_kernel(u_tab, v_tab, idx_tm.reshape(t * nk), g_tm, h)


def _unpack(words):
    w = pltpu.bitcast(words, jnp.uint32)
    lo = pltpu.bitcast(w << 16, F32)
    hi = pltpu.bitcast(w & jnp.uint32(0xFFFF0000), F32)
    return lo, hi


def _peer_act_kernel(h_ref, gu_ref, g_ref, coef_ref, act_ref):
    c = pl.program_id(1)
    half = h_ref.shape[1] // 2
    h_lo, h_hi = h_ref[:, :half], h_ref[:, half:]
    lane = lax.broadcasted_iota(jnp.int32, act_ref.shape, 1)

    @pl.when(c == 0)
    def _():
        act_ref[...] = jnp.zeros(act_ref.shape, F32)

    act = act_ref[...]
    for k in range(PEER_HKC):
        lo, hi = _unpack(gu_ref[k])
        a = jnp.sum(lo * h_lo + hi * h_hi, axis=-1, keepdims=True)
        act = jnp.where(lane == c * PEER_HKC + k, a, act)
    act_ref[...] = act

    @pl.when(c == pl.num_programs(1) - 1)
    def _():
        coef_ref[...] = g_ref[...].T * jax.nn.gelu(act)


def _peer_act(h, gu, g):
    t, d = h.shape
    tt = PEER_TT
    nt = t // tt
    return pl.pallas_call(
        _peer_act_kernel,
        grid=(nt, PEER_HK // PEER_HKC),
        in_specs=[pl.BlockSpec((tt, d), lambda i, c: (i, 0)),
                  pl.BlockSpec((None, PEER_HKC, tt, d // 2), lambda i, c: (i, c, 0, 0)),
                  pl.BlockSpec((None, PEER_HK, tt), lambda i, c: (i, 0, 0))],
        out_specs=pl.BlockSpec((None, tt, PEER_HK), lambda i, c: (i, 0, 0)),
        out_shape=jax.ShapeDtypeStruct((nt, tt, PEER_HK), F32),
        scratch_shapes=[pltpu.VMEM((tt, PEER_HK), F32)],
        compiler_params=_cparams("parallel", "arbitrary"),
        name="peer_act",
    )(h, gu, g)


def _peer_mix_kernel(coef_ref, gv_ref, y_ref):
    c = pl.program_id(1)
    half = y_ref.shape[1] // 2

    @pl.when(c == 0)
    def _():
        y_ref[...] = jnp.zeros(y_ref.shape, F32)

    coef = coef_ref[...]
    lane = lax.broadcasted_iota(jnp.int32, coef.shape, 1)
    acc_lo, acc_hi = y_ref[:, :half], y_ref[:, half:]
    for k in range(PEER_HKC):
        ck = jnp.sum(jnp.where(lane == c * PEER_HKC + k, coef, 0.0), axis=-1, keepdims=True)
        lo, hi = _unpack(gv_ref[k])
        acc_lo = acc_lo + ck * lo
        acc_hi = acc_hi + ck * hi
    y_ref[:, :half] = acc_lo
    y_ref[:, half:] = acc_hi


def _peer_mix(coef, gv):
    nt, tt, _ = coef.shape
    w = gv.shape[-1]
    return pl.pallas_call(
        _peer_mix_kernel,
        grid=(nt, PEER_HK // PEER_HKC),
        in_specs=[pl.BlockSpec((None, tt, PEER_HK), lambda i, c: (i, 0, 0)),
                  pl.BlockSpec((None, PEER_HKC, tt, w), lambda i, c: (i, c, 0, 0))],
        out_specs=pl.BlockSpec((tt, 2 * w), lambda i, c: (i, 0)),
        out_shape=jax.ShapeDtypeStruct((nt * tt, 2 * w), F32),
        compiler_params=_cparams("parallel", "arbitrary"),
        name="peer_mix",
    )(coef, gv)


def _peer_out_kernel(x_ref, gt_ref, y_ref, gf_ref, o_ref, *, final):
    x = x_ref[...] + gt_ref[...] * y_ref[...]
    if final:
        x = x * lax.rsqrt(jnp.mean(x * x, axis=-1, keepdims=True) + EPS) * gf_ref[...]
    o_ref[...] = x


def _peer_out(x, gt, y, g_final, final):
    t, d = x.shape
    tt = PEER_TT
    nt = t // tt
    row = pl.BlockSpec((tt, d), lambda i: (i, 0))
    if gt.ndim == 3:
        per_seq = nt // gt.shape[0]
        gt_spec = pl.BlockSpec((None, 1, d), lambda i: (i // per_seq, 0, 0))
    else:
        gt_spec = row
    return pl.pallas_call(
        functools.partial(_peer_out_kernel, final=final),
        grid=(nt,),
        in_specs=[row, gt_spec, row, _full((1, d))],
        out_specs=row, out_shape=jax.ShapeDtypeStruct((t, d), F32),
        compiler_params=_cparams("parallel"),
        name="peer_out",
    )(x, gt, y, g_final.reshape(1, d))


def _peer(x, g_ffn, sc2, sh2, gt2, w_q, k1, k2, u_pack, v_pack, g_final, final, tl):
    b, l, d = x.shape
    t = b * l
    q, h = _norm_mod_matmul(x, g_ffn, sc2, sh2, w_q, tl, want_h=True)
    idx, g = _peer_route(q.reshape(t, -1), k1, k2)
    h = h.reshape(t, d)
    nt = t // PEER_TT
    n_tc = nt * PEER_TC_TILES // 32
    n_sc = nt - n_tc
    t_sc = n_sc * PEER_TT
    token_major = lambda a: jnp.transpose(a, (0, 2, 1)).reshape(-1, PEER_HK)
    if n_tc:
        y_sc, gu, gv = _sc_peer(u_pack, v_pack, token_major(idx[:n_sc]), token_major(g[:n_sc]), h[:t_sc],
                                idx[n_sc:].reshape(-1))
        rows = lambda a: a.reshape(n_tc, PEER_HK, PEER_TT, d // 2)
        y_tc = _peer_mix(_peer_act(h[t_sc:], rows(gu), g[n_sc:]), rows(gv))
        y = jnp.concatenate([y_sc, y_tc], axis=0)
    else:
        y = _sc_peer(u_pack, v_pack, token_major(idx), token_major(g), h)
    if l % PEER_TT:
        gt2 = jnp.broadcast_to(gt2, (b, l, d)).reshape(t, d)
    return _peer_out(x.reshape(t, d), gt2, y, g_final, final).reshape(b, l, d)


def _trunk(x, mod, pos0, past, p, tl):
    b, l, d = x.shape
    outs = {}
    for layer in range(DEPTH):
        sh1, sc1, gt1, sh2, sc2, gt2 = [m[:, None, :] for m in jnp.split(mod[layer], 6, axis=-1)]
        if layer not in p['packed']:
            u_tab, v_tab = p['peer_u'][layer], p['peer_v'][layer]
            if layer:
                u_tab, v_tab, x = lax.optimization_barrier((u_tab, v_tab, x))
            p['packed'][layer] = (_pack_table(u_tab), _pack_table(v_tab))
        u_pack, v_pack = p['packed'][layer]
        peer = functools.partial(
            _peer, g_ffn=p['g_ffn'][layer], sc2=sc2, sh2=sh2, gt2=gt2, w_q=p['peer_w_q'][layer].astype(BF16),
            k1=p['peer_k1'][layer], k2=p['peer_k2'][layer], u_pack=u_pack, v_pack=v_pack,
            g_final=p['g_final'], final=layer == DEPTH - 1, tl=tl)
        if layer % 2 == 0:
            e = layer // 2
            if past is None:
                hr = jnp.zeros((b, S5_GROUPS, S5_STATE), F32)
                hi = jnp.zeros((b, S5_GROUPS, S5_STATE), F32)
            else:
                hr, hi = past['s5_re'][e], past['s5_im'][e]
            n_seg = EVEN_SEGMENTS if l % (EVEN_SEGMENTS * max(tl, GM_CHUNK, PEER_TT)) == 0 else 1
            segs = []
            for xs in jnp.split(x, n_seg, axis=1):
                y_in = _norm_mod_matmul(xs, p['g_mix'][layer], sc1, sh1, p['w_in_e'][e].astype(BF16), tl)
                ya, hr, hi = _s5(y_in, hr, hi, p['s5'][e], p['s5_d'][e], p['w_glu'][e], p['b_glu'][e], tl)
                yb, vn = _gmlp(y_in, p['gm_g_v'][e], p['gm_w_s'][e], p['gm_b_s'][e], tl, want_vn=past is not None)
                segs.append(peer(_proj_residual(xs, gt1, ya, yb, p['w_out_e'][e], tl)))
            x = segs[0] if n_seg == 1 else jnp.concatenate(segs, axis=1)
            outs.update(s5_re=hr, s5_im=hi, gm_v=vn)
        else:
            o = layer // 2
            w_in = jnp.pad(p['w_in_o'][o], ((0, 0), (0, ODD_IN_PAD - p['w_in_o'][o].shape[1]))).astype(BF16)
            y_in = _norm_mod_matmul(x, p['g_mix'][layer], sc1, sh1, w_in, tl)
            qd, kd, qm, lat, kpe = _odd_prep(y_in, pos0, p['mla_g_q'][o], p['mla_w_uq'][o], p['mla_g_kv'][o], tl)
            v_new = y_in[:, :, 2 * 512:3 * 512]
            lam_init = 0.8 - 0.6 * math.exp(-0.3 * layer)
            lam = (jnp.exp(jnp.sum(p['da_lq1'][o] * p['da_lk1'][o])) - jnp.exp(jnp.sum(p['da_lq2'][o] * p['da_lk2'][o]))
                   + lam_init).reshape(1).astype(F32)
            g_sub = p['da_g_sub'][o].reshape(1, DA_DV)
            if past is None:
                k_all, v_all, v_col, lat_all, kpe_all = kd, y_in, 2, lat, kpe
                tq = tk = tl
            else:
                k_all = jnp.concatenate([past['da_k'][o].reshape(b, -1, 512), kd], axis=1)
                v_all = jnp.concatenate([past['da_v'][o].reshape(b, -1, 512), v_new], axis=1)
                lat_all = jnp.concatenate([past['mla_lat'][o], lat], axis=1)
                kpe_all = jnp.concatenate([past['mla_kpe'][o], kpe], axis=1)
                v_col, tq, tk = 0, l, k_all.shape[1]
            yc = _flash(qd, k_all, v_all, lam, g_sub, nhm=2 * DA_HEADS, dqk=DA_DH, maps=2, scale=DA_DH ** -0.5,
                        pos0=pos0, tq=tq, tk=tk, out_scale=1.0 - lam_init, v_col=v_col)
            km, vm = _mla_kv(lat_all, kpe_all, p['mla_w_ukv'][o], tk)
            yd = _flash(qm, km, vm, lam, g_sub, nhm=MLA_HEADS, dqk=LANES, maps=1,
                        scale=(MLA_NOPE + MLA_ROPE) ** -0.5, pos0=pos0, tq=tq, tk=tk, out_scale=1.0)
            x = peer(_proj_residual(x, gt1, yc, yd, p['w_out_o'][o], tl))
            outs.update(da_k=kd.reshape(b, l, DA_HEADS, 2 * DA_DH), da_v=v_new.reshape(b, l, DA_HEADS, DA_DV),
                        mla_lat=lat, mla_kpe=kpe)
    return x, outs


def kernel(x_prompt, x_sample, c_prompt, c_sample, state_s5_re, state_s5_im, cache_da_k, cache_da_v, cache_mla_latent, cache_mla_kpe, w_ada, b_ada, g_mix, g_ffn, g_final, w_in_e, w_out_e, s5_lam_re, s5_lam_im, s5_log_dt, s5_b_re, s5_b_im, s5_c_re, s5_c_im, s5_d, w_glu, b_glu, gm_g_v, gm_w_s, gm_b_s, w_in_o, w_out_o, da_lq1, da_lk1, da_lq2, da_lk2, da_g_sub, mla_g_q, mla_w_uq, mla_g_kv, mla_w_ukv, peer_w_q, peer_k1, peer_k2, peer_u, peer_v):
    p = dict(g_mix=g_mix, g_ffn=g_ffn, g_final=g_final, w_in_e=w_in_e, w_out_e=w_out_e, s5_d=s5_d, w_glu=w_glu,
             b_glu=b_glu, gm_g_v=gm_g_v, gm_w_s=gm_w_s, gm_b_s=gm_b_s, w_in_o=w_in_o, w_out_o=w_out_o,
             da_lq1=da_lq1, da_lk1=da_lk1, da_lq2=da_lq2, da_lk2=da_lk2, da_g_sub=da_g_sub, mla_g_q=mla_g_q,
             mla_w_uq=mla_w_uq, mla_g_kv=mla_g_kv, mla_w_ukv=mla_w_ukv, peer_w_q=peer_w_q, peer_k1=peer_k1,
             peer_k2=peer_k2)
    n_even = (DEPTH + 1) // 2
    p['s5'] = [_s5_params(s5_lam_re[e], s5_lam_im[e], s5_log_dt[e], s5_b_re[e], s5_b_im[e], s5_c_re[e], s5_c_im[e])
               for e in range(n_even)]
    p['peer_u'], p['peer_v'], p['packed'] = peer_u, peer_v, {}
    past = dict(s5_re=state_s5_re, s5_im=state_s5_im, da_k=cache_da_k, da_v=cache_da_v,
                mla_lat=cache_mla_latent, mla_kpe=cache_mla_kpe)
    nb = x_prompt.shape[0]
    mod = _ada(jnp.concatenate([c_prompt, c_sample], axis=0), w_ada, b_ada)
    past_len = cache_da_k.shape[2]
    parts = [_trunk(x_prompt[s:s + 1], mod[:, s:s + 1], 0, None, p, tl=PROMPT_TILE) for s in range(nb)]
    y_p = jnp.concatenate([y for y, _ in parts], axis=0)
    o_p = {k: jnp.concatenate([o[k] for _, o in parts], axis=0) for k in parts[0][1] if parts[0][1][k] is not None}
    y_s, o_s = _trunk(x_sample, mod[:, nb:], past_len, past, p, tl=x_sample.shape[1])
    st = lambda a: a[None]
    return (y_p, y_s, st(o_p['s5_re']), st(o_p['s5_im']), st(o_s['s5_re']), st(o_s['s5_im']), st(o_s['gm_v']),
            st(o_p['da_k']), st(o_p['da_v']), st(o_s['da_k']), st(o_s['da_v']),
            st(o_p['mla_lat']), st(o_p['mla_kpe']), st(o_s['mla_lat']), st(o_s['mla_kpe']))
```

```python
import functools
import math

import jax
import jax.numpy as jnp
from jax import lax
from jax.experimental import pallas as pl
from jax.experimental.pallas import tpu as pltpu
from jax.experimental.pallas import tpu_sc as plsc

F32 = jnp.float32
BF16 = jnp.bfloat16

D_MODEL = 1024
DEPTH = 2
CHUNK = 64
ROPE_THETA = 500000.0
EPS = 1e-6
NEG = -1e30

S5_WIDTH = 512
S5_GROUP = 16
S5_GROUPS = 32
S5_STATE = 64
GM_WIDTH = 512
GM_HEADS = 4
GM_CHUNK = 128
DA_HEADS = 4
DA_DH = 64
DA_DV = 128
DA_ROPE = 16
MLA_HEADS = 4
MLA_Q_RANK = 256
MLA_KV_RANK = 128
MLA_NOPE = 64
MLA_ROPE = 32
MLA_DV = 128
PEER_HEADS = 8
PEER_NKEYS = 128
PEER_TOPK = 16
PEER_HK = PEER_HEADS * PEER_TOPK

LANES = 128
VMEM_LIMIT = 48 << 20
PROMPT_TILE = 512
ODD_IN_PAD = 2048
EVEN_SEGMENTS = 2
PEER_TT = 256
SC_LANES = 16
SC_TOKEN_BLOCK = 8
SC_GATHER_ROWS = 64
SC_BUFFERS = 2
SC_SIDE_ROWS = 16
PEER_TC_TILES = 6
PEER_HKC = 16

_HI = lax.Precision.HIGHEST


def _cparams(*sem):
    return pltpu.CompilerParams(dimension_semantics=sem, vmem_limit_bytes=VMEM_LIMIT)


def _full(shape):
    n = len(shape)
    return pl.BlockSpec(shape, lambda *_: (0,) * n)


def _ada_kernel(c_ref, w_ref, b_ref, o_ref):
    c = c_ref[...]
    s = c * jax.nn.sigmoid(c)
    o_ref[...] = jnp.dot(s, w_ref[...], precision=_HI, preferred_element_type=F32) + b_ref[...]


def _ada(c_all, w_ada, b_ada):
    r = c_all.shape[0]
    tn = 1536
    return pl.pallas_call(
        _ada_kernel,
        grid=(DEPTH, 6 * D_MODEL // tn),
        in_specs=[_full((r, D_MODEL)),
                  pl.BlockSpec((None, D_MODEL, tn), lambda l, j: (l, 0, j)),
                  pl.BlockSpec((None, 1, tn), lambda l, j: (l, 0, j))],
        out_specs=pl.BlockSpec((None, r, tn), lambda l, j: (l, 0, j)),
        out_shape=jax.ShapeDtypeStruct((DEPTH, r, 6 * D_MODEL), F32),
        compiler_params=_cparams("parallel", "parallel"),
        name="ada",
    )(c_all, w_ada, b_ada.reshape(DEPTH, 1, 6 * D_MODEL))


def _nmm_kernel(x_ref, g_ref, sc_ref, sh_ref, w_ref, o_ref, *h_ref):
    x = x_ref[...]
    ms = jnp.mean(x * x, axis=-1, keepdims=True)
    h = x * lax.rsqrt(ms + EPS) * g_ref[...] * (1.0 + sc_ref[...]) + sh_ref[...]
    o_ref[...] = jnp.dot(h.astype(BF16), w_ref[...], preferred_element_type=F32)
    if h_ref:
        h_ref[0][...] = h


def _norm_mod_matmul(x, g, sc, sh, w, tl, want_h=False):
    b, l, d = x.shape
    n = w.shape[1]
    row = pl.BlockSpec((None, 1, d), lambda bi, i: (bi, 0, 0))
    out_shape = [jax.ShapeDtypeStruct((b, l, n), F32)]
    out_specs = [pl.BlockSpec((None, tl, n), lambda bi, i: (bi, i, 0))]
    if want_h:
        out_shape.append(jax.ShapeDtypeStruct((b, l, d), F32))
        out_specs.append(pl.BlockSpec((None, tl, d), lambda bi, i: (bi, i, 0)))
    res = pl.pallas_call(
        _nmm_kernel,
        grid=(b, l // tl),
        in_specs=[pl.BlockSpec((None, tl, d), lambda bi, i: (bi, i, 0)), _full((1, d)), row, row, _full((d, n))],
        out_specs=out_specs, out_shape=out_shape,
        compiler_params=_cparams("parallel", "parallel"),
        name="norm_mod_matmul",
    )(x, g.reshape(1, d), sc, sh, w)
    return res if want_h else res[0]


def _s5_kernel(xa_ref, h0r_ref, h0i_ref, bblk_ref, ar_ref, ai_ref, cr_ref, ci_ref, d_ref, wglu_ref, bglu_ref,
               ya_ref, hr_out_ref, hi_out_ref, bur, bui, st_r, st_i, *, tl):
    nj = S5_WIDTH // LANES
    half = 8 * S5_STATE

    @pl.when(pl.program_id(1) == 0)
    def _():
        st_r[...] = h0r_ref[...]
        st_i[...] = h0i_ref[...]

    xa = xa_ref[...]
    for j in range(nj):
        bu = jnp.dot(xa[:, j * LANES:(j + 1) * LANES], bblk_ref[j], precision=_HI, preferred_element_type=F32)
        bur[j] = bu[:, :half]
        bui[j] = bu[:, half:]

    for j0 in range(0, nj, 2):
        js = (j0, j0 + 1)
        a_r = [ar_ref[j] for j in js]
        a_i = [ai_ref[j] for j in js]

        def body(k, carry, js=js, a_r=a_r, a_i=a_i):
            carry = list(carry)
            base = pl.multiple_of(k * 8, 8)
            for s in range(8):
                t = base + s
                for q, j in enumerate(js):
                    hr, hi = carry[2 * q], carry[2 * q + 1]
                    nhr = a_r[q] * hr - a_i[q] * hi + bur[j, pl.ds(t, 1), :]
                    nhi = a_r[q] * hi + a_i[q] * hr + bui[j, pl.ds(t, 1), :]
                    bur[j, pl.ds(t, 1), :] = nhr
                    bui[j, pl.ds(t, 1), :] = nhi
                    carry[2 * q], carry[2 * q + 1] = nhr, nhi
            return tuple(carry)

        init = (st_r[js[0]], st_i[js[0]], st_r[js[1]], st_i[js[1]])
        fin = lax.fori_loop(0, tl // 8, body, init)
        st_r[js[0]], st_i[js[0]], st_r[js[1]], st_i[js[1]] = fin

    hr_out_ref[...] = st_r[...]
    hi_out_ref[...] = st_i[...]

    ys = []
    for j in range(nj):
        y = (jnp.dot(bur[j].astype(BF16), cr_ref[j], preferred_element_type=F32)
             + jnp.dot(bui[j].astype(BF16), ci_ref[j], preferred_element_type=F32))
        ys.append(y + d_ref[:, j * LANES:(j + 1) * LANES] * xa[:, j * LANES:(j + 1) * LANES])
    y = jax.nn.gelu(jnp.concatenate(ys, axis=1))
    z = jnp.dot(y.astype(BF16), wglu_ref[...], preferred_element_type=F32) + bglu_ref[...]
    ya_ref[...] = z[:, :S5_WIDTH] * jax.nn.sigmoid(z[:, S5_WIDTH:])


def _s5_params(lam_re, lam_im, log_dt, b_re, b_im, c_re, c_im):
    lam = lax.complex(lam_re.astype(F32), lam_im.astype(F32))
    dt = jnp.exp(log_dt.astype(F32))[:, None]
    a_bar = jnp.exp(lam * dt)
    b_bar = ((a_bar - 1.0) / lam)[..., None] * lax.complex(b_re.astype(F32), b_im.astype(F32))
    nj = S5_GROUPS // 8
    eye = jnp.eye(8, dtype=F32)

    def blk_b(m):
        m = m.reshape(nj, 8, S5_STATE, S5_GROUP)
        return jnp.einsum('jgpc,gh->jgchp', m, eye).reshape(nj, 8 * S5_GROUP, 8 * S5_STATE)

    def blk_c(m):
        m = m.reshape(nj, 8, S5_GROUP, S5_STATE)
        return jnp.einsum('jgcp,gh->jgphc', m, eye).reshape(nj, 8 * S5_STATE, 8 * S5_GROUP)

    bblk = jnp.concatenate([blk_b(jnp.real(b_bar)), blk_b(jnp.imag(b_bar))], axis=-1)
    cr = blk_c(c_re.astype(F32)).astype(BF16)
    ci = (-blk_c(c_im.astype(F32))).astype(BF16)
    a_r = jnp.real(a_bar).reshape(nj, 1, 8 * S5_STATE)
    a_i = jnp.imag(a_bar).reshape(nj, 1, 8 * S5_STATE)
    return bblk, a_r, a_i, cr, ci


def _s5(y_in, h0_re, h0_im, sp, d_skip, w_glu, b_glu, tl):
    b, l, _ = y_in.shape
    nj = S5_GROUPS // 8
    half = 8 * S5_STATE
    bblk, a_r, a_i, cr, ci = sp
    st = pl.BlockSpec((None, nj, 1, half), lambda bi, i: (bi, 0, 0, 0))
    ya, hr, hi = pl.pallas_call(
        functools.partial(_s5_kernel, tl=tl),
        grid=(b, l // tl),
        in_specs=[pl.BlockSpec((None, tl, S5_WIDTH), lambda bi, i: (bi, i, 0)), st, st,
                  _full(bblk.shape), _full(a_r.shape), _full(a_i.shape), _full(cr.shape), _full(ci.shape),
                  _full((1, S5_WIDTH)), _full((S5_WIDTH, 2 * S5_WIDTH)), _full((1, 2 * S5_WIDTH))],
        out_specs=[pl.BlockSpec((None, tl, S5_WIDTH), lambda bi, i: (bi, i, 0)), st, st],
        out_shape=[jax.ShapeDtypeStruct((b, l, S5_WIDTH), F32),
                   jax.ShapeDtypeStruct((b, nj, 1, half), F32), jax.ShapeDtypeStruct((b, nj, 1, half), F32)],
        scratch_shapes=[pltpu.VMEM((nj, tl, half), F32), pltpu.VMEM((nj, tl, half), F32),
                        pltpu.VMEM((nj, 1, half), F32), pltpu.VMEM((nj, 1, half), F32)],
        compiler_params=_cparams("parallel", "arbitrary"),
        name="s5",
    )(y_in, h0_re.reshape(b, nj, 1, half), h0_im.reshape(b, nj, 1, half), bblk, a_r, a_i, cr, ci,
      d_skip.reshape(1, S5_WIDTH), w_glu.astype(BF16), b_glu.reshape(1, 2 * S5_WIDTH))
    return ya, hr.reshape(b, S5_GROUPS, S5_STATE), hi.reshape(b, S5_GROUPS, S5_STATE)


def _gmlp_kernel(u_ref, v_ref, gv_ref, ws_ref, bias_ref, yb_ref, *vn_ref, tl, lc):
    hd = GM_WIDTH // GM_HEADS
    gv = jax.nn.gelu(v_ref[...])
    vn = gv * lax.rsqrt(jnp.mean(gv * gv, axis=-1, keepdims=True) + EPS) * gv_ref[...]
    if vn_ref:
        vn_ref[0][...] = vn
    gu = jax.nn.gelu(u_ref[...])
    vb = vn.astype(BF16)
    for c in range(tl // lc):
        rows = slice(c * lc, (c + 1) * lc)
        for h in range(GM_HEADS):
            cols = slice(h * hd, (h + 1) * hd)
            mixed = jnp.dot(ws_ref[h], vb[rows, cols], preferred_element_type=F32) + bias_ref[:, cols]
            yb_ref[rows, cols] = gu[rows, cols] * mixed


def _gmlp(y_in, g_v, w_s, b_s, tl, want_vn):
    b, l, _ = y_in.shape
    lc = min(GM_CHUNK, l)
    hd = GM_WIDTH // GM_HEADS
    tri = jnp.tril(jnp.ones((GM_CHUNK, GM_CHUNK), dtype=bool))
    ws = jnp.where(tri[None], w_s, 0.0)[:, :lc, :lc].astype(BF16)
    bias = jnp.repeat(jnp.transpose(b_s)[:lc], hd, axis=1)
    out_shape = [jax.ShapeDtypeStruct((b, l, GM_WIDTH), F32)]
    out_specs = [pl.BlockSpec((None, tl, GM_WIDTH), lambda bi, i: (bi, i, 0))]
    if want_vn:
        out_shape.append(jax.ShapeDtypeStruct((b, l, GM_WIDTH), F32))
        out_specs.append(pl.BlockSpec((None, tl, GM_WIDTH), lambda bi, i: (bi, i, 0)))
    res = pl.pallas_call(
        functools.partial(_gmlp_kernel, tl=tl, lc=lc),
        grid=(b, l // tl),
        in_specs=[pl.BlockSpec((None, tl, GM_WIDTH), lambda bi, i: (bi, i, 1)),
                  pl.BlockSpec((None, tl, GM_WIDTH), lambda bi, i: (bi, i, 2)),
                  _full((1, GM_WIDTH)), _full(ws.shape), _full(bias.shape)],
        out_specs=out_specs, out_shape=out_shape,
        compiler_params=_cparams("parallel", "parallel"),
        name="gmlp",
    )(y_in, y_in, g_v.reshape(1, GM_WIDTH), ws, bias)
    return (res[0], res[1]) if want_vn else (res[0], None)


def _proj_res_kernel(x_ref, gt_ref, a_ref, b_ref, wa_ref, wb_ref, o_ref):
    mix = (jnp.dot(a_ref[...].astype(BF16), wa_ref[...], preferred_element_type=F32)
           + jnp.dot(b_ref[...].astype(BF16), wb_ref[...], preferred_element_type=F32))
    o_ref[...] = x_ref[...] + gt_ref[...] * mix


def _proj_residual(x, gt, a, bm, w, tl):
    b, l, d = x.shape
    ka = a.shape[-1]
    wa, wb = w[:ka].astype(BF16), w[ka:].astype(BF16)
    blk = lambda n: pl.BlockSpec((None, tl, n), lambda bi, i: (bi, i, 0))
    return pl.pallas_call(
        _proj_res_kernel,
        grid=(b, l // tl),
        in_specs=[blk(d), pl.BlockSpec((None, 1, d), lambda bi, i: (bi, 0, 0)), blk(ka), blk(bm.shape[-1]),
                  _full(wa.shape), _full(wb.shape)],
        out_specs=blk(d), out_shape=jax.ShapeDtypeStruct((b, l, d), F32),
        compiler_params=_cparams("parallel", "parallel"),
        name="proj_residual",
    )(x, gt, a, bm, wa, wb)


def _rope_tables(pos0, l):
    pos = (pos0 + jnp.arange(l, dtype=jnp.int32)).astype(F32)[:, None]
    lane = jnp.arange(LANES)

    def table(period, start, rot):
        half = rot // 2
        r = lane % period - start
        inside = (r >= 0) & (r < rot)
        k = jnp.where(inside, r % half, 0)
        inv = ROPE_THETA ** (-k.astype(F32) * 2.0 / rot)
        ang = pos * inv[None, :]
        cos = jnp.where(inside[None], jnp.cos(ang), 1.0)
        sgn = jnp.where(r < half, -1.0, 1.0)
        sin = jnp.where(inside[None], jnp.sin(ang) * sgn[None], 0.0)
        return cos, sin

    return table(DA_DH, 0, DA_ROPE), table(LANES, MLA_NOPE, MLA_ROPE), table(LANES, 0, MLA_ROPE)


def _rotate(x, cos, sin, period, start, rot):
    half = rot // 2
    n = x.shape[-1]
    reps = n // LANES
    lane = (lax.broadcasted_iota(jnp.int32, x.shape, 1) & (period - 1)) - start
    up = pltpu.roll(x, n - half, 1)
    dn = pltpu.roll(x, half, 1)
    partner = jnp.where(lane < half, up, dn)
    if reps > 1:
        cos = jnp.concatenate([cos] * reps, axis=1)
        sin = jnp.concatenate([sin] * reps, axis=1)
    return x * cos + partner * sin


def _odd_prep_kernel(q_ref, k_ref, cq_ref, ckv_ref, kpe_ref, cd_ref, sd_ref, cm_ref, sm_ref, ck_ref, sk_ref,
                     gq_ref, wuq_ref, gkv_ref, qd_ref, kd_ref, qm_ref, lat_ref, kpe_out_ref):
    cd, sd = cd_ref[...], sd_ref[...]
    qd_ref[...] = _rotate(q_ref[...], cd, sd, DA_DH, 0, DA_ROPE).astype(BF16)
    kd_ref[...] = _rotate(k_ref[...], cd, sd, DA_DH, 0, DA_ROPE)
    cq = cq_ref[...]
    cqn = cq * lax.rsqrt(jnp.mean(cq * cq, axis=-1, keepdims=True) + EPS) * gq_ref[...]
    qm = jnp.dot(cqn.astype(BF16), wuq_ref[...], preferred_element_type=F32)
    qm_ref[...] = _rotate(qm, cm_ref[...], sm_ref[...], LANES, MLA_NOPE, MLA_ROPE).astype(BF16)
    ckv = ckv_ref[...]
    lat_ref[...] = ckv * lax.rsqrt(jnp.mean(ckv * ckv, axis=-1, keepdims=True) + EPS) * gkv_ref[...]
    kpe = _rotate(kpe_ref[...], ck_ref[...], sk_ref[...], LANES, 0, MLA_ROPE)
    kpe_out_ref[...] = kpe[:, :MLA_ROPE]


def _odd_prep(y_in, pos0, g_q, w_uq, g_kv, tl):
    b, l, _ = y_in.shape
    (cd, sd), (cm, sm), (ck, sk) = _rope_tables(pos0, l)
    per = MLA_NOPE + MLA_ROPE
    wuq = jnp.pad(w_uq.reshape(MLA_Q_RANK, MLA_HEADS, per), ((0, 0), (0, 0), (0, LANES - per)))
    wuq = wuq.reshape(MLA_Q_RANK, MLA_HEADS * LANES).astype(BF16)
    col = lambda w, j: pl.BlockSpec((None, tl, w), lambda bi, i: (bi, i, j))
    tab = pl.BlockSpec((tl, LANES), lambda bi, i: (i, 0))
    out = lambda w: pl.BlockSpec((None, tl, w), lambda bi, i: (bi, i, 0))
    return pl.pallas_call(
        _odd_prep_kernel,
        grid=(b, l // tl),
        in_specs=[col(512, 0), col(512, 1), col(256, 6), col(128, 14), col(128, 15),
                  tab, tab, tab, tab, tab, tab,
                  _full((1, MLA_Q_RANK)), _full(wuq.shape), _full((1, MLA_KV_RANK))],
        out_specs=[out(512), out(512), out(512), out(MLA_KV_RANK), out(MLA_ROPE)],
        out_shape=[jax.ShapeDtypeStruct((b, l, 512), BF16), jax.ShapeDtypeStruct((b, l, 512), F32),
                   jax.ShapeDtypeStruct((b, l, 512), BF16), jax.ShapeDtypeStruct((b, l, MLA_KV_RANK), F32),
                   jax.ShapeDtypeStruct((b, l, MLA_ROPE), F32)],
        compiler_params=_cparams("parallel", "parallel"),
        name="odd_prep",
    )(y_in, y_in, y_in, y_in, y_in, cd, sd, cm, sm, ck, sk,
      g_q.reshape(1, MLA_Q_RANK), wuq, g_kv.reshape(1, MLA_KV_RANK))


def _mla_kv_kernel(lat_ref, kpe_ref, wk_ref, wv_ref, place_ref, k_ref, v_ref):
    lat = lat_ref[...].astype(BF16)
    k = (jnp.dot(lat, wk_ref[...], preferred_element_type=F32)
         + jnp.dot(kpe_ref[...].astype(BF16), place_ref[...], preferred_element_type=F32))
    k_ref[...] = k.astype(BF16)
    v_ref[...] = jnp.dot(lat, wv_ref[...], preferred_element_type=F32).astype(BF16)


def _mla_kv(lat_all, kpe_all, w_ukv, tk):
    b, k, _ = lat_all.shape
    per = MLA_NOPE + MLA_DV
    w3 = w_ukv.reshape(MLA_KV_RANK, MLA_HEADS, per)
    wk = jnp.pad(w3[:, :, :MLA_NOPE], ((0, 0), (0, 0), (0, LANES - MLA_NOPE))).reshape(MLA_KV_RANK, MLA_HEADS * LANES)
    wv = w3[:, :, MLA_NOPE:].reshape(MLA_KV_RANK, MLA_HEADS * MLA_DV)
    place = jnp.pad(jnp.eye(MLA_ROPE, dtype=F32), ((0, 0), (MLA_NOPE, LANES - MLA_NOPE - MLA_ROPE)))
    place = jnp.tile(place, (1, MLA_HEADS))
    blk = lambda w: pl.BlockSpec((None, tk, w), lambda bi, i: (bi, i, 0))
    return pl.pallas_call(
        _mla_kv_kernel,
        grid=(b, k // tk),
        in_specs=[blk(MLA_KV_RANK), blk(MLA_ROPE), _full(wk.shape), _full(wv.shape), _full(place.shape)],
        out_specs=[blk(512), blk(512)],
        out_shape=[jax.ShapeDtypeStruct((b, k, 512), BF16), jax.ShapeDtypeStruct((b, k, 512), BF16)],
        compiler_params=_cparams("parallel", "parallel"),
        name="mla_kv",
    )(lat_all, kpe_all, wk.astype(BF16), wv.astype(BF16), place.astype(BF16))


def _visible_limit(pos_last):
    shift = CHUNK.bit_length() - 1
    return ((pos_last >> shift) + 1) << shift


def _fold_lanes(x, op):
    n = x.shape[-1]
    if n % LANES:
        return x
    out = x[:, :LANES]
    for c in range(1, n // LANES):
        out = op(out, x[:, c * LANES:(c + 1) * LANES])
    return out


def _flash_kernel(lam_ref, q_ref, k_ref, v_ref, g_ref, o_ref, m_ref, l_ref, acc_ref, *,
                  nhm, dqk, maps, scale, pos0, tq, tk, nkv, out_scale):
    i, j = pl.program_id(1), pl.program_id(2)

    @pl.when(j == 0)
    def _():
        m_ref[...] = jnp.full(m_ref.shape, NEG, F32)
        l_ref[...] = jnp.zeros(l_ref.shape, F32)
        acc_ref[...] = jnp.zeros(acc_ref.shape, F32)

    c2 = scale * math.log2(math.e)
    visible = j * tk < _visible_limit(pos0 + (i + 1) * tq - 1)
    unmasked = (j + 1) * tk <= _visible_limit(pos0 + i * tq)

    def scores(hm):
        q = q_ref[:, hm * dqk:(hm + 1) * dqk].astype(BF16)
        k = k_ref[:, hm * dqk:(hm + 1) * dqk].astype(BF16)
        return lax.dot_general(q, k, (((1,), (1,)), ((), ())), preferred_element_type=F32)

    def process(masked):
        if masked:
            q_pos = pos0 + i * tq + lax.broadcasted_iota(jnp.int32, (tq, tk), 0)
            k_pos = j * tk + lax.broadcasted_iota(jnp.int32, (tq, tk), 1)
            mask = k_pos < _visible_limit(q_pos)
        s_next = scores(0)
        for hm in range(nhm):
            s = s_next
            if hm + 1 < nhm:
                s_next = scores(hm + 1)
            if masked:
                s = jnp.where(mask, s, NEG)
            v = v_ref[:, (hm // maps) * DA_DV:(hm // maps + 1) * DA_DV].astype(BF16)
            m_prev = m_ref[hm]
            m_new = jnp.maximum(m_prev, jnp.max(_fold_lanes(s, jnp.maximum), axis=-1, keepdims=True))
            p = jnp.exp2((s - m_new) * c2)
            alpha = jnp.exp2((m_prev - m_new) * c2)
            l_ref[hm] = alpha * l_ref[hm] + jnp.sum(_fold_lanes(p, jnp.add), axis=-1, keepdims=True)
            acc_ref[hm] = alpha * acc_ref[hm] + jnp.dot(p.astype(BF16), v, preferred_element_type=F32)
            m_ref[hm] = m_new

    pl.when(visible & unmasked)(lambda: process(False))
    pl.when(visible & jnp.logical_not(unmasked))(lambda: process(True))

    @pl.when(j == nkv - 1)
    def _():
        for h in range(nhm // maps):
            if maps == 2:
                o = acc_ref[2 * h] / l_ref[2 * h] - lam_ref[0] * (acc_ref[2 * h + 1] / l_ref[2 * h + 1])
                o = o * lax.rsqrt(jnp.mean(o * o, axis=-1, keepdims=True) + EPS) * g_ref[...] * out_scale
            else:
                o = acc_ref[h] / l_ref[h]
            o_ref[:, h * DA_DV:(h + 1) * DA_DV] = o


def _flash(q, k, v, lam, g_sub, *, nhm, dqk, maps, scale, pos0, tq, tk, out_scale, v_col=0):
    b, l, _ = q.shape
    kk = k.shape[1]
    nkv = kk // tk
    nh = nhm // maps

    def kv_idx(bi, i, j):
        last = (_visible_limit(pos0 + (i + 1) * tq - 1) - 1) // tk
        return jnp.minimum(j, last)

    return pl.pallas_call(
        functools.partial(_flash_kernel, nhm=nhm, dqk=dqk, maps=maps, scale=scale, pos0=pos0, tq=tq, tk=tk,
                          nkv=nkv, out_scale=out_scale),
        grid=(b, l // tq, nkv),
        in_specs=[pl.BlockSpec(memory_space=pltpu.SMEM),
                  pl.BlockSpec((None, tq, nhm * dqk), lambda bi, i, j: (bi, i, 0)),
                  pl.BlockSpec((None, tk, nhm * dqk), lambda bi, i, j: (bi, kv_idx(bi, i, j), 0)),
                  pl.BlockSpec((None, tk, nh * DA_DV), lambda bi, i, j: (bi, kv_idx(bi, i, j), v_col)),
                  _full((1, DA_DV))],
        out_specs=pl.BlockSpec((None, tq, nh * DA_DV), lambda bi, i, j: (bi, i, 0)),
        out_shape=jax.ShapeDtypeStruct((b, l, nh * DA_DV), F32),
        scratch_shapes=[pltpu.VMEM((nhm, tq, 1), F32), pltpu.VMEM((nhm, tq, 1), F32),
                        pltpu.VMEM((nhm, tq, DA_DV), F32)],
        compiler_params=_cparams("parallel", "parallel", "arbitrary"),
        name="flash_da" if maps == 2 else "flash_mla",
    )(lam, q, k, v, g_sub)


def _top16(s, rid, payload, val_ref, pay_ref):
    big = float(s.shape[0])
    for it in range(PEER_TOPK):
        m = jnp.max(s, axis=0, keepdims=True)
        am = jnp.min(jnp.where(s == m, rid, big), axis=0, keepdims=True)
        sel = rid == am
        val_ref[it:it + 1, :] = m
        if payload is None:
            pay_ref[it:it + 1, :] = am
        else:
            pay_ref[it:it + 1, :] = jnp.max(jnp.where(sel, payload, -1.0), axis=0, keepdims=True)
        s = jnp.where(sel, -jnp.inf, s)


_PEER_PAIRS = [(a, b) for a in range(PEER_TOPK) for b in range(PEER_TOPK) if (a + 1) * (b + 1) <= PEER_TOPK]
_PEER_CAND_ROWS = -(-len(_PEER_PAIRS) // 8) * 8


def _route_kernel(q_ref, k1_ref, k2_ref, idx_ref, g_ref, v1_ref, i1_ref, v2_ref, i2_ref, vc_ref, ic_ref,
                  cand_ref, cidx_ref):
    tt = q_ref.shape[0]
    half = q_ref.shape[1] // 2
    nt = (((1,), (1,)), ((), ()))
    s1 = lax.dot_general(k1_ref[...].astype(BF16), q_ref[:, :half].astype(BF16), nt, preferred_element_type=F32)
    s2 = lax.dot_general(k2_ref[...].astype(BF16), q_ref[:, half:].astype(BF16), nt, preferred_element_type=F32)
    rid = lax.broadcasted_iota(jnp.int32, (PEER_NKEYS, tt), 0).astype(F32)
    _top16(s1, rid, None, v1_ref, i1_ref)
    _top16(s2, rid, None, v2_ref, i2_ref)
    npairs = len(_PEER_PAIRS)
    cand_ref[npairs:, :] = jnp.full((_PEER_CAND_ROWS - npairs, tt), -jnp.inf, F32)
    cidx_ref[npairs:, :] = jnp.full((_PEER_CAND_ROWS - npairs, tt), -1.0, F32)
    for r, (a, b) in enumerate(_PEER_PAIRS):
        cand_ref[r:r + 1, :] = v1_ref[a:a + 1, :] + v2_ref[b:b + 1, :]
        cidx_ref[r:r + 1, :] = i1_ref[a:a + 1, :] * float(PEER_NKEYS) + i2_ref[b:b + 1, :]
    rid2 = lax.broadcasted_iota(jnp.int32, (_PEER_CAND_ROWS, tt), 0).astype(F32)
    _top16(cand_ref[...], rid2, cidx_ref[...], vc_ref, ic_ref)
    sc = vc_ref[...]
    e = jnp.exp(sc - sc[0:1, :])
    g_ref[...] = e / jnp.sum(e, axis=0, keepdims=True)
    idx_ref[...] = ic_ref[...].astype(jnp.int32)


def _peer_route(q, k1, k2):
    t = q.shape[0]
    tt = PEER_TT
    dk = 2 * PEER_NKEYS
    kspec = pl.BlockSpec((None, PEER_NKEYS, dk // 2), lambda i, h: (h, 0, 0))
    ospec = pl.BlockSpec((None, PEER_TOPK, tt), lambda i, h: (i, h, 0))
    sm = pltpu.VMEM((PEER_TOPK, tt), F32)
    return pl.pallas_call(
        _route_kernel,
        grid=(t // tt, PEER_HEADS),
        in_specs=[pl.BlockSpec((tt, dk), lambda i, h: (i, h)), kspec, kspec],
        out_specs=[ospec, ospec],
        out_shape=[jax.ShapeDtypeStruct((t // tt, PEER_HK, tt), jnp.int32),
                   jax.ShapeDtypeStruct((t // tt, PEER_HK, tt), F32)],
        scratch_shapes=[sm, sm, sm, sm, sm, sm,
                        pltpu.VMEM((_PEER_CAND_ROWS, tt), F32), pltpu.VMEM((_PEER_CAND_ROWS, tt), F32)],
        compiler_params=_cparams("parallel", "parallel"),
        name="peer_route",
    )(q, k1, k2)


def _pack_table(tab):
    e, d = tab.shape
    tb = lax.bitcast_convert_type(tab.astype(BF16), jnp.uint16).astype(jnp.uint32)
    words = tb[:, :d // 2] | (tb[:, d // 2:] << 16)
    return lax.bitcast_convert_type(words, F32)


def _sc_unpack(words):
    bits = lax.bitcast_convert_type(words, jnp.int32)
    return lax.bitcast_convert_type(bits << 16, F32), lax.bitcast_convert_type(bits & jnp.int32(-65536), F32)


def _sc_gelu(x):
    u = math.sqrt(2.0 / math.pi) * (x + 0.044715 * (x * x * x))
    tanh_u = 1.0 - 2.0 / (jnp.exp(2.0 * u) + 1.0)
    return 0.5 * x * (1.0 + tanh_u)


def _sc_peer(u_tab, v_tab, idx_tm, g_tm, h, side_idx=None):
    t, nk = idx_tm.shape
    w = u_tab.shape[1]
    d = 2 * w
    half_rows = SC_GATHER_ROWS
    nbuf = SC_BUFFERS
    per_tok = nk // half_rows
    mesh = plsc.VectorSubcoreMesh(core_axis_name="core", subcore_axis_name="subcore")
    n_workers = mesh.num_cores * mesh.num_subcores
    tpw = t // n_workers
    tb = SC_TOKEN_BLOCK
    steps = per_tok * tb
    dot_rows = SC_LANES
    sum_rows = 16
    assert t % n_workers == 0 and tpw % tb == 0 and w % SC_LANES == 0
    assert nk % half_rows == 0 and half_rows % SC_LANES == 0 and steps % nbuf == 0

    n_side = 0 if side_idx is None else side_idx.shape[0]
    side_w = SC_SIDE_ROWS
    lw = n_side // n_workers
    n_win = lw // side_w
    blk_side = steps * side_w
    if n_side:
        assert nbuf == 2 and steps % 2 == 0 and n_side % (n_workers * blk_side) == 0
        assert n_win + 2 <= (tpw // tb) * steps

    def lanes(start):
        return pl.ds(pl.multiple_of(start, SC_LANES), SC_LANES)

    y_type = jax.ShapeDtypeStruct((t, d), F32)
    side_type = jax.ShapeDtypeStruct((n_side, w), F32)
    scratch = [pltpu.VMEM((2 * tb * nk,), jnp.int32), pltpu.VMEM((2, tb, d), F32), pltpu.VMEM((2, tb, nk), F32),
               pltpu.VMEM((nbuf, half_rows, w), F32), pltpu.VMEM((tb, nk), F32), pltpu.VMEM((tb, d), F32),
               pltpu.SemaphoreType.DMA((nbuf + 3,))]
    if n_side:
        scratch += [pltpu.VMEM((2 * blk_side,), jnp.int32), pltpu.VMEM((2, 2, side_w, w), F32),
                    pltpu.SemaphoreType.DMA((8,))]

    @functools.partial(
        pl.kernel, out_type=[y_type, side_type, side_type] if n_side else y_type, mesh=mesh, scratch_types=scratch,
        compiler_params=pltpu.CompilerParams(needs_layout_passes=False))
    def peer_kernel(u_hbm, v_hbm, idx_hbm, g_hbm, h_hbm, *refs):
        if n_side:
            (sidx_hbm, y_hbm, gu_hbm, gv_hbm, idx_v, h_v, g_v, rows_v, coef_v, y_v, sems,
             sidx_v, sbuf, ssems) = refs
        else:
            y_hbm, idx_v, h_v, g_v, rows_v, coef_v, y_v, sems = refs
        wid = lax.axis_index("subcore") * mesh.num_cores + lax.axis_index("core")
        lane = lax.iota(jnp.int32, SC_LANES)
        n_blocks = tpw // tb
        cur = {}

        def fetch(blk, par):
            tok0 = wid * tpw + blk * tb
            half = par * (tb * nk)
            ids = idx_v.at[pl.ds(half if isinstance(half, int) else pl.multiple_of(half, tb * nk), tb * nk)]
            return (pltpu.make_async_copy(idx_hbm.at[pl.ds(pl.multiple_of(tok0 * nk, nk), tb * nk)], ids, sems.at[nbuf]),
                    pltpu.make_async_copy(h_hbm.at[pl.ds(tok0, tb)], h_v.at[par], sems.at[nbuf + 1]),
                    pltpu.make_async_copy(g_hbm.at[pl.ds(tok0, tb)], g_v.at[par], sems.at[nbuf + 2]))

        def gather(tab_hbm, step, buf, par=None):
            par = cur['par'] if par is None else par
            start = par * (tb * nk) + step * half_rows
            rows = idx_v.at[pl.ds(start if isinstance(start, int) else pl.multiple_of(start, half_rows), half_rows)]
            return pltpu.make_async_copy(tab_hbm.at[rows], rows_v.at[buf], sems.at[buf])

        def prime(tab_hbm, par=None):
            for j in range(nbuf - 1):
                gather(tab_hbm, j, j, par).start()

        def side_in(tab_hbm, tab, n, par):
            slot = ((n // steps) % 2) * blk_side + (n % steps) * side_w
            rows = sidx_v.at[pl.ds(pl.multiple_of(slot, side_w), side_w)]
            return pltpu.make_async_copy(tab_hbm.at[rows], sbuf.at[tab, par], ssems.at[tab * 4 + par])

        def side_out(out_hbm, tab, n, par):
            dst = out_hbm.at[pl.ds(pl.multiple_of(wid * lw + n * side_w, side_w), side_w)]
            return pltpu.make_async_copy(sbuf.at[tab, par], dst, ssems.at[tab * 4 + 2 + par])

        def side_step(tab_hbm, out_hbm, tab, n, par):
            @pl.when((n >= 2) & (n < n_win + 2))
            def _():
                side_out(out_hbm, tab, n - 2, par).wait()

            @pl.when(n < n_win)
            def _():
                side_in(tab_hbm, tab, n, par).start()

            @pl.when((n >= 1) & (n < n_win + 1))
            def _():
                side_in(tab_hbm, tab, n - 1, 1 - par).wait()
                side_out(out_hbm, tab, n - 1, 1 - par).start()

        def run(tab_hbm, work, blk=None, side=None):
            @pl.loop(0, steps, step=nbuf)
            def _(s):
                for j in range(nbuf):
                    if side is not None:
                        side_step(tab_hbm, side[0], side[1], blk * steps + s + j, j)
                    ahead = s + j + nbuf - 1

                    @pl.when(ahead < steps)
                    def _():
                        gather(tab_hbm, ahead, (j + nbuf - 1) % nbuf).start()

                    gather(tab_hbm, s + j, j).wait()
                    work(s + j, j)

        def dots(step, buf):
            tok = step // per_tok
            col0 = (step % per_tok) * half_rows

            def group(q, _):
                r0 = q * dot_rows

                def chunk(c, accs):
                    h_lo = h_v[cur['par'], tok, lanes(c * SC_LANES)]
                    h_hi = h_v[cur['par'], tok, lanes(w + c * SC_LANES)]
                    out = []
                    for r in range(dot_rows):
                        lo, hi = _sc_unpack(rows_v[buf, r0 + r, lanes(c * SC_LANES)])
                        out.append(accs[r] + lo * h_lo + hi * h_hi)
                    return tuple(out)

                accs = lax.fori_loop(0, w // SC_LANES, chunk, (jnp.zeros((SC_LANES,), F32),) * dot_rows)
                vec = jnp.zeros((SC_LANES,), F32)
                for r in range(dot_rows):
                    vec = jnp.where(lane == r, jnp.sum(accs[r]), vec)
                coef_v[tok, lanes(col0 + r0)] = vec
                return 0

            lax.fori_loop(0, half_rows // dot_rows, group, 0)

        def gates():
            @pl.loop(0, tb)
            def _(tok):
                @pl.loop(0, nk // SC_LANES)
                def _(q):
                    ds = lanes(q * SC_LANES)
                    coef_v[tok, ds] = g_v[cur['par'], tok, ds] * _sc_gelu(coef_v[tok, ds])

        def weighted_sum(step, buf):
            tok = step // per_tok
            col0 = (step % per_tok) * half_rows

            @pl.when(col0 == 0)
            def _():
                @plsc.parallel_loop(0, d // SC_LANES)
                def _(c):
                    y_v[tok, lanes(c * SC_LANES)] = jnp.zeros((SC_LANES,), F32)

            def group(g, _):
                r0 = g * sum_rows
                cvec = coef_v[tok, lanes(col0 + (r0 // SC_LANES) * SC_LANES)]
                base = r0 % SC_LANES
                cs = [cvec.at[jnp.full((SC_LANES,), base + r, jnp.int32)].get(mode="promise_in_bounds")
                      for r in range(sum_rows)]

                @plsc.parallel_loop(0, w // SC_LANES, unroll=SC_LANES // sum_rows)
                def _(c):
                    los, his = [], []
                    for r in range(sum_rows):
                        lo, hi = _sc_unpack(rows_v[buf, r0 + r, lanes(c * SC_LANES)])
                        los.append(cs[r] * lo)
                        his.append(cs[r] * hi)
                    while len(los) > 1:
                        los = [a + b for a, b in zip(los[::2], los[1::2])]
                        his = [a + b for a, b in zip(his[::2], his[1::2])]
                    y_v[tok, lanes(c * SC_LANES)] = y_v[tok, lanes(c * SC_LANES)] + los[0]
                    y_v[tok, lanes(w + c * SC_LANES)] = y_v[tok, lanes(w + c * SC_LANES)] + his[0]

                return 0

            lax.fori_loop(0, half_rows // sum_rows, group, 0)

        for c in fetch(0, 0):
            c.start()
        for c in fetch(0, 0):
            c.wait()
        prime(u_hbm, 0)

        @pl.loop(0, n_blocks)
        def _(blk):
            tok0 = wid * tpw + blk * tb
            cur['par'] = blk % 2
            if n_side:
                @pl.when(blk * steps < n_win)
                def _():
                    src = sidx_hbm.at[pl.ds(pl.multiple_of(wid * lw + blk * blk_side, blk_side), blk_side)]
                    pltpu.sync_copy(src, sidx_v.at[pl.ds(pl.multiple_of((blk % 2) * blk_side, blk_side), blk_side)])

            run(u_hbm, dots, blk, (gu_hbm, 0) if n_side else None)
            prime(v_hbm)

            @pl.when(blk + 1 < n_blocks)
            def _():
                for c in fetch(blk + 1, 1 - cur['par']):
                    c.start()

            gates()
            run(v_hbm, weighted_sum, blk, (gv_hbm, 1) if n_side else None)

            @pl.when(blk + 1 < n_blocks)
            def _():
                for c in fetch(blk + 1, 1 - cur['par']):
                    c.wait()
                prime(u_hbm, 1 - cur['par'])

            pltpu.sync_copy(y_v, y_hbm.at[pl.ds(tok0, tb)])

    if n_side:
        return peer_kernel(u_tab, v_tab, idx_tm.reshape(t * nk), g_tm, h, side_idx)
    return peer_kernel(u_tab, v_tab, idx_tm.reshape(t * nk), g_tm, h)


def _unpack(words):
    w = pltpu.bitcast(words, jnp.uint32)
    lo = pltpu.bitcast(w << 16, F32)
    hi = pltpu.bitcast(w & jnp.uint32(0xFFFF0000), F32)
    return lo, hi


def _peer_act_kernel(h_ref, gu_ref, g_ref, coef_ref, act_ref):
    c = pl.program_id(1)
    half = h_ref.shape[1] // 2
    h_lo, h_hi = h_ref[:, :half], h_ref[:, half:]
    lane = lax.broadcasted_iota(jnp.int32, act_ref.shape, 1)

    @pl.when(c == 0)
    def _():
        act_ref[...] = jnp.zeros(act_ref.shape, F32)

    act = act_ref[...]
    for k in range(PEER_HKC):
        lo, hi = _unpack(gu_ref[k])
        a = jnp.sum(lo * h_lo + hi * h_hi, axis=-1, keepdims=True)
        act = jnp.where(lane == c * PEER_HKC + k, a, act)
    act_ref[...] = act

    @pl.when(c == pl.num_programs(1) - 1)
    def _():
        coef_ref[...] = g_ref[...].T * jax.nn.gelu(act)


def _peer_act(h, gu, g):
    t, d = h.shape
    tt = PEER_TT
    nt = t // tt
    return pl.pallas_call(
        _peer_act_kernel,
        grid=(nt, PEER_HK // PEER_HKC),
        in_specs=[pl.BlockSpec((tt, d), lambda i, c: (i, 0)),
                  pl.BlockSpec((None, PEER_HKC, tt, d // 2), lambda i, c: (i, c, 0, 0)),
                  pl.BlockSpec((None, PEER_HK, tt), lambda i, c: (i, 0, 0))],
        out_specs=pl.BlockSpec((None, tt, PEER_HK), lambda i, c: (i, 0, 0)),
        out_shape=jax.ShapeDtypeStruct((nt, tt, PEER_HK), F32),
        scratch_shapes=[pltpu.VMEM((tt, PEER_HK), F32)],
        compiler_params=_cparams("parallel", "arbitrary"),
        name="peer_act",
    )(h, gu, g)


def _peer_mix_kernel(coef_ref, gv_ref, y_ref):
    c = pl.program_id(1)
    half = y_ref.shape[1] // 2

    @pl.when(c == 0)
    def _():
        y_ref[...] = jnp.zeros(y_ref.shape, F32)

    coef = coef_ref[...]
    lane = lax.broadcasted_iota(jnp.int32, coef.shape, 1)
    acc_lo, acc_hi = y_ref[:, :half], y_ref[:, half:]
    for k in range(PEER_HKC):
        ck = jnp.sum(jnp.where(lane == c * PEER_HKC + k, coef, 0.0), axis=-1, keepdims=True)
        lo, hi = _unpack(gv_ref[k])
        acc_lo = acc_lo + ck * lo
        acc_hi = acc_hi + ck * hi
    y_ref[:, :half] = acc_lo
    y_ref[:, half:] = acc_hi


def _peer_mix(coef, gv):
    nt, tt, _ = coef.shape
    w = gv.shape[-1]
    return pl.pallas_call(
        _peer_mix_kernel,
        grid=(nt, PEER_HK // PEER_HKC),
        in_specs=[pl.BlockSpec((None, tt, PEER_HK), lambda i, c: (i, 0, 0)),
                  pl.BlockSpec((None, PEER_HKC, tt, w), lambda i, c: (i, c, 0, 0))],
        out_specs=pl.BlockSpec((tt, 2 * w), lambda i, c: (i, 0)),
        out_shape=jax.ShapeDtypeStruct((nt * tt, 2 * w), F32),
        compiler_params=_cparams("parallel", "arbitrary"),
        name="peer_mix",
    )(coef, gv)


def _peer_out_kernel(x_ref, gt_ref, y_ref, gf_ref, o_ref, *, final):
    x = x_ref[...] + gt_ref[...] * y_ref[...]
    if final:
        x = x * lax.rsqrt(jnp.mean(x * x, axis=-1, keepdims=True) + EPS) * gf_ref[...]
    o_ref[...] = x


def _peer_out(x, gt, y, g_final, final):
    t, d = x.shape
    tt = PEER_TT
    nt = t // tt
    row = pl.BlockSpec((tt, d), lambda i: (i, 0))
    if gt.ndim == 3:
        per_seq = nt // gt.shape[0]
        gt_spec = pl.BlockSpec((None, 1, d), lambda i: (i // per_seq, 0, 0))
    else:
        gt_spec = row
    return pl.pallas_call(
        functools.partial(_peer_out_kernel, final=final),
        grid=(nt,),
        in_specs=[row, gt_spec, row, _full((1, d))],
        out_specs=row, out_shape=jax.ShapeDtypeStruct((t, d), F32),
        compiler_params=_cparams("parallel"),
        name="peer_out",
    )(x, gt, y, g_final.reshape(1, d))


def _peer(x, g_ffn, sc2, sh2, gt2, w_q, k1, k2, u_pack, v_pack, g_final, final, tl):
    b, l, d = x.shape
    t = b * l
    q, h = _norm_mod_matmul(x, g_ffn, sc2, sh2, w_q, tl, want_h=True)
    idx, g = _peer_route(q.reshape(t, -1), k1, k2)
    h = h.reshape(t, d)
    nt = t // PEER_TT
    n_tc = nt * PEER_TC_TILES // 32
    n_sc = nt - n_tc
    t_sc = n_sc * PEER_TT
    token_major = lambda a: jnp.transpose(a, (0, 2, 1)).reshape(-1, PEER_HK)
    if n_tc:
        y_sc, gu, gv = _sc_peer(u_pack, v_pack, token_major(idx[:n_sc]), token_major(g[:n_sc]), h[:t_sc],
                                idx[n_sc:].reshape(-1))
        rows = lambda a: a.reshape(n_tc, PEER_HK, PEER_TT, d // 2)
        y_tc = _peer_mix(_peer_act(h[t_sc:], rows(gu), g[n_sc:]), rows(gv))
        y = jnp.concatenate([y_sc, y_tc], axis=0)
    else:
        y = _sc_peer(u_pack, v_pack, token_major(idx), token_major(g), h)
    if l % PEER_TT:
        gt2 = jnp.broadcast_to(gt2, (b, l, d)).reshape(t, d)
    return _peer_out(x.reshape(t, d), gt2, y, g_final, final).reshape(b, l, d)


def _trunk(x, mod, pos0, past, p, tl):
    b, l, d = x.shape
    outs = {}
    for layer in range(DEPTH):
        sh1, sc1, gt1, sh2, sc2, gt2 = [m[:, None, :] for m in jnp.split(mod[layer], 6, axis=-1)]
        if layer not in p['packed']:
            u_tab, v_tab = p['peer_u'][layer], p['peer_v'][layer]
            if layer:
                u_tab, v_tab, x = lax.optimization_barrier((u_tab, v_tab, x))
            p['packed'][layer] = (_pack_table(u_tab), _pack_table(v_tab))
        u_pack, v_pack = p['packed'][layer]
        peer = functools.partial(
            _peer, g_ffn=p['g_ffn'][layer], sc2=sc2, sh2=sh2, gt2=gt2, w_q=p['peer_w_q'][layer].astype(BF16),
            k1=p['peer_k1'][layer], k2=p['peer_k2'][layer], u_pack=u_pack, v_pack=v_pack,
            g_final=p['g_final'], final=layer == DEPTH - 1, tl=tl)
        if layer % 2 == 0:
            e = layer // 2
            if past is None:
                hr = jnp.zeros((b, S5_GROUPS, S5_STATE), F32)
                hi = jnp.zeros((b, S5_GROUPS, S5_STATE), F32)
            else:
                hr, hi = past['s5_re'][e], past['s5_im'][e]
            n_seg = EVEN_SEGMENTS if l % (EVEN_SEGMENTS * max(tl, GM_CHUNK, PEER_TT)) == 0 else 1
            segs = []
            for xs in jnp.split(x, n_seg, axis=1):
                y_in = _norm_mod_matmul(xs, p['g_mix'][layer], sc1, sh1, p['w_in_e'][e].astype(BF16), tl)
                ya, hr, hi = _s5(y_in, hr, hi, p['s5'][e], p['s5_d'][e], p['w_glu'][e], p['b_glu'][e], tl)
                yb, vn = _gmlp(y_in, p['gm_g_v'][e], p['gm_w_s'][e], p['gm_b_s'][e], tl, want_vn=past is not None)
                segs.append(peer(_proj_residual(xs, gt1, ya, yb, p['w_out_e'][e], tl)))
            x = segs[0] if n_seg == 1 else jnp.concatenate(segs, axis=1)
            outs.update(s5_re=hr, s5_im=hi, gm_v=vn)
        else:
            o = layer // 2
            w_in = jnp.pad(p['w_in_o'][o], ((0, 0), (0, ODD_IN_PAD - p['w_in_o'][o].shape[1]))).astype(BF16)
            y_in = _norm_mod_matmul(x, p['g_mix'][layer], sc1, sh1, w_in, tl)
            qd, kd, qm, lat, kpe = _odd_prep(y_in, pos0, p['mla_g_q'][o], p['mla_w_uq'][o], p['mla_g_kv'][o], tl)
            v_new = y_in[:, :, 2 * 512:3 * 512]
            lam_init = 0.8 - 0.6 * math.exp(-0.3 * layer)
            lam = (jnp.exp(jnp.sum(p['da_lq1'][o] * p['da_lk1'][o])) - jnp.exp(jnp.sum(p['da_lq2'][o] * p['da_lk2'][o]))
                   + lam_init).reshape(1).astype(F32)
            g_sub = p['da_g_sub'][o].reshape(1, DA_DV)
            if past is None:
                k_all, v_all, v_col, lat_all, kpe_all = kd, y_in, 2, lat, kpe
                tq = tk = tl
            else:
                k_all = jnp.concatenate([past['da_k'][o].reshape(b, -1, 512), kd], axis=1)
                v_all = jnp.concatenate([past['da_v'][o].reshape(b, -1, 512), v_new], axis=1)
                lat_all = jnp.concatenate([past['mla_lat'][o], lat], axis=1)
                kpe_all = jnp.concatenate([past['mla_kpe'][o], kpe], axis=1)
                v_col, tq, tk = 0, l, k_all.shape[1]
            yc = _flash(qd, k_all, v_all, lam, g_sub, nhm=2 * DA_HEADS, dqk=DA_DH, maps=2, scale=DA_DH ** -0.5,
                        pos0=pos0, tq=tq, tk=tk, out_scale=1.0 - lam_init, v_col=v_col)
            km, vm = _mla_kv(lat_all, kpe_all, p['mla_w_ukv'][o], tk)
            yd = _flash(qm, km, vm, lam, g_sub, nhm=MLA_HEADS, dqk=LANES, maps=1,
                        scale=(MLA_NOPE + MLA_ROPE) ** -0.5, pos0=pos0, tq=tq, tk=tk, out_scale=1.0)
            x = peer(_proj_residual(x, gt1, yc, yd, p['w_out_o'][o], tl))
            outs.update(da_k=kd.reshape(b, l, DA_HEADS, 2 * DA_DH), da_v=v_new.reshape(b, l, DA_HEADS, DA_DV),
                        mla_lat=lat, mla_kpe=kpe)
    return x, outs


def kernel(x_prompt, x_sample, c_prompt, c_sample, state_s5_re, state_s5_im, cache_da_k, cache_da_v, cache_mla_latent, cache_mla_kpe, w_ada, b_ada, g_mix, g_ffn, g_final, w_in_e, w_out_e, s5_lam_re, s5_lam_im, s5_log_dt, s5_b_re, s5_b_im, s5_c_re, s5_c_im, s5_d, w_glu, b_glu, gm_g_v, gm_w_s, gm_b_s, w_in_o, w_out_o, da_lq1, da_lk1, da_lq2, da_lk2, da_g_sub, mla_g_q, mla_w_uq, mla_g_kv, mla_w_ukv, peer_w_q, peer_k1, peer_k2, peer_u, peer_v):
    p = dict(g_mix=g_mix, g_ffn=g_ffn, g_final=g_final, w_in_e=w_in_e, w_out_e=w_out_e, s5_d=s5_d, w_glu=w_glu,
             b_glu=b_glu, gm_g_v=gm_g_v, gm_w_s=gm_w_s, gm_b_s=gm_b_s, w_in_o=w_in_o, w_out_o=w_out_o,
             da_lq1=da_lq1, da_lk1=da_lk1, da_lq2=da_lq2, da_lk2=da_lk2, da_g_sub=da_g_sub, mla_g_q=mla_g_q,
             mla_w_uq=mla_w_uq, mla_g_kv=mla_g_kv, mla_w_ukv=mla_w_ukv, peer_w_q=peer_w_q, peer_k1=peer_k1,
             peer_k2=peer_k2)
    n_even = (DEPTH + 1) // 2
    p['s5'] = [_s5_params(s5_lam_re[e], s5_lam_im[e], s5_log_dt[e], s5_b_re[e], s5_b_im[e], s5_c_re[e], s5_c_im[e])
               for e in range(n_even)]
    p['peer_u'], p['peer_v'], p['packed'] = peer_u, peer_v, {}
    past = dict(s5_re=state_s5_re, s5_im=state_s5_im, da_k=cache_da_k, da_v=cache_da_v,
                mla_lat=cache_mla_latent, mla_kpe=cache_mla_kpe)
    nb = x_prompt.shape[0]
    mod = _ada(jnp.concatenate([c_prompt, c_sample], axis=0), w_ada, b_ada)
    past_len = cache_da_k.shape[2]
    y_s, o_s = _trunk(x_sample, mod[:, nb:], past_len, past, p, tl=x_sample.shape[1])
    parts = [_trunk(x_prompt[s:s + 1], mod[:, s:s + 1], 0, None, p, tl=PROMPT_TILE) for s in range(nb)]
    y_p = jnp.concatenate([y for y, _ in parts], axis=0)
    o_p = {k: jnp.concatenate([o[k] for _, o in parts], axis=0) for k in parts[0][1] if parts[0][1][k] is not None}
    st = lambda a: a[None]
    return (y_p, y_s, st(o_p['s5_re']), st(o_p['s5_im']), st(o_s['s5_re']), st(o_s['s5_im']), st(o_s['gm_v']),
            st(o_p['da_k']), st(o_p['da_v']), st(o_s['da_k']), st(o_s['da_v']),
            st(o_p['mla_lat']), st(o_p['mla_kpe']), st(o_s['mla_lat']), st(o_s['mla_kpe']))
```
